```python
import jax, jax.numpy as jnp
from jax import lax
import numpy as np

D_MODEL = 1024
BATCH = 16
SEQ = 2048
DEPTH = 1

N_HEADS = 8
N_KV_HEADS = 2
HEAD_DIM = 64
Q_WIDTH = N_HEADS * HEAD_DIM
KV_WIDTH = N_KV_HEADS * HEAD_DIM
ROPE_THETA = 500000.0
ROT_FRACTION = 4
IDX_HEADS = 8
IDX_DIM = 64
IDXQ_WIDTH = IDX_HEADS * IDX_DIM
TOPK_MAX = 256
Q_BLOCK = 128
CONV_DIM = 512
CONV_WIDTH = 3
N_BRANCHES = 2
IN_SPLIT_SIZES = (Q_WIDTH, KV_WIDTH, KV_WIDTH, IDXQ_WIDTH, IDX_DIM, IDX_HEADS,
                  CONV_DIM, CONV_DIM, CONV_DIM, N_BRANCHES * D_MODEL)
IN_WIDTH = Q_WIDTH + 2 * KV_WIDTH + IDXQ_WIDTH + IDX_DIM + IDX_HEADS + 3 * CONV_DIM + N_BRANCHES * D_MODEL
N_GROUPS = 4
EXPERTS_PER_GROUP = 8
N_EXPERTS = N_GROUPS * EXPERTS_PER_GROUP
EXPERT_TOPK = 2
D_EXPERT = 512
EXPERT_BLOCK = 128
LN_EPS = 1e-5
DEEPNORM_ALPHA = (2 * DEPTH) ** 0.25
DEEPNORM_BETA = (8 * DEPTH) ** -0.25

kernel_name = "hybrid_dsa_shortconv_hmoe_block"


def _split_points():
    return [int(s) for s in np.cumsum(IN_SPLIT_SIZES)[:-1]]


def layer_norm(x, g, b):
    xf = x.astype(jnp.float32)
    mu = jnp.mean(xf, axis=-1, keepdims=True)
    var = jnp.mean(jnp.square(xf - mu), axis=-1, keepdims=True)
    y = (xf - mu) * lax.rsqrt(var + LN_EPS)
    return (y * g.astype(jnp.float32) + b.astype(jnp.float32)).astype(x.dtype)


def partial_rope(t, pos):
    rot = t.shape[-1] // ROT_FRACTION
    half = rot // 2
    inv_freq = ROPE_THETA ** (-jnp.arange(half, dtype=jnp.float32) / half)
    ang = pos.astype(jnp.float32)[:, None] * inv_freq[None, :]
    cos = jnp.cos(ang)[None, :, None, :].astype(t.dtype)
    sin = jnp.sin(ang)[None, :, None, :].astype(t.dtype)
    x1 = t[..., :half]
    x2 = t[..., half:rot]
    return jnp.concatenate([x1 * cos - x2 * sin, x2 * cos + x1 * sin, t[..., rot:]], axis=-1)


def dsa_attention(q, k, v, qi, ki, wi, top_k):
    B, S = q.shape[0], q.shape[1]
    n_blocks = S // Q_BLOCK
    group = N_HEADS // N_KV_HEADS
    kpos = jnp.arange(S)

    def to_blocks(t):
        return jnp.moveaxis(t.reshape((B, n_blocks, Q_BLOCK) + t.shape[2:]), 1, 0)

    def one_block(args):
        qb, qib, wib, t0 = args
        qpos = t0 + jnp.arange(Q_BLOCK)
        admissible = kpos[None, :] <= qpos[:, None]
        rel = jax.nn.relu(jnp.einsum('bqhd,bsd->bqhs', qib, ki) * (IDX_DIM ** -0.5))
        score = jnp.einsum('bqhs,bqh->bqs', rel, wib) * (IDX_HEADS ** -0.5)
        score = jnp.where(admissible[None], score.astype(jnp.float32), -jnp.inf)
        _, idx = lax.top_k(score, top_k)
        valid = idx <= qpos[None, :, None]
        kg = jax.vmap(lambda kb, ib: kb[ib])(k, idx)
        vg = jax.vmap(lambda vb, ib: vb[ib])(v, idx)
        qg = qb.reshape(B, Q_BLOCK, N_KV_HEADS, group, HEAD_DIM)
        logits = jnp.einsum('bqhgd,bqkhd->bqhgk', qg, kg).astype(jnp.float32) * (HEAD_DIM ** -0.5)
        logits = jnp.where(valid[:, :, None, None, :], logits, -jnp.inf)
        p = jax.nn.softmax(logits, axis=-1).astype(vg.dtype)
        o = jnp.einsum('bqhgk,bqkhd->bqhgd', p, vg)
        return o.reshape(B, Q_BLOCK, N_HEADS * HEAD_DIM)

    t0s = jnp.arange(n_blocks) * Q_BLOCK
    out = lax.map(one_block, (to_blocks(q), to_blocks(qi), to_blocks(wi), t0s))
    return jnp.moveaxis(out, 0, 1).reshape(B, S, Q_WIDTH)


def short_conv(conv_in, conv_b, conv_c, conv_w):
    u = conv_c * conv_in
    rhs = conv_w[:, None, :].astype(u.dtype)
    y = lax.conv_general_dilated(u, rhs, window_strides=(1,), padding=[(CONV_WIDTH - 1, 0)],
                                 dimension_numbers=('NWC', 'WIO', 'NWC'),
                                 feature_group_count=CONV_DIM)
    return conv_b * y


def token_mixer(h, pos, top_k, w_in, gate_bias, w_attn_up, w_conv_up, conv_w, w_out):
    B, S, _ = h.shape
    proj = h @ w_in
    q, k, v, qi, ki, wi, conv_in, conv_b, conv_c, gates = jnp.split(proj, _split_points(), axis=-1)
    q = partial_rope(q.reshape(B, S, N_HEADS, HEAD_DIM), pos)
    k = partial_rope(k.reshape(B, S, N_KV_HEADS, HEAD_DIM), pos)
    v = v.reshape(B, S, N_KV_HEADS, HEAD_DIM)
    qi = partial_rope(qi.reshape(B, S, IDX_HEADS, IDX_DIM), pos)
    ki = partial_rope(ki.reshape(B, S, 1, IDX_DIM), pos)[:, :, 0]
    attn = dsa_attention(q, k, v, qi, ki, wi, top_k)
    conv = short_conv(conv_in, conv_b, conv_c, conv_w)
    g_attn, g_conv = jnp.split(jax.nn.sigmoid(gates + gate_bias), N_BRANCHES, axis=-1)
    merged = g_attn * (attn @ w_attn_up) + g_conv * (conv @ w_conv_up)
    return merged @ w_out


def hierarchical_moe(h, rg_w, rg_b, re_w, re_b, w_gate, w_up, w_down):
    B, S, D = h.shape
    n_tok = B * S
    xf = h.reshape(n_tok, D)
    g_logits = (xf @ rg_w + rg_b).astype(jnp.float32)
    grp = jnp.argmax(g_logits, axis=-1).astype(jnp.int32)
    p_grp = jnp.take_along_axis(jax.nn.softmax(g_logits, axis=-1), grp[:, None], axis=1)[:, 0]
    e_all = jnp.einsum('nd,gde->nge', xf, re_w) + re_b
    e_logits = jnp.take_along_axis(e_all, grp[:, None, None], axis=1)[:, 0].astype(jnp.float32)
    top_val, top_loc = lax.top_k(e_logits, EXPERT_TOPK)
    gate = p_grp[:, None] * jax.nn.softmax(top_val, axis=-1)
    expert = grp[:, None] * EXPERTS_PER_GROUP + top_loc.astype(jnp.int32)
    n_assign = n_tok * EXPERT_TOPK
    a_expert = expert.reshape(-1)
    a_token = jnp.repeat(jnp.arange(n_tok, dtype=jnp.int32), EXPERT_TOPK)
    a_w = gate.reshape(-1).astype(h.dtype)
    order = jnp.argsort(a_expert, stable=True)
    se, st, sw = a_expert[order], a_token[order], a_w[order]
    counts = jax.ops.segment_sum(jnp.ones_like(a_expert), a_expert, num_segments=N_EXPERTS)
    padded = ((counts + EXPERT_BLOCK - 1) // EXPERT_BLOCK) * EXPERT_BLOCK
    pad_end = jnp.cumsum(padded)
    pad_start = pad_end - padded
    raw_start = jnp.cumsum(counts) - counts
    dest = pad_start[se] + jnp.arange(n_assign, dtype=jnp.int32) - raw_start[se]
    cap = (-(-n_assign // EXPERT_BLOCK)) * EXPERT_BLOCK + N_EXPERTS * EXPERT_BLOCK
    n_blocks = cap // EXPERT_BLOCK
    row_token = jnp.zeros((cap,), jnp.int32).at[dest].set(st)
    row_w = jnp.zeros((cap,), h.dtype).at[dest].set(sw)
    block_start = jnp.arange(n_blocks, dtype=jnp.int32) * EXPERT_BLOCK
    block_expert = jnp.minimum(jnp.searchsorted(pad_end, block_start, side='right'), N_EXPERTS - 1)
    xs = xf[row_token].reshape(n_blocks, EXPERT_BLOCK, D)

    def expert_block(args):
        xb, e = args
        hid = jax.nn.silu(xb @ w_gate[e]) * (xb @ w_up[e])
        return hid @ w_down[e]

    ys = lax.map(expert_block, (xs, block_expert)).reshape(cap, D)
    out = jax.ops.segment_sum(ys * row_w[:, None], row_token, num_segments=n_tok)
    return out.reshape(B, S, D)


def setup_inputs(seed: int = 0) -> dict:
    key = jax.random.key(seed)
    ks = jax.random.split(key, 20)
    nrm = jax.random.normal
    L = DEPTH
    v_lo = Q_WIDTH + KV_WIDTH
    v_hi = v_lo + KV_WIDTH
    w_in = nrm(ks[1], (L, D_MODEL, IN_WIDTH), jnp.float32) * D_MODEL ** -0.5
    w_in = w_in.at[:, :, v_lo:v_hi].multiply(DEEPNORM_BETA)
    return {
        "x": nrm(ks[0], (BATCH, SEQ, D_MODEL), jnp.float32),
        "w_in": w_in,
        "gate_bias": 0.01 * nrm(ks[2], (L, N_BRANCHES * D_MODEL), jnp.float32),
        "w_attn_up": nrm(ks[3], (L, Q_WIDTH, D_MODEL), jnp.float32) * Q_WIDTH ** -0.5 * DEEPNORM_BETA,
        "w_conv_up": nrm(ks[4], (L, CONV_DIM, D_MODEL), jnp.float32) * CONV_DIM ** -0.5 * DEEPNORM_BETA,
        "conv_w": nrm(ks[5], (L, CONV_WIDTH, CONV_DIM), jnp.float32) * CONV_WIDTH ** -0.5,
        "w_out": nrm(ks[6], (L, D_MODEL, D_MODEL), jnp.float32) * D_MODEL ** -0.5 * DEEPNORM_BETA,
        "ln1_g": 1.0 + 0.02 * nrm(ks[7], (L, D_MODEL), jnp.float32),
        "ln1_b": 0.02 * nrm(ks[8], (L, D_MODEL), jnp.float32),
        "router_group_w": nrm(ks[9], (L, D_MODEL, N_GROUPS), jnp.float32) * D_MODEL ** -0.5,
        "router_group_b": 0.01 * nrm(ks[10], (L, N_GROUPS), jnp.float32),
        "router_expert_w": nrm(ks[11], (L, N_GROUPS, D_MODEL, EXPERTS_PER_GROUP), jnp.float32) * D_MODEL ** -0.5,
        "router_expert_b": 0.01 * nrm(ks[12], (L, N_GROUPS, EXPERTS_PER_GROUP), jnp.float32),
        "w_gate_e": nrm(ks[13], (L, N_EXPERTS, D_MODEL, D_EXPERT), jnp.float32) * D_MODEL ** -0.5,
        "w_up_e": nrm(ks[14], (L, N_EXPERTS, D_MODEL, D_EXPERT), jnp.float32) * D_MODEL ** -0.5,
        "w_down_e": nrm(ks[15], (L, N_EXPERTS, D_EXPERT, D_MODEL), jnp.float32) * D_EXPERT ** -0.5 * DEEPNORM_BETA,
        "ln2_g": 1.0 + 0.02 * nrm(ks[16], (L, D_MODEL), jnp.float32),
        "ln2_b": 0.02 * nrm(ks[17], (L, D_MODEL), jnp.float32),
    }


def reference(x, w_in, gate_bias, w_attn_up, w_conv_up, conv_w, w_out, ln1_g, ln1_b,
              router_group_w, router_group_b, router_expert_w, router_expert_b,
              w_gate_e, w_up_e, w_down_e, ln2_g, ln2_b):
    S = x.shape[1]
    top_k = min(TOPK_MAX, S // 4)
    pos = jnp.arange(S, dtype=jnp.int32)
    h = x
    for l in range(DEPTH):
        mix = token_mixer(h, pos, top_k, w_in[l], gate_bias[l], w_attn_up[l], w_conv_up[l],
                          conv_w[l], w_out[l])
        h = layer_norm(DEEPNORM_ALPHA * h + mix, ln1_g[l], ln1_b[l])
        ffn = hierarchical_moe(h, router_group_w[l], router_group_b[l], router_expert_w[l],
                               router_expert_b[l], w_gate_e[l], w_up_e[l], w_down_e[l])
        h = layer_norm(DEEPNORM_ALPHA * h + ffn, ln2_g[l], ln2_b[l])
    return h
```

```python
import functools

import jax
import jax.numpy as jnp
import numpy as np
from jax import lax
from jax.experimental import pallas as pl
from jax.experimental.pallas import tpu as pltpu

D_MODEL = 1024
N_HEADS = 8
N_KV_HEADS = 2
HEAD_DIM = 64
Q_WIDTH = N_HEADS * HEAD_DIM
KV_WIDTH = N_KV_HEADS * HEAD_DIM
ROPE_THETA = 500000.0
ROT_DIMS = HEAD_DIM // 4
ROT_HALF = ROT_DIMS // 2
IDX_HEADS = 8
IDX_DIM = 64
IDXQ_WIDTH = IDX_HEADS * IDX_DIM
TOPK_MAX = 256
CONV_DIM = 512
CONV_WIDTH = 3
N_BRANCHES = 2
N_GROUPS = 4
EXPERTS_PER_GROUP = 8
N_EXPERTS = N_GROUPS * EXPERTS_PER_GROUP
D_EXPERT = 512
LN_EPS = 1e-5
DEPTH = 1
DEEPNORM_ALPHA = (2 * DEPTH) ** 0.25

V7X_LANES = 128
V7X_SUBLANES = 8
V7X_VMEM_LIMIT_BYTES = 56 * 1024 * 1024

PROJ_ROWS = 512
ATTN_CHUNK = 256
MIX_ROWS = 512
EXPERT_ROWS = 256
MOVE_ROWS = 512
HALF = D_MODEL // 2

T_Q0, T_QI0, T_V0, T_WI0 = 0, Q_WIDTH, Q_WIDTH + IDXQ_WIDTH, Q_WIDTH + IDXQ_WIDTH + KV_WIDTH
T_ROWS = T_WI0 + 16
KN_WIDTH = 256

INT_MIN = -2147483648
ROUTER_LANES = 128
E0 = N_GROUPS


def _cparams(n_axes, vmem_bytes):
    return pltpu.CompilerParams(
        dimension_semantics=("arbitrary",) * n_axes,
        vmem_limit_bytes=int(min(vmem_bytes, V7X_VMEM_LIMIT_BYTES)),
    )


def _proj_kernel(x_ref, wt_ref, wn_ref, cost_ref, sint_ref, cosn_ref, sgnn_ref,
                 qt_ref, qit_ref, vt_ref, wit_ref, kn_ref):
    xb = x_ref[0].astype(jnp.bfloat16)
    pt = lax.dot_general(wt_ref[...], xb, (((1,), (1,)), ((), ())),
                         preferred_element_type=jnp.float32)
    cos = cost_ref[...]
    sin = sint_ref[...]

    def rope_t(dst_ref, base):
        for h in range(N_HEADS):
            r0 = base + h * HEAD_DIM
            x1 = pt[r0:r0 + ROT_HALF]
            x2 = pt[r0 + ROT_HALF:r0 + ROT_DIMS]
            head = jnp.concatenate([x1 * cos - x2 * sin, x2 * cos + x1 * sin,
                                    pt[r0 + ROT_DIMS:r0 + HEAD_DIM]], axis=0)
            dst_ref[0, h * HEAD_DIM:(h + 1) * HEAD_DIM, :] = head.astype(dst_ref.dtype)

    rope_t(qt_ref, T_Q0)
    rope_t(qit_ref, T_QI0)
    for jj in range(PROJ_ROWS // ATTN_CHUNK):
        vt_ref[0, jj] = pt[T_V0:T_V0 + KV_WIDTH, jj * ATTN_CHUNK:(jj + 1) * ATTN_CHUNK].astype(vt_ref.dtype)
    wit_ref[0] = pt[T_WI0:T_WI0 + IDX_HEADS] * (IDX_DIM ** -0.5 * IDX_HEADS ** -0.5)
    pn = jnp.dot(xb, wn_ref[...], preferred_element_type=jnp.float32)
    kn = pn[:, :KN_WIDTH] * cosn_ref[...] + pn[:, KN_WIDTH:] * sgnn_ref[...]
    kn_ref[0] = kn.astype(kn_ref.dtype)


def _proj_call(x, wt, wn, cos_t, sin_t, cos_n, sgn_n):
    B, S, D = x.shape
    R = PROJ_ROWS
    nt = S // R
    grid = (B, nt)
    out_shape = (
        jax.ShapeDtypeStruct((B, Q_WIDTH, S), jnp.bfloat16),
        jax.ShapeDtypeStruct((B, IDXQ_WIDTH, S), jnp.bfloat16),
        jax.ShapeDtypeStruct((B, S // ATTN_CHUNK, KV_WIDTH, ATTN_CHUNK), jnp.bfloat16),
        jax.ShapeDtypeStruct((B, IDX_HEADS, S), jnp.float32),
        jax.ShapeDtypeStruct((B, S, KN_WIDTH), jnp.bfloat16),
    )
    in_specs = [
        pl.BlockSpec((1, R, D), lambda b, j: (b, j, 0)),
        pl.BlockSpec((T_ROWS, D), lambda b, j: (0, 0)),
        pl.BlockSpec((D, 2 * KN_WIDTH), lambda b, j: (0, 0)),
        pl.BlockSpec((ROT_HALF, R), lambda b, j: (0, j)),
        pl.BlockSpec((ROT_HALF, R), lambda b, j: (0, j)),
        pl.BlockSpec((R, KN_WIDTH), lambda b, j: (j, 0)),
        pl.BlockSpec((R, KN_WIDTH), lambda b, j: (j, 0)),
    ]
    out_specs = (
        pl.BlockSpec((1, Q_WIDTH, R), lambda b, j: (b, 0, j)),
        pl.BlockSpec((1, IDXQ_WIDTH, R), lambda b, j: (b, 0, j)),
        pl.BlockSpec((1, R // ATTN_CHUNK, KV_WIDTH, ATTN_CHUNK), lambda b, j: (b, j, 0, 0)),
        pl.BlockSpec((1, IDX_HEADS, R), lambda b, j: (b, 0, j)),
        pl.BlockSpec((1, R, KN_WIDTH), lambda b, j: (b, j, 0)),
    )
    vmem = 2 * (R * D * 4 + T_ROWS * D * 2 + D * 2 * KN_WIDTH * 2) + 6 * T_ROWS * R * 4 + (8 << 20)
    return pl.pallas_call(
        _proj_kernel, grid=grid, in_specs=in_specs, out_specs=out_specs, out_shape=out_shape,
        compiler_params=_cparams(2, vmem), name="dsa_proj",
    )(x, wt, wn, cos_t, sin_t, cos_n, sgn_n)


def _float_to_key(s):
    b = lax.bitcast_convert_type(s, jnp.int32)
    k = b ^ (lax.shift_right_arithmetic(b, 31) & jnp.int32(0x7FFFFFFF))
    return jnp.where(b == jnp.int32(INT_MIN), jnp.int32(0), k)


def _attn_kernel(qt_ref, qit_ref, wit_ref, kn_ref, vt_ref, o_ref, key_ref, acc_ref, m_ref, l_ref, *, seq, top_k):
    C = ATTN_CHUNK
    c = pl.program_id(1)
    nkb = c + 1
    zeros_half = jnp.zeros((HEAD_DIM, C), jnp.bfloat16)
    row_i = lax.broadcasted_iota(jnp.int32, (C, C), 0)
    lane_i = lax.broadcasted_iota(jnp.int32, (C, C), 1)

    def score_body(kb, carry):
        r0 = pl.multiple_of(kb * C, C)
        kix = kn_ref[0, pl.ds(r0, C), KV_WIDTH:KN_WIDTH]
        acc = None
        for h in range(IDX_HEADS):
            rhs = jnp.concatenate([qit_ref[0, h * IDX_DIM:(h + 1) * IDX_DIM, :], zeros_half], axis=0)
            s = jnp.dot(kix, rhs, preferred_element_type=jnp.float32)
            t = jnp.maximum(s, 0.0) * wit_ref[0, h:h + 1, :]
            acc = t if acc is None else acc + t
        key_ref[kb] = _float_to_key(acc)
        return carry

    lax.fori_loop(0, nkb, score_body, 0)
    key_ref[c] = jnp.where(row_i <= lane_i, key_ref[c], jnp.int32(INT_MIN))

    def count(pred):
        def body(kb, part):
            hit = pred(key_ref[kb], kb).astype(jnp.int32)
            return part + jnp.sum(hit.reshape(C // V7X_SUBLANES, V7X_SUBLANES, C), axis=0)

        part = lax.fori_loop(0, nkb, body, jnp.zeros((V7X_SUBLANES, C), jnp.int32))
        return jnp.sum(part, axis=0, keepdims=True)

    def search_body(i, ans_u):
        cand_u = ans_u | lax.shift_left(jnp.int32(1), jnp.int32(31) - i)
        cand = cand_u ^ jnp.int32(INT_MIN)
        return jnp.where(count(lambda k, kb: k >= cand) >= top_k, cand_u, ans_u)

    ans_u = lax.fori_loop(0, 32, search_body, jnp.zeros((1, C), jnp.int32))
    thr = jnp.maximum(ans_u ^ jnp.int32(INT_MIN), jnp.int32(INT_MIN + 1))

    has_ties = jnp.max(count(lambda k, kb: k >= thr)) > top_k

    def tie_limit():
        need = top_k - count(lambda k, kb: k > thr)
        n_bits = int(seq - 1).bit_length()

        def body(i, ans):
            cand = ans | lax.shift_left(jnp.int32(1), jnp.int32(n_bits - 1) - i)
            below = count(lambda k, kb: (k == thr) & (kb * C + row_i < cand))
            return jnp.where(below < need, cand, ans)

        return lax.fori_loop(0, n_bits, body, jnp.zeros((1, C), jnp.int32))

    last_tie = lax.cond(has_ties, tie_limit, lambda: jnp.full((1, C), seq, jnp.int32))

    neg_inf = jnp.float32(-jnp.inf)
    m_ref[...] = jnp.full(m_ref.shape, neg_inf, jnp.float32)
    l_ref[...] = jnp.zeros(l_ref.shape, jnp.float32)
    acc_ref[...] = jnp.zeros(acc_ref.shape, jnp.float32)

    def attn_body(kb, carry):
        r0 = pl.multiple_of(kb * C, C)
        kk = key_ref[kb]
        sel = (kk > thr) | ((kk == thr) & (kb * C + row_i <= last_tie))
        bias = jnp.where(sel, 0.0, neg_inf)
        k2 = kn_ref[0, pl.ds(r0, C), 0:KV_WIDTH]
        for h in range(N_HEADS):
            g = h // (N_HEADS // N_KV_HEADS)
            qh = qt_ref[0, h * HEAD_DIM:(h + 1) * HEAD_DIM, :]
            rhs = jnp.concatenate([qh, zeros_half] if g == 0 else [zeros_half, qh], axis=0)
            logit = jnp.dot(k2, rhs, preferred_element_type=jnp.float32) + bias
            m_old = m_ref[h:h + 1, :]
            m_new = jnp.maximum(m_old, jnp.max(logit, axis=0, keepdims=True))
            m_safe = jnp.where(m_new == neg_inf, 0.0, m_new)
            p = jnp.exp(logit - m_safe)
            alpha = jnp.exp(m_old - m_safe)
            l_ref[h:h + 1, :] = alpha * l_ref[h:h + 1, :] + jnp.sum(p, axis=0, keepdims=True)
            vt = vt_ref[0, kb, g * HEAD_DIM:(g + 1) * HEAD_DIM, :]
            pv = jnp.dot(vt, p.astype(jnp.bfloat16), preferred_element_type=jnp.float32)
            hs = slice(h * HEAD_DIM, (h + 1) * HEAD_DIM)
            acc_ref[hs, :] = alpha * acc_ref[hs, :] + pv
            m_ref[h:h + 1, :] = m_new
        return carry

    lax.fori_loop(0, nkb, attn_body, 0)
    for h in range(N_HEADS):
        hs = slice(h * HEAD_DIM, (h + 1) * HEAD_DIM)
        acc_ref[hs, :] = acc_ref[hs, :] / l_ref[h:h + 1, :]
    o_ref[0] = jnp.transpose(acc_ref[...]).astype(o_ref.dtype)


def _attn_call(qt, qit, wit, kn, vt, top_k):
    B, _, S = qt.shape
    C = ATTN_CHUNK
    nc = S // C
    kern = functools.partial(_attn_kernel, seq=S, top_k=top_k)
    in_specs = [
        pl.BlockSpec((1, Q_WIDTH, C), lambda b, c: (b, 0, c)),
        pl.BlockSpec((1, IDXQ_WIDTH, C), lambda b, c: (b, 0, c)),
        pl.BlockSpec((1, IDX_HEADS, C), lambda b, c: (b, 0, c)),
        pl.BlockSpec((1, S, KN_WIDTH), lambda b, c: (b, 0, 0)),
        pl.BlockSpec((1, nc, KV_WIDTH, C), lambda b, c: (b, 0, 0, 0)),
    ]
    out_specs = pl.BlockSpec((1, C, Q_WIDTH), lambda b, c: (b, c, 0))
    scratch = [
        pltpu.VMEM((nc, C, C), jnp.int32),
        pltpu.VMEM((Q_WIDTH, C), jnp.float32),
        pltpu.VMEM((N_HEADS, C), jnp.float32),
        pltpu.VMEM((N_HEADS, C), jnp.float32),
    ]
    vmem = (S * C * 4 + Q_WIDTH * C * 4 + 2 * 2 * (2 * Q_WIDTH * C + S * KN_WIDTH + S * KV_WIDTH + C * Q_WIDTH)
            + 24 * C * C * 4 + (8 << 20))
    return pl.pallas_call(
        kern, grid=(B, nc), in_specs=in_specs, out_specs=out_specs,
        out_shape=jax.ShapeDtypeStruct((B, S, Q_WIDTH), jnp.bfloat16),
        scratch_shapes=scratch, compiler_params=_cparams(2, vmem), name="dsa_attn",
    )(qt, qit, wit, kn, vt)


def _layer_norm(v, g, b):
    mu = jnp.mean(v, axis=-1, keepdims=True)
    d = v - mu
    var = jnp.mean(d * d, axis=-1, keepdims=True)
    return d * lax.rsqrt(var + LN_EPS) * g + b


def _pack_rows(h):
    hi = lax.bitcast_convert_type(h[:, :HALF].astype(jnp.bfloat16).astype(jnp.float32), jnp.int32)
    lo = lax.bitcast_convert_type(h[:, HALF:].astype(jnp.bfloat16).astype(jnp.float32), jnp.int32)
    return (hi & jnp.int32(-65536)) | lax.shift_right_logical(lo, 16)


def _unpack_rows(w):
    hi = lax.bitcast_convert_type(w & jnp.int32(-65536), jnp.float32)
    lo = lax.bitcast_convert_type(lax.shift_left(w, 16), jnp.float32)
    return jnp.concatenate([hi, lo], axis=1)


def _mix_kernel(x_ref, attn_ref, wc_ref, wg_ref, gb_ref, cw_ref, wau_ref, wcu_ref, wo_ref,
                g1_ref, b1_ref, wrh_ref, wrl_ref, rb_ref,
                h_ref, hp_ref, eid_ref, rank_ref, gate_ref, cnt_ref,
                ubuf_ref, base_ref, *, steps_per_seq):
    R = MIX_ROWS
    i = pl.program_id(0)

    @pl.when(i == 0)
    def _():
        base_ref[...] = jnp.zeros_like(base_ref)

    @pl.when(i % steps_per_seq == 0)
    def _():
        ubuf_ref[0:V7X_SUBLANES, :] = jnp.zeros((V7X_SUBLANES, CONV_DIM), jnp.float32)

    x = x_ref[...]
    xb = x.astype(jnp.bfloat16)
    cv = jnp.dot(xb, wc_ref[...], preferred_element_type=jnp.float32)
    u = cv[:, 2 * CONV_DIM:] * cv[:, :CONV_DIM]
    ubuf_ref[V7X_SUBLANES:, :] = u
    u1 = ubuf_ref[V7X_SUBLANES - 1:V7X_SUBLANES - 1 + R, :]
    u2 = ubuf_ref[V7X_SUBLANES - 2:V7X_SUBLANES - 2 + R, :]
    y = cw_ref[0:1, :] * u2 + cw_ref[1:2, :] * u1 + cw_ref[2:3, :] * u
    conv = (cv[:, CONV_DIM:2 * CONV_DIM] * y).astype(jnp.bfloat16)
    ubuf_ref[0:V7X_SUBLANES, :] = u[R - V7X_SUBLANES:, :]
    z = jnp.dot(xb, wg_ref[...], preferred_element_type=jnp.float32) + gb_ref[...]
    gates = 1.0 / (1.0 + jnp.exp(-z))
    au = jnp.dot(attn_ref[...], wau_ref[...], preferred_element_type=jnp.float32)
    cu = jnp.dot(conv, wcu_ref[...], preferred_element_type=jnp.float32)
    merged = gates[:, :D_MODEL] * au + gates[:, D_MODEL:] * cu
    mix = jnp.dot(merged.astype(jnp.bfloat16), wo_ref[...], preferred_element_type=jnp.float32)
    h = _layer_norm(DEEPNORM_ALPHA * x + mix, g1_ref[...], b1_ref[...])
    h_ref[...] = h
    hp_ref[...] = _pack_rows(h)

    h_hi = h.astype(jnp.bfloat16)
    h_lo = (h - h_hi.astype(jnp.float32)).astype(jnp.bfloat16)
    lg = (jnp.dot(h_hi, wrh_ref[...], preferred_element_type=jnp.float32)
          + jnp.dot(h_lo, wrh_ref[...], preferred_element_type=jnp.float32)
          + jnp.dot(h_hi, wrl_ref[...], preferred_element_type=jnp.float32)) + rb_ref[...]
    lane = lax.broadcasted_iota(jnp.int32, (R, ROUTER_LANES), 1).astype(jnp.float32)
    neg = jnp.float32(-jnp.inf)
    no_lane = jnp.float32(ROUTER_LANES)
    gmask = lane < N_GROUPS
    gl = jnp.where(gmask, lg, neg)
    gmax = jnp.max(gl, axis=1, keepdims=True)
    grp = jnp.min(jnp.where(gl == gmax, lane, no_lane), axis=1, keepdims=True)
    gsum = jnp.sum(jnp.where(gmask, jnp.exp(gl - gmax), 0.0), axis=1, keepdims=True)
    p_grp = 1.0 / gsum
    lo_lane = E0 + grp * EXPERTS_PER_GROUP
    emask = (lane >= lo_lane) & (lane < lo_lane + EXPERTS_PER_GROUP)
    el = jnp.where(emask, lg, neg)
    v1 = jnp.max(el, axis=1, keepdims=True)
    i1 = jnp.min(jnp.where(el == v1, lane, no_lane), axis=1, keepdims=True)
    el2 = jnp.where(lane == i1, neg, el)
    v2 = jnp.max(el2, axis=1, keepdims=True)
    i2 = jnp.min(jnp.where(el2 == v2, lane, no_lane), axis=1, keepdims=True)
    a = jnp.exp(v2 - v1)
    inv = 1.0 / (1.0 + a)
    gate_ref[:, 0:1] = p_grp * inv
    gate_ref[:, 1:2] = p_grp * (a * inv)
    eid_ref[:, 0:1] = (i1 - E0).astype(jnp.int32)
    eid_ref[:, 1:2] = (i2 - E0).astype(jnp.int32)
    oh1 = lane == i1
    oh2 = lane == i2
    oh = (oh1 | oh2).astype(jnp.bfloat16)
    r_i = lax.broadcasted_iota(jnp.int32, (R, R), 0)
    c_i = lax.broadcasted_iota(jnp.int32, (R, R), 1)
    tri = (r_i > c_i).astype(jnp.bfloat16)
    before = jnp.dot(tri, oh, preferred_element_type=jnp.float32) + base_ref[...]
    rank_ref[:, 0:1] = jnp.sum(jnp.where(oh1, before, 0.0), axis=1, keepdims=True).astype(jnp.int32)
    rank_ref[:, 1:2] = jnp.sum(jnp.where(oh2, before, 0.0), axis=1, keepdims=True).astype(jnp.int32)
    base_ref[...] = base_ref[...] + jnp.sum(oh.astype(jnp.float32), axis=0, keepdims=True)
    cnt_ref[...] = base_ref[...]


def _mix_call(x2, attn2, wc, wg, gb, cw, wau, wcu, wo, g1, b1, wrh, wrl, rb, seq):
    N, D = x2.shape
    R = MIX_ROWS
    nt = N // R
    kern = functools.partial(_mix_kernel, steps_per_seq=seq // R)

    def full(a):
        return pl.BlockSpec(a.shape, lambda i: (0,) * a.ndim)

    in_specs = [
        pl.BlockSpec((R, D), lambda i: (i, 0)),
        pl.BlockSpec((R, Q_WIDTH), lambda i: (i, 0)),
        full(wc), full(wg), full(gb), full(cw), full(wau), full(wcu), full(wo),
        full(g1), full(b1), full(wrh), full(wrl), full(rb),
    ]
    out_shape = (
        jax.ShapeDtypeStruct((N, D), jnp.float32),
        jax.ShapeDtypeStruct((N, HALF), jnp.int32),
        jax.ShapeDtypeStruct((N, 2), jnp.int32),
        jax.ShapeDtypeStruct((N, 2), jnp.int32),
        jax.ShapeDtypeStruct((N, 2), jnp.float32),
        jax.ShapeDtypeStruct((1, ROUTER_LANES), jnp.float32),
    )
    out_specs = (
        pl.BlockSpec((R, D), lambda i: (i, 0)),
        pl.BlockSpec((R, HALF), lambda i: (i, 0)),
        pl.BlockSpec((R, 2), lambda i: (i, 0)),
        pl.BlockSpec((R, 2), lambda i: (i, 0)),
        pl.BlockSpec((R, 2), lambda i: (i, 0)),
        pl.BlockSpec((1, ROUTER_LANES), lambda i: (0, 0)),
    )
    scratch = [pltpu.VMEM((R + V7X_SUBLANES, CONV_DIM), jnp.float32),
               pltpu.VMEM((1, ROUTER_LANES), jnp.float32)]
    w_bytes = 2 * (wc.size + wg.size + wau.size + wcu.size + wo.size + wrh.size + wrl.size)
    vmem = 2 * w_bytes + 2 * (R * D * 4 * 2 + R * Q_WIDTH * 2 + R * HALF * 4) + 10 * R * 2048 * 4 + (6 << 20)
    return pl.pallas_call(
        kern, grid=(nt,), in_specs=in_specs, out_specs=out_specs, out_shape=out_shape,
        scratch_shapes=scratch, compiler_params=_cparams(1, vmem), name="mix_ln_router",
    )(x2, attn2, wc, wg, gb, cw, wau, wcu, wo, g1, b1, wrh, wrl, rb)


def _row_copy(src_ref, s, dst_ref, d, sem):
    return pltpu.make_async_copy(src_ref.at[pl.ds(s, 1), :], dst_ref.at[pl.ds(d, 1), :], sem)


def _dispatch_kernel(dest_ref, hp_ref, xs_in_ref, xs_ref, sem):
    del xs_in_ref
    R = MOVE_ROWS

    def issue(r, carry):
        _row_copy(hp_ref, r, xs_ref, dest_ref[0, 0, 2 * r], sem).start()
        _row_copy(hp_ref, r, xs_ref, dest_ref[0, 0, 2 * r + 1], sem).start()
        return carry

    lax.fori_loop(0, R, issue, 0)

    def drain(r, carry):
        _row_copy(hp_ref, 0, xs_ref, 0, sem).wait()
        _row_copy(hp_ref, 0, xs_ref, 0, sem).wait()
        return carry

    lax.fori_loop(0, R, drain, 0)


def _dispatch_call(dest3, hp, xs0):
    N, W = hp.shape
    R = MOVE_ROWS
    nt = N // R
    return pl.pallas_call(
        _dispatch_kernel, grid=(nt,),
        in_specs=[
            pl.BlockSpec((1, 1, 2 * R), lambda i: (i, 0, 0), memory_space=pltpu.SMEM),
            pl.BlockSpec((R, W), lambda i: (i, 0)),
            pl.BlockSpec(memory_space=pl.ANY),
        ],
        out_specs=pl.BlockSpec(memory_space=pl.ANY),
        out_shape=jax.ShapeDtypeStruct(xs0.shape, xs0.dtype),
        scratch_shapes=[pltpu.SemaphoreType.DMA],
        input_output_aliases={2: 0},
        compiler_params=_cparams(1, 4 * R * W * 4 + (4 << 20)), name="moe_dispatch",
    )(dest3, hp, xs0)


def _expert_kernel(be_ref, nb_ref, xs_ref, wg_ref, wu_ref, wd_ref, ys_ref):
    j = pl.program_id(0)

    @pl.when(j < nb_ref[0])
    def _():
        xb = _unpack_rows(xs_ref[...]).astype(jnp.bfloat16)
        hg = jnp.dot(xb, wg_ref[0], preferred_element_type=jnp.float32)
        hu = jnp.dot(xb, wu_ref[0], preferred_element_type=jnp.float32)
        hid = (hg / (1.0 + jnp.exp(-hg))) * hu
        y = jnp.dot(hid.astype(jnp.bfloat16), wd_ref[0], preferred_element_type=jnp.float32)
        ys_ref[...] = _pack_rows(y)

    @pl.when(j >= nb_ref[0])
    def _():
        ys_ref[...] = jnp.zeros_like(ys_ref)


def _expert_call(block_expert, n_used, xs, wg, wu, wd):
    cap, W = xs.shape
    RB = EXPERT_ROWS
    nb = cap // RB
    grid_spec = pltpu.PrefetchScalarGridSpec(
        num_scalar_prefetch=2, grid=(nb,),
        in_specs=[
            pl.BlockSpec((RB, W), lambda j, be, nu: (j, 0)),
            pl.BlockSpec((1, D_MODEL, D_EXPERT), lambda j, be, nu: (be[j], 0, 0)),
            pl.BlockSpec((1, D_MODEL, D_EXPERT), lambda j, be, nu: (be[j], 0, 0)),
            pl.BlockSpec((1, D_EXPERT, D_MODEL), lambda j, be, nu: (be[j], 0, 0)),
        ],
        out_specs=pl.BlockSpec((RB, W), lambda j, be, nu: (j, 0)),
    )
    vmem = 2 * 3 * D_MODEL * D_EXPERT * 2 + 4 * RB * W * 4 + 8 * RB * D_MODEL * 4 + (6 << 20)
    return pl.pallas_call(
        _expert_kernel, grid_spec=grid_spec,
        out_shape=jax.ShapeDtypeStruct((cap, W), jnp.int32),
        compiler_params=_cparams(1, vmem), name="moe_experts",
    )(block_expert, n_used, xs, wg, wu, wd)


def _combine_kernel(dest_ref, h_ref, gate_ref, g2_ref, b2_ref, ys_ref, o_ref, buf_ref, sem):
    R = MOVE_ROWS

    def issue(r, carry):
        _row_copy(ys_ref, dest_ref[0, 0, 2 * r], buf_ref.at[0], r, sem).start()
        _row_copy(ys_ref, dest_ref[0, 0, 2 * r + 1], buf_ref.at[1], r, sem).start()
        return carry

    lax.fori_loop(0, R, issue, 0)

    def drain(r, carry):
        _row_copy(ys_ref, 0, buf_ref.at[0], 0, sem).wait()
        _row_copy(ys_ref, 0, buf_ref.at[1], 0, sem).wait()
        return carry

    lax.fori_loop(0, R, drain, 0)
    y0 = _unpack_rows(buf_ref[0])
    y1 = _unpack_rows(buf_ref[1])
    ffn = gate_ref[:, 0:1] * y0 + gate_ref[:, 1:2] * y1
    o_ref[...] = _layer_norm(DEEPNORM_ALPHA * h_ref[...] + ffn, g2_ref[...], b2_ref[...])


def _combine_call(dest3, h, gate, g2, b2, ys):
    N, D = h.shape
    W = ys.shape[1]
    R = MOVE_ROWS
    nt = N // R
    return pl.pallas_call(
        _combine_kernel, grid=(nt,),
        in_specs=[
            pl.BlockSpec((1, 1, 2 * R), lambda i: (i, 0, 0), memory_space=pltpu.SMEM),
            pl.BlockSpec((R, D), lambda i: (i, 0)),
            pl.BlockSpec((R, 2), lambda i: (i, 0)),
            pl.BlockSpec((1, D), lambda i: (0, 0)),
            pl.BlockSpec((1, D), lambda i: (0, 0)),
            pl.BlockSpec(memory_space=pl.ANY),
        ],
        out_specs=pl.BlockSpec((R, D), lambda i: (i, 0)),
        out_shape=jax.ShapeDtypeStruct((N, D), jnp.float32),
        scratch_shapes=[pltpu.VMEM((2, R, W), jnp.int32), pltpu.SemaphoreType.DMA],
        compiler_params=_cparams(1, 4 * R * D * 4 + 2 * R * W * 4 + 8 * R * D * 4 + (4 << 20)),
        name="moe_combine",
    )(dest3, h, gate, g2, b2, ys)


def _rope_tables(seq):
    inv_freq = ROPE_THETA ** (-jnp.arange(ROT_HALF, dtype=jnp.float32) / ROT_HALF)
    ang = jnp.arange(seq, dtype=jnp.int32).astype(jnp.float32)[:, None] * inv_freq[None, :]
    cos, sin = jnp.cos(ang), jnp.sin(ang)
    ones = jnp.ones((seq, HEAD_DIM - ROT_DIMS), jnp.float32)
    zeros = jnp.zeros((seq, HEAD_DIM - ROT_DIMS), jnp.float32)
    c_head = jnp.concatenate([cos, cos, ones], axis=1)
    s_head = jnp.concatenate([-sin, sin, zeros], axis=1)
    reps = KN_WIDTH // HEAD_DIM
    return cos.T, sin.T, jnp.tile(c_head, (1, reps)), jnp.tile(s_head, (1, reps))


def _swap_rot_cols(w):
    d, n = w.shape
    wh = w.reshape(d, n // HEAD_DIM, HEAD_DIM)
    sw = jnp.concatenate([wh[:, :, ROT_HALF:ROT_DIMS], wh[:, :, :ROT_HALF],
                          jnp.zeros((d, n // HEAD_DIM, HEAD_DIM - ROT_DIMS), w.dtype)], axis=2)
    return sw.reshape(d, n)


def _block(x, w_in, gate_bias, w_attn_up, w_conv_up, conv_w, w_out, ln_g, ln_b,
           rg_w, rg_b, re_w, re_b, w_gate_e, w_up_e, w_down_e, ln2_g, ln2_b):
    B, S, D = x.shape
    N = B * S
    top_k = min(TOPK_MAX, S // 4)
    bf = jnp.bfloat16
    o = np.cumsum([0, Q_WIDTH, KV_WIDTH, KV_WIDTH, IDXQ_WIDTH, IDX_DIM, IDX_HEADS,
                   CONV_DIM, CONV_DIM, CONV_DIM, N_BRANCHES * D_MODEL])
    w_q, w_k, w_v, w_qi, w_ki, w_wi = (w_in[:, o[i]:o[i + 1]] for i in range(6))
    w_conv = w_in[:, o[6]:o[9]]
    w_gates = w_in[:, o[9]:o[10]]

    wt = jnp.concatenate([w_q.T * (HEAD_DIM ** -0.5), w_qi.T, w_v.T, w_wi.T,
                          jnp.zeros((T_ROWS - T_WI0 - IDX_HEADS, D), w_in.dtype)], axis=0).astype(bf)
    pad = jnp.zeros((D, KN_WIDTH - KV_WIDTH - IDX_DIM), w_in.dtype)
    wn = jnp.concatenate([w_k, w_ki, pad, _swap_rot_cols(w_k), _swap_rot_cols(w_ki), pad], axis=1).astype(bf)
    cos_t, sin_t, cos_n, sgn_n = _rope_tables(S)

    qt, qit, vt, wit, kn = _proj_call(x, wt, wn, cos_t, sin_t, cos_n, sgn_n)
    attn = _attn_call(qt, qit, wit, kn, vt, top_k)

    w_r = jnp.concatenate([rg_w, jnp.transpose(re_w, (1, 0, 2)).reshape(D, N_EXPERTS),
                           jnp.zeros((D, ROUTER_LANES - E0 - N_EXPERTS), rg_w.dtype)], axis=1)
    b_r = jnp.concatenate([rg_b, re_b.reshape(-1),
                           jnp.zeros((ROUTER_LANES - E0 - N_EXPERTS,), rg_b.dtype)])[None, :]
    w_rh = w_r.astype(bf)
    w_rl = (w_r - w_rh.astype(jnp.float32)).astype(bf)

    h, hp, eid, rank, gate, cnt = _mix_call(
        x.reshape(N, D), attn.reshape(N, Q_WIDTH), w_conv.astype(bf), w_gates.astype(bf),
        gate_bias[None, :], conv_w, w_attn_up.astype(bf), w_conv_up.astype(bf), w_out.astype(bf),
        ln_g[None, :], ln_b[None, :], w_rh, w_rl, b_r, S)

    RB = EXPERT_ROWS
    counts = cnt[0, E0:E0 + N_EXPERTS].astype(jnp.int32)
    padded = ((counts + RB - 1) // RB) * RB
    pad_end = jnp.cumsum(padded)
    pad_start = pad_end - padded
    dest = jnp.sum(jnp.where(eid[:, :, None] == jnp.arange(N_EXPERTS, dtype=jnp.int32)[None, None, :],
                             pad_start[None, None, :], 0), axis=-1) + rank
    cap = N * 2 + N_EXPERTS * RB
    nb = cap // RB
    block_start = jnp.arange(nb, dtype=jnp.int32) * RB
    block_expert = jnp.minimum(jnp.searchsorted(pad_end, block_start, side="right"),
                               N_EXPERTS - 1).astype(jnp.int32)
    n_used = (pad_end[-1:] // RB).astype(jnp.int32)
    dest3 = dest.reshape(N // MOVE_ROWS, 1, 2 * MOVE_ROWS)

    xs = _dispatch_call(dest3, hp, jnp.zeros((cap, HALF), jnp.int32))
    ys = _expert_call(block_expert, n_used, xs, w_gate_e.astype(bf), w_up_e.astype(bf), w_down_e.astype(bf))
    out = _combine_call(dest3, h, gate, ln2_g[None, :], ln2_b[None, :], ys)
    return out.reshape(B, S, D)


def kernel(x, w_in, gate_bias, w_attn_up, w_conv_up, conv_w, w_out, ln1_g, ln1_b, router_group_w,
           router_group_b, router_expert_w, router_expert_b, w_gate_e, w_up_e, w_down_e, ln2_g, ln2_b):
    h = x
    for l in range(DEPTH):
        h = _block(h, w_in[l], gate_bias[l], w_attn_up[l], w_conv_up[l], conv_w[l], w_out[l],
                   ln1_g[l], ln1_b[l], router_group_w[l], router_group_b[l], router_expert_w[l],
                   router_expert_b[l], w_gate_e[l], w_up_e[l], w_down_e[l], ln2_g[l], ln2_b[l])
    return h
```

```python
import functools

import jax
import jax.numpy as jnp
import numpy as np
from jax import lax
from jax.experimental import pallas as pl
from jax.experimental.pallas import tpu as pltpu

D_MODEL = 1024
N_HEADS = 8
N_KV_HEADS = 2
HEAD_DIM = 64
Q_WIDTH = N_HEADS * HEAD_DIM
KV_WIDTH = N_KV_HEADS * HEAD_DIM
ROPE_THETA = 500000.0
ROT_DIMS = HEAD_DIM // 4
ROT_HALF = ROT_DIMS // 2
IDX_HEADS = 8
IDX_DIM = 64
IDXQ_WIDTH = IDX_HEADS * IDX_DIM
TOPK_MAX = 256
CONV_DIM = 512
CONV_WIDTH = 3
N_BRANCHES = 2
N_GROUPS = 4
EXPERTS_PER_GROUP = 8
N_EXPERTS = N_GROUPS * EXPERTS_PER_GROUP
D_EXPERT = 512
LN_EPS = 1e-5
DEPTH = 1
DEEPNORM_ALPHA = (2 * DEPTH) ** 0.25

V7X_LANES = 128
V7X_SUBLANES = 8
V7X_VMEM_LIMIT_BYTES = 56 * 1024 * 1024

PROJ_ROWS = 512
ATTN_CHUNK = 256
ATTN_KEY_TILE = 128
MIX_ROWS = 512
EXPERT_ROWS = 256
MOVE_ROWS = 512
ISSUE_UNROLL = 8
HALF = D_MODEL // 2

T_Q0, T_QI0, T_V0, T_WI0 = 0, Q_WIDTH, Q_WIDTH + IDXQ_WIDTH, Q_WIDTH + IDXQ_WIDTH + KV_WIDTH
T_ROWS = T_WI0 + 16
KN_WIDTH = 256

INT_MIN = -2147483648
HALF_RANGE = 32768
ROUTER_LANES = 128
E0 = N_GROUPS


def _cparams(n_axes, vmem_bytes):
    return pltpu.CompilerParams(
        dimension_semantics=("arbitrary",) * n_axes,
        vmem_limit_bytes=int(min(vmem_bytes, V7X_VMEM_LIMIT_BYTES)),
    )


def _proj_kernel(x_ref, wt_ref, wn_ref, cost_ref, sint_ref, cosn_ref, sgnn_ref,
                 qt_ref, qit_ref, vt_ref, wit_ref, kn_ref):
    xb = x_ref[0].astype(jnp.bfloat16)
    pt = lax.dot_general(wt_ref[...], xb, (((1,), (1,)), ((), ())),
                         preferred_element_type=jnp.float32)
    cos = cost_ref[...]
    sin = sint_ref[...]

    def rope_t(dst_ref, base):
        for h in range(N_HEADS):
            r0 = base + h * HEAD_DIM
            x1 = pt[r0:r0 + ROT_HALF]
            x2 = pt[r0 + ROT_HALF:r0 + ROT_DIMS]
            head = jnp.concatenate([x1 * cos - x2 * sin, x2 * cos + x1 * sin,
                                    pt[r0 + ROT_DIMS:r0 + HEAD_DIM]], axis=0)
            dst_ref[0, h * HEAD_DIM:(h + 1) * HEAD_DIM, :] = head.astype(dst_ref.dtype)

    rope_t(qt_ref, T_Q0)
    rope_t(qit_ref, T_QI0)
    for jj in range(PROJ_ROWS // ATTN_CHUNK):
        vt_ref[0, jj] = pt[T_V0:T_V0 + KV_WIDTH, jj * ATTN_CHUNK:(jj + 1) * ATTN_CHUNK].astype(vt_ref.dtype)
    wit_ref[0] = pt[T_WI0:T_WI0 + IDX_HEADS]
    pn = jnp.dot(xb, wn_ref[...], preferred_element_type=jnp.float32)
    kn = pn[:, :KN_WIDTH] * cosn_ref[...] + pn[:, KN_WIDTH:] * sgnn_ref[...]
    kn_ref[0] = kn.astype(kn_ref.dtype)


def _proj_call(x, wt, wn, cos_t, sin_t, cos_n, sgn_n):
    B, S, D = x.shape
    R = PROJ_ROWS
    nt = S // R
    grid = (B, nt)
    out_shape = (
        jax.ShapeDtypeStruct((B, Q_WIDTH, S), jnp.bfloat16),
        jax.ShapeDtypeStruct((B, IDXQ_WIDTH, S), jnp.bfloat16),
        jax.ShapeDtypeStruct((B, S // ATTN_CHUNK, KV_WIDTH, ATTN_CHUNK), jnp.bfloat16),
        jax.ShapeDtypeStruct((B, IDX_HEADS, S), jnp.float32),
        jax.ShapeDtypeStruct((B, S, KN_WIDTH), jnp.bfloat16),
    )
    in_specs = [
        pl.BlockSpec((1, R, D), lambda b, j: (b, j, 0)),
        pl.BlockSpec((T_ROWS, D), lambda b, j: (0, 0)),
        pl.BlockSpec((D, 2 * KN_WIDTH), lambda b, j: (0, 0)),
        pl.BlockSpec((ROT_HALF, R), lambda b, j: (0, j)),
        pl.BlockSpec((ROT_HALF, R), lambda b, j: (0, j)),
        pl.BlockSpec((R, KN_WIDTH), lambda b, j: (j, 0)),
        pl.BlockSpec((R, KN_WIDTH), lambda b, j: (j, 0)),
    ]
    out_specs = (
        pl.BlockSpec((1, Q_WIDTH, R), lambda b, j: (b, 0, j)),
        pl.BlockSpec((1, IDXQ_WIDTH, R), lambda b, j: (b, 0, j)),
        pl.BlockSpec((1, R // ATTN_CHUNK, KV_WIDTH, ATTN_CHUNK), lambda b, j: (b, j, 0, 0)),
        pl.BlockSpec((1, IDX_HEADS, R), lambda b, j: (b, 0, j)),
        pl.BlockSpec((1, R, KN_WIDTH), lambda b, j: (b, j, 0)),
    )
    vmem = 2 * (R * D * 4 + T_ROWS * D * 2 + D * 2 * KN_WIDTH * 2) + 6 * T_ROWS * R * 4 + (8 << 20)
    return pl.pallas_call(
        _proj_kernel, grid=grid, in_specs=in_specs, out_specs=out_specs, out_shape=out_shape,
        compiler_params=_cparams(2, vmem), name="dsa_proj",
    )(x, wt, wn, cos_t, sin_t, cos_n, sgn_n)


def _float_to_key(s):
    b = lax.bitcast_convert_type(s, jnp.int32)
    k = b ^ (lax.shift_right_arithmetic(b, 31) & jnp.int32(0x7FFFFFFF))
    return jnp.where(b == jnp.int32(INT_MIN), jnp.int32(0), k)


def _attn_kernel(qt_ref, qit_ref, wit_ref, kn_ref, vt_ref, o_ref,
                 key_ref, hi_ref, lo_ref, lga_ref, lgb_ref, acc_ref, m_ref, l_ref, *, seq, top_k):
    C = ATTN_CHUNK
    c = pl.program_id(1)
    nkb = c + 1
    zeros_half = jnp.zeros((HEAD_DIM, C), jnp.bfloat16)
    row_i = lax.broadcasted_iota(jnp.int32, (C, C), 0)
    lane_i = lax.broadcasted_iota(jnp.int32, (C, C), 1)
    causal_in_block = row_i <= lane_i

    def score_body(kb, carry):
        r0 = pl.multiple_of(kb * C, C)
        kix = kn_ref[0, pl.ds(r0, C), KV_WIDTH:KN_WIDTH]
        acc = None
        for h in range(IDX_HEADS):
            rhs = jnp.concatenate([qit_ref[0, h * IDX_DIM:(h + 1) * IDX_DIM, :], zeros_half], axis=0)
            s = jnp.dot(kix, rhs, preferred_element_type=jnp.float32)
            t = jnp.maximum(s, 0.0) * wit_ref[0, h:h + 1, :]
            acc = t if acc is None else acc + t
        score = acc * (IDX_DIM ** -0.5 * IDX_HEADS ** -0.5)
        keys = jnp.where(causal_in_block | (kb != c), _float_to_key(score), jnp.int32(INT_MIN))
        key_ref[kb] = keys
        hi_ref[kb] = lax.shift_right_arithmetic(keys, 16).astype(jnp.int16)
        lo_ref[kb] = ((keys & jnp.int32(0xFFFF)) - jnp.int32(HALF_RANGE)).astype(jnp.int16)
        return carry

    lax.fori_loop(0, nkb, score_body, 0)

    def count(ref, pred):
        packed = ref.dtype == jnp.int16

        def body(kb, part):
            hit = pred(ref[kb], kb)
            if packed:
                words = pltpu.bitcast(jnp.where(hit, jnp.int16(1), jnp.int16(0)), jnp.int32)
            else:
                words = hit.astype(jnp.int32)
            return part + jnp.sum(words.reshape(-1, V7X_SUBLANES, C), axis=0)

        part = lax.fori_loop(0, nkb, body, jnp.zeros((V7X_SUBLANES, C), jnp.int32))
        if packed:
            part = (part & jnp.int32(0xFFFF)) + lax.shift_right_logical(part, 16)
        return jnp.sum(part, axis=0, keepdims=True)

    def kth_largest_16(ref):
        def body(i, ans_u):
            cand_u = ans_u | lax.shift_left(jnp.int32(1), jnp.int32(15) - i)
            cand = (cand_u - jnp.int32(HALF_RANGE)).astype(jnp.int16)
            return jnp.where(count(ref, lambda k, kb: k >= cand) >= top_k, cand_u, ans_u)

        return lax.fori_loop(0, 16, body, jnp.zeros((1, C), jnp.int32)) - jnp.int32(HALF_RANGE)

    t_hi = kth_largest_16(hi_ref)
    t_hi16 = t_hi.astype(jnp.int16)

    def refine_body(kb, carry):
        hi = hi_ref[kb]
        lo_ref[kb] = jnp.where(hi > t_hi16, jnp.int16(HALF_RANGE - 1),
                               jnp.where(hi == t_hi16, lo_ref[kb], jnp.int16(-HALF_RANGE)))
        return carry

    lax.fori_loop(0, nkb, refine_body, 0)
    t_lo = kth_largest_16(lo_ref)
    thr_raw = t_hi * jnp.int32(2 * HALF_RANGE) + (t_lo + jnp.int32(HALF_RANGE))
    thr = jnp.maximum(thr_raw, jnp.int32(INT_MIN + 1))

    t_lo16 = t_lo.astype(jnp.int16)
    n_ge = count(lo_ref, lambda k, kb: k >= t_lo16)
    has_ties = jnp.max(jnp.where(thr_raw != jnp.int32(INT_MIN), n_ge, 0)) > top_k

    @pl.when(has_ties)
    def _():
        need = top_k - count(key_ref, lambda k, kb: k > thr)
        n_bits = int(seq - 1).bit_length()

        def body(i, ans):
            cand = ans | lax.shift_left(jnp.int32(1), jnp.int32(n_bits - 1) - i)
            below = count(key_ref, lambda k, kb: (k == thr) & (kb * C + row_i < cand))
            return jnp.where(below < need, cand, ans)

        last_tie = lax.fori_loop(0, n_bits, body, jnp.zeros((1, C), jnp.int32))

        def drop_body(kb, carry):
            kk = key_ref[kb]
            key_ref[kb] = jnp.where((kk == thr) & (kb * C + row_i > last_tie), kk - 1, kk)
            return carry

        lax.fori_loop(0, nkb, drop_body, 0)

    neg_inf = jnp.float32(-jnp.inf)
    m_ref[...] = jnp.full(m_ref.shape, neg_inf, jnp.float32)
    l_ref[...] = jnp.zeros(l_ref.shape, jnp.float32)
    acc_ref[...] = jnp.zeros(acc_ref.shape, jnp.float32)
    T = ATTN_KEY_TILE

    kv_group = N_HEADS // N_KV_HEADS

    tiles = [(half, h) for half in range(C // T) for h in range(N_HEADS)]

    def bias_body(kb, carry):
        key_ref[kb] = lax.bitcast_convert_type(jnp.where(key_ref[kb] >= thr, 0.0, neg_inf), jnp.int32)
        return carry

    lax.fori_loop(0, nkb, bias_body, 0)
    n_blocks = seq // C
    key_ref[n_blocks] = lax.bitcast_convert_type(jnp.full((C, C), neg_inf, jnp.float32), jnp.int32)

    def store_logits(kb, dst_ref, n):
        half, h = tiles[n]
        kb_mem = jnp.minimum(kb, nkb - 1)
        kb_bias = jnp.where(kb < nkb, kb, n_blocks)
        r0 = pl.multiple_of(kb_mem * C + half * T, T)
        bias = lax.bitcast_convert_type(key_ref[kb_bias, half * T:(half + 1) * T, :], jnp.float32)
        k2 = kn_ref[0, pl.ds(r0, T), 0:KV_WIDTH]
        qh = qt_ref[0, h * HEAD_DIM:(h + 1) * HEAD_DIM, :]
        rhs = jnp.concatenate([qh, zeros_half] if h < kv_group else [zeros_half, qh], axis=0)
        dst_ref[n] = jnp.dot(k2, rhs, preferred_element_type=jnp.float32) + bias

    def softmax_block(kb, src_ref, kb_next, dst_ref):
        kb_mem = jnp.minimum(kb, nkb - 1)
        for n, (half, h) in enumerate(tiles):
            store_logits(kb_next, dst_ref, n)
            g = h // kv_group
            logit = src_ref[n]
            m_old = m_ref[h:h + 1, :]
            m_new = jnp.maximum(m_old, jnp.max(logit, axis=0, keepdims=True))
            m_safe = jnp.where(m_new == neg_inf, 0.0, m_new)
            p = jnp.exp(logit - m_safe)
            alpha = jnp.exp(m_old - m_safe)
            l_ref[h:h + 1, :] = alpha * l_ref[h:h + 1, :] + jnp.sum(p, axis=0, keepdims=True)
            vt = vt_ref[0, kb_mem, g * HEAD_DIM:(g + 1) * HEAD_DIM, half * T:(half + 1) * T]
            pv = jnp.dot(vt, p.astype(jnp.bfloat16), preferred_element_type=jnp.float32)
            hs = slice(h * HEAD_DIM, (h + 1) * HEAD_DIM)
            acc_ref[hs, :] = alpha * acc_ref[hs, :] + pv
            m_ref[h:h + 1, :] = m_new

    for n in range(len(tiles)):
        store_logits(0, lga_ref, n)

    def attn_body(pair, carry):
        kb = 2 * pair
        softmax_block(kb, lga_ref, kb + 1, lgb_ref)
        softmax_block(kb + 1, lgb_ref, kb + 2, lga_ref)
        return carry

    lax.fori_loop(0, (nkb + 1) // 2, attn_body, 0)
    for h in range(N_HEADS):
        hs = slice(h * HEAD_DIM, (h + 1) * HEAD_DIM)
        acc_ref[hs, :] = acc_ref[hs, :] / l_ref[h:h + 1, :]
    o_ref[0] = jnp.transpose(acc_ref[...]).astype(o_ref.dtype)


def _attn_call(qt, qit, wit, kn, vt, top_k):
    B, _, S = qt.shape
    C = ATTN_CHUNK
    nc = S // C
    kern = functools.partial(_attn_kernel, seq=S, top_k=top_k)
    in_specs = [
        pl.BlockSpec((1, Q_WIDTH, C), lambda b, c: (b, 0, c)),
        pl.BlockSpec((1, IDXQ_WIDTH, C), lambda b, c: (b, 0, c)),
        pl.BlockSpec((1, IDX_HEADS, C), lambda b, c: (b, 0, c)),
        pl.BlockSpec((1, S, KN_WIDTH), lambda b, c: (b, 0, 0)),
        pl.BlockSpec((1, nc, KV_WIDTH, C), lambda b, c: (b, 0, 0, 0)),
    ]
    out_specs = pl.BlockSpec((1, C, Q_WIDTH), lambda b, c: (b, c, 0))
    scratch = [
        pltpu.VMEM((nc + 1, C, C), jnp.int32),
        pltpu.VMEM((nc, C, C), jnp.int16),
        pltpu.VMEM((nc, C, C), jnp.int16),
        pltpu.VMEM((N_HEADS * C // ATTN_KEY_TILE, ATTN_KEY_TILE, C), jnp.float32),
        pltpu.VMEM((N_HEADS * C // ATTN_KEY_TILE, ATTN_KEY_TILE, C), jnp.float32),
        pltpu.VMEM((Q_WIDTH, C), jnp.float32),
        pltpu.VMEM((N_HEADS, C), jnp.float32),
        pltpu.VMEM((N_HEADS, C), jnp.float32),
    ]
    vmem = (2 * S * C * 4 + Q_WIDTH * C * 4 + 2 * 2 * (2 * Q_WIDTH * C + S * KN_WIDTH + S * KV_WIDTH + C * Q_WIDTH)
            + 24 * C * C * 4 + (8 << 20))
    return pl.pallas_call(
        kern, grid=(B, nc), in_specs=in_specs, out_specs=out_specs,
        out_shape=jax.ShapeDtypeStruct((B, S, Q_WIDTH), jnp.bfloat16),
        scratch_shapes=scratch, compiler_params=_cparams(2, vmem), name="dsa_attn",
    )(qt, qit, wit, kn, vt)


def _layer_norm(v, g, b):
    mu = jnp.mean(v, axis=-1, keepdims=True)
    d = v - mu
    var = jnp.mean(d * d, axis=-1, keepdims=True)
    return d * lax.rsqrt(var + LN_EPS) * g + b


def _pack_rows(h):
    hi = lax.bitcast_convert_type(h[:, :HALF].astype(jnp.bfloat16).astype(jnp.float32), jnp.int32)
    lo = lax.bitcast_convert_type(h[:, HALF:].astype(jnp.bfloat16).astype(jnp.float32), jnp.int32)
    return (hi & jnp.int32(-65536)) | lax.shift_right_logical(lo, 16)


def _unpack_rows(w):
    hi = lax.bitcast_convert_type(w & jnp.int32(-65536), jnp.float32)
    lo = lax.bitcast_convert_type(lax.shift_left(w, 16), jnp.float32)
    return jnp.concatenate([hi, lo], axis=1)


def _mix_kernel(x_ref, attn_ref, wc_ref, wg_ref, gb_ref, cw_ref, wau_ref, wcu_ref, wo_ref,
                g1_ref, b1_ref, wrh_ref, wrl_ref, rb_ref,
                h_ref, hp_ref, eid_ref, rank_ref, gate_ref, cnt_ref,
                ubuf_ref, base_ref, *, steps_per_seq):
    R = MIX_ROWS
    i = pl.program_id(0)

    @pl.when(i == 0)
    def _():
        base_ref[...] = jnp.zeros_like(base_ref)

    @pl.when(i % steps_per_seq == 0)
    def _():
        ubuf_ref[0:V7X_SUBLANES, :] = jnp.zeros((V7X_SUBLANES, CONV_DIM), jnp.float32)

    x = x_ref[...]
    xb = x.astype(jnp.bfloat16)
    cv = jnp.dot(xb, wc_ref[...], preferred_element_type=jnp.float32)
    u = cv[:, 2 * CONV_DIM:] * cv[:, :CONV_DIM]
    ubuf_ref[V7X_SUBLANES:, :] = u
    u1 = ubuf_ref[V7X_SUBLANES - 1:V7X_SUBLANES - 1 + R, :]
    u2 = ubuf_ref[V7X_SUBLANES - 2:V7X_SUBLANES - 2 + R, :]
    y = cw_ref[0:1, :] * u2 + cw_ref[1:2, :] * u1 + cw_ref[2:3, :] * u
    conv = (cv[:, CONV_DIM:2 * CONV_DIM] * y).astype(jnp.bfloat16)
    ubuf_ref[0:V7X_SUBLANES, :] = u[R - V7X_SUBLANES:, :]
    z = jnp.dot(xb, wg_ref[...], preferred_element_type=jnp.float32) + gb_ref[...]
    gates = 1.0 / (1.0 + jnp.exp(-z))
    au = jnp.dot(attn_ref[...], wau_ref[...], preferred_element_type=jnp.float32)
    cu = jnp.dot(conv, wcu_ref[...], preferred_element_type=jnp.float32)
    merged = gates[:, :D_MODEL] * au + gates[:, D_MODEL:] * cu
    mix = jnp.dot(merged.astype(jnp.bfloat16), wo_ref[...], preferred_element_type=jnp.float32)
    h = _layer_norm(DEEPNORM_ALPHA * x + mix, g1_ref[...], b1_ref[...])
    h_ref[...] = h
    hp_ref[...] = _pack_rows(h)

    h_hi = h.astype(jnp.bfloat16)
    h_lo = (h - h_hi.astype(jnp.float32)).astype(jnp.bfloat16)
    lg = (jnp.dot(h_hi, wrh_ref[...], preferred_element_type=jnp.float32)
          + jnp.dot(h_lo, wrh_ref[...], preferred_element_type=jnp.float32)
          + jnp.dot(h_hi, wrl_ref[...], preferred_element_type=jnp.float32)) + rb_ref[...]
    lane = lax.broadcasted_iota(jnp.int32, (R, ROUTER_LANES), 1).astype(jnp.float32)
    neg = jnp.float32(-jnp.inf)
    no_lane = jnp.float32(ROUTER_LANES)
    gmask = lane < N_GROUPS
    gl = jnp.where(gmask, lg, neg)
    gmax = jnp.max(gl, axis=1, keepdims=True)
    grp = jnp.min(jnp.where(gl == gmax, lane, no_lane), axis=1, keepdims=True)
    gsum = jnp.sum(jnp.where(gmask, jnp.exp(gl - gmax), 0.0), axis=1, keepdims=True)
    p_grp = 1.0 / gsum
    lo_lane = E0 + grp * EXPERTS_PER_GROUP
    emask = (lane >= lo_lane) & (lane < lo_lane + EXPERTS_PER_GROUP)
    el = jnp.where(emask, lg, neg)
    v1 = jnp.max(el, axis=1, keepdims=True)
    i1 = jnp.min(jnp.where(el == v1, lane, no_lane), axis=1, keepdims=True)
    el2 = jnp.where(lane == i1, neg, el)
    v2 = jnp.max(el2, axis=1, keepdims=True)
    i2 = jnp.min(jnp.where(el2 == v2, lane, no_lane), axis=1, keepdims=True)
    a = jnp.exp(v2 - v1)
    inv = 1.0 / (1.0 + a)
    gate_ref[:, 0:1] = p_grp * inv
    gate_ref[:, 1:2] = p_grp * (a * inv)
    eid_ref[:, 0:1] = (i1 - E0).astype(jnp.int32)
    eid_ref[:, 1:2] = (i2 - E0).astype(jnp.int32)
    oh1 = lane == i1
    oh2 = lane == i2
    oh = (oh1 | oh2).astype(jnp.bfloat16)
    r_i = lax.broadcasted_iota(jnp.int32, (R, R), 0)
    c_i = lax.broadcasted_iota(jnp.int32, (R, R), 1)
    tri = (r_i > c_i).astype(jnp.bfloat16)
    before = jnp.dot(tri, oh, preferred_element_type=jnp.float32) + base_ref[...]
    rank_ref[:, 0:1] = jnp.sum(jnp.where(oh1, before, 0.0), axis=1, keepdims=True).astype(jnp.int32)
    rank_ref[:, 1:2] = jnp.sum(jnp.where(oh2, before, 0.0), axis=1, keepdims=True).astype(jnp.int32)
    base_ref[...] = base_ref[...] + jnp.sum(oh.astype(jnp.float32), axis=0, keepdims=True)
    cnt_ref[...] = base_ref[...]


def _mix_call(x2, attn2, wc, wg, gb, cw, wau, wcu, wo, g1, b1, wrh, wrl, rb, seq):
    N, D = x2.shape
    R = MIX_ROWS
    nt = N // R
    kern = functools.partial(_mix_kernel, steps_per_seq=seq // R)

    def full(a):
        return pl.BlockSpec(a.shape, lambda i: (0,) * a.ndim)

    in_specs = [
        pl.BlockSpec((R, D), lambda i: (i, 0)),
        pl.BlockSpec((R, Q_WIDTH), lambda i: (i, 0)),
        full(wc), full(wg), full(gb), full(cw), full(wau), full(wcu), full(wo),
        full(g1), full(b1), full(wrh), full(wrl), full(rb),
    ]
    out_shape = (
        jax.ShapeDtypeStruct((N, D), jnp.float32),
        jax.ShapeDtypeStruct((N, HALF), jnp.int32),
        jax.ShapeDtypeStruct((N, 2), jnp.int32),
        jax.ShapeDtypeStruct((N, 2), jnp.int32),
        jax.ShapeDtypeStruct((N, 2), jnp.float32),
        jax.ShapeDtypeStruct((1, ROUTER_LANES), jnp.float32),
    )
    out_specs = (
        pl.BlockSpec((R, D), lambda i: (i, 0)),
        pl.BlockSpec((R, HALF), lambda i: (i, 0)),
        pl.BlockSpec((R, 2), lambda i: (i, 0)),
        pl.BlockSpec((R, 2), lambda i: (i, 0)),
        pl.BlockSpec((R, 2), lambda i: (i, 0)),
        pl.BlockSpec((1, ROUTER_LANES), lambda i: (0, 0)),
    )
    scratch = [pltpu.VMEM((R + V7X_SUBLANES, CONV_DIM), jnp.float32),
               pltpu.VMEM((1, ROUTER_LANES), jnp.float32)]
    w_bytes = 2 * (wc.size + wg.size + wau.size + wcu.size + wo.size + wrh.size + wrl.size)
    vmem = 2 * w_bytes + 2 * (R * D * 4 * 2 + R * Q_WIDTH * 2 + R * HALF * 4) + 10 * R * 2048 * 4 + (6 << 20)
    return pl.pallas_call(
        kern, grid=(nt,), in_specs=in_specs, out_specs=out_specs, out_shape=out_shape,
        scratch_shapes=scratch, compiler_params=_cparams(1, vmem), name="mix_ln_router",
    )(x2, attn2, wc, wg, gb, cw, wau, wcu, wo, g1, b1, wrh, wrl, rb)


def _row_copy(src_ref, s, dst_ref, d, sem):
    return pltpu.make_async_copy(src_ref.at[pl.ds(s, 1), :], dst_ref.at[pl.ds(d, 1), :], sem)


def _dispatch_kernel(dest_ref, hp_ref, xs_in_ref, xs_ref, sem):
    del xs_in_ref
    R = MOVE_ROWS

    def issue(r, carry):
        _row_copy(hp_ref, r, xs_ref, dest_ref[0, 0, 2 * r], sem).start()
        _row_copy(hp_ref, r, xs_ref, dest_ref[0, 0, 2 * r + 1], sem).start()
        return carry

    lax.fori_loop(0, R, issue, 0, unroll=ISSUE_UNROLL)
    for _ in range(2):
        pltpu.make_async_copy(hp_ref, xs_ref.at[pl.ds(0, R), :], sem).wait()


def _dispatch_call(dest3, hp, xs0):
    N, W = hp.shape
    R = MOVE_ROWS
    nt = N // R
    return pl.pallas_call(
        _dispatch_kernel, grid=(nt,),
        in_specs=[
            pl.BlockSpec((1, 1, 2 * R), lambda i: (i, 0, 0), memory_space=pltpu.SMEM),
            pl.BlockSpec((R, W), lambda i: (i, 0)),
            pl.BlockSpec(memory_space=pl.ANY),
        ],
        out_specs=pl.BlockSpec(memory_space=pl.ANY),
        out_shape=jax.ShapeDtypeStruct(xs0.shape, xs0.dtype),
        scratch_shapes=[pltpu.SemaphoreType.DMA],
        input_output_aliases={2: 0},
        compiler_params=_cparams(1, 4 * R * W * 4 + (4 << 20)), name="moe_dispatch",
    )(dest3, hp, xs0)


def _expert_kernel(be_ref, nb_ref, xs_ref, wg_ref, wu_ref, wd_ref, ys_ref):
    j = pl.program_id(0)

    @pl.when(j < nb_ref[0])
    def _():
        xb = _unpack_rows(xs_ref[...]).astype(jnp.bfloat16)
        hg = jnp.dot(xb, wg_ref[0], preferred_element_type=jnp.float32)
        hu = jnp.dot(xb, wu_ref[0], preferred_element_type=jnp.float32)
        hid = (hg / (1.0 + jnp.exp(-hg))) * hu
        y = jnp.dot(hid.astype(jnp.bfloat16), wd_ref[0], preferred_element_type=jnp.float32)
        ys_ref[...] = _pack_rows(y)

    @pl.when(j >= nb_ref[0])
    def _():
        ys_ref[...] = jnp.zeros_like(ys_ref)


def _expert_call(block_expert, n_used, xs, wg, wu, wd):
    cap, W = xs.shape
    RB = EXPERT_ROWS
    nb = cap // RB
    grid_spec = pltpu.PrefetchScalarGridSpec(
        num_scalar_prefetch=2, grid=(nb,),
        in_specs=[
            pl.BlockSpec((RB, W), lambda j, be, nu: (j, 0)),
            pl.BlockSpec((1, D_MODEL, D_EXPERT), lambda j, be, nu: (be[j], 0, 0)),
            pl.BlockSpec((1, D_MODEL, D_EXPERT), lambda j, be, nu: (be[j], 0, 0)),
            pl.BlockSpec((1, D_EXPERT, D_MODEL), lambda j, be, nu: (be[j], 0, 0)),
        ],
        out_specs=pl.BlockSpec((RB, W), lambda j, be, nu: (j, 0)),
    )
    vmem = 2 * 3 * D_MODEL * D_EXPERT * 2 + 4 * RB * W * 4 + 8 * RB * D_MODEL * 4 + (6 << 20)
    return pl.pallas_call(
        _expert_kernel, grid_spec=grid_spec,
        out_shape=jax.ShapeDtypeStruct((cap, W), jnp.int32),
        compiler_params=_cparams(1, vmem), name="moe_experts",
    )(block_expert, n_used, xs, wg, wu, wd)


def _combine_kernel(dest_ref, h_ref, gate_ref, g2_ref, b2_ref, ys_ref, o_ref, buf_ref, sem):
    R = MOVE_ROWS

    def issue(r, carry):
        _row_copy(ys_ref, dest_ref[0, 0, 2 * r], buf_ref.at[0], r, sem).start()
        _row_copy(ys_ref, dest_ref[0, 0, 2 * r + 1], buf_ref.at[1], r, sem).start()
        return carry

    lax.fori_loop(0, R, issue, 0, unroll=ISSUE_UNROLL)
    for slot in range(2):
        pltpu.make_async_copy(ys_ref.at[pl.ds(0, R), :], buf_ref.at[slot], sem).wait()
    y0 = _unpack_rows(buf_ref[0])
    y1 = _unpack_rows(buf_ref[1])
    ffn = gate_ref[:, 0:1] * y0 + gate_ref[:, 1:2] * y1
    o_ref[...] = _layer_norm(DEEPNORM_ALPHA * h_ref[...] + ffn, g2_ref[...], b2_ref[...])


def _combine_call(dest3, h, gate, g2, b2, ys):
    N, D = h.shape
    W = ys.shape[1]
    R = MOVE_ROWS
    nt = N // R
    return pl.pallas_call(
        _combine_kernel, grid=(nt,),
        in_specs=[
            pl.BlockSpec((1, 1, 2 * R), lambda i: (i, 0, 0), memory_space=pltpu.SMEM),
            pl.BlockSpec((R, D), lambda i: (i, 0)),
            pl.BlockSpec((R, 2), lambda i: (i, 0)),
            pl.BlockSpec((1, D), lambda i: (0, 0)),
            pl.BlockSpec((1, D), lambda i: (0, 0)),
            pl.BlockSpec(memory_space=pl.ANY),
        ],
        out_specs=pl.BlockSpec((R, D), lambda i: (i, 0)),
        out_shape=jax.ShapeDtypeStruct((N, D), jnp.float32),
        scratch_shapes=[pltpu.VMEM((2, R, W), jnp.int32), pltpu.SemaphoreType.DMA],
        compiler_params=_cparams(1, 4 * R * D * 4 + 2 * R * W * 4 + 8 * R * D * 4 + (4 << 20)),
        name="moe_combine",
    )(dest3, h, gate, g2, b2, ys)


def _rope_tables(seq):
    inv_freq = ROPE_THETA ** (-jnp.arange(ROT_HALF, dtype=jnp.float32) / ROT_HALF)
    ang = jnp.arange(seq, dtype=jnp.int32).astype(jnp.float32)[:, None] * inv_freq[None, :]
    cos, sin = jnp.cos(ang), jnp.sin(ang)
    ones = jnp.ones((seq, HEAD_DIM - ROT_DIMS), jnp.float32)
    zeros = jnp.zeros((seq, HEAD_DIM - ROT_DIMS), jnp.float32)
    c_head = jnp.concatenate([cos, cos, ones], axis=1)
    s_head = jnp.concatenate([-sin, sin, zeros], axis=1)
    reps = KN_WIDTH // HEAD_DIM
    return cos.T, sin.T, jnp.tile(c_head, (1, reps)), jnp.tile(s_head, (1, reps))


def _swap_rot_cols(w):
    d, n = w.shape
    wh = w.reshape(d, n // HEAD_DIM, HEAD_DIM)
    sw = jnp.concatenate([wh[:, :, ROT_HALF:ROT_DIMS], wh[:, :, :ROT_HALF],
                          jnp.zeros((d, n // HEAD_DIM, HEAD_DIM - ROT_DIMS), w.dtype)], axis=2)
    return sw.reshape(d, n)


def _block(x, w_in, gate_bias, w_attn_up, w_conv_up, conv_w, w_out, ln_g, ln_b,
           rg_w, rg_b, re_w, re_b, w_gate_e, w_up_e, w_down_e, ln2_g, ln2_b):
    B, S, D = x.shape
    N = B * S
    top_k = min(TOPK_MAX, S // 4)
    bf = jnp.bfloat16
    o = np.cumsum([0, Q_WIDTH, KV_WIDTH, KV_WIDTH, IDXQ_WIDTH, IDX_DIM, IDX_HEADS,
                   CONV_DIM, CONV_DIM, CONV_DIM, N_BRANCHES * D_MODEL])
    w_q, w_k, w_v, w_qi, w_ki, w_wi = (w_in[:, o[i]:o[i + 1]] for i in range(6))
    w_conv = w_in[:, o[6]:o[9]]
    w_gates = w_in[:, o[9]:o[10]]

    wt = jnp.concatenate([w_q.T * (HEAD_DIM ** -0.5), w_qi.T, w_v.T, w_wi.T,
                          jnp.zeros((T_ROWS - T_WI0 - IDX_HEADS, D), w_in.dtype)], axis=0).astype(bf)
    pad = jnp.zeros((D, KN_WIDTH - KV_WIDTH - IDX_DIM), w_in.dtype)
    wn = jnp.concatenate([w_k, w_ki, pad, _swap_rot_cols(w_k), _swap_rot_cols(w_ki), pad], axis=1).astype(bf)
    cos_t, sin_t, cos_n, sgn_n = _rope_tables(S)

    qt, qit, vt, wit, kn = _proj_call(x, wt, wn, cos_t, sin_t, cos_n, sgn_n)
    attn = _attn_call(qt, qit, wit, kn, vt, top_k)

    w_r = jnp.concatenate([rg_w, jnp.transpose(re_w, (1, 0, 2)).reshape(D, N_EXPERTS),
                           jnp.zeros((D, ROUTER_LANES - E0 - N_EXPERTS), rg_w.dtype)], axis=1)
    b_r = jnp.concatenate([rg_b, re_b.reshape(-1),
                           jnp.zeros((ROUTER_LANES - E0 - N_EXPERTS,), rg_b.dtype)])[None, :]
    w_rh = w_r.astype(bf)
    w_rl = (w_r - w_rh.astype(jnp.float32)).astype(bf)

    h, hp, eid, rank, gate, cnt = _mix_call(
        x.reshape(N, D), attn.reshape(N, Q_WIDTH), w_conv.astype(bf), w_gates.astype(bf),
        gate_bias[None, :], conv_w, w_attn_up.astype(bf), w_conv_up.astype(bf), w_out.astype(bf),
        ln_g[None, :], ln_b[None, :], w_rh, w_rl, b_r, S)

    RB = EXPERT_ROWS
    counts = cnt[0, E0:E0 + N_EXPERTS].astype(jnp.int32)
    padded = ((counts + RB - 1) // RB) * RB
    pad_end = jnp.cumsum(padded)
    pad_start = pad_end - padded
    dest = jnp.sum(jnp.where(eid[:, :, None] == jnp.arange(N_EXPERTS, dtype=jnp.int32)[None, None, :],
                             pad_start[None, None, :], 0), axis=-1) + rank
    cap = N * 2 + N_EXPERTS * RB
    nb = cap // RB
    block_start = jnp.arange(nb, dtype=jnp.int32) * RB
    block_expert = jnp.minimum(jnp.sum((block_start[:, None] >= pad_end[None, :]).astype(jnp.int32), axis=1),
                               N_EXPERTS - 1)
    n_used = (pad_end[-1:] // RB).astype(jnp.int32)
    dest3 = dest.reshape(N // MOVE_ROWS, 1, 2 * MOVE_ROWS)

    xs = _dispatch_call(dest3, hp, jnp.zeros((cap, HALF), jnp.int32))
    ys = _expert_call(block_expert, n_used, xs, w_gate_e.astype(bf), w_up_e.astype(bf), w_down_e.astype(bf))
    out = _combine_call(dest3, h, gate, ln2_g[None, :], ln2_b[None, :], ys)
    return out.reshape(B, S, D)


def kernel(x, w_in, gate_bias, w_attn_up, w_conv_up, conv_w, w_out, ln1_g, ln1_b, router_group_w,
           router_group_b, router_expert_w, router_expert_b, w_gate_e, w_up_e, w_down_e, ln2_g, ln2_b):
    h = x
    for l in range(DEPTH):
        h = _block(h, w_in[l], gate_bias[l], w_attn_up[l], w_conv_up[l], conv_w[l], w_out[l],
                   ln1_g[l], ln1_b[l], router_group_w[l], router_group_b[l], router_expert_w[l],
                   router_expert_b[l], w_gate_e[l], w_up_e[l], w_down_e[l], ln2_g[l], ln2_b[l])
    return h
```

```python
import functools

import jax
import jax.numpy as jnp
import numpy as np
from jax import lax
from jax.experimental import pallas as pl
from jax.experimental.pallas import tpu as pltpu

D_MODEL = 1024
N_HEADS = 8
N_KV_HEADS = 2
HEAD_DIM = 64
Q_WIDTH = N_HEADS * HEAD_DIM
KV_WIDTH = N_KV_HEADS * HEAD_DIM
ROPE_THETA = 500000.0
ROT_DIMS = HEAD_DIM // 4
ROT_HALF = ROT_DIMS // 2
IDX_HEADS = 8
IDX_DIM = 64
IDXQ_WIDTH = IDX_HEADS * IDX_DIM
TOPK_MAX = 256
CONV_DIM = 512
CONV_WIDTH = 3
N_BRANCHES = 2
N_GROUPS = 4
EXPERTS_PER_GROUP = 8
N_EXPERTS = N_GROUPS * EXPERTS_PER_GROUP
D_EXPERT = 512
LN_EPS = 1e-5
DEPTH = 1
DEEPNORM_ALPHA = (2 * DEPTH) ** 0.25

V7X_LANES = 128
V7X_SUBLANES = 8
V7X_VMEM_LIMIT_BYTES = 56 * 1024 * 1024

PROJ_ROWS = 512
ATTN_CHUNK = 256
ATTN_KEY_TILE = 128
DEN_ROWS = 16
ACC_ROWS = HEAD_DIM + DEN_ROWS
LOG2_E = 1.4426950408889634
MIX_ROWS = 512
EXPERT_ROWS = 256
MOVE_ROWS = 512
ISSUE_UNROLL = 8
HALF = D_MODEL // 2

T_Q0, T_QI0, T_V0, T_WI0 = 0, Q_WIDTH, Q_WIDTH + IDXQ_WIDTH, Q_WIDTH + IDXQ_WIDTH + KV_WIDTH
T_ROWS = T_WI0 + 16
KN_WIDTH = 256

INT_MIN = -2147483648
HALF_RANGE = 32768
ROUTER_LANES = 128
E0 = N_GROUPS


def _cparams(n_axes, vmem_bytes):
    return pltpu.CompilerParams(
        dimension_semantics=("arbitrary",) * n_axes,
        vmem_limit_bytes=int(min(vmem_bytes, V7X_VMEM_LIMIT_BYTES)),
    )


def _proj_kernel(x_ref, wt_ref, wn_ref, cost_ref, sint_ref, cosn_ref, sgnn_ref,
                 qt_ref, qit_ref, vt_ref, wit_ref, kn_ref):
    xb = x_ref[0].astype(jnp.bfloat16)
    pt = lax.dot_general(wt_ref[...], xb, (((1,), (1,)), ((), ())),
                         preferred_element_type=jnp.float32)
    cos = cost_ref[...]
    sin = sint_ref[...]

    def rope_t(dst_ref, base):
        for h in range(N_HEADS):
            r0 = base + h * HEAD_DIM
            x1 = pt[r0:r0 + ROT_HALF]
            x2 = pt[r0 + ROT_HALF:r0 + ROT_DIMS]
            head = jnp.concatenate([x1 * cos - x2 * sin, x2 * cos + x1 * sin,
                                    pt[r0 + ROT_DIMS:r0 + HEAD_DIM]], axis=0)
            dst_ref[0, h * HEAD_DIM:(h + 1) * HEAD_DIM, :] = head.astype(dst_ref.dtype)

    rope_t(qt_ref, T_Q0)
    rope_t(qit_ref, T_QI0)
    for jj in range(PROJ_ROWS // ATTN_CHUNK):
        vt_ref[0, jj] = pt[T_V0:T_V0 + KV_WIDTH, jj * ATTN_CHUNK:(jj + 1) * ATTN_CHUNK].astype(vt_ref.dtype)
    wit_ref[0] = pt[T_WI0:T_WI0 + IDX_HEADS]
    pn = jnp.dot(xb, wn_ref[...], preferred_element_type=jnp.float32)
    kn = pn[:, :KN_WIDTH] * cosn_ref[...] + pn[:, KN_WIDTH:] * sgnn_ref[...]
    kn_ref[0] = kn.astype(kn_ref.dtype)


def _proj_call(x, wt, wn, cos_t, sin_t, cos_n, sgn_n):
    B, S, D = x.shape
    R = PROJ_ROWS
    nt = S // R
    grid = (B, nt)
    out_shape = (
        jax.ShapeDtypeStruct((B, Q_WIDTH, S), jnp.bfloat16),
        jax.ShapeDtypeStruct((B, IDXQ_WIDTH, S), jnp.bfloat16),
        jax.ShapeDtypeStruct((B, S // ATTN_CHUNK, KV_WIDTH, ATTN_CHUNK), jnp.bfloat16),
        jax.ShapeDtypeStruct((B, IDX_HEADS, S), jnp.float32),
        jax.ShapeDtypeStruct((B, S, KN_WIDTH), jnp.bfloat16),
    )
    in_specs = [
        pl.BlockSpec((1, R, D), lambda b, j: (b, j, 0)),
        pl.BlockSpec((T_ROWS, D), lambda b, j: (0, 0)),
        pl.BlockSpec((D, 2 * KN_WIDTH), lambda b, j: (0, 0)),
        pl.BlockSpec((ROT_HALF, R), lambda b, j: (0, j)),
        pl.BlockSpec((ROT_HALF, R), lambda b, j: (0, j)),
        pl.BlockSpec((R, KN_WIDTH), lambda b, j: (j, 0)),
        pl.BlockSpec((R, KN_WIDTH), lambda b, j: (j, 0)),
    ]
    out_specs = (
        pl.BlockSpec((1, Q_WIDTH, R), lambda b, j: (b, 0, j)),
        pl.BlockSpec((1, IDXQ_WIDTH, R), lambda b, j: (b, 0, j)),
        pl.BlockSpec((1, R // ATTN_CHUNK, KV_WIDTH, ATTN_CHUNK), lambda b, j: (b, j, 0, 0)),
        pl.BlockSpec((1, IDX_HEADS, R), lambda b, j: (b, 0, j)),
        pl.BlockSpec((1, R, KN_WIDTH), lambda b, j: (b, j, 0)),
    )
    vmem = 2 * (R * D * 4 + T_ROWS * D * 2 + D * 2 * KN_WIDTH * 2) + 6 * T_ROWS * R * 4 + (8 << 20)
    return pl.pallas_call(
        _proj_kernel, grid=grid, in_specs=in_specs, out_specs=out_specs, out_shape=out_shape,
        compiler_params=_cparams(2, vmem), name="dsa_proj",
    )(x, wt, wn, cos_t, sin_t, cos_n, sgn_n)


def _float_to_key(s):
    b = lax.bitcast_convert_type(s, jnp.int32)
    k = b ^ (lax.shift_right_arithmetic(b, 31) & jnp.int32(0x7FFFFFFF))
    return jnp.where(b == jnp.int32(INT_MIN), jnp.int32(0), k)


def _attn_kernel(qt_ref, qit_ref, wit_ref, kn_ref, vt_ref, o_ref,
                 key_ref, hi_ref, lo_ref, lga_ref, lgb_ref, acc_ref, m_ref, *, seq, top_k):
    C = ATTN_CHUNK
    c = pl.program_id(1)
    nkb = c + 1
    zeros_half = jnp.zeros((HEAD_DIM, C), jnp.bfloat16)
    row_i = lax.broadcasted_iota(jnp.int32, (C, C), 0)
    lane_i = lax.broadcasted_iota(jnp.int32, (C, C), 1)
    causal_in_block = row_i <= lane_i

    def score_body(kb, carry):
        r0 = pl.multiple_of(kb * C, C)
        kix = kn_ref[0, pl.ds(r0, C), KV_WIDTH:KN_WIDTH]
        acc = None
        for h in range(IDX_HEADS):
            rhs = jnp.concatenate([qit_ref[0, h * IDX_DIM:(h + 1) * IDX_DIM, :], zeros_half], axis=0)
            s = jnp.dot(kix, rhs, preferred_element_type=jnp.float32)
            t = jnp.maximum(s, 0.0) * wit_ref[0, h:h + 1, :]
            acc = t if acc is None else acc + t
        score = acc * (IDX_DIM ** -0.5 * IDX_HEADS ** -0.5)
        keys = jnp.where(causal_in_block | (kb != c), _float_to_key(score), jnp.int32(INT_MIN))
        key_ref[kb] = keys
        hi_ref[kb] = lax.shift_right_arithmetic(keys, 16).astype(jnp.int16)
        lo_ref[kb] = ((keys & jnp.int32(0xFFFF)) - jnp.int32(HALF_RANGE)).astype(jnp.int16)
        return carry

    lax.fori_loop(0, nkb, score_body, 0)

    def count(ref, pred):
        packed = ref.dtype == jnp.int16

        def body(kb, part):
            hit = pred(ref[kb], kb)
            if packed:
                words = pltpu.bitcast(jnp.where(hit, jnp.int16(1), jnp.int16(0)), jnp.int32)
            else:
                words = hit.astype(jnp.int32)
            return part + jnp.sum(words.reshape(-1, V7X_SUBLANES, C), axis=0)

        part = lax.fori_loop(0, nkb, body, jnp.zeros((V7X_SUBLANES, C), jnp.int32))
        if packed:
            part = (part & jnp.int32(0xFFFF)) + lax.shift_right_logical(part, 16)
        return jnp.sum(part, axis=0, keepdims=True)

    def kth_largest_16(ref):
        def body(i, ans_u):
            cand_u = ans_u | lax.shift_left(jnp.int32(1), jnp.int32(15) - i)
            cand = (cand_u - jnp.int32(HALF_RANGE)).astype(jnp.int16)
            return jnp.where(count(ref, lambda k, kb: k >= cand) >= top_k, cand_u, ans_u)

        return lax.fori_loop(0, 16, body, jnp.zeros((1, C), jnp.int32)) - jnp.int32(HALF_RANGE)

    t_hi = kth_largest_16(hi_ref)
    t_hi16 = t_hi.astype(jnp.int16)

    def refine_body(kb, carry):
        hi = hi_ref[kb]
        lo_ref[kb] = jnp.where(hi > t_hi16, jnp.int16(HALF_RANGE - 1),
                               jnp.where(hi == t_hi16, lo_ref[kb], jnp.int16(-HALF_RANGE)))
        return carry

    lax.fori_loop(0, nkb, refine_body, 0)
    t_lo = kth_largest_16(lo_ref)
    thr_raw = t_hi * jnp.int32(2 * HALF_RANGE) + (t_lo + jnp.int32(HALF_RANGE))
    thr = jnp.maximum(thr_raw, jnp.int32(INT_MIN + 1))

    t_lo16 = t_lo.astype(jnp.int16)
    n_ge = count(lo_ref, lambda k, kb: k >= t_lo16)
    has_ties = jnp.max(jnp.where(thr_raw != jnp.int32(INT_MIN), n_ge, 0)) > top_k

    @pl.when(has_ties)
    def _():
        need = top_k - count(key_ref, lambda k, kb: k > thr)
        n_bits = int(seq - 1).bit_length()

        def body(i, ans):
            cand = ans | lax.shift_left(jnp.int32(1), jnp.int32(n_bits - 1) - i)
            below = count(key_ref, lambda k, kb: (k == thr) & (kb * C + row_i < cand))
            return jnp.where(below < need, cand, ans)

        last_tie = lax.fori_loop(0, n_bits, body, jnp.zeros((1, C), jnp.int32))

        def drop_body(kb, carry):
            kk = key_ref[kb]
            key_ref[kb] = jnp.where((kk == thr) & (kb * C + row_i > last_tie), kk - 1, kk)
            return carry

        lax.fori_loop(0, nkb, drop_body, 0)

    neg_inf = jnp.float32(-jnp.inf)
    m_ref[...] = jnp.full(m_ref.shape, neg_inf, jnp.float32)
    acc_ref[...] = jnp.zeros(acc_ref.shape, jnp.float32)
    T = ATTN_KEY_TILE
    ones_rows = jnp.ones((DEN_ROWS, T), jnp.bfloat16)

    kv_group = N_HEADS // N_KV_HEADS

    tiles = [(half, h) for half in range(C // T) for h in range(N_HEADS)]

    def bias_body(kb, carry):
        key_ref[kb] = lax.bitcast_convert_type(jnp.where(key_ref[kb] >= thr, 0.0, neg_inf), jnp.int32)
        return carry

    lax.fori_loop(0, nkb, bias_body, 0)
    n_blocks = seq // C
    key_ref[n_blocks] = lax.bitcast_convert_type(jnp.full((C, C), neg_inf, jnp.float32), jnp.int32)

    def store_logits(kb, dst_ref, n):
        half, h = tiles[n]
        kb_mem = jnp.minimum(kb, nkb - 1)
        kb_bias = jnp.where(kb < nkb, kb, n_blocks)
        r0 = pl.multiple_of(kb_mem * C + half * T, T)
        bias = lax.bitcast_convert_type(key_ref[kb_bias, half * T:(half + 1) * T, :], jnp.float32)
        k2 = kn_ref[0, pl.ds(r0, T), 0:KV_WIDTH]
        qh = qt_ref[0, h * HEAD_DIM:(h + 1) * HEAD_DIM, :]
        rhs = jnp.concatenate([qh, zeros_half] if h < kv_group else [zeros_half, qh], axis=0)
        dst_ref[n] = jnp.dot(k2, rhs, preferred_element_type=jnp.float32) + bias

    def softmax_block(kb, src_ref, kb_next, dst_ref):
        kb_mem = jnp.minimum(kb, nkb - 1)
        for n, (half, h) in enumerate(tiles):
            store_logits(kb_next, dst_ref, n)
            g = h // kv_group
            logit = src_ref[n]
            m_old = m_ref[h:h + 1, :]
            m_new = jnp.maximum(m_old, jnp.max(logit, axis=0, keepdims=True))
            m_safe = jnp.where(m_new == neg_inf, 0.0, m_new)
            p = jnp.exp2(logit - m_safe)
            alpha = jnp.exp2(m_old - m_safe)
            vt = vt_ref[0, kb_mem, g * HEAD_DIM:(g + 1) * HEAD_DIM, half * T:(half + 1) * T]
            pv = jnp.dot(jnp.concatenate([vt, ones_rows], axis=0), p.astype(jnp.bfloat16),
                         preferred_element_type=jnp.float32)
            hs = slice(h * ACC_ROWS, (h + 1) * ACC_ROWS)
            acc_ref[hs, :] = alpha * acc_ref[hs, :] + pv
            m_ref[h:h + 1, :] = m_new

    for n in range(len(tiles)):
        store_logits(0, lga_ref, n)

    def attn_body(pair, carry):
        kb = 2 * pair
        softmax_block(kb, lga_ref, kb + 1, lgb_ref)
        softmax_block(kb + 1, lgb_ref, kb + 2, lga_ref)
        return carry

    lax.fori_loop(0, (nkb + 1) // 2, attn_body, 0)
    outs = []
    for h in range(N_HEADS):
        num = acc_ref[h * ACC_ROWS:h * ACC_ROWS + HEAD_DIM, :]
        den = acc_ref[h * ACC_ROWS + HEAD_DIM:h * ACC_ROWS + HEAD_DIM + 1, :]
        outs.append(num / den)
    o_ref[0] = jnp.transpose(jnp.concatenate(outs, axis=0)).astype(o_ref.dtype)


def _attn_call(qt, qit, wit, kn, vt, top_k):
    B, _, S = qt.shape
    C = ATTN_CHUNK
    nc = S // C
    kern = functools.partial(_attn_kernel, seq=S, top_k=top_k)
    in_specs = [
        pl.BlockSpec((1, Q_WIDTH, C), lambda b, c: (b, 0, c)),
        pl.BlockSpec((1, IDXQ_WIDTH, C), lambda b, c: (b, 0, c)),
        pl.BlockSpec((1, IDX_HEADS, C), lambda b, c: (b, 0, c)),
        pl.BlockSpec((1, S, KN_WIDTH), lambda b, c: (b, 0, 0)),
        pl.BlockSpec((1, nc, KV_WIDTH, C), lambda b, c: (b, 0, 0, 0)),
    ]
    out_specs = pl.BlockSpec((1, C, Q_WIDTH), lambda b, c: (b, c, 0))
    scratch = [
        pltpu.VMEM((nc + 1, C, C), jnp.int32),
        pltpu.VMEM((nc, C, C), jnp.int16),
        pltpu.VMEM((nc, C, C), jnp.int16),
        pltpu.VMEM((N_HEADS * C // ATTN_KEY_TILE, ATTN_KEY_TILE, C), jnp.float32),
        pltpu.VMEM((N_HEADS * C // ATTN_KEY_TILE, ATTN_KEY_TILE, C), jnp.float32),
        pltpu.VMEM((N_HEADS * ACC_ROWS, C), jnp.float32),
        pltpu.VMEM((N_HEADS, C), jnp.float32),
    ]
    vmem = (2 * S * C * 4 + Q_WIDTH * C * 4 + 2 * 2 * (2 * Q_WIDTH * C + S * KN_WIDTH + S * KV_WIDTH + C * Q_WIDTH)
            + 24 * C * C * 4 + (8 << 20))
    return pl.pallas_call(
        kern, grid=(B, nc), in_specs=in_specs, out_specs=out_specs,
        out_shape=jax.ShapeDtypeStruct((B, S, Q_WIDTH), jnp.bfloat16),
        scratch_shapes=scratch, compiler_params=_cparams(2, vmem), name="dsa_attn",
    )(qt, qit, wit, kn, vt)


def _layer_norm(v, g, b):
    mu = jnp.mean(v, axis=-1, keepdims=True)
    d = v - mu
    var = jnp.mean(d * d, axis=-1, keepdims=True)
    return d * lax.rsqrt(var + LN_EPS) * g + b


def _pack_rows(h):
    hi = lax.bitcast_convert_type(h[:, :HALF].astype(jnp.bfloat16).astype(jnp.float32), jnp.int32)
    lo = lax.bitcast_convert_type(h[:, HALF:].astype(jnp.bfloat16).astype(jnp.float32), jnp.int32)
    return (hi & jnp.int32(-65536)) | lax.shift_right_logical(lo, 16)


def _unpack_rows(w):
    hi = lax.bitcast_convert_type(w & jnp.int32(-65536), jnp.float32)
    lo = lax.bitcast_convert_type(lax.shift_left(w, 16), jnp.float32)
    return jnp.concatenate([hi, lo], axis=1)


def _mix_kernel(x_ref, attn_ref, wc_ref, wg_ref, gb_ref, cw_ref, wau_ref, wcu_ref, wo_ref,
                g1_ref, b1_ref, wrh_ref, wrl_ref, rb_ref,
                h_ref, hp_ref, eid_ref, rank_ref, gate_ref, cnt_ref,
                ubuf_ref, base_ref, *, steps_per_seq):
    R = MIX_ROWS
    i = pl.program_id(0)

    @pl.when(i == 0)
    def _():
        base_ref[...] = jnp.zeros_like(base_ref)

    @pl.when(i % steps_per_seq == 0)
    def _():
        ubuf_ref[0:V7X_SUBLANES, :] = jnp.zeros((V7X_SUBLANES, CONV_DIM), jnp.float32)

    x = x_ref[...]
    xb = x.astype(jnp.bfloat16)
    cv = jnp.dot(xb, wc_ref[...], preferred_element_type=jnp.float32)
    u = cv[:, 2 * CONV_DIM:] * cv[:, :CONV_DIM]
    ubuf_ref[V7X_SUBLANES:, :] = u
    u1 = ubuf_ref[V7X_SUBLANES - 1:V7X_SUBLANES - 1 + R, :]
    u2 = ubuf_ref[V7X_SUBLANES - 2:V7X_SUBLANES - 2 + R, :]
    y = cw_ref[0:1, :] * u2 + cw_ref[1:2, :] * u1 + cw_ref[2:3, :] * u
    conv = (cv[:, CONV_DIM:2 * CONV_DIM] * y).astype(jnp.bfloat16)
    ubuf_ref[0:V7X_SUBLANES, :] = u[R - V7X_SUBLANES:, :]
    z = jnp.dot(xb, wg_ref[...], preferred_element_type=jnp.float32) + gb_ref[...]
    gates = 1.0 / (1.0 + jnp.exp(-z))
    au = jnp.dot(attn_ref[...], wau_ref[...], preferred_element_type=jnp.float32)
    cu = jnp.dot(conv, wcu_ref[...], preferred_element_type=jnp.float32)
    merged = gates[:, :D_MODEL] * au + gates[:, D_MODEL:] * cu
    mix = jnp.dot(merged.astype(jnp.bfloat16), wo_ref[...], preferred_element_type=jnp.float32)
    h = _layer_norm(DEEPNORM_ALPHA * x + mix, g1_ref[...], b1_ref[...])
    h_ref[...] = h
    hp_ref[...] = _pack_rows(h)

    h_hi = h.astype(jnp.bfloat16)
    h_lo = (h - h_hi.astype(jnp.float32)).astype(jnp.bfloat16)
    lg = (jnp.dot(h_hi, wrh_ref[...], preferred_element_type=jnp.float32)
          + jnp.dot(h_lo, wrh_ref[...], preferred_element_type=jnp.float32)
          + jnp.dot(h_hi, wrl_ref[...], preferred_element_type=jnp.float32)) + rb_ref[...]
    lane = lax.broadcasted_iota(jnp.int32, (R, ROUTER_LANES), 1).astype(jnp.float32)
    neg = jnp.float32(-jnp.inf)
    no_lane = jnp.float32(ROUTER_LANES)
    gmask = lane < N_GROUPS
    gl = jnp.where(gmask, lg, neg)
    gmax = jnp.max(gl, axis=1, keepdims=True)
    grp = jnp.min(jnp.where(gl == gmax, lane, no_lane), axis=1, keepdims=True)
    gsum = jnp.sum(jnp.where(gmask, jnp.exp(gl - gmax), 0.0), axis=1, keepdims=True)
    p_grp = 1.0 / gsum
    lo_lane = E0 + grp * EXPERTS_PER_GROUP
    emask = (lane >= lo_lane) & (lane < lo_lane + EXPERTS_PER_GROUP)
    el = jnp.where(emask, lg, neg)
    v1 = jnp.max(el, axis=1, keepdims=True)
    i1 = jnp.min(jnp.where(el == v1, lane, no_lane), axis=1, keepdims=True)
    el2 = jnp.where(lane == i1, neg, el)
    v2 = jnp.max(el2, axis=1, keepdims=True)
    i2 = jnp.min(jnp.where(el2 == v2, lane, no_lane), axis=1, keepdims=True)
    a = jnp.exp(v2 - v1)
    inv = 1.0 / (1.0 + a)
    gate_ref[:, 0:1] = p_grp * inv
    gate_ref[:, 1:2] = p_grp * (a * inv)
    eid_ref[:, 0:1] = (i1 - E0).astype(jnp.int32)
    eid_ref[:, 1:2] = (i2 - E0).astype(jnp.int32)
    oh1 = lane == i1
    oh2 = lane == i2
    oh = (oh1 | oh2).astype(jnp.bfloat16)
    r_i = lax.broadcasted_iota(jnp.int32, (R, R), 0)
    c_i = lax.broadcasted_iota(jnp.int32, (R, R), 1)
    tri = (r_i > c_i).astype(jnp.bfloat16)
    before = jnp.dot(tri, oh, preferred_element_type=jnp.float32) + base_ref[...]
    rank_ref[:, 0:1] = jnp.sum(jnp.where(oh1, before, 0.0), axis=1, keepdims=True).astype(jnp.int32)
    rank_ref[:, 1:2] = jnp.sum(jnp.where(oh2, before, 0.0), axis=1, keepdims=True).astype(jnp.int32)
    base_ref[...] = base_ref[...] + jnp.sum(oh.astype(jnp.float32), axis=0, keepdims=True)
    cnt_ref[...] = base_ref[...]


def _mix_call(x2, attn2, wc, wg, gb, cw, wau, wcu, wo, g1, b1, wrh, wrl, rb, seq):
    N, D = x2.shape
    R = MIX_ROWS
    nt = N // R
    kern = functools.partial(_mix_kernel, steps_per_seq=seq // R)

    def full(a):
        return pl.BlockSpec(a.shape, lambda i: (0,) * a.ndim)

    in_specs = [
        pl.BlockSpec((R, D), lambda i: (i, 0)),
        pl.BlockSpec((R, Q_WIDTH), lambda i: (i, 0)),
        full(wc), full(wg), full(gb), full(cw), full(wau), full(wcu), full(wo),
        full(g1), full(b1), full(wrh), full(wrl), full(rb),
    ]
    out_shape = (
        jax.ShapeDtypeStruct((N, D), jnp.float32),
        jax.ShapeDtypeStruct((N, HALF), jnp.int32),
        jax.ShapeDtypeStruct((N, 2), jnp.int32),
        jax.ShapeDtypeStruct((N, 2), jnp.int32),
        jax.ShapeDtypeStruct((N, 2), jnp.float32),
        jax.ShapeDtypeStruct((1, ROUTER_LANES), jnp.float32),
    )
    out_specs = (
        pl.BlockSpec((R, D), lambda i: (i, 0)),
        pl.BlockSpec((R, HALF), lambda i: (i, 0)),
        pl.BlockSpec((R, 2), lambda i: (i, 0)),
        pl.BlockSpec((R, 2), lambda i: (i, 0)),
        pl.BlockSpec((R, 2), lambda i: (i, 0)),
        pl.BlockSpec((1, ROUTER_LANES), lambda i: (0, 0)),
    )
    scratch = [pltpu.VMEM((R + V7X_SUBLANES, CONV_DIM), jnp.float32),
               pltpu.VMEM((1, ROUTER_LANES), jnp.float32)]
    w_bytes = 2 * (wc.size + wg.size + wau.size + wcu.size + wo.size + wrh.size + wrl.size)
    vmem = 2 * w_bytes + 2 * (R * D * 4 * 2 + R * Q_WIDTH * 2 + R * HALF * 4) + 10 * R * 2048 * 4 + (6 << 20)
    return pl.pallas_call(
        kern, grid=(nt,), in_specs=in_specs, out_specs=out_specs, out_shape=out_shape,
        scratch_shapes=scratch, compiler_params=_cparams(1, vmem), name="mix_ln_router",
    )(x2, attn2, wc, wg, gb, cw, wau, wcu, wo, g1, b1, wrh, wrl, rb)


def _row_copy(src_ref, s, dst_ref, d, sem):
    return pltpu.make_async_copy(src_ref.at[pl.ds(s, 1), :], dst_ref.at[pl.ds(d, 1), :], sem)


def _dispatch_kernel(dest_ref, hp_ref, xs_in_ref, xs_ref, sem):
    del xs_in_ref
    R = MOVE_ROWS

    def issue(r, carry):
        _row_copy(hp_ref, r, xs_ref, dest_ref[0, 0, 2 * r], sem).start(priority=0)
        _row_copy(hp_ref, r, xs_ref, dest_ref[0, 0, 2 * r + 1], sem).start(priority=1)
        return carry

    lax.fori_loop(0, R, issue, 0, unroll=ISSUE_UNROLL)
    for _ in range(2):
        pltpu.make_async_copy(hp_ref, xs_ref.at[pl.ds(0, R), :], sem).wait()


def _dispatch_call(dest3, hp, xs0):
    N, W = hp.shape
    R = MOVE_ROWS
    nt = N // R
    return pl.pallas_call(
        _dispatch_kernel, grid=(nt,),
        in_specs=[
            pl.BlockSpec((1, 1, 2 * R), lambda i: (i, 0, 0), memory_space=pltpu.SMEM),
            pl.BlockSpec((R, W), lambda i: (i, 0)),
            pl.BlockSpec(memory_space=pl.ANY),
        ],
        out_specs=pl.BlockSpec(memory_space=pl.ANY),
        out_shape=jax.ShapeDtypeStruct(xs0.shape, xs0.dtype),
        scratch_shapes=[pltpu.SemaphoreType.DMA],
        input_output_aliases={2: 0},
        compiler_params=_cparams(1, 4 * R * W * 4 + (4 << 20)), name="moe_dispatch",
    )(dest3, hp, xs0)


def _expert_kernel(be_ref, nb_ref, xs_ref, wg_ref, wu_ref, wd_ref, ys_ref, wgb_ref, wub_ref, wdb_ref):
    j = pl.program_id(0)

    @pl.when((j == 0) | (be_ref[j] != be_ref[jnp.maximum(j - 1, 0)]))
    def _():
        wgb_ref[...] = wg_ref[0].astype(jnp.bfloat16)
        wub_ref[...] = wu_ref[0].astype(jnp.bfloat16)
        wdb_ref[...] = wd_ref[0].astype(jnp.bfloat16)

    @pl.when(j < nb_ref[0])
    def _():
        xb = _unpack_rows(xs_ref[...]).astype(jnp.bfloat16)
        hg = jnp.dot(xb, wgb_ref[...], preferred_element_type=jnp.float32)
        hu = jnp.dot(xb, wub_ref[...], preferred_element_type=jnp.float32)
        hid = (hg / (1.0 + jnp.exp(-hg))) * hu
        y = jnp.dot(hid.astype(jnp.bfloat16), wdb_ref[...], preferred_element_type=jnp.float32)
        ys_ref[...] = _pack_rows(y)

    @pl.when(j >= nb_ref[0])
    def _():
        ys_ref[...] = jnp.zeros_like(ys_ref)


def _expert_call(block_expert, n_used, xs, wg, wu, wd):
    cap, W = xs.shape
    RB = EXPERT_ROWS
    nb = cap // RB
    grid_spec = pltpu.PrefetchScalarGridSpec(
        num_scalar_prefetch=2, grid=(nb,),
        in_specs=[
            pl.BlockSpec((RB, W), lambda j, be, nu: (j, 0)),
            pl.BlockSpec((1, D_MODEL, D_EXPERT), lambda j, be, nu: (be[j], 0, 0)),
            pl.BlockSpec((1, D_MODEL, D_EXPERT), lambda j, be, nu: (be[j], 0, 0)),
            pl.BlockSpec((1, D_EXPERT, D_MODEL), lambda j, be, nu: (be[j], 0, 0)),
        ],
        out_specs=pl.BlockSpec((RB, W), lambda j, be, nu: (j, 0)),
        scratch_shapes=[pltpu.VMEM((D_MODEL, D_EXPERT), jnp.bfloat16),
                        pltpu.VMEM((D_MODEL, D_EXPERT), jnp.bfloat16),
                        pltpu.VMEM((D_EXPERT, D_MODEL), jnp.bfloat16)],
    )
    vmem = (2 * 4 + 2) * 3 * D_MODEL * D_EXPERT + 4 * RB * W * 4 + 8 * RB * D_MODEL * 4 + (6 << 20)
    return pl.pallas_call(
        _expert_kernel, grid_spec=grid_spec,
        out_shape=jax.ShapeDtypeStruct((cap, W), jnp.int32),
        compiler_params=_cparams(1, vmem), name="moe_experts",
    )(block_expert, n_used, xs, wg, wu, wd)


def _combine_kernel(dest_ref, h_ref, gate_ref, g2_ref, b2_ref, ys_ref, o_ref, buf_ref, sem):
    R = MOVE_ROWS

    def issue(r, carry):
        _row_copy(ys_ref, dest_ref[0, 0, 2 * r], buf_ref.at[0], r, sem).start(priority=0)
        _row_copy(ys_ref, dest_ref[0, 0, 2 * r + 1], buf_ref.at[1], r, sem).start(priority=1)
        return carry

    lax.fori_loop(0, R, issue, 0, unroll=ISSUE_UNROLL)
    for slot in range(2):
        pltpu.make_async_copy(ys_ref.at[pl.ds(0, R), :], buf_ref.at[slot], sem).wait()
    y0 = _unpack_rows(buf_ref[0])
    y1 = _unpack_rows(buf_ref[1])
    ffn = gate_ref[:, 0:1] * y0 + gate_ref[:, 1:2] * y1
    o_ref[...] = _layer_norm(DEEPNORM_ALPHA * h_ref[...] + ffn, g2_ref[...], b2_ref[...])


def _combine_call(dest3, h, gate, g2, b2, ys):
    N, D = h.shape
    W = ys.shape[1]
    R = MOVE_ROWS
    nt = N // R
    return pl.pallas_call(
        _combine_kernel, grid=(nt,),
        in_specs=[
            pl.BlockSpec((1, 1, 2 * R), lambda i: (i, 0, 0), memory_space=pltpu.SMEM),
            pl.BlockSpec((R, D), lambda i: (i, 0)),
            pl.BlockSpec((R, 2), lambda i: (i, 0)),
            pl.BlockSpec((1, D), lambda i: (0, 0)),
            pl.BlockSpec((1, D), lambda i: (0, 0)),
            pl.BlockSpec(memory_space=pl.ANY),
        ],
        out_specs=pl.BlockSpec((R, D), lambda i: (i, 0)),
        out_shape=jax.ShapeDtypeStruct((N, D), jnp.float32),
        scratch_shapes=[pltpu.VMEM((2, R, W), jnp.int32), pltpu.SemaphoreType.DMA],
        compiler_params=_cparams(1, 4 * R * D * 4 + 2 * R * W * 4 + 8 * R * D * 4 + (4 << 20)),
        name="moe_combine",
    )(dest3, h, gate, g2, b2, ys)


def _rope_tables(seq):
    inv_freq = ROPE_THETA ** (-jnp.arange(ROT_HALF, dtype=jnp.float32) / ROT_HALF)
    ang = jnp.arange(seq, dtype=jnp.int32).astype(jnp.float32)[:, None] * inv_freq[None, :]
    cos, sin = jnp.cos(ang), jnp.sin(ang)
    ones = jnp.ones((seq, HEAD_DIM - ROT_DIMS), jnp.float32)
    zeros = jnp.zeros((seq, HEAD_DIM - ROT_DIMS), jnp.float32)
    c_head = jnp.concatenate([cos, cos, ones], axis=1)
    s_head = jnp.concatenate([-sin, sin, zeros], axis=1)
    reps = KN_WIDTH // HEAD_DIM
    return cos.T, sin.T, jnp.tile(c_head, (1, reps)), jnp.tile(s_head, (1, reps))


def _swap_rot_cols(w):
    d, n = w.shape
    wh = w.reshape(d, n // HEAD_DIM, HEAD_DIM)
    sw = jnp.concatenate([wh[:, :, ROT_HALF:ROT_DIMS], wh[:, :, :ROT_HALF],
                          jnp.zeros((d, n // HEAD_DIM, HEAD_DIM - ROT_DIMS), w.dtype)], axis=2)
    return sw.reshape(d, n)


def _block(x, w_in, gate_bias, w_attn_up, w_conv_up, conv_w, w_out, ln_g, ln_b,
           rg_w, rg_b, re_w, re_b, w_gate_e, w_up_e, w_down_e, ln2_g, ln2_b):
    B, S, D = x.shape
    N = B * S
    top_k = min(TOPK_MAX, S // 4)
    bf = jnp.bfloat16
    o = np.cumsum([0, Q_WIDTH, KV_WIDTH, KV_WIDTH, IDXQ_WIDTH, IDX_DIM, IDX_HEADS,
                   CONV_DIM, CONV_DIM, CONV_DIM, N_BRANCHES * D_MODEL])
    w_q, w_k, w_v, w_qi, w_ki, w_wi = (w_in[:, o[i]:o[i + 1]] for i in range(6))
    w_conv = w_in[:, o[6]:o[9]]
    w_gates = w_in[:, o[9]:o[10]]

    wt = jnp.concatenate([w_q.T * (HEAD_DIM ** -0.5 * LOG2_E), w_qi.T, w_v.T, w_wi.T,
                          jnp.zeros((T_ROWS - T_WI0 - IDX_HEADS, D), w_in.dtype)], axis=0).astype(bf)
    pad = jnp.zeros((D, KN_WIDTH - KV_WIDTH - IDX_DIM), w_in.dtype)
    wn = jnp.concatenate([w_k, w_ki, pad, _swap_rot_cols(w_k), _swap_rot_cols(w_ki), pad], axis=1).astype(bf)
    cos_t, sin_t, cos_n, sgn_n = _rope_tables(S)

    qt, qit, vt, wit, kn = _proj_call(x, wt, wn, cos_t, sin_t, cos_n, sgn_n)
    attn = _attn_call(qt, qit, wit, kn, vt, top_k)

    w_r = jnp.concatenate([rg_w, jnp.transpose(re_w, (1, 0, 2)).reshape(D, N_EXPERTS),
                           jnp.zeros((D, ROUTER_LANES - E0 - N_EXPERTS), rg_w.dtype)], axis=1)
    b_r = jnp.concatenate([rg_b, re_b.reshape(-1),
                           jnp.zeros((ROUTER_LANES - E0 - N_EXPERTS,), rg_b.dtype)])[None, :]
    w_rh = w_r.astype(bf)
    w_rl = (w_r - w_rh.astype(jnp.float32)).astype(bf)

    h, hp, eid, rank, gate, cnt = _mix_call(
        x.reshape(N, D), attn.reshape(N, Q_WIDTH), w_conv.astype(bf), w_gates.astype(bf),
        gate_bias[None, :], conv_w, w_attn_up.astype(bf), w_conv_up.astype(bf), w_out.astype(bf),
        ln_g[None, :], ln_b[None, :], w_rh, w_rl, b_r, S)

    RB = EXPERT_ROWS
    counts = cnt[0, E0:E0 + N_EXPERTS].astype(jnp.int32)
    padded = ((counts + RB - 1) // RB) * RB
    pad_end = jnp.cumsum(padded)
    pad_start = pad_end - padded
    dest = jnp.sum(jnp.where(eid[:, :, None] == jnp.arange(N_EXPERTS, dtype=jnp.int32)[None, None, :],
                             pad_start[None, None, :], 0), axis=-1) + rank
    cap = N * 2 + N_EXPERTS * RB
    nb = cap // RB
    block_start = jnp.arange(nb, dtype=jnp.int32) * RB
    block_expert = jnp.minimum(jnp.sum((block_start[:, None] >= pad_end[None, :]).astype(jnp.int32), axis=1),
                               N_EXPERTS - 1)
    n_used = (pad_end[-1:] // RB).astype(jnp.int32)
    dest3 = dest.reshape(N // MOVE_ROWS, 1, 2 * MOVE_ROWS)

    xs = _dispatch_call(dest3, hp, jnp.zeros((cap, HALF), jnp.int32))
    ys = _expert_call(block_expert, n_used, xs, w_gate_e, w_up_e, w_down_e)
    out = _combine_call(dest3, h, gate, ln2_g[None, :], ln2_b[None, :], ys)
    return out.reshape(B, S, D)


def kernel(x, w_in, gate_bias, w_attn_up, w_conv_up, conv_w, w_out, ln1_g, ln1_b, router_group_w,
           router_group_b, router_expert_w, router_expert_b, w_gate_e, w_up_e, w_down_e, ln2_g, ln2_b):
    h = x
    for l in range(DEPTH):
        h = _block(h, w_in[l], gate_bias[l], w_attn_up[l], w_conv_up[l], conv_w[l], w_out[l],
                   ln1_g[l], ln1_b[l], router_group_w[l], router_group_b[l], router_expert_w[l],
                   router_expert_b[l], w_gate_e[l], w_up_e[l], w_down_e[l], ln2_g[l], ln2_b[l])
    return h
```

```python
import functools

import jax
import jax.numpy as jnp
import numpy as np
from jax import lax
from jax.experimental import pallas as pl
from jax.experimental.pallas import tpu as pltpu

D_MODEL = 1024
N_HEADS = 8
N_KV_HEADS = 2
HEAD_DIM = 64
Q_WIDTH = N_HEADS * HEAD_DIM
KV_WIDTH = N_KV_HEADS * HEAD_DIM
ROPE_THETA = 500000.0
ROT_DIMS = HEAD_DIM // 4
ROT_HALF = ROT_DIMS // 2
IDX_HEADS = 8
IDX_DIM = 64
IDXQ_WIDTH = IDX_HEADS * IDX_DIM
TOPK_MAX = 256
CONV_DIM = 512
CONV_WIDTH = 3
N_BRANCHES = 2
N_GROUPS = 4
EXPERTS_PER_GROUP = 8
N_EXPERTS = N_GROUPS * EXPERTS_PER_GROUP
D_EXPERT = 512
LN_EPS = 1e-5
DEPTH = 1
DEEPNORM_ALPHA = (2 * DEPTH) ** 0.25

V7X_LANES = 128
V7X_SUBLANES = 8
V7X_VMEM_LIMIT_BYTES = 56 * 1024 * 1024

PROJ_ROWS = 512
ATTN_CHUNK = 256
ATTN_KEY_TILE = 128
DEN_ROWS = 16
ACC_ROWS = HEAD_DIM + DEN_ROWS
LOG2_E = 1.4426950408889634
MIX_ROWS = 512
EXPERT_ROWS = 256
MOVE_ROWS = 512
ISSUE_UNROLL = 8
HALF = D_MODEL // 2

T_Q0, T_QI0, T_V0, T_WI0 = 0, Q_WIDTH, Q_WIDTH + IDXQ_WIDTH, Q_WIDTH + IDXQ_WIDTH + KV_WIDTH
T_ROWS = T_WI0 + 16
KN_WIDTH = 256

INT_MIN = -2147483648
HALF_RANGE = 32768
ROUTER_LANES = 128
E0 = N_GROUPS


def _cparams(n_axes, vmem_bytes):
    return pltpu.CompilerParams(
        dimension_semantics=("arbitrary",) * n_axes,
        vmem_limit_bytes=int(min(vmem_bytes, V7X_VMEM_LIMIT_BYTES)),
    )


def _proj_kernel(x_ref, wt_ref, wn_ref, cost_ref, sint_ref, cosn_ref, sgnn_ref,
                 qt_ref, qit_ref, vt_ref, wit_ref, kn_ref):
    xb = x_ref[0].astype(jnp.bfloat16)
    pt = lax.dot_general(wt_ref[...], xb, (((1,), (1,)), ((), ())),
                         preferred_element_type=jnp.float32)
    cos = cost_ref[...]
    sin = sint_ref[...]

    def rope_t(dst_ref, base):
        for h in range(N_HEADS):
            r0 = base + h * HEAD_DIM
            x1 = pt[r0:r0 + ROT_HALF]
            x2 = pt[r0 + ROT_HALF:r0 + ROT_DIMS]
            head = jnp.concatenate([x1 * cos - x2 * sin, x2 * cos + x1 * sin,
                                    pt[r0 + ROT_DIMS:r0 + HEAD_DIM]], axis=0)
            dst_ref[0, h * HEAD_DIM:(h + 1) * HEAD_DIM, :] = head.astype(dst_ref.dtype)

    rope_t(qt_ref, T_Q0)
    rope_t(qit_ref, T_QI0)
    for jj in range(PROJ_ROWS // ATTN_CHUNK):
        vt_ref[0, jj] = pt[T_V0:T_V0 + KV_WIDTH, jj * ATTN_CHUNK:(jj + 1) * ATTN_CHUNK].astype(vt_ref.dtype)
    wit_ref[0] = pt[T_WI0:T_WI0 + IDX_HEADS]
    pn = jnp.dot(xb, wn_ref[...], preferred_element_type=jnp.float32)
    kn = pn[:, :KN_WIDTH] * cosn_ref[...] + pn[:, KN_WIDTH:] * sgnn_ref[...]
    kn_ref[0] = kn.astype(kn_ref.dtype)


def _proj_call(x, wt, wn, cos_t, sin_t, cos_n, sgn_n):
    B, S, D = x.shape
    R = PROJ_ROWS
    nt = S // R
    grid = (B, nt)
    out_shape = (
        jax.ShapeDtypeStruct((B, Q_WIDTH, S), jnp.bfloat16),
        jax.ShapeDtypeStruct((B, IDXQ_WIDTH, S), jnp.bfloat16),
        jax.ShapeDtypeStruct((B, S // ATTN_CHUNK, KV_WIDTH, ATTN_CHUNK), jnp.bfloat16),
        jax.ShapeDtypeStruct((B, IDX_HEADS, S), jnp.float32),
        jax.ShapeDtypeStruct((B, S, KN_WIDTH), jnp.bfloat16),
    )
    in_specs = [
        pl.BlockSpec((1, R, D), lambda b, j: (b, j, 0)),
        pl.BlockSpec((T_ROWS, D), lambda b, j: (0, 0)),
        pl.BlockSpec((D, 2 * KN_WIDTH), lambda b, j: (0, 0)),
        pl.BlockSpec((ROT_HALF, R), lambda b, j: (0, j)),
        pl.BlockSpec((ROT_HALF, R), lambda b, j: (0, j)),
        pl.BlockSpec((R, KN_WIDTH), lambda b, j: (j, 0)),
        pl.BlockSpec((R, KN_WIDTH), lambda b, j: (j, 0)),
    ]
    out_specs = (
        pl.BlockSpec((1, Q_WIDTH, R), lambda b, j: (b, 0, j)),
        pl.BlockSpec((1, IDXQ_WIDTH, R), lambda b, j: (b, 0, j)),
        pl.BlockSpec((1, R // ATTN_CHUNK, KV_WIDTH, ATTN_CHUNK), lambda b, j: (b, j, 0, 0)),
        pl.BlockSpec((1, IDX_HEADS, R), lambda b, j: (b, 0, j)),
        pl.BlockSpec((1, R, KN_WIDTH), lambda b, j: (b, j, 0)),
    )
    vmem = 2 * (R * D * 4 + T_ROWS * D * 2 + D * 2 * KN_WIDTH * 2) + 6 * T_ROWS * R * 4 + (8 << 20)
    return pl.pallas_call(
        _proj_kernel, grid=grid, in_specs=in_specs, out_specs=out_specs, out_shape=out_shape,
        compiler_params=_cparams(2, vmem), name="dsa_proj",
    )(x, wt, wn, cos_t, sin_t, cos_n, sgn_n)


def _float_to_key(s):
    b = lax.bitcast_convert_type(s, jnp.int32)
    k = b ^ (lax.shift_right_arithmetic(b, 31) & jnp.int32(0x7FFFFFFF))
    return jnp.where(b == jnp.int32(INT_MIN), jnp.int32(0), k)


def _attn_kernel(qt_ref, qit_ref, wit_ref, kn_ref, vt_ref, o_ref,
                 key_ref, hi_ref, lo_ref, lga_ref, lgb_ref, acc_ref, m_ref, *, seq, top_k):
    C = ATTN_CHUNK
    c = pl.program_id(1)
    nkb = c + 1
    zeros_half = jnp.zeros((HEAD_DIM, C), jnp.bfloat16)
    row_i = lax.broadcasted_iota(jnp.int32, (C, C), 0)
    lane_i = lax.broadcasted_iota(jnp.int32, (C, C), 1)
    causal_in_block = row_i <= lane_i

    def score_body(kb, carry):
        r0 = pl.multiple_of(kb * C, C)
        kix = kn_ref[0, pl.ds(r0, C), KV_WIDTH:KN_WIDTH]
        acc = None
        for h in range(IDX_HEADS):
            rhs = jnp.concatenate([qit_ref[0, h * IDX_DIM:(h + 1) * IDX_DIM, :], zeros_half], axis=0)
            s = jnp.dot(kix, rhs, preferred_element_type=jnp.float32)
            t = jnp.maximum(s, 0.0) * wit_ref[0, h:h + 1, :]
            acc = t if acc is None else acc + t
        score = acc * (IDX_DIM ** -0.5 * IDX_HEADS ** -0.5)
        keys = jnp.where(causal_in_block | (kb != c), _float_to_key(score), jnp.int32(INT_MIN))
        key_ref[kb] = keys
        hi_ref[kb] = lax.shift_right_arithmetic(keys, 16).astype(jnp.int16)
        lo_ref[kb] = ((keys & jnp.int32(0xFFFF)) - jnp.int32(HALF_RANGE)).astype(jnp.int16)
        return carry

    lax.fori_loop(0, nkb, score_body, 0)

    def count(ref, pred):
        packed = ref.dtype == jnp.int16

        def body(kb, part):
            hit = pred(ref[kb], kb)
            if packed:
                words = pltpu.bitcast(jnp.where(hit, jnp.int16(1), jnp.int16(0)), jnp.int32)
            else:
                words = hit.astype(jnp.int32)
            return part + jnp.sum(words.reshape(-1, V7X_SUBLANES, C), axis=0)

        part = lax.fori_loop(0, nkb, body, jnp.zeros((V7X_SUBLANES, C), jnp.int32))
        if packed:
            part = (part & jnp.int32(0xFFFF)) + lax.shift_right_logical(part, 16)
        return jnp.sum(part, axis=0, keepdims=True)

    def kth_largest_16(ref):
        def body(i, ans_u):
            cand_u = ans_u | lax.shift_left(jnp.int32(1), jnp.int32(15) - i)
            cand = (cand_u - jnp.int32(HALF_RANGE)).astype(jnp.int16)
            return jnp.where(count(ref, lambda k, kb: k >= cand) >= top_k, cand_u, ans_u)

        return lax.fori_loop(0, 16, body, jnp.zeros((1, C), jnp.int32)) - jnp.int32(HALF_RANGE)

    t_hi = kth_largest_16(hi_ref)
    t_hi16 = t_hi.astype(jnp.int16)

    def refine_body(kb, carry):
        hi = hi_ref[kb]
        lo_ref[kb] = jnp.where(hi > t_hi16, jnp.int16(HALF_RANGE - 1),
                               jnp.where(hi == t_hi16, lo_ref[kb], jnp.int16(-HALF_RANGE)))
        return carry

    lax.fori_loop(0, nkb, refine_body, 0)
    t_lo = kth_largest_16(lo_ref)
    thr_raw = t_hi * jnp.int32(2 * HALF_RANGE) + (t_lo + jnp.int32(HALF_RANGE))
    thr = jnp.maximum(thr_raw, jnp.int32(INT_MIN + 1))

    t_lo16 = t_lo.astype(jnp.int16)
    n_ge = count(lo_ref, lambda k, kb: k >= t_lo16)
    has_ties = jnp.max(jnp.where(thr_raw != jnp.int32(INT_MIN), n_ge, 0)) > top_k

    @pl.when(has_ties)
    def _():
        need = top_k - count(key_ref, lambda k, kb: k > thr)
        n_bits = int(seq - 1).bit_length()

        def body(i, ans):
            cand = ans | lax.shift_left(jnp.int32(1), jnp.int32(n_bits - 1) - i)
            below = count(key_ref, lambda k, kb: (k == thr) & (kb * C + row_i < cand))
            return jnp.where(below < need, cand, ans)

        last_tie = lax.fori_loop(0, n_bits, body, jnp.zeros((1, C), jnp.int32))

        def drop_body(kb, carry):
            kk = key_ref[kb]
            key_ref[kb] = jnp.where((kk == thr) & (kb * C + row_i > last_tie), kk - 1, kk)
            return carry

        lax.fori_loop(0, nkb, drop_body, 0)

    neg_inf = jnp.float32(-jnp.inf)
    m_ref[...] = jnp.full(m_ref.shape, neg_inf, jnp.float32)
    acc_ref[...] = jnp.zeros(acc_ref.shape, jnp.float32)
    T = ATTN_KEY_TILE
    ones_rows = jnp.ones((DEN_ROWS, C), jnp.bfloat16)

    kv_group = N_HEADS // N_KV_HEADS

    def bias_body(kb, carry):
        key_ref[kb] = lax.bitcast_convert_type(jnp.where(key_ref[kb] >= thr, 0.0, neg_inf), jnp.int32)
        return carry

    lax.fori_loop(0, nkb, bias_body, 0)
    n_blocks = seq // C
    key_ref[n_blocks] = lax.bitcast_convert_type(jnp.full((C, C), neg_inf, jnp.float32), jnp.int32)

    def store_logits(kb, dst_ref, h):
        kb_mem = jnp.minimum(kb, nkb - 1)
        kb_bias = jnp.where(kb < nkb, kb, n_blocks)
        r0 = pl.multiple_of(kb_mem * C, C)
        bias = lax.bitcast_convert_type(key_ref[kb_bias], jnp.float32)
        k2 = kn_ref[0, pl.ds(r0, C), 0:KV_WIDTH]
        qh = qt_ref[0, h * HEAD_DIM:(h + 1) * HEAD_DIM, :]
        rhs = jnp.concatenate([qh, zeros_half] if h < kv_group else [zeros_half, qh], axis=0)
        dst_ref[h] = jnp.dot(k2, rhs, preferred_element_type=jnp.float32) + bias

    def softmax_block(kb, src_ref, kb_next, dst_ref):
        kb_mem = jnp.minimum(kb, nkb - 1)
        for h in range(N_HEADS):
            store_logits(kb_next, dst_ref, h)
            g = h // kv_group
            m_old = m_ref[h:h + 1, :]
            m_new = jnp.maximum(m_old, jnp.max(src_ref[h], axis=0, keepdims=True))
            m_safe = jnp.where(m_new == neg_inf, 0.0, m_new)
            p = jnp.concatenate(
                [jnp.exp2(src_ref[h, s * T:(s + 1) * T, :] - m_safe).astype(jnp.bfloat16) for s in range(C // T)],
                axis=0)
            alpha = jnp.exp2(m_old - m_safe)
            vt = vt_ref[0, kb_mem, g * HEAD_DIM:(g + 1) * HEAD_DIM, :]
            pv = jnp.dot(jnp.concatenate([vt, ones_rows], axis=0), p,
                         preferred_element_type=jnp.float32)
            hs = slice(h * ACC_ROWS, (h + 1) * ACC_ROWS)
            acc_ref[hs, :] = alpha * acc_ref[hs, :] + pv
            m_ref[h:h + 1, :] = m_new

    for h in range(N_HEADS):
        store_logits(0, lga_ref, h)

    def attn_body(pair, carry):
        kb = 2 * pair
        softmax_block(kb, lga_ref, kb + 1, lgb_ref)
        softmax_block(kb + 1, lgb_ref, kb + 2, lga_ref)
        return carry

    lax.fori_loop(0, (nkb + 1) // 2, attn_body, 0)
    outs = []
    for h in range(N_HEADS):
        num = acc_ref[h * ACC_ROWS:h * ACC_ROWS + HEAD_DIM, :]
        den = acc_ref[h * ACC_ROWS + HEAD_DIM:h * ACC_ROWS + HEAD_DIM + 1, :]
        outs.append(num / den)
    o_ref[0] = jnp.transpose(jnp.concatenate(outs, axis=0)).astype(o_ref.dtype)


def _attn_call(qt, qit, wit, kn, vt, top_k):
    B, _, S = qt.shape
    C = ATTN_CHUNK
    nc = S // C
    kern = functools.partial(_attn_kernel, seq=S, top_k=top_k)
    in_specs = [
        pl.BlockSpec((1, Q_WIDTH, C), lambda b, c: (b, 0, c)),
        pl.BlockSpec((1, IDXQ_WIDTH, C), lambda b, c: (b, 0, c)),
        pl.BlockSpec((1, IDX_HEADS, C), lambda b, c: (b, 0, c)),
        pl.BlockSpec((1, S, KN_WIDTH), lambda b, c: (b, 0, 0)),
        pl.BlockSpec((1, nc, KV_WIDTH, C), lambda b, c: (b, 0, 0, 0)),
    ]
    out_specs = pl.BlockSpec((1, C, Q_WIDTH), lambda b, c: (b, c, 0))
    scratch = [
        pltpu.VMEM((nc + 1, C, C), jnp.int32),
        pltpu.VMEM((nc, C, C), jnp.int16),
        pltpu.VMEM((nc, C, C), jnp.int16),
        pltpu.VMEM((N_HEADS, C, C), jnp.float32),
        pltpu.VMEM((N_HEADS, C, C), jnp.float32),
        pltpu.VMEM((N_HEADS * ACC_ROWS, C), jnp.float32),
        pltpu.VMEM((N_HEADS, C), jnp.float32),
    ]
    vmem = (2 * S * C * 4 + Q_WIDTH * C * 4 + 2 * 2 * (2 * Q_WIDTH * C + S * KN_WIDTH + S * KV_WIDTH + C * Q_WIDTH)
            + 24 * C * C * 4 + (8 << 20))
    return pl.pallas_call(
        kern, grid=(B, nc), in_specs=in_specs, out_specs=out_specs,
        out_shape=jax.ShapeDtypeStruct((B, S, Q_WIDTH), jnp.bfloat16),
        scratch_shapes=scratch, compiler_params=_cparams(2, vmem), name="dsa_attn",
    )(qt, qit, wit, kn, vt)


def _layer_norm(v, g, b):
    mu = jnp.mean(v, axis=-1, keepdims=True)
    d = v - mu
    var = jnp.mean(d * d, axis=-1, keepdims=True)
    return d * lax.rsqrt(var + LN_EPS) * g + b


def _pack_rows(h):
    hi = lax.bitcast_convert_type(h[:, :HALF].astype(jnp.bfloat16).astype(jnp.float32), jnp.int32)
    lo = lax.bitcast_convert_type(h[:, HALF:].astype(jnp.bfloat16).astype(jnp.float32), jnp.int32)
    return (hi & jnp.int32(-65536)) | lax.shift_right_logical(lo, 16)


def _unpack_rows(w):
    hi = lax.bitcast_convert_type(w & jnp.int32(-65536), jnp.float32)
    lo = lax.bitcast_convert_type(lax.shift_left(w, 16), jnp.float32)
    return jnp.concatenate([hi, lo], axis=1)


def _mix_kernel(x_ref, attn_ref, wc_ref, wg_ref, gb_ref, cw_ref, wau_ref, wcu_ref, wo_ref,
                g1_ref, b1_ref, wrh_ref, wrl_ref, rb_ref,
                h_ref, hp_ref, eid_ref, rank_ref, gate_ref, cnt_ref,
                ubuf_ref, base_ref, *, steps_per_seq):
    R = MIX_ROWS
    i = pl.program_id(0)

    @pl.when(i == 0)
    def _():
        base_ref[...] = jnp.zeros_like(base_ref)

    @pl.when(i % steps_per_seq == 0)
    def _():
        ubuf_ref[0:V7X_SUBLANES, :] = jnp.zeros((V7X_SUBLANES, CONV_DIM), jnp.float32)

    x = x_ref[...]
    xb = x.astype(jnp.bfloat16)
    cv = jnp.dot(xb, wc_ref[...], preferred_element_type=jnp.float32)
    u = cv[:, 2 * CONV_DIM:] * cv[:, :CONV_DIM]
    ubuf_ref[V7X_SUBLANES:, :] = u
    u1 = ubuf_ref[V7X_SUBLANES - 1:V7X_SUBLANES - 1 + R, :]
    u2 = ubuf_ref[V7X_SUBLANES - 2:V7X_SUBLANES - 2 + R, :]
    y = cw_ref[0:1, :] * u2 + cw_ref[1:2, :] * u1 + cw_ref[2:3, :] * u
    conv = (cv[:, CONV_DIM:2 * CONV_DIM] * y).astype(jnp.bfloat16)
    ubuf_ref[0:V7X_SUBLANES, :] = u[R - V7X_SUBLANES:, :]
    z = jnp.dot(xb, wg_ref[...], preferred_element_type=jnp.float32) + gb_ref[...]
    gates = 1.0 / (1.0 + jnp.exp(-z))
    au = jnp.dot(attn_ref[...], wau_ref[...], preferred_element_type=jnp.float32)
    cu = jnp.dot(conv, wcu_ref[...], preferred_element_type=jnp.float32)
    merged = gates[:, :D_MODEL] * au + gates[:, D_MODEL:] * cu
    mix = jnp.dot(merged.astype(jnp.bfloat16), wo_ref[...], preferred_element_type=jnp.float32)
    h = _layer_norm(DEEPNORM_ALPHA * x + mix, g1_ref[...], b1_ref[...])
    h_ref[...] = h
    hp_ref[...] = _pack_rows(h)

    h_hi = h.astype(jnp.bfloat16)
    h_lo = (h - h_hi.astype(jnp.float32)).astype(jnp.bfloat16)
    lg = (jnp.dot(h_hi, wrh_ref[...], preferred_element_type=jnp.float32)
          + jnp.dot(h_lo, wrh_ref[...], preferred_element_type=jnp.float32)
          + jnp.dot(h_hi, wrl_ref[...], preferred_element_type=jnp.float32)) + rb_ref[...]
    lane = lax.broadcasted_iota(jnp.int32, (R, ROUTER_LANES), 1).astype(jnp.float32)
    neg = jnp.float32(-jnp.inf)
    no_lane = jnp.float32(ROUTER_LANES)
    gmask = lane < N_GROUPS
    gl = jnp.where(gmask, lg, neg)
    gmax = jnp.max(gl, axis=1, keepdims=True)
    grp = jnp.min(jnp.where(gl == gmax, lane, no_lane), axis=1, keepdims=True)
    gsum = jnp.sum(jnp.where(gmask, jnp.exp(gl - gmax), 0.0), axis=1, keepdims=True)
    p_grp = 1.0 / gsum
    lo_lane = E0 + grp * EXPERTS_PER_GROUP
    emask = (lane >= lo_lane) & (lane < lo_lane + EXPERTS_PER_GROUP)
    el = jnp.where(emask, lg, neg)
    v1 = jnp.max(el, axis=1, keepdims=True)
    i1 = jnp.min(jnp.where(el == v1, lane, no_lane), axis=1, keepdims=True)
    el2 = jnp.where(lane == i1, neg, el)
    v2 = jnp.max(el2, axis=1, keepdims=True)
    i2 = jnp.min(jnp.where(el2 == v2, lane, no_lane), axis=1, keepdims=True)
    a = jnp.exp(v2 - v1)
    inv = 1.0 / (1.0 + a)
    gate_ref[:, 0:1] = p_grp * inv
    gate_ref[:, 1:2] = p_grp * (a * inv)
    eid_ref[:, 0:1] = (i1 - E0).astype(jnp.int32)
    eid_ref[:, 1:2] = (i2 - E0).astype(jnp.int32)
    oh1 = lane == i1
    oh2 = lane == i2
    oh = (oh1 | oh2).astype(jnp.bfloat16)
    r_i = lax.broadcasted_iota(jnp.int32, (R, R), 0)
    c_i = lax.broadcasted_iota(jnp.int32, (R, R), 1)
    tri = (r_i > c_i).astype(jnp.bfloat16)
    before = jnp.dot(tri, oh, preferred_element_type=jnp.float32) + base_ref[...]
    rank_ref[:, 0:1] = jnp.sum(jnp.where(oh1, before, 0.0), axis=1, keepdims=True).astype(jnp.int32)
    rank_ref[:, 1:2] = jnp.sum(jnp.where(oh2, before, 0.0), axis=1, keepdims=True).astype(jnp.int32)
    base_ref[...] = base_ref[...] + jnp.sum(oh.astype(jnp.float32), axis=0, keepdims=True)
    cnt_ref[...] = base_ref[...]


def _mix_call(x2, attn2, wc, wg, gb, cw, wau, wcu, wo, g1, b1, wrh, wrl, rb, seq):
    N, D = x2.shape
    R = MIX_ROWS
    nt = N // R
    kern = functools.partial(_mix_kernel, steps_per_seq=seq // R)

    def full(a):
        return pl.BlockSpec(a.shape, lambda i: (0,) * a.ndim)

    in_specs = [
        pl.BlockSpec((R, D), lambda i: (i, 0)),
        pl.BlockSpec((R, Q_WIDTH), lambda i: (i, 0)),
        full(wc), full(wg), full(gb), full(cw), full(wau), full(wcu), full(wo),
        full(g1), full(b1), full(wrh), full(wrl), full(rb),
    ]
    out_shape = (
        jax.ShapeDtypeStruct((N, D), jnp.float32),
        jax.ShapeDtypeStruct((N, HALF), jnp.int32),
        jax.ShapeDtypeStruct((N, 2), jnp.int32),
        jax.ShapeDtypeStruct((N, 2), jnp.int32),
        jax.ShapeDtypeStruct((N, 2), jnp.float32),
        jax.ShapeDtypeStruct((1, ROUTER_LANES), jnp.float32),
    )
    out_specs = (
        pl.BlockSpec((R, D), lambda i: (i, 0)),
        pl.BlockSpec((R, HALF), lambda i: (i, 0)),
        pl.BlockSpec((R, 2), lambda i: (i, 0)),
        pl.BlockSpec((R, 2), lambda i: (i, 0)),
        pl.BlockSpec((R, 2), lambda i: (i, 0)),
        pl.BlockSpec((1, ROUTER_LANES), lambda i: (0, 0)),
    )
    scratch = [pltpu.VMEM((R + V7X_SUBLANES, CONV_DIM), jnp.float32),
               pltpu.VMEM((1, ROUTER_LANES), jnp.float32)]
    w_bytes = 2 * (wc.size + wg.size + wau.size + wcu.size + wo.size + wrh.size + wrl.size)
    vmem = 2 * w_bytes + 2 * (R * D * 4 * 2 + R * Q_WIDTH * 2 + R * HALF * 4) + 10 * R * 2048 * 4 + (6 << 20)
    return pl.pallas_call(
        kern, grid=(nt,), in_specs=in_specs, out_specs=out_specs, out_shape=out_shape,
        scratch_shapes=scratch, compiler_params=_cparams(1, vmem), name="mix_ln_router",
    )(x2, attn2, wc, wg, gb, cw, wau, wcu, wo, g1, b1, wrh, wrl, rb)


def _row_copy(src_ref, s, dst_ref, d, sem):
    return pltpu.make_async_copy(src_ref.at[pl.ds(s, 1), :], dst_ref.at[pl.ds(d, 1), :], sem)


def _dispatch_kernel(dest_ref, hp_ref, xs_in_ref, xs_ref, sem):
    del xs_in_ref
    R = MOVE_ROWS

    def issue(r, carry):
        _row_copy(hp_ref, r, xs_ref, dest_ref[0, 0, 2 * r], sem).start(priority=0)
        _row_copy(hp_ref, r, xs_ref, dest_ref[0, 0, 2 * r + 1], sem).start(priority=1)
        return carry

    lax.fori_loop(0, R, issue, 0, unroll=ISSUE_UNROLL)
    for _ in range(2):
        pltpu.make_async_copy(hp_ref, xs_ref.at[pl.ds(0, R), :], sem).wait()


def _dispatch_call(dest3, hp, xs0):
    N, W = hp.shape
    R = MOVE_ROWS
    nt = N // R
    return pl.pallas_call(
        _dispatch_kernel, grid=(nt,),
        in_specs=[
            pl.BlockSpec((1, 1, 2 * R), lambda i: (i, 0, 0), memory_space=pltpu.SMEM),
            pl.BlockSpec((R, W), lambda i: (i, 0)),
            pl.BlockSpec(memory_space=pl.ANY),
        ],
        out_specs=pl.BlockSpec(memory_space=pl.ANY),
        out_shape=jax.ShapeDtypeStruct(xs0.shape, xs0.dtype),
        scratch_shapes=[pltpu.SemaphoreType.DMA],
        input_output_aliases={2: 0},
        compiler_params=_cparams(1, 4 * R * W * 4 + (4 << 20)), name="moe_dispatch",
    )(dest3, hp, xs0)


def _expert_kernel(be_ref, nb_ref, xs_ref, wg_ref, wu_ref, wd_ref, ys_ref, wgb_ref, wub_ref, wdb_ref):
    j = pl.program_id(0)

    @pl.when((j == 0) | (be_ref[j] != be_ref[jnp.maximum(j - 1, 0)]))
    def _():
        wgb_ref[...] = wg_ref[0].astype(jnp.bfloat16)
        wub_ref[...] = wu_ref[0].astype(jnp.bfloat16)
        wdb_ref[...] = wd_ref[0].astype(jnp.bfloat16)

    @pl.when(j < nb_ref[0])
    def _():
        xb = _unpack_rows(xs_ref[...]).astype(jnp.bfloat16)
        hg = jnp.dot(xb, wgb_ref[...], preferred_element_type=jnp.float32)
        hu = jnp.dot(xb, wub_ref[...], preferred_element_type=jnp.float32)
        hid = (hg / (1.0 + jnp.exp(-hg))) * hu
        y = jnp.dot(hid.astype(jnp.bfloat16), wdb_ref[...], preferred_element_type=jnp.float32)
        ys_ref[...] = _pack_rows(y)

    @pl.when(j >= nb_ref[0])
    def _():
        ys_ref[...] = jnp.zeros_like(ys_ref)


def _expert_call(block_expert, n_used, xs, wg, wu, wd):
    cap, W = xs.shape
    RB = EXPERT_ROWS
    nb = cap // RB
    grid_spec = pltpu.PrefetchScalarGridSpec(
        num_scalar_prefetch=2, grid=(nb,),
        in_specs=[
            pl.BlockSpec((RB, W), lambda j, be, nu: (j, 0)),
            pl.BlockSpec((1, D_MODEL, D_EXPERT), lambda j, be, nu: (be[j], 0, 0)),
            pl.BlockSpec((1, D_MODEL, D_EXPERT), lambda j, be, nu: (be[j], 0, 0)),
            pl.BlockSpec((1, D_EXPERT, D_MODEL), lambda j, be, nu: (be[j], 0, 0)),
        ],
        out_specs=pl.BlockSpec((RB, W), lambda j, be, nu: (j, 0)),
        scratch_shapes=[pltpu.VMEM((D_MODEL, D_EXPERT), jnp.bfloat16),
                        pltpu.VMEM((D_MODEL, D_EXPERT), jnp.bfloat16),
                        pltpu.VMEM((D_EXPERT, D_MODEL), jnp.bfloat16)],
    )
    vmem = (2 * 4 + 2) * 3 * D_MODEL * D_EXPERT + 4 * RB * W * 4 + 8 * RB * D_MODEL * 4 + (6 << 20)
    return pl.pallas_call(
        _expert_kernel, grid_spec=grid_spec,
        out_shape=jax.ShapeDtypeStruct((cap, W), jnp.int32),
        compiler_params=_cparams(1, vmem), name="moe_experts",
    )(block_expert, n_used, xs, wg, wu, wd)


def _combine_kernel(dest_ref, h_ref, gate_ref, g2_ref, b2_ref, ys_ref, o_ref, buf_ref, sem):
    R = MOVE_ROWS

    def issue(r, carry):
        _row_copy(ys_ref, dest_ref[0, 0, 2 * r], buf_ref.at[0], r, sem).start(priority=0)
        _row_copy(ys_ref, dest_ref[0, 0, 2 * r + 1], buf_ref.at[1], r, sem).start(priority=1)
        return carry

    lax.fori_loop(0, R, issue, 0, unroll=ISSUE_UNROLL)
    for slot in range(2):
        pltpu.make_async_copy(ys_ref.at[pl.ds(0, R), :], buf_ref.at[slot], sem).wait()
    y0 = _unpack_rows(buf_ref[0])
    y1 = _unpack_rows(buf_ref[1])
    ffn = gate_ref[:, 0:1] * y0 + gate_ref[:, 1:2] * y1
    o_ref[...] = _layer_norm(DEEPNORM_ALPHA * h_ref[...] + ffn, g2_ref[...], b2_ref[...])


def _combine_call(dest3, h, gate, g2, b2, ys):
    N, D = h.shape
    W = ys.shape[1]
    R = MOVE_ROWS
    nt = N // R
    return pl.pallas_call(
        _combine_kernel, grid=(nt,),
        in_specs=[
            pl.BlockSpec((1, 1, 2 * R), lambda i: (i, 0, 0), memory_space=pltpu.SMEM),
            pl.BlockSpec((R, D), lambda i: (i, 0)),
            pl.BlockSpec((R, 2), lambda i: (i, 0)),
            pl.BlockSpec((1, D), lambda i: (0, 0)),
            pl.BlockSpec((1, D), lambda i: (0, 0)),
            pl.BlockSpec(memory_space=pl.ANY),
        ],
        out_specs=pl.BlockSpec((R, D), lambda i: (i, 0)),
        out_shape=jax.ShapeDtypeStruct((N, D), jnp.float32),
        scratch_shapes=[pltpu.VMEM((2, R, W), jnp.int32), pltpu.SemaphoreType.DMA],
        compiler_params=_cparams(1, 4 * R * D * 4 + 2 * R * W * 4 + 8 * R * D * 4 + (4 << 20)),
        name="moe_combine",
    )(dest3, h, gate, g2, b2, ys)


def _rope_tables(seq):
    inv_freq = ROPE_THETA ** (-jnp.arange(ROT_HALF, dtype=jnp.float32) / ROT_HALF)
    ang = jnp.arange(seq, dtype=jnp.int32).astype(jnp.float32)[:, None] * inv_freq[None, :]
    cos, sin = jnp.cos(ang), jnp.sin(ang)
    ones = jnp.ones((seq, HEAD_DIM - ROT_DIMS), jnp.float32)
    zeros = jnp.zeros((seq, HEAD_DIM - ROT_DIMS), jnp.float32)
    c_head = jnp.concatenate([cos, cos, ones], axis=1)
    s_head = jnp.concatenate([-sin, sin, zeros], axis=1)
    reps = KN_WIDTH // HEAD_DIM
    return cos.T, sin.T, jnp.tile(c_head, (1, reps)), jnp.tile(s_head, (1, reps))


def _swap_rot_cols(w):
    d, n = w.shape
    wh = w.reshape(d, n // HEAD_DIM, HEAD_DIM)
    sw = jnp.concatenate([wh[:, :, ROT_HALF:ROT_DIMS], wh[:, :, :ROT_HALF],
                          jnp.zeros((d, n // HEAD_DIM, HEAD_DIM - ROT_DIMS), w.dtype)], axis=2)
    return sw.reshape(d, n)


def _block(x, w_in, gate_bias, w_attn_up, w_conv_up, conv_w, w_out, ln_g, ln_b,
           rg_w, rg_b, re_w, re_b, w_gate_e, w_up_e, w_down_e, ln2_g, ln2_b):
    B, S, D = x.shape
    N = B * S
    top_k = min(TOPK_MAX, S // 4)
    bf = jnp.bfloat16
    o = np.cumsum([0, Q_WIDTH, KV_WIDTH, KV_WIDTH, IDXQ_WIDTH, IDX_DIM, IDX_HEADS,
                   CONV_DIM, CONV_DIM, CONV_DIM, N_BRANCHES * D_MODEL])
    w_q, w_k, w_v, w_qi, w_ki, w_wi = (w_in[:, o[i]:o[i + 1]] for i in range(6))
    w_conv = w_in[:, o[6]:o[9]]
    w_gates = w_in[:, o[9]:o[10]]

    wt = jnp.concatenate([w_q.T * (HEAD_DIM ** -0.5 * LOG2_E), w_qi.T, w_v.T, w_wi.T,
                          jnp.zeros((T_ROWS - T_WI0 - IDX_HEADS, D), w_in.dtype)], axis=0).astype(bf)
    pad = jnp.zeros((D, KN_WIDTH - KV_WIDTH - IDX_DIM), w_in.dtype)
    wn = jnp.concatenate([w_k, w_ki, pad, _swap_rot_cols(w_k), _swap_rot_cols(w_ki), pad], axis=1).astype(bf)
    cos_t, sin_t, cos_n, sgn_n = _rope_tables(S)

    qt, qit, vt, wit, kn = _proj_call(x, wt, wn, cos_t, sin_t, cos_n, sgn_n)
    attn = _attn_call(qt, qit, wit, kn, vt, top_k)

    w_r = jnp.concatenate([rg_w, jnp.transpose(re_w, (1, 0, 2)).reshape(D, N_EXPERTS),
                           jnp.zeros((D, ROUTER_LANES - E0 - N_EXPERTS), rg_w.dtype)], axis=1)
    b_r = jnp.concatenate([rg_b, re_b.reshape(-1),
                           jnp.zeros((ROUTER_LANES - E0 - N_EXPERTS,), rg_b.dtype)])[None, :]
    w_rh = w_r.astype(bf)
    w_rl = (w_r - w_rh.astype(jnp.float32)).astype(bf)

    h, hp, eid, rank, gate, cnt = _mix_call(
        x.reshape(N, D), attn.reshape(N, Q_WIDTH), w_conv.astype(bf), w_gates.astype(bf),
        gate_bias[None, :], conv_w, w_attn_up.astype(bf), w_conv_up.astype(bf), w_out.astype(bf),
        ln_g[None, :], ln_b[None, :], w_rh, w_rl, b_r, S)

    RB = EXPERT_ROWS
    counts = cnt[0, E0:E0 + N_EXPERTS].astype(jnp.int32)
    padded = ((counts + RB - 1) // RB) * RB
    pad_end = jnp.cumsum(padded)
    pad_start = pad_end - padded
    dest = jnp.sum(jnp.where(eid[:, :, None] == jnp.arange(N_EXPERTS, dtype=jnp.int32)[None, None, :],
                             pad_start[None, None, :], 0), axis=-1) + rank
    cap = N * 2 + N_EXPERTS * RB
    nb = cap // RB
    block_start = jnp.arange(nb, dtype=jnp.int32) * RB
    block_expert = jnp.minimum(jnp.sum((block_start[:, None] >= pad_end[None, :]).astype(jnp.int32), axis=1),
                               N_EXPERTS - 1)
    n_used = (pad_end[-1:] // RB).astype(jnp.int32)
    dest3 = dest.reshape(N // MOVE_ROWS, 1, 2 * MOVE_ROWS)

    xs = _dispatch_call(dest3, hp, jnp.zeros((cap, HALF), jnp.int32))
    ys = _expert_call(block_expert, n_used, xs, w_gate_e, w_up_e, w_down_e)
    out = _combine_call(dest3, h, gate, ln2_g[None, :], ln2_b[None, :], ys)
    return out.reshape(B, S, D)


def kernel(x, w_in, gate_bias, w_attn_up, w_conv_up, conv_w, w_out, ln1_g, ln1_b, router_group_w,
           router_group_b, router_expert_w, router_expert_b, w_gate_e, w_up_e, w_down_e, ln2_g, ln2_b):
    h = x
    for l in range(DEPTH):
        h = _block(h, w_in[l], gate_bias[l], w_attn_up[l], w_conv_up[l], conv_w[l], w_out[l],
                   ln1_g[l], ln1_b[l], router_group_w[l], router_group_b[l], router_expert_w[l],
                   router_expert_b[l], w_gate_e[l], w_up_e[l], w_down_e[l], ln2_g[l], ln2_b[l])
    return h
```

```python
import functools

import jax
import jax.numpy as jnp
import numpy as np
from jax import lax
from jax.experimental import pallas as pl
from jax.experimental.pallas import tpu as pltpu

D_MODEL = 1024
N_HEADS = 8
N_KV_HEADS = 2
HEAD_DIM = 64
Q_WIDTH = N_HEADS * HEAD_DIM
KV_WIDTH = N_KV_HEADS * HEAD_DIM
ROPE_THETA = 500000.0
ROT_DIMS = HEAD_DIM // 4
ROT_HALF = ROT_DIMS // 2
IDX_HEADS = 8
IDX_DIM = 64
IDXQ_WIDTH = IDX_HEADS * IDX_DIM
TOPK_MAX = 256
CONV_DIM = 512
CONV_WIDTH = 3
N_BRANCHES = 2
N_GROUPS = 4
EXPERTS_PER_GROUP = 8
N_EXPERTS = N_GROUPS * EXPERTS_PER_GROUP
D_EXPERT = 512
LN_EPS = 1e-5
DEPTH = 1
DEEPNORM_ALPHA = (2 * DEPTH) ** 0.25

V7X_LANES = 128
V7X_SUBLANES = 8
V7X_VMEM_LIMIT_BYTES = 56 * 1024 * 1024

PROJ_ROWS = 512
ATTN_CHUNK = 256
ATTN_KEY_TILE = 128
DEN_ROWS = 16
ACC_ROWS = HEAD_DIM + DEN_ROWS
LOG2_E = 1.4426950408889634
MIX_ROWS = 512
EXPERT_ROWS = 256
MOVE_ROWS = 512
ISSUE_UNROLL = 8
HALF = D_MODEL // 2

T_Q0, T_QI0, T_V0, T_WI0 = 0, Q_WIDTH, Q_WIDTH + IDXQ_WIDTH, Q_WIDTH + IDXQ_WIDTH + KV_WIDTH
T_ROWS = T_WI0 + 16
KN_WIDTH = 256

INT_MIN = -2147483648
HALF_RANGE = 32768
ROUTER_LANES = 128
E0 = N_GROUPS


def _cparams(n_axes, vmem_bytes):
    return pltpu.CompilerParams(
        dimension_semantics=("arbitrary",) * n_axes,
        vmem_limit_bytes=int(min(vmem_bytes, V7X_VMEM_LIMIT_BYTES)),
    )


def _proj_kernel(x_ref, wt_ref, wn_ref, cost_ref, sint_ref, cosn_ref, sgnn_ref,
                 qt_ref, qit_ref, vt_ref, wit_ref, kn_ref):
    xb = x_ref[0].astype(jnp.bfloat16)
    pt = lax.dot_general(wt_ref[...], xb, (((1,), (1,)), ((), ())),
                         preferred_element_type=jnp.float32)
    cos = cost_ref[...]
    sin = sint_ref[...]

    def rope_t(dst_ref, base):
        for h in range(N_HEADS):
            r0 = base + h * HEAD_DIM
            x1 = pt[r0:r0 + ROT_HALF]
            x2 = pt[r0 + ROT_HALF:r0 + ROT_DIMS]
            head = jnp.concatenate([x1 * cos - x2 * sin, x2 * cos + x1 * sin,
                                    pt[r0 + ROT_DIMS:r0 + HEAD_DIM]], axis=0).astype(dst_ref.dtype)
            for jj, cols in enumerate(chunks):
                dst_ref[0, jj, h * HEAD_DIM:(h + 1) * HEAD_DIM, :] = head[:, cols]

    chunks = [slice(jj * ATTN_CHUNK, (jj + 1) * ATTN_CHUNK) for jj in range(PROJ_ROWS // ATTN_CHUNK)]
    rope_t(qt_ref, T_Q0)
    rope_t(qit_ref, T_QI0)
    for jj, cols in enumerate(chunks):
        vt_ref[0, jj] = pt[T_V0:T_V0 + KV_WIDTH, cols].astype(vt_ref.dtype)
        wit_ref[0, jj] = pt[T_WI0:T_WI0 + IDX_HEADS, cols]
    pn = jnp.dot(xb, wn_ref[...], preferred_element_type=jnp.float32)
    kn = pn[:, :KN_WIDTH] * cosn_ref[...] + pn[:, KN_WIDTH:] * sgnn_ref[...]
    kn_ref[0] = kn.astype(kn_ref.dtype)


def _proj_call(x, wt, wn, cos_t, sin_t, cos_n, sgn_n):
    B, S, D = x.shape
    R = PROJ_ROWS
    nt = S // R
    grid = (B, nt)
    out_shape = (
        jax.ShapeDtypeStruct((B, S // ATTN_CHUNK, Q_WIDTH, ATTN_CHUNK), jnp.bfloat16),
        jax.ShapeDtypeStruct((B, S // ATTN_CHUNK, IDXQ_WIDTH, ATTN_CHUNK), jnp.bfloat16),
        jax.ShapeDtypeStruct((B, S // ATTN_CHUNK, KV_WIDTH, ATTN_CHUNK), jnp.bfloat16),
        jax.ShapeDtypeStruct((B, S // ATTN_CHUNK, IDX_HEADS, ATTN_CHUNK), jnp.float32),
        jax.ShapeDtypeStruct((B, S, KN_WIDTH), jnp.bfloat16),
    )
    in_specs = [
        pl.BlockSpec((1, R, D), lambda b, j: (b, j, 0)),
        pl.BlockSpec((T_ROWS, D), lambda b, j: (0, 0)),
        pl.BlockSpec((D, 2 * KN_WIDTH), lambda b, j: (0, 0)),
        pl.BlockSpec((ROT_HALF, R), lambda b, j: (0, j)),
        pl.BlockSpec((ROT_HALF, R), lambda b, j: (0, j)),
        pl.BlockSpec((R, KN_WIDTH), lambda b, j: (j, 0)),
        pl.BlockSpec((R, KN_WIDTH), lambda b, j: (j, 0)),
    ]
    out_specs = (
        pl.BlockSpec((1, R // ATTN_CHUNK, Q_WIDTH, ATTN_CHUNK), lambda b, j: (b, j, 0, 0)),
        pl.BlockSpec((1, R // ATTN_CHUNK, IDXQ_WIDTH, ATTN_CHUNK), lambda b, j: (b, j, 0, 0)),
        pl.BlockSpec((1, R // ATTN_CHUNK, KV_WIDTH, ATTN_CHUNK), lambda b, j: (b, j, 0, 0)),
        pl.BlockSpec((1, R // ATTN_CHUNK, IDX_HEADS, ATTN_CHUNK), lambda b, j: (b, j, 0, 0)),
        pl.BlockSpec((1, R, KN_WIDTH), lambda b, j: (b, j, 0)),
    )
    vmem = 2 * (R * D * 4 + T_ROWS * D * 2 + D * 2 * KN_WIDTH * 2) + 6 * T_ROWS * R * 4 + (8 << 20)
    return pl.pallas_call(
        _proj_kernel, grid=grid, in_specs=in_specs, out_specs=out_specs, out_shape=out_shape,
        compiler_params=_cparams(2, vmem), name="dsa_proj",
    )(x, wt, wn, cos_t, sin_t, cos_n, sgn_n)


def _float_to_key(s):
    b = lax.bitcast_convert_type(s, jnp.int32)
    k = b ^ (lax.shift_right_arithmetic(b, 31) & jnp.int32(0x7FFFFFFF))
    return jnp.where(b == jnp.int32(INT_MIN), jnp.int32(0), k)


def _attn_kernel(qt_ref, qit_ref, wit_ref, kn_ref, vt_ref, o_ref,
                 key_ref, hi_ref, lo_ref, lga_ref, lgb_ref, acc_ref, m_ref, *, seq, top_k):
    C = ATTN_CHUNK
    c = pl.program_id(1)
    nkb = c + 1
    zeros_half = jnp.zeros((HEAD_DIM, C), jnp.bfloat16)
    row_i = lax.broadcasted_iota(jnp.int32, (C, C), 0)
    lane_i = lax.broadcasted_iota(jnp.int32, (C, C), 1)
    causal_in_block = row_i <= lane_i

    def score_body(kb, carry):
        r0 = pl.multiple_of(kb * C, C)
        kix = kn_ref[0, pl.ds(r0, C), KV_WIDTH:KN_WIDTH]
        acc = None
        for h in range(IDX_HEADS):
            rhs = jnp.concatenate([qit_ref[0, 0, h * IDX_DIM:(h + 1) * IDX_DIM, :], zeros_half], axis=0)
            s = jnp.dot(kix, rhs, preferred_element_type=jnp.float32)
            t = jnp.maximum(s, 0.0) * wit_ref[0, 0, h:h + 1, :]
            acc = t if acc is None else acc + t
        score = acc * (IDX_DIM ** -0.5 * IDX_HEADS ** -0.5)
        keys = jnp.where(causal_in_block | (kb != c), _float_to_key(score), jnp.int32(INT_MIN))
        key_ref[kb] = keys
        hi_ref[kb] = lax.shift_right_arithmetic(keys, 16).astype(jnp.int16)
        lo_ref[kb] = ((keys & jnp.int32(0xFFFF)) - jnp.int32(HALF_RANGE)).astype(jnp.int16)
        return carry

    lax.fori_loop(0, nkb, score_body, 0)

    def count(ref, pred):
        packed = ref.dtype == jnp.int16

        def body(kb, part):
            hit = pred(ref[kb], kb)
            if packed:
                words = pltpu.bitcast(jnp.where(hit, jnp.int16(1), jnp.int16(0)), jnp.int32)
            else:
                words = hit.astype(jnp.int32)
            return part + jnp.sum(words.reshape(-1, V7X_SUBLANES, C), axis=0)

        part = lax.fori_loop(0, nkb, body, jnp.zeros((V7X_SUBLANES, C), jnp.int32))
        if packed:
            part = (part & jnp.int32(0xFFFF)) + lax.shift_right_logical(part, 16)
        return jnp.sum(part, axis=0, keepdims=True)

    def kth_largest_16(ref):
        def body(i, ans_u):
            cand_u = ans_u | lax.shift_left(jnp.int32(1), jnp.int32(15) - i)
            cand = (cand_u - jnp.int32(HALF_RANGE)).astype(jnp.int16)
            return jnp.where(count(ref, lambda k, kb: k >= cand) >= top_k, cand_u, ans_u)

        return lax.fori_loop(0, 16, body, jnp.zeros((1, C), jnp.int32)) - jnp.int32(HALF_RANGE)

    t_hi = kth_largest_16(hi_ref)
    t_hi16 = t_hi.astype(jnp.int16)

    def refine_body(kb, carry):
        hi = hi_ref[kb]
        lo_ref[kb] = jnp.where(hi > t_hi16, jnp.int16(HALF_RANGE - 1),
                               jnp.where(hi == t_hi16, lo_ref[kb], jnp.int16(-HALF_RANGE)))
        return carry

    lax.fori_loop(0, nkb, refine_body, 0)
    t_lo = kth_largest_16(lo_ref)
    thr_raw = t_hi * jnp.int32(2 * HALF_RANGE) + (t_lo + jnp.int32(HALF_RANGE))
    thr = jnp.maximum(thr_raw, jnp.int32(INT_MIN + 1))

    t_lo16 = t_lo.astype(jnp.int16)
    n_ge = count(lo_ref, lambda k, kb: k >= t_lo16)
    has_ties = jnp.max(jnp.where(thr_raw != jnp.int32(INT_MIN), n_ge, 0)) > top_k

    @pl.when(has_ties)
    def _():
        need = top_k - count(key_ref, lambda k, kb: k > thr)
        n_bits = int(seq - 1).bit_length()

        def body(i, ans):
            cand = ans | lax.shift_left(jnp.int32(1), jnp.int32(n_bits - 1) - i)
            below = count(key_ref, lambda k, kb: (k == thr) & (kb * C + row_i < cand))
            return jnp.where(below < need, cand, ans)

        last_tie = lax.fori_loop(0, n_bits, body, jnp.zeros((1, C), jnp.int32))

        def drop_body(kb, carry):
            kk = key_ref[kb]
            key_ref[kb] = jnp.where((kk == thr) & (kb * C + row_i > last_tie), kk - 1, kk)
            return carry

        lax.fori_loop(0, nkb, drop_body, 0)

    neg_inf = jnp.float32(-jnp.inf)
    m_ref[...] = jnp.full(m_ref.shape, neg_inf, jnp.float32)
    acc_ref[...] = jnp.zeros(acc_ref.shape, jnp.float32)
    T = ATTN_KEY_TILE
    ones_rows = jnp.ones((DEN_ROWS, C), jnp.bfloat16)

    kv_group = N_HEADS // N_KV_HEADS

    def bias_body(kb, carry):
        key_ref[kb] = lax.bitcast_convert_type(jnp.where(key_ref[kb] >= thr, 0.0, neg_inf), jnp.int32)
        return carry

    lax.fori_loop(0, nkb, bias_body, 0)
    n_blocks = seq // C
    key_ref[n_blocks] = lax.bitcast_convert_type(jnp.full((C, C), neg_inf, jnp.float32), jnp.int32)

    def store_logits(kb, dst_ref, h):
        kb_mem = jnp.minimum(kb, nkb - 1)
        kb_bias = jnp.where(kb < nkb, kb, n_blocks)
        r0 = pl.multiple_of(kb_mem * C, C)
        bias = lax.bitcast_convert_type(key_ref[kb_bias], jnp.float32)
        k2 = kn_ref[0, pl.ds(r0, C), 0:KV_WIDTH]
        qh = qt_ref[0, 0, h * HEAD_DIM:(h + 1) * HEAD_DIM, :]
        rhs = jnp.concatenate([qh, zeros_half] if h < kv_group else [zeros_half, qh], axis=0)
        dst_ref[h] = jnp.dot(k2, rhs, preferred_element_type=jnp.float32) + bias

    def softmax_block(kb, src_ref, kb_next, dst_ref):
        kb_mem = jnp.minimum(kb, nkb - 1)
        for h in range(N_HEADS):
            store_logits(kb_next, dst_ref, h)
            g = h // kv_group
            m_old = m_ref[h:h + 1, :]
            m_new = jnp.maximum(m_old, jnp.max(src_ref[h], axis=0, keepdims=True))
            m_safe = jnp.where(m_new == neg_inf, 0.0, m_new)
            p = jnp.concatenate(
                [jnp.exp2(src_ref[h, s * T:(s + 1) * T, :] - m_safe).astype(jnp.bfloat16) for s in range(C // T)],
                axis=0)
            alpha = jnp.exp2(m_old - m_safe)
            vt = vt_ref[0, kb_mem, g * HEAD_DIM:(g + 1) * HEAD_DIM, :]
            pv = jnp.dot(jnp.concatenate([vt, ones_rows], axis=0), p,
                         preferred_element_type=jnp.float32)
            hs = slice(h * ACC_ROWS, (h + 1) * ACC_ROWS)
            acc_ref[hs, :] = alpha * acc_ref[hs, :] + pv
            m_ref[h:h + 1, :] = m_new

    for h in range(N_HEADS):
        store_logits(0, lga_ref, h)

    def attn_body(pair, carry):
        kb = 2 * pair
        softmax_block(kb, lga_ref, kb + 1, lgb_ref)
        softmax_block(kb + 1, lgb_ref, kb + 2, lga_ref)
        return carry

    lax.fori_loop(0, (nkb + 1) // 2, attn_body, 0)
    outs = []
    for h in range(N_HEADS):
        num = acc_ref[h * ACC_ROWS:h * ACC_ROWS + HEAD_DIM, :]
        den = acc_ref[h * ACC_ROWS + HEAD_DIM:h * ACC_ROWS + HEAD_DIM + 1, :]
        outs.append(num / den)
    o_ref[0] = jnp.transpose(jnp.concatenate(outs, axis=0)).astype(o_ref.dtype)


def _attn_call(qt, qit, wit, kn, vt, top_k):
    B, S = kn.shape[:2]
    C = ATTN_CHUNK
    nc = S // C
    kern = functools.partial(_attn_kernel, seq=S, top_k=top_k)
    in_specs = [
        pl.BlockSpec((1, 1, Q_WIDTH, C), lambda b, c: (b, c, 0, 0)),
        pl.BlockSpec((1, 1, IDXQ_WIDTH, C), lambda b, c: (b, c, 0, 0)),
        pl.BlockSpec((1, 1, IDX_HEADS, C), lambda b, c: (b, c, 0, 0)),
        pl.BlockSpec((1, S, KN_WIDTH), lambda b, c: (b, 0, 0)),
        pl.BlockSpec((1, nc, KV_WIDTH, C), lambda b, c: (b, 0, 0, 0)),
    ]
    out_specs = pl.BlockSpec((1, C, Q_WIDTH), lambda b, c: (b, c, 0))
    scratch = [
        pltpu.VMEM((nc + 1, C, C), jnp.int32),
        pltpu.VMEM((nc, C, C), jnp.int16),
        pltpu.VMEM((nc, C, C), jnp.int16),
        pltpu.VMEM((N_HEADS, C, C), jnp.float32),
        pltpu.VMEM((N_HEADS, C, C), jnp.float32),
        pltpu.VMEM((N_HEADS * ACC_ROWS, C), jnp.float32),
        pltpu.VMEM((N_HEADS, C), jnp.float32),
    ]
    vmem = (2 * S * C * 4 + Q_WIDTH * C * 4 + 2 * 2 * (2 * Q_WIDTH * C + S * KN_WIDTH + S * KV_WIDTH + C * Q_WIDTH)
            + 24 * C * C * 4 + (8 << 20))
    return pl.pallas_call(
        kern, grid=(B, nc), in_specs=in_specs, out_specs=out_specs,
        out_shape=jax.ShapeDtypeStruct((B, S, Q_WIDTH), jnp.bfloat16),
        scratch_shapes=scratch, compiler_params=_cparams(2, vmem), name="dsa_attn",
    )(qt, qit, wit, kn, vt)


def _layer_norm(v, g, b):
    mu = jnp.mean(v, axis=-1, keepdims=True)
    d = v - mu
    var = jnp.mean(d * d, axis=-1, keepdims=True)
    return d * lax.rsqrt(var + LN_EPS) * g + b


def _pack_rows(h):
    hi = lax.bitcast_convert_type(h[:, :HALF].astype(jnp.bfloat16).astype(jnp.float32), jnp.int32)
    lo = lax.bitcast_convert_type(h[:, HALF:].astype(jnp.bfloat16).astype(jnp.float32), jnp.int32)
    return (hi & jnp.int32(-65536)) | lax.shift_right_logical(lo, 16)


def _unpack_rows(w):
    hi = lax.bitcast_convert_type(w & jnp.int32(-65536), jnp.float32)
    lo = lax.bitcast_convert_type(lax.shift_left(w, 16), jnp.float32)
    return jnp.concatenate([hi, lo], axis=1)


def _mix_kernel(x_ref, attn_ref, wc_ref, wg_ref, gb_ref, cw_ref, wau_ref, wcu_ref, wo_ref,
                g1_ref, b1_ref, wrh_ref, wrl_ref, rb_ref,
                h_ref, hp_ref, eid_ref, rank_ref, gate_ref, cnt_ref,
                ubuf_ref, base_ref, *, steps_per_seq):
    R = MIX_ROWS
    i = pl.program_id(0)

    @pl.when(i == 0)
    def _():
        base_ref[...] = jnp.zeros_like(base_ref)

    @pl.when(i % steps_per_seq == 0)
    def _():
        ubuf_ref[0:V7X_SUBLANES, :] = jnp.zeros((V7X_SUBLANES, CONV_DIM), jnp.float32)

    x = x_ref[...]
    xb = x.astype(jnp.bfloat16)
    cv = jnp.dot(xb, wc_ref[...], preferred_element_type=jnp.float32)
    u = cv[:, 2 * CONV_DIM:] * cv[:, :CONV_DIM]
    ubuf_ref[V7X_SUBLANES:, :] = u
    u1 = ubuf_ref[V7X_SUBLANES - 1:V7X_SUBLANES - 1 + R, :]
    u2 = ubuf_ref[V7X_SUBLANES - 2:V7X_SUBLANES - 2 + R, :]
    y = cw_ref[0:1, :] * u2 + cw_ref[1:2, :] * u1 + cw_ref[2:3, :] * u
    conv = (cv[:, CONV_DIM:2 * CONV_DIM] * y).astype(jnp.bfloat16)
    ubuf_ref[0:V7X_SUBLANES, :] = u[R - V7X_SUBLANES:, :]
    z = jnp.dot(xb, wg_ref[...], preferred_element_type=jnp.float32) + gb_ref[...]
    gates = 1.0 / (1.0 + jnp.exp(-z))
    au = jnp.dot(attn_ref[...], wau_ref[...], preferred_element_type=jnp.float32)
    cu = jnp.dot(conv, wcu_ref[...], preferred_element_type=jnp.float32)
    merged = gates[:, :D_MODEL] * au + gates[:, D_MODEL:] * cu
    mix = jnp.dot(merged.astype(jnp.bfloat16), wo_ref[...], preferred_element_type=jnp.float32)
    h = _layer_norm(DEEPNORM_ALPHA * x + mix, g1_ref[...], b1_ref[...])
    h_ref[...] = h
    hp_ref[...] = _pack_rows(h)

    h_hi = h.astype(jnp.bfloat16)
    h_lo = (h - h_hi.astype(jnp.float32)).astype(jnp.bfloat16)
    lg = (jnp.dot(h_hi, wrh_ref[...], preferred_element_type=jnp.float32)
          + jnp.dot(h_lo, wrh_ref[...], preferred_element_type=jnp.float32)
          + jnp.dot(h_hi, wrl_ref[...], preferred_element_type=jnp.float32)) + rb_ref[...]
    lane = lax.broadcasted_iota(jnp.int32, (R, ROUTER_LANES), 1).astype(jnp.float32)
    neg = jnp.float32(-jnp.inf)
    no_lane = jnp.float32(ROUTER_LANES)
    gmask = lane < N_GROUPS
    gl = jnp.where(gmask, lg, neg)
    gmax = jnp.max(gl, axis=1, keepdims=True)
    grp = jnp.min(jnp.where(gl == gmax, lane, no_lane), axis=1, keepdims=True)
    gsum = jnp.sum(jnp.where(gmask, jnp.exp(gl - gmax), 0.0), axis=1, keepdims=True)
    p_grp = 1.0 / gsum
    lo_lane = E0 + grp * EXPERTS_PER_GROUP
    emask = (lane >= lo_lane) & (lane < lo_lane + EXPERTS_PER_GROUP)
    el = jnp.where(emask, lg, neg)
    v1 = jnp.max(el, axis=1, keepdims=True)
    i1 = jnp.min(jnp.where(el == v1, lane, no_lane), axis=1, keepdims=True)
    el2 = jnp.where(lane == i1, neg, el)
    v2 = jnp.max(el2, axis=1, keepdims=True)
    i2 = jnp.min(jnp.where(el2 == v2, lane, no_lane), axis=1, keepdims=True)
    a = jnp.exp(v2 - v1)
    inv = 1.0 / (1.0 + a)
    gate_ref[:, 0:1] = p_grp * inv
    gate_ref[:, 1:2] = p_grp * (a * inv)
    eid_ref[:, 0:1] = (i1 - E0).astype(jnp.int32)
    eid_ref[:, 1:2] = (i2 - E0).astype(jnp.int32)
    oh1 = lane == i1
    oh2 = lane == i2
    oh = (oh1 | oh2).astype(jnp.bfloat16)
    r_i = lax.broadcasted_iota(jnp.int32, (R, R), 0)
    c_i = lax.broadcasted_iota(jnp.int32, (R, R), 1)
    tri = (r_i > c_i).astype(jnp.bfloat16)
    before = jnp.dot(tri, oh, preferred_element_type=jnp.float32) + base_ref[...]
    rank_ref[:, 0:1] = jnp.sum(jnp.where(oh1, before, 0.0), axis=1, keepdims=True).astype(jnp.int32)
    rank_ref[:, 1:2] = jnp.sum(jnp.where(oh2, before, 0.0), axis=1, keepdims=True).astype(jnp.int32)
    base_ref[...] = base_ref[...] + jnp.sum(oh.astype(jnp.float32), axis=0, keepdims=True)
    cnt_ref[...] = base_ref[...]


def _mix_call(x2, attn2, wc, wg, gb, cw, wau, wcu, wo, g1, b1, wrh, wrl, rb, seq):
    N, D = x2.shape
    R = MIX_ROWS
    nt = N // R
    kern = functools.partial(_mix_kernel, steps_per_seq=seq // R)

    def full(a):
        return pl.BlockSpec(a.shape, lambda i: (0,) * a.ndim)

    in_specs = [
        pl.BlockSpec((R, D), lambda i: (i, 0)),
        pl.BlockSpec((R, Q_WIDTH), lambda i: (i, 0)),
        full(wc), full(wg), full(gb), full(cw), full(wau), full(wcu), full(wo),
        full(g1), full(b1), full(wrh), full(wrl), full(rb),
    ]
    out_shape = (
        jax.ShapeDtypeStruct((N, D), jnp.float32),
        jax.ShapeDtypeStruct((N, HALF), jnp.int32),
        jax.ShapeDtypeStruct((N, 2), jnp.int32),
        jax.ShapeDtypeStruct((N, 2), jnp.int32),
        jax.ShapeDtypeStruct((N, 2), jnp.float32),
        jax.ShapeDtypeStruct((1, ROUTER_LANES), jnp.float32),
    )
    out_specs = (
        pl.BlockSpec((R, D), lambda i: (i, 0)),
        pl.BlockSpec((R, HALF), lambda i: (i, 0)),
        pl.BlockSpec((R, 2), lambda i: (i, 0)),
        pl.BlockSpec((R, 2), lambda i: (i, 0)),
        pl.BlockSpec((R, 2), lambda i: (i, 0)),
        pl.BlockSpec((1, ROUTER_LANES), lambda i: (0, 0)),
    )
    scratch = [pltpu.VMEM((R + V7X_SUBLANES, CONV_DIM), jnp.float32),
               pltpu.VMEM((1, ROUTER_LANES), jnp.float32)]
    w_bytes = 2 * (wc.size + wg.size + wau.size + wcu.size + wo.size + wrh.size + wrl.size)
    vmem = 2 * w_bytes + 2 * (R * D * 4 * 2 + R * Q_WIDTH * 2 + R * HALF * 4) + 10 * R * 2048 * 4 + (6 << 20)
    return pl.pallas_call(
        kern, grid=(nt,), in_specs=in_specs, out_specs=out_specs, out_shape=out_shape,
        scratch_shapes=scratch, compiler_params=_cparams(1, vmem), name="mix_ln_router",
    )(x2, attn2, wc, wg, gb, cw, wau, wcu, wo, g1, b1, wrh, wrl, rb)


def _row_copy(src_ref, s, dst_ref, d, sem):
    return pltpu.make_async_copy(src_ref.at[pl.ds(s, 1), :], dst_ref.at[pl.ds(d, 1), :], sem)


def _dispatch_kernel(dest_ref, hp_ref, xs_in_ref, xs_ref, sem):
    del xs_in_ref
    R = MOVE_ROWS

    def issue(r, carry):
        _row_copy(hp_ref, r, xs_ref, dest_ref[0, 0, 2 * r], sem).start(priority=0)
        _row_copy(hp_ref, r, xs_ref, dest_ref[0, 0, 2 * r + 1], sem).start(priority=1)
        return carry

    lax.fori_loop(0, R, issue, 0, unroll=ISSUE_UNROLL)
    for _ in range(2):
        pltpu.make_async_copy(hp_ref, xs_ref.at[pl.ds(0, R), :], sem).wait()


def _dispatch_call(dest3, hp, xs0):
    N, W = hp.shape
    R = MOVE_ROWS
    nt = N // R
    return pl.pallas_call(
        _dispatch_kernel, grid=(nt,),
        in_specs=[
            pl.BlockSpec((1, 1, 2 * R), lambda i: (i, 0, 0), memory_space=pltpu.SMEM),
            pl.BlockSpec((R, W), lambda i: (i, 0)),
            pl.BlockSpec(memory_space=pl.ANY),
        ],
        out_specs=pl.BlockSpec(memory_space=pl.ANY),
        out_shape=jax.ShapeDtypeStruct(xs0.shape, xs0.dtype),
        scratch_shapes=[pltpu.SemaphoreType.DMA],
        input_output_aliases={2: 0},
        compiler_params=_cparams(1, 4 * R * W * 4 + (4 << 20)), name="moe_dispatch",
    )(dest3, hp, xs0)


def _expert_kernel(be_ref, nb_ref, xs_ref, wg_ref, wu_ref, wd_ref, ys_ref, wgb_ref, wub_ref, wdb_ref):
    j = pl.program_id(0)

    @pl.when((j == 0) | (be_ref[j] != be_ref[jnp.maximum(j - 1, 0)]))
    def _():
        wgb_ref[...] = wg_ref[0].astype(jnp.bfloat16)
        wub_ref[...] = wu_ref[0].astype(jnp.bfloat16)
        wdb_ref[...] = wd_ref[0].astype(jnp.bfloat16)

    @pl.when(j < nb_ref[0])
    def _():
        xb = _unpack_rows(xs_ref[...]).astype(jnp.bfloat16)
        hg = jnp.dot(xb, wgb_ref[...], preferred_element_type=jnp.float32)
        hu = jnp.dot(xb, wub_ref[...], preferred_element_type=jnp.float32)
        hid = (hg / (1.0 + jnp.exp(-hg))) * hu
        y = jnp.dot(hid.astype(jnp.bfloat16), wdb_ref[...], preferred_element_type=jnp.float32)
        ys_ref[...] = _pack_rows(y)

    @pl.when(j >= nb_ref[0])
    def _():
        ys_ref[...] = jnp.zeros_like(ys_ref)


def _expert_call(block_expert, n_used, xs, wg, wu, wd):
    cap, W = xs.shape
    RB = EXPERT_ROWS
    nb = cap // RB
    grid_spec = pltpu.PrefetchScalarGridSpec(
        num_scalar_prefetch=2, grid=(nb,),
        in_specs=[
            pl.BlockSpec((RB, W), lambda j, be, nu: (j, 0)),
            pl.BlockSpec((1, D_MODEL, D_EXPERT), lambda j, be, nu: (be[j], 0, 0)),
            pl.BlockSpec((1, D_MODEL, D_EXPERT), lambda j, be, nu: (be[j], 0, 0)),
            pl.BlockSpec((1, D_EXPERT, D_MODEL), lambda j, be, nu: (be[j], 0, 0)),
        ],
        out_specs=pl.BlockSpec((RB, W), lambda j, be, nu: (j, 0)),
        scratch_shapes=[pltpu.VMEM((D_MODEL, D_EXPERT), jnp.bfloat16),
                        pltpu.VMEM((D_MODEL, D_EXPERT), jnp.bfloat16),
                        pltpu.VMEM((D_EXPERT, D_MODEL), jnp.bfloat16)],
    )
    vmem = (2 * 4 + 2) * 3 * D_MODEL * D_EXPERT + 4 * RB * W * 4 + 8 * RB * D_MODEL * 4 + (6 << 20)
    return pl.pallas_call(
        _expert_kernel, grid_spec=grid_spec,
        out_shape=jax.ShapeDtypeStruct((cap, W), jnp.int32),
        compiler_params=_cparams(1, vmem), name="moe_experts",
    )(block_expert, n_used, xs, wg, wu, wd)


def _combine_kernel(dest_ref, h_ref, gate_ref, g2_ref, b2_ref, ys_ref, o_ref, buf_ref, sem):
    R = MOVE_ROWS

    def issue(r, carry):
        _row_copy(ys_ref, dest_ref[0, 0, 2 * r], buf_ref.at[0], r, sem).start(priority=0)
        _row_copy(ys_ref, dest_ref[0, 0, 2 * r + 1], buf_ref.at[1], r, sem).start(priority=1)
        return carry

    lax.fori_loop(0, R, issue, 0, unroll=ISSUE_UNROLL)
    for slot in range(2):
        pltpu.make_async_copy(ys_ref.at[pl.ds(0, R), :], buf_ref.at[slot], sem).wait()
    y0 = _unpack_rows(buf_ref[0])
    y1 = _unpack_rows(buf_ref[1])
    ffn = gate_ref[:, 0:1] * y0 + gate_ref[:, 1:2] * y1
    o_ref[...] = _layer_norm(DEEPNORM_ALPHA * h_ref[...] + ffn, g2_ref[...], b2_ref[...])


def _combine_call(dest3, h, gate, g2, b2, ys):
    N, D = h.shape
    W = ys.shape[1]
    R = MOVE_ROWS
    nt = N // R
    return pl.pallas_call(
        _combine_kernel, grid=(nt,),
        in_specs=[
            pl.BlockSpec((1, 1, 2 * R), lambda i: (i, 0, 0), memory_space=pltpu.SMEM),
            pl.BlockSpec((R, D), lambda i: (i, 0)),
            pl.BlockSpec((R, 2), lambda i: (i, 0)),
            pl.BlockSpec((1, D), lambda i: (0, 0)),
            pl.BlockSpec((1, D), lambda i: (0, 0)),
            pl.BlockSpec(memory_space=pl.ANY),
        ],
        out_specs=pl.BlockSpec((R, D), lambda i: (i, 0)),
        out_shape=jax.ShapeDtypeStruct((N, D), jnp.float32),
        scratch_shapes=[pltpu.VMEM((2, R, W), jnp.int32), pltpu.SemaphoreType.DMA],
        compiler_params=_cparams(1, 4 * R * D * 4 + 2 * R * W * 4 + 8 * R * D * 4 + (4 << 20)),
        name="moe_combine",
    )(dest3, h, gate, g2, b2, ys)


def _rope_tables(seq):
    inv_freq = ROPE_THETA ** (-jnp.arange(ROT_HALF, dtype=jnp.float32) / ROT_HALF)
    ang = jnp.arange(seq, dtype=jnp.int32).astype(jnp.float32)[:, None] * inv_freq[None, :]
    cos, sin = jnp.cos(ang), jnp.sin(ang)
    ones = jnp.ones((seq, HEAD_DIM - ROT_DIMS), jnp.float32)
    zeros = jnp.zeros((seq, HEAD_DIM - ROT_DIMS), jnp.float32)
    c_head = jnp.concatenate([cos, cos, ones], axis=1)
    s_head = jnp.concatenate([-sin, sin, zeros], axis=1)
    reps = KN_WIDTH // HEAD_DIM
    return cos.T, sin.T, jnp.tile(c_head, (1, reps)), jnp.tile(s_head, (1, reps))


def _swap_rot_cols(w):
    d, n = w.shape
    wh = w.reshape(d, n // HEAD_DIM, HEAD_DIM)
    sw = jnp.concatenate([wh[:, :, ROT_HALF:ROT_DIMS], wh[:, :, :ROT_HALF],
                          jnp.zeros((d, n // HEAD_DIM, HEAD_DIM - ROT_DIMS), w.dtype)], axis=2)
    return sw.reshape(d, n)


def _block(x, w_in, gate_bias, w_attn_up, w_conv_up, conv_w, w_out, ln_g, ln_b,
           rg_w, rg_b, re_w, re_b, w_gate_e, w_up_e, w_down_e, ln2_g, ln2_b):
    B, S, D = x.shape
    N = B * S
    top_k = min(TOPK_MAX, S // 4)
    bf = jnp.bfloat16
    o = np.cumsum([0, Q_WIDTH, KV_WIDTH, KV_WIDTH, IDXQ_WIDTH, IDX_DIM, IDX_HEADS,
                   CONV_DIM, CONV_DIM, CONV_DIM, N_BRANCHES * D_MODEL])
    w_q, w_k, w_v, w_qi, w_ki, w_wi = (w_in[:, o[i]:o[i + 1]] for i in range(6))
    w_conv = w_in[:, o[6]:o[9]]
    w_gates = w_in[:, o[9]:o[10]]

    wt = jnp.concatenate([w_q.T * (HEAD_DIM ** -0.5 * LOG2_E), w_qi.T, w_v.T, w_wi.T,
                          jnp.zeros((T_ROWS - T_WI0 - IDX_HEADS, D), w_in.dtype)], axis=0).astype(bf)
    pad = jnp.zeros((D, KN_WIDTH - KV_WIDTH - IDX_DIM), w_in.dtype)
    wn = jnp.concatenate([w_k, w_ki, pad, _swap_rot_cols(w_k), _swap_rot_cols(w_ki), pad], axis=1).astype(bf)
    cos_t, sin_t, cos_n, sgn_n = _rope_tables(S)

    qt, qit, vt, wit, kn = _proj_call(x, wt, wn, cos_t, sin_t, cos_n, sgn_n)
    attn = _attn_call(qt, qit, wit, kn, vt, top_k)

    w_r = jnp.concatenate([rg_w, jnp.transpose(re_w, (1, 0, 2)).reshape(D, N_EXPERTS),
                           jnp.zeros((D, ROUTER_LANES - E0 - N_EXPERTS), rg_w.dtype)], axis=1)
    b_r = jnp.concatenate([rg_b, re_b.reshape(-1),
                           jnp.zeros((ROUTER_LANES - E0 - N_EXPERTS,), rg_b.dtype)])[None, :]
    w_rh = w_r.astype(bf)
    w_rl = (w_r - w_rh.astype(jnp.float32)).astype(bf)

    h, hp, eid, rank, gate, cnt = _mix_call(
        x.reshape(N, D), attn.reshape(N, Q_WIDTH), w_conv.astype(bf), w_gates.astype(bf),
        gate_bias[None, :], conv_w, w_attn_up.astype(bf), w_conv_up.astype(bf), w_out.astype(bf),
        ln_g[None, :], ln_b[None, :], w_rh, w_rl, b_r, S)

    RB = EXPERT_ROWS
    counts = cnt[0, E0:E0 + N_EXPERTS].astype(jnp.int32)
    padded = ((counts + RB - 1) // RB) * RB
    pad_end = jnp.cumsum(padded)
    pad_start = pad_end - padded
    dest = jnp.sum(jnp.where(eid[:, :, None] == jnp.arange(N_EXPERTS, dtype=jnp.int32)[None, None, :],
                             pad_start[None, None, :], 0), axis=-1) + rank
    cap = N * 2 + N_EXPERTS * RB
    nb = cap // RB
    block_start = jnp.arange(nb, dtype=jnp.int32) * RB
    block_expert = jnp.minimum(jnp.sum((block_start[:, None] >= pad_end[None, :]).astype(jnp.int32), axis=1),
                               N_EXPERTS - 1)
    n_used = (pad_end[-1:] // RB).astype(jnp.int32)
    dest3 = dest.reshape(N // MOVE_ROWS, 1, 2 * MOVE_ROWS)

    xs = _dispatch_call(dest3, hp, jnp.zeros((cap, HALF), jnp.int32))
    ys = _expert_call(block_expert, n_used, xs, w_gate_e, w_up_e, w_down_e)
    out = _combine_call(dest3, h, gate, ln2_g[None, :], ln2_b[None, :], ys)
    return out.reshape(B, S, D)


def kernel(x, w_in, gate_bias, w_attn_up, w_conv_up, conv_w, w_out, ln1_g, ln1_b, router_group_w,
           router_group_b, router_expert_w, router_expert_b, w_gate_e, w_up_e, w_down_e, ln2_g, ln2_b):
    h = x
    for l in range(DEPTH):
        h = _block(h, w_in[l], gate_bias[l], w_attn_up[l], w_conv_up[l], conv_w[l], w_out[l],
                   ln1_g[l], ln1_b[l], router_group_w[l], router_group_b[l], router_expert_w[l],
                   router_expert_b[l], w_gate_e[l], w_up_e[l], w_down_e[l], ln2_g[l], ln2_b[l])
    return h
```

```python
import functools

import jax
import jax.numpy as jnp
import numpy as np
from jax import lax
from jax.experimental import pallas as pl
from jax.experimental.pallas import tpu as pltpu

D_MODEL = 1024
N_HEADS = 8
N_KV_HEADS = 2
HEAD_DIM = 64
Q_WIDTH = N_HEADS * HEAD_DIM
KV_WIDTH = N_KV_HEADS * HEAD_DIM
ROPE_THETA = 500000.0
ROT_DIMS = HEAD_DIM // 4
ROT_HALF = ROT_DIMS // 2
IDX_HEADS = 8
IDX_DIM = 64
IDXQ_WIDTH = IDX_HEADS * IDX_DIM
TOPK_MAX = 256
CONV_DIM = 512
CONV_WIDTH = 3
N_BRANCHES = 2
N_GROUPS = 4
EXPERTS_PER_GROUP = 8
N_EXPERTS = N_GROUPS * EXPERTS_PER_GROUP
D_EXPERT = 512
LN_EPS = 1e-5
DEPTH = 1
DEEPNORM_ALPHA = (2 * DEPTH) ** 0.25

V7X_LANES = 128
V7X_SUBLANES = 8
V7X_VMEM_LIMIT_BYTES = 56 * 1024 * 1024

PROJ_ROWS = 512
ATTN_CHUNK = 256
ATTN_KEY_TILE = 128
DEN_ROWS = 16
ACC_ROWS = HEAD_DIM + DEN_ROWS
LOG2_E = 1.4426950408889634
MIX_ROWS = 512
EXPERT_ROWS = 256
MOVE_ROWS = 512
ISSUE_UNROLL = 8
HALF = D_MODEL // 2

T_Q0, T_QI0, T_V0, T_WI0 = 0, Q_WIDTH, Q_WIDTH + IDXQ_WIDTH, Q_WIDTH + IDXQ_WIDTH + KV_WIDTH
T_ROWS = T_WI0 + 16
KN_WIDTH = 256

INT_MIN = -2147483648
HALF_RANGE = 32768
ROUTER_LANES = 128
E0 = N_GROUPS


def _cparams(n_axes, vmem_bytes):
    return pltpu.CompilerParams(
        dimension_semantics=("arbitrary",) * n_axes,
        vmem_limit_bytes=int(min(vmem_bytes, V7X_VMEM_LIMIT_BYTES)),
    )


def _proj_kernel(x_ref, wt_ref, wn_ref, cost_ref, sint_ref, cosn_ref, sgnn_ref,
                 qt_ref, qit_ref, vt_ref, wit_ref, kn_ref):
    xb = x_ref[0].astype(jnp.bfloat16)
    pt = lax.dot_general(wt_ref[...], xb, (((1,), (1,)), ((), ())),
                         preferred_element_type=jnp.float32)
    cos = cost_ref[...]
    sin = sint_ref[...]

    def rope_t(dst_ref, base):
        for h in range(N_HEADS):
            r0 = base + h * HEAD_DIM
            x1 = pt[r0:r0 + ROT_HALF]
            x2 = pt[r0 + ROT_HALF:r0 + ROT_DIMS]
            head = jnp.concatenate([x1 * cos - x2 * sin, x2 * cos + x1 * sin,
                                    pt[r0 + ROT_DIMS:r0 + HEAD_DIM]], axis=0).astype(dst_ref.dtype)
            for jj, cols in enumerate(chunks):
                dst_ref[0, jj, h * HEAD_DIM:(h + 1) * HEAD_DIM, :] = head[:, cols]

    chunks = [slice(jj * ATTN_CHUNK, (jj + 1) * ATTN_CHUNK) for jj in range(PROJ_ROWS // ATTN_CHUNK)]
    rope_t(qt_ref, T_Q0)
    rope_t(qit_ref, T_QI0)
    for jj, cols in enumerate(chunks):
        vt_ref[0, jj] = pt[T_V0:T_V0 + KV_WIDTH, cols].astype(vt_ref.dtype)
        wit_ref[0, jj] = pt[T_WI0:T_WI0 + IDX_HEADS, cols]
    pn = jnp.dot(xb, wn_ref[...], preferred_element_type=jnp.float32)
    kn = pn[:, :KN_WIDTH] * cosn_ref[...] + pn[:, KN_WIDTH:] * sgnn_ref[...]
    kn_ref[0] = kn.astype(kn_ref.dtype)


def _proj_call(x, wt, wn, cos_t, sin_t, cos_n, sgn_n):
    B, S, D = x.shape
    R = PROJ_ROWS
    nt = S // R
    grid = (B, nt)
    out_shape = (
        jax.ShapeDtypeStruct((B, S // ATTN_CHUNK, Q_WIDTH, ATTN_CHUNK), jnp.bfloat16),
        jax.ShapeDtypeStruct((B, S // ATTN_CHUNK, IDXQ_WIDTH, ATTN_CHUNK), jnp.bfloat16),
        jax.ShapeDtypeStruct((B, S // ATTN_CHUNK, KV_WIDTH, ATTN_CHUNK), jnp.bfloat16),
        jax.ShapeDtypeStruct((B, S // ATTN_CHUNK, IDX_HEADS, ATTN_CHUNK), jnp.float32),
        jax.ShapeDtypeStruct((B, S, KN_WIDTH), jnp.bfloat16),
    )
    in_specs = [
        pl.BlockSpec((1, R, D), lambda b, j: (b, j, 0)),
        pl.BlockSpec((T_ROWS, D), lambda b, j: (0, 0)),
        pl.BlockSpec((D, 2 * KN_WIDTH), lambda b, j: (0, 0)),
        pl.BlockSpec((ROT_HALF, R), lambda b, j: (0, j)),
        pl.BlockSpec((ROT_HALF, R), lambda b, j: (0, j)),
        pl.BlockSpec((R, KN_WIDTH), lambda b, j: (j, 0)),
        pl.BlockSpec((R, KN_WIDTH), lambda b, j: (j, 0)),
    ]
    out_specs = (
        pl.BlockSpec((1, R // ATTN_CHUNK, Q_WIDTH, ATTN_CHUNK), lambda b, j: (b, j, 0, 0)),
        pl.BlockSpec((1, R // ATTN_CHUNK, IDXQ_WIDTH, ATTN_CHUNK), lambda b, j: (b, j, 0, 0)),
        pl.BlockSpec((1, R // ATTN_CHUNK, KV_WIDTH, ATTN_CHUNK), lambda b, j: (b, j, 0, 0)),
        pl.BlockSpec((1, R // ATTN_CHUNK, IDX_HEADS, ATTN_CHUNK), lambda b, j: (b, j, 0, 0)),
        pl.BlockSpec((1, R, KN_WIDTH), lambda b, j: (b, j, 0)),
    )
    vmem = 2 * (R * D * 4 + T_ROWS * D * 2 + D * 2 * KN_WIDTH * 2) + 6 * T_ROWS * R * 4 + (8 << 20)
    return pl.pallas_call(
        _proj_kernel, grid=grid, in_specs=in_specs, out_specs=out_specs, out_shape=out_shape,
        compiler_params=_cparams(2, vmem), name="dsa_proj",
    )(x, wt, wn, cos_t, sin_t, cos_n, sgn_n)


def _float_to_key(s):
    b = lax.bitcast_convert_type(s, jnp.int32)
    k = b ^ (lax.shift_right_arithmetic(b, 31) & jnp.int32(0x7FFFFFFF))
    return jnp.where(b == jnp.int32(INT_MIN), jnp.int32(0), k)


def _attn_kernel(qt_ref, qit_ref, wit_ref, kn_ref, vt_ref, o_ref,
                 key_ref, hi_ref, lo_ref, lga_ref, lgb_ref, acc_ref, m_ref, *, seq, top_k):
    C = ATTN_CHUNK
    c = pl.program_id(1)
    nkb = c + 1
    zeros_half = jnp.zeros((HEAD_DIM, C), jnp.bfloat16)
    row_i = lax.broadcasted_iota(jnp.int32, (C, C), 0)
    lane_i = lax.broadcasted_iota(jnp.int32, (C, C), 1)
    causal_in_block = row_i <= lane_i

    def score_body(kb, carry):
        r0 = pl.multiple_of(kb * C, C)
        kix = kn_ref[0, pl.ds(r0, C), KV_WIDTH:KN_WIDTH]
        acc = None
        for h in range(IDX_HEADS):
            rhs = jnp.concatenate([qit_ref[0, 0, h * IDX_DIM:(h + 1) * IDX_DIM, :], zeros_half], axis=0)
            s = jnp.dot(kix, rhs, preferred_element_type=jnp.float32)
            t = jnp.maximum(s, 0.0) * wit_ref[0, 0, h:h + 1, :]
            acc = t if acc is None else acc + t
        score = acc * (IDX_DIM ** -0.5 * IDX_HEADS ** -0.5)
        keys = jnp.where(causal_in_block | (kb != c), _float_to_key(score), jnp.int32(INT_MIN))
        key_ref[kb] = keys
        hi_ref[kb] = lax.shift_right_arithmetic(keys, 16).astype(jnp.int16)
        lo_ref[kb] = ((keys & jnp.int32(0xFFFF)) - jnp.int32(HALF_RANGE)).astype(jnp.int16)
        return carry

    lax.fori_loop(0, nkb, score_body, 0)

    def count(ref, pred):
        packed = ref.dtype == jnp.int16

        def body(kb, part):
            hit = pred(ref[kb], kb)
            if packed:
                words = pltpu.bitcast(jnp.where(hit, jnp.int16(1), jnp.int16(0)), jnp.int32)
            else:
                words = hit.astype(jnp.int32)
            return part + jnp.sum(words.reshape(-1, V7X_SUBLANES, C), axis=0)

        part = lax.fori_loop(0, nkb, body, jnp.zeros((V7X_SUBLANES, C), jnp.int32))
        if packed:
            part = (part & jnp.int32(0xFFFF)) + lax.shift_right_logical(part, 16)
        return jnp.sum(part, axis=0, keepdims=True)

    def kth_largest_16(ref):
        def body(i, ans_u):
            cand_u = ans_u | lax.shift_left(jnp.int32(1), jnp.int32(15) - i)
            cand = (cand_u - jnp.int32(HALF_RANGE)).astype(jnp.int16)
            return jnp.where(count(ref, lambda k, kb: k >= cand) >= top_k, cand_u, ans_u)

        return lax.fori_loop(0, 16, body, jnp.zeros((1, C), jnp.int32)) - jnp.int32(HALF_RANGE)

    t_hi = kth_largest_16(hi_ref)
    t_hi16 = t_hi.astype(jnp.int16)

    def refine_body(kb, carry):
        hi = hi_ref[kb]
        lo_ref[kb] = jnp.where(hi > t_hi16, jnp.int16(HALF_RANGE - 1),
                               jnp.where(hi == t_hi16, lo_ref[kb], jnp.int16(-HALF_RANGE)))
        return carry

    lax.fori_loop(0, nkb, refine_body, 0)
    t_lo = kth_largest_16(lo_ref)
    thr_raw = t_hi * jnp.int32(2 * HALF_RANGE) + (t_lo + jnp.int32(HALF_RANGE))
    thr = jnp.maximum(thr_raw, jnp.int32(INT_MIN + 1))

    t_lo16 = t_lo.astype(jnp.int16)
    n_ge = count(lo_ref, lambda k, kb: k >= t_lo16)
    has_ties = jnp.max(jnp.where(thr_raw != jnp.int32(INT_MIN), n_ge, 0)) > top_k

    @pl.when(has_ties)
    def _():
        need = (top_k - count(key_ref, lambda k, kb: k > thr)).astype(jnp.float32)
        earlier_rows = (row_i > lane_i).astype(jnp.bfloat16)

        def drop_body(kb, seen):
            kk = key_ref[kb]
            tied = kk == thr
            tied_b = jnp.where(tied, 1.0, 0.0).astype(jnp.bfloat16)
            rank = jnp.dot(earlier_rows, tied_b, preferred_element_type=jnp.float32) + seen
            key_ref[kb] = jnp.where(tied & (rank >= need), kk - 1, kk)
            return seen + jnp.sum(tied_b.astype(jnp.float32), axis=0, keepdims=True)

        lax.fori_loop(0, nkb, drop_body, jnp.zeros((1, C), jnp.float32))

    neg_inf = jnp.float32(-jnp.inf)
    m_ref[...] = jnp.full(m_ref.shape, neg_inf, jnp.float32)
    acc_ref[...] = jnp.zeros(acc_ref.shape, jnp.float32)
    T = ATTN_KEY_TILE
    ones_rows = jnp.ones((DEN_ROWS, C), jnp.bfloat16)

    kv_group = N_HEADS // N_KV_HEADS

    def bias_body(kb, carry):
        key_ref[kb] = lax.bitcast_convert_type(jnp.where(key_ref[kb] >= thr, 0.0, neg_inf), jnp.int32)
        return carry

    lax.fori_loop(0, nkb, bias_body, 0)
    n_blocks = seq // C
    key_ref[n_blocks] = lax.bitcast_convert_type(jnp.full((C, C), neg_inf, jnp.float32), jnp.int32)

    def store_logits(kb, dst_ref, h):
        kb_mem = jnp.minimum(kb, nkb - 1)
        kb_bias = jnp.where(kb < nkb, kb, n_blocks)
        r0 = pl.multiple_of(kb_mem * C, C)
        bias = lax.bitcast_convert_type(key_ref[kb_bias], jnp.float32)
        k2 = kn_ref[0, pl.ds(r0, C), 0:KV_WIDTH]
        qh = qt_ref[0, 0, h * HEAD_DIM:(h + 1) * HEAD_DIM, :]
        rhs = jnp.concatenate([qh, zeros_half] if h < kv_group else [zeros_half, qh], axis=0)
        dst_ref[h] = jnp.dot(k2, rhs, preferred_element_type=jnp.float32) + bias

    def softmax_block(kb, src_ref, kb_next, dst_ref):
        kb_mem = jnp.minimum(kb, nkb - 1)
        for h in range(N_HEADS):
            store_logits(kb_next, dst_ref, h)
            g = h // kv_group
            m_old = m_ref[h:h + 1, :]
            m_new = jnp.maximum(m_old, jnp.max(src_ref[h], axis=0, keepdims=True))
            m_safe = jnp.where(m_new == neg_inf, 0.0, m_new)
            p = jnp.concatenate(
                [jnp.exp2(src_ref[h, s * T:(s + 1) * T, :] - m_safe).astype(jnp.bfloat16) for s in range(C // T)],
                axis=0)
            alpha = jnp.exp2(m_old - m_safe)
            vt = vt_ref[0, kb_mem, g * HEAD_DIM:(g + 1) * HEAD_DIM, :]
            pv = jnp.dot(jnp.concatenate([vt, ones_rows], axis=0), p,
                         preferred_element_type=jnp.float32)
            hs = slice(h * ACC_ROWS, (h + 1) * ACC_ROWS)
            acc_ref[hs, :] = alpha * acc_ref[hs, :] + pv
            m_ref[h:h + 1, :] = m_new

    for h in range(N_HEADS):
        store_logits(0, lga_ref, h)

    def attn_body(pair, carry):
        kb = 2 * pair
        softmax_block(kb, lga_ref, kb + 1, lgb_ref)
        softmax_block(kb + 1, lgb_ref, kb + 2, lga_ref)
        return carry

    lax.fori_loop(0, (nkb + 1) // 2, attn_body, 0)
    outs = []
    for h in range(N_HEADS):
        num = acc_ref[h * ACC_ROWS:h * ACC_ROWS + HEAD_DIM, :]
        den = acc_ref[h * ACC_ROWS + HEAD_DIM:h * ACC_ROWS + HEAD_DIM + 1, :]
        outs.append(num / den)
    o_ref[0] = jnp.transpose(jnp.concatenate(outs, axis=0)).astype(o_ref.dtype)


def _attn_call(qt, qit, wit, kn, vt, top_k):
    B, S = kn.shape[:2]
    C = ATTN_CHUNK
    nc = S // C
    kern = functools.partial(_attn_kernel, seq=S, top_k=top_k)
    in_specs = [
        pl.BlockSpec((1, 1, Q_WIDTH, C), lambda b, c: (b, c, 0, 0)),
        pl.BlockSpec((1, 1, IDXQ_WIDTH, C), lambda b, c: (b, c, 0, 0)),
        pl.BlockSpec((1, 1, IDX_HEADS, C), lambda b, c: (b, c, 0, 0)),
        pl.BlockSpec((1, S, KN_WIDTH), lambda b, c: (b, 0, 0)),
        pl.BlockSpec((1, nc, KV_WIDTH, C), lambda b, c: (b, 0, 0, 0)),
    ]
    out_specs = pl.BlockSpec((1, C, Q_WIDTH), lambda b, c: (b, c, 0))
    scratch = [
        pltpu.VMEM((nc + 1, C, C), jnp.int32),
        pltpu.VMEM((nc, C, C), jnp.int16),
        pltpu.VMEM((nc, C, C), jnp.int16),
        pltpu.VMEM((N_HEADS, C, C), jnp.float32),
        pltpu.VMEM((N_HEADS, C, C), jnp.float32),
        pltpu.VMEM((N_HEADS * ACC_ROWS, C), jnp.float32),
        pltpu.VMEM((N_HEADS, C), jnp.float32),
    ]
    vmem = (2 * S * C * 4 + Q_WIDTH * C * 4 + 2 * 2 * (2 * Q_WIDTH * C + S * KN_WIDTH + S * KV_WIDTH + C * Q_WIDTH)
            + 24 * C * C * 4 + (8 << 20))
    return pl.pallas_call(
        kern, grid=(B, nc), in_specs=in_specs, out_specs=out_specs,
        out_shape=jax.ShapeDtypeStruct((B, S, Q_WIDTH), jnp.bfloat16),
        scratch_shapes=scratch, compiler_params=_cparams(2, vmem), name="dsa_attn",
    )(qt, qit, wit, kn, vt)


def _layer_norm(v, g, b):
    mu = jnp.mean(v, axis=-1, keepdims=True)
    d = v - mu
    var = jnp.mean(d * d, axis=-1, keepdims=True)
    return d * lax.rsqrt(var + LN_EPS) * g + b


def _pack_rows(h):
    hi = lax.bitcast_convert_type(h[:, :HALF].astype(jnp.bfloat16).astype(jnp.float32), jnp.int32)
    lo = lax.bitcast_convert_type(h[:, HALF:].astype(jnp.bfloat16).astype(jnp.float32), jnp.int32)
    return (hi & jnp.int32(-65536)) | lax.shift_right_logical(lo, 16)


def _unpack_rows(w):
    hi = lax.bitcast_convert_type(w & jnp.int32(-65536), jnp.float32)
    lo = lax.bitcast_convert_type(lax.shift_left(w, 16), jnp.float32)
    return jnp.concatenate([hi, lo], axis=1)


def _mix_kernel(x_ref, attn_ref, wc_ref, wg_ref, gb_ref, cw_ref, wau_ref, wcu_ref, wo_ref,
                g1_ref, b1_ref, wrh_ref, wrl_ref, rb_ref,
                h_ref, hp_ref, eid_ref, rank_ref, gate_ref, cnt_ref,
                ubuf_ref, base_ref, *, steps_per_seq):
    R = MIX_ROWS
    i = pl.program_id(0)

    @pl.when(i == 0)
    def _():
        base_ref[...] = jnp.zeros_like(base_ref)

    @pl.when(i % steps_per_seq == 0)
    def _():
        ubuf_ref[0:V7X_SUBLANES, :] = jnp.zeros((V7X_SUBLANES, CONV_DIM), jnp.float32)

    x = x_ref[...]
    xb = x.astype(jnp.bfloat16)
    cv = jnp.dot(xb, wc_ref[...], preferred_element_type=jnp.float32)
    u = cv[:, 2 * CONV_DIM:] * cv[:, :CONV_DIM]
    ubuf_ref[V7X_SUBLANES:, :] = u
    u1 = ubuf_ref[V7X_SUBLANES - 1:V7X_SUBLANES - 1 + R, :]
    u2 = ubuf_ref[V7X_SUBLANES - 2:V7X_SUBLANES - 2 + R, :]
    y = cw_ref[0:1, :] * u2 + cw_ref[1:2, :] * u1 + cw_ref[2:3, :] * u
    conv = (cv[:, CONV_DIM:2 * CONV_DIM] * y).astype(jnp.bfloat16)
    ubuf_ref[0:V7X_SUBLANES, :] = u[R - V7X_SUBLANES:, :]
    z = jnp.dot(xb, wg_ref[...], preferred_element_type=jnp.float32) + gb_ref[...]
    gates = 1.0 / (1.0 + jnp.exp(-z))
    au = jnp.dot(attn_ref[...], wau_ref[...], preferred_element_type=jnp.float32)
    cu = jnp.dot(conv, wcu_ref[...], preferred_element_type=jnp.float32)
    merged = gates[:, :D_MODEL] * au + gates[:, D_MODEL:] * cu
    mix = jnp.dot(merged.astype(jnp.bfloat16), wo_ref[...], preferred_element_type=jnp.float32)
    h = _layer_norm(DEEPNORM_ALPHA * x + mix, g1_ref[...], b1_ref[...])
    h_ref[...] = h
    hp_ref[...] = _pack_rows(h)

    h_hi = h.astype(jnp.bfloat16)
    h_lo = (h - h_hi.astype(jnp.float32)).astype(jnp.bfloat16)
    lg = (jnp.dot(h_hi, wrh_ref[...], preferred_element_type=jnp.float32)
          + jnp.dot(h_lo, wrh_ref[...], preferred_element_type=jnp.float32)
          + jnp.dot(h_hi, wrl_ref[...], preferred_element_type=jnp.float32)) + rb_ref[...]
    lane = lax.broadcasted_iota(jnp.int32, (R, ROUTER_LANES), 1).astype(jnp.float32)
    neg = jnp.float32(-jnp.inf)
    no_lane = jnp.float32(ROUTER_LANES)
    gmask = lane < N_GROUPS
    gl = jnp.where(gmask, lg, neg)
    gmax = jnp.max(gl, axis=1, keepdims=True)
    grp = jnp.min(jnp.where(gl == gmax, lane, no_lane), axis=1, keepdims=True)
    gsum = jnp.sum(jnp.where(gmask, jnp.exp(gl - gmax), 0.0), axis=1, keepdims=True)
    p_grp = 1.0 / gsum
    lo_lane = E0 + grp * EXPERTS_PER_GROUP
    emask = (lane >= lo_lane) & (lane < lo_lane + EXPERTS_PER_GROUP)
    el = jnp.where(emask, lg, neg)
    v1 = jnp.max(el, axis=1, keepdims=True)
    i1 = jnp.min(jnp.where(el == v1, lane, no_lane), axis=1, keepdims=True)
    el2 = jnp.where(lane == i1, neg, el)
    v2 = jnp.max(el2, axis=1, keepdims=True)
    i2 = jnp.min(jnp.where(el2 == v2, lane, no_lane), axis=1, keepdims=True)
    a = jnp.exp(v2 - v1)
    inv = 1.0 / (1.0 + a)
    gate_ref[:, 0:1] = p_grp * inv
    gate_ref[:, 1:2] = p_grp * (a * inv)
    eid_ref[:, 0:1] = (i1 - E0).astype(jnp.int32)
    eid_ref[:, 1:2] = (i2 - E0).astype(jnp.int32)
    oh1 = lane == i1
    oh2 = lane == i2
    oh = (oh1 | oh2).astype(jnp.bfloat16)
    r_i = lax.broadcasted_iota(jnp.int32, (R, R), 0)
    c_i = lax.broadcasted_iota(jnp.int32, (R, R), 1)
    tri = (r_i > c_i).astype(jnp.bfloat16)
    before = jnp.dot(tri, oh, preferred_element_type=jnp.float32) + base_ref[...]
    rank_ref[:, 0:1] = jnp.sum(jnp.where(oh1, before, 0.0), axis=1, keepdims=True).astype(jnp.int32)
    rank_ref[:, 1:2] = jnp.sum(jnp.where(oh2, before, 0.0), axis=1, keepdims=True).astype(jnp.int32)
    base_ref[...] = base_ref[...] + jnp.sum(oh.astype(jnp.float32), axis=0, keepdims=True)
    cnt_ref[...] = base_ref[...]


def _mix_call(x2, attn2, wc, wg, gb, cw, wau, wcu, wo, g1, b1, wrh, wrl, rb, seq):
    N, D = x2.shape
    R = MIX_ROWS
    nt = N // R
    kern = functools.partial(_mix_kernel, steps_per_seq=seq // R)

    def full(a):
        return pl.BlockSpec(a.shape, lambda i: (0,) * a.ndim)

    in_specs = [
        pl.BlockSpec((R, D), lambda i: (i, 0)),
        pl.BlockSpec((R, Q_WIDTH), lambda i: (i, 0)),
        full(wc), full(wg), full(gb), full(cw), full(wau), full(wcu), full(wo),
        full(g1), full(b1), full(wrh), full(wrl), full(rb),
    ]
    out_shape = (
        jax.ShapeDtypeStruct((N, D), jnp.float32),
        jax.ShapeDtypeStruct((N, HALF), jnp.int32),
        jax.ShapeDtypeStruct((N, 2), jnp.int32),
        jax.ShapeDtypeStruct((N, 2), jnp.int32),
        jax.ShapeDtypeStruct((N, 2), jnp.float32),
        jax.ShapeDtypeStruct((1, ROUTER_LANES), jnp.float32),
    )
    out_specs = (
        pl.BlockSpec((R, D), lambda i: (i, 0)),
        pl.BlockSpec((R, HALF), lambda i: (i, 0)),
        pl.BlockSpec((R, 2), lambda i: (i, 0)),
        pl.BlockSpec((R, 2), lambda i: (i, 0)),
        pl.BlockSpec((R, 2), lambda i: (i, 0)),
        pl.BlockSpec((1, ROUTER_LANES), lambda i: (0, 0)),
    )
    scratch = [pltpu.VMEM((R + V7X_SUBLANES, CONV_DIM), jnp.float32),
               pltpu.VMEM((1, ROUTER_LANES), jnp.float32)]
    w_bytes = 2 * (wc.size + wg.size + wau.size + wcu.size + wo.size + wrh.size + wrl.size)
    vmem = 2 * w_bytes + 2 * (R * D * 4 * 2 + R * Q_WIDTH * 2 + R * HALF * 4) + 10 * R * 2048 * 4 + (6 << 20)
    return pl.pallas_call(
        kern, grid=(nt,), in_specs=in_specs, out_specs=out_specs, out_shape=out_shape,
        scratch_shapes=scratch, compiler_params=_cparams(1, vmem), name="mix_ln_router",
    )(x2, attn2, wc, wg, gb, cw, wau, wcu, wo, g1, b1, wrh, wrl, rb)


def _row_copy(src_ref, s, dst_ref, d, sem):
    return pltpu.make_async_copy(src_ref.at[pl.ds(s, 1), :], dst_ref.at[pl.ds(d, 1), :], sem)


def _dispatch_kernel(dest_ref, hp_ref, xs_in_ref, xs_ref, sem):
    del xs_in_ref
    R = MOVE_ROWS

    def issue(r, carry):
        _row_copy(hp_ref, r, xs_ref, dest_ref[0, 0, 2 * r], sem).start(priority=0)
        _row_copy(hp_ref, r, xs_ref, dest_ref[0, 0, 2 * r + 1], sem).start(priority=1)
        return carry

    lax.fori_loop(0, R, issue, 0, unroll=ISSUE_UNROLL)
    for _ in range(2):
        pltpu.make_async_copy(hp_ref, xs_ref.at[pl.ds(0, R), :], sem).wait()


def _dispatch_call(dest3, hp, xs0):
    N, W = hp.shape
    R = MOVE_ROWS
    nt = N // R
    return pl.pallas_call(
        _dispatch_kernel, grid=(nt,),
        in_specs=[
            pl.BlockSpec((1, 1, 2 * R), lambda i: (i, 0, 0), memory_space=pltpu.SMEM),
            pl.BlockSpec((R, W), lambda i: (i, 0)),
            pl.BlockSpec(memory_space=pl.ANY),
        ],
        out_specs=pl.BlockSpec(memory_space=pl.ANY),
        out_shape=jax.ShapeDtypeStruct(xs0.shape, xs0.dtype),
        scratch_shapes=[pltpu.SemaphoreType.DMA],
        input_output_aliases={2: 0},
        compiler_params=_cparams(1, 4 * R * W * 4 + (4 << 20)), name="moe_dispatch",
    )(dest3, hp, xs0)


def _expert_kernel(be_ref, nb_ref, xs_ref, wg_ref, wu_ref, wd_ref, ys_ref, wgb_ref, wub_ref, wdb_ref):
    j = pl.program_id(0)

    @pl.when((j == 0) | (be_ref[j] != be_ref[jnp.maximum(j - 1, 0)]))
    def _():
        wgb_ref[...] = wg_ref[0].astype(jnp.bfloat16)
        wub_ref[...] = wu_ref[0].astype(jnp.bfloat16)
        wdb_ref[...] = wd_ref[0].astype(jnp.bfloat16)

    @pl.when(j < nb_ref[0])
    def _():
        xb = _unpack_rows(xs_ref[...]).astype(jnp.bfloat16)
        hg = jnp.dot(xb, wgb_ref[...], preferred_element_type=jnp.float32)
        hu = jnp.dot(xb, wub_ref[...], preferred_element_type=jnp.float32)
        hid = (hg / (1.0 + jnp.exp(-hg))) * hu
        y = jnp.dot(hid.astype(jnp.bfloat16), wdb_ref[...], preferred_element_type=jnp.float32)
        ys_ref[...] = _pack_rows(y)

    @pl.when(j >= nb_ref[0])
    def _():
        ys_ref[...] = jnp.zeros_like(ys_ref)


def _expert_call(block_expert, n_used, xs, wg, wu, wd):
    cap, W = xs.shape
    RB = EXPERT_ROWS
    nb = cap // RB
    grid_spec = pltpu.PrefetchScalarGridSpec(
        num_scalar_prefetch=2, grid=(nb,),
        in_specs=[
            pl.BlockSpec((RB, W), lambda j, be, nu: (j, 0)),
            pl.BlockSpec((1, D_MODEL, D_EXPERT), lambda j, be, nu: (be[j], 0, 0)),
            pl.BlockSpec((1, D_MODEL, D_EXPERT), lambda j, be, nu: (be[j], 0, 0)),
            pl.BlockSpec((1, D_EXPERT, D_MODEL), lambda j, be, nu: (be[j], 0, 0)),
        ],
        out_specs=pl.BlockSpec((RB, W), lambda j, be, nu: (j, 0)),
        scratch_shapes=[pltpu.VMEM((D_MODEL, D_EXPERT), jnp.bfloat16),
                        pltpu.VMEM((D_MODEL, D_EXPERT), jnp.bfloat16),
                        pltpu.VMEM((D_EXPERT, D_MODEL), jnp.bfloat16)],
    )
    vmem = (2 * 4 + 2) * 3 * D_MODEL * D_EXPERT + 4 * RB * W * 4 + 8 * RB * D_MODEL * 4 + (6 << 20)
    return pl.pallas_call(
        _expert_kernel, grid_spec=grid_spec,
        out_shape=jax.ShapeDtypeStruct((cap, W), jnp.int32),
        compiler_params=_cparams(1, vmem), name="moe_experts",
    )(block_expert, n_used, xs, wg, wu, wd)


def _combine_kernel(dest_ref, h_ref, gate_ref, g2_ref, b2_ref, ys_ref, o_ref, buf_ref, sem):
    R = MOVE_ROWS

    def issue(r, carry):
        _row_copy(ys_ref, dest_ref[0, 0, 2 * r], buf_ref.at[0], r, sem).start(priority=0)
        _row_copy(ys_ref, dest_ref[0, 0, 2 * r + 1], buf_ref.at[1], r, sem).start(priority=1)
        return carry

    lax.fori_loop(0, R, issue, 0, unroll=ISSUE_UNROLL)
    for slot in range(2):
        pltpu.make_async_copy(ys_ref.at[pl.ds(0, R), :], buf_ref.at[slot], sem).wait()
    y0 = _unpack_rows(buf_ref[0])
    y1 = _unpack_rows(buf_ref[1])
    ffn = gate_ref[:, 0:1] * y0 + gate_ref[:, 1:2] * y1
    o_ref[...] = _layer_norm(DEEPNORM_ALPHA * h_ref[...] + ffn, g2_ref[...], b2_ref[...])


def _combine_call(dest3, h, gate, g2, b2, ys):
    N, D = h.shape
    W = ys.shape[1]
    R = MOVE_ROWS
    nt = N // R
    return pl.pallas_call(
        _combine_kernel, grid=(nt,),
        in_specs=[
            pl.BlockSpec((1, 1, 2 * R), lambda i: (i, 0, 0), memory_space=pltpu.SMEM),
            pl.BlockSpec((R, D), lambda i: (i, 0)),
            pl.BlockSpec((R, 2), lambda i: (i, 0)),
            pl.BlockSpec((1, D), lambda i: (0, 0)),
            pl.BlockSpec((1, D), lambda i: (0, 0)),
            pl.BlockSpec(memory_space=pl.ANY),
        ],
        out_specs=pl.BlockSpec((R, D), lambda i: (i, 0)),
        out_shape=jax.ShapeDtypeStruct((N, D), jnp.float32),
        scratch_shapes=[pltpu.VMEM((2, R, W), jnp.int32), pltpu.SemaphoreType.DMA],
        compiler_params=_cparams(1, 4 * R * D * 4 + 2 * R * W * 4 + 8 * R * D * 4 + (4 << 20)),
        name="moe_combine",
    )(dest3, h, gate, g2, b2, ys)


def _rope_tables(seq):
    inv_freq = ROPE_THETA ** (-jnp.arange(ROT_HALF, dtype=jnp.float32) / ROT_HALF)
    ang = jnp.arange(seq, dtype=jnp.int32).astype(jnp.float32)[:, None] * inv_freq[None, :]
    cos, sin = jnp.cos(ang), jnp.sin(ang)
    ones = jnp.ones((seq, HEAD_DIM - ROT_DIMS), jnp.float32)
    zeros = jnp.zeros((seq, HEAD_DIM - ROT_DIMS), jnp.float32)
    c_head = jnp.concatenate([cos, cos, ones], axis=1)
    s_head = jnp.concatenate([-sin, sin, zeros], axis=1)
    reps = KN_WIDTH // HEAD_DIM
    return cos.T, sin.T, jnp.tile(c_head, (1, reps)), jnp.tile(s_head, (1, reps))


def _swap_rot_cols(w):
    d, n = w.shape
    wh = w.reshape(d, n // HEAD_DIM, HEAD_DIM)
    sw = jnp.concatenate([wh[:, :, ROT_HALF:ROT_DIMS], wh[:, :, :ROT_HALF],
                          jnp.zeros((d, n // HEAD_DIM, HEAD_DIM - ROT_DIMS), w.dtype)], axis=2)
    return sw.reshape(d, n)


def _block(x, w_in, gate_bias, w_attn_up, w_conv_up, conv_w, w_out, ln_g, ln_b,
           rg_w, rg_b, re_w, re_b, w_gate_e, w_up_e, w_down_e, ln2_g, ln2_b):
    B, S, D = x.shape
    N = B * S
    top_k = min(TOPK_MAX, S // 4)
    bf = jnp.bfloat16
    o = np.cumsum([0, Q_WIDTH, KV_WIDTH, KV_WIDTH, IDXQ_WIDTH, IDX_DIM, IDX_HEADS,
                   CONV_DIM, CONV_DIM, CONV_DIM, N_BRANCHES * D_MODEL])
    w_q, w_k, w_v, w_qi, w_ki, w_wi = (w_in[:, o[i]:o[i + 1]] for i in range(6))
    w_conv = w_in[:, o[6]:o[9]]
    w_gates = w_in[:, o[9]:o[10]]

    wt = jnp.concatenate([w_q.T * (HEAD_DIM ** -0.5 * LOG2_E), w_qi.T, w_v.T, w_wi.T,
                          jnp.zeros((T_ROWS - T_WI0 - IDX_HEADS, D), w_in.dtype)], axis=0).astype(bf)
    pad = jnp.zeros((D, KN_WIDTH - KV_WIDTH - IDX_DIM), w_in.dtype)
    wn = jnp.concatenate([w_k, w_ki, pad, _swap_rot_cols(w_k), _swap_rot_cols(w_ki), pad], axis=1).astype(bf)
    cos_t, sin_t, cos_n, sgn_n = _rope_tables(S)

    qt, qit, vt, wit, kn = _proj_call(x, wt, wn, cos_t, sin_t, cos_n, sgn_n)
    attn = _attn_call(qt, qit, wit, kn, vt, top_k)

    w_r = jnp.concatenate([rg_w, jnp.transpose(re_w, (1, 0, 2)).reshape(D, N_EXPERTS),
                           jnp.zeros((D, ROUTER_LANES - E0 - N_EXPERTS), rg_w.dtype)], axis=1)
    b_r = jnp.concatenate([rg_b, re_b.reshape(-1),
                           jnp.zeros((ROUTER_LANES - E0 - N_EXPERTS,), rg_b.dtype)])[None, :]
    w_rh = w_r.astype(bf)
    w_rl = (w_r - w_rh.astype(jnp.float32)).astype(bf)

    h, hp, eid, rank, gate, cnt = _mix_call(
        x.reshape(N, D), attn.reshape(N, Q_WIDTH), w_conv.astype(bf), w_gates.astype(bf),
        gate_bias[None, :], conv_w, w_attn_up.astype(bf), w_conv_up.astype(bf), w_out.astype(bf),
        ln_g[None, :], ln_b[None, :], w_rh, w_rl, b_r, S)

    RB = EXPERT_ROWS
    counts = cnt[0, E0:E0 + N_EXPERTS].astype(jnp.int32)
    padded = ((counts + RB - 1) // RB) * RB
    pad_end = jnp.cumsum(padded)
    pad_start = pad_end - padded
    dest = jnp.sum(jnp.where(eid[:, :, None] == jnp.arange(N_EXPERTS, dtype=jnp.int32)[None, None, :],
                             pad_start[None, None, :], 0), axis=-1) + rank
    cap = N * 2 + N_EXPERTS * RB
    nb = cap // RB
    block_start = jnp.arange(nb, dtype=jnp.int32) * RB
    block_expert = jnp.minimum(jnp.sum((block_start[:, None] >= pad_end[None, :]).astype(jnp.int32), axis=1),
                               N_EXPERTS - 1)
    n_used = (pad_end[-1:] // RB).astype(jnp.int32)
    dest3 = dest.reshape(N // MOVE_ROWS, 1, 2 * MOVE_ROWS)

    xs = _dispatch_call(dest3, hp, jnp.zeros((cap, HALF), jnp.int32))
    ys = _expert_call(block_expert, n_used, xs, w_gate_e, w_up_e, w_down_e)
    out = _combine_call(dest3, h, gate, ln2_g[None, :], ln2_b[None, :], ys)
    return out.reshape(B, S, D)


def kernel(x, w_in, gate_bias, w_attn_up, w_conv_up, conv_w, w_out, ln1_g, ln1_b, router_group_w,
           router_group_b, router_expert_w, router_expert_b, w_gate_e, w_up_e, w_down_e, ln2_g, ln2_b):
    h = x
    for l in range(DEPTH):
        h = _block(h, w_in[l], gate_bias[l], w_attn_up[l], w_conv_up[l], conv_w[l], w_out[l],
                   ln1_g[l], ln1_b[l], router_group_w[l], router_group_b[l], router_expert_w[l],
                   router_expert_b[l], w_gate_e[l], w_up_e[l], w_down_e[l], ln2_g[l], ln2_b[l])
    return h
```

```python
import functools

import jax
import jax.numpy as jnp
import numpy as np
from jax import lax
from jax.experimental import pallas as pl
from jax.experimental.pallas import tpu as pltpu

D_MODEL = 1024
N_HEADS = 8
N_KV_HEADS = 2
HEAD_DIM = 64
Q_WIDTH = N_HEADS * HEAD_DIM
KV_WIDTH = N_KV_HEADS * HEAD_DIM
ROPE_THETA = 500000.0
ROT_DIMS = HEAD_DIM // 4
ROT_HALF = ROT_DIMS // 2
IDX_HEADS = 8
IDX_DIM = 64
IDXQ_WIDTH = IDX_HEADS * IDX_DIM
TOPK_MAX = 256
CONV_DIM = 512
CONV_WIDTH = 3
N_BRANCHES = 2
N_GROUPS = 4
EXPERTS_PER_GROUP = 8
N_EXPERTS = N_GROUPS * EXPERTS_PER_GROUP
D_EXPERT = 512
LN_EPS = 1e-5
DEPTH = 1
DEEPNORM_ALPHA = (2 * DEPTH) ** 0.25

V7X_LANES = 128
V7X_SUBLANES = 8
V7X_VMEM_LIMIT_BYTES = 56 * 1024 * 1024

PROJ_ROWS = 512
ATTN_CHUNK = 256
ATTN_KEY_TILE = 128
DEN_ROWS = 16
ACC_ROWS = HEAD_DIM + DEN_ROWS
LOG2_E = 1.4426950408889634
MIX_ROWS = 512
EXPERT_ROWS = 256
ROW_GROUP = V7X_SUBLANES
SORTED_ROWS = -(-(2 * MIX_ROWS + N_EXPERTS * (ROW_GROUP - 1) + ROW_GROUP) // 256) * 256
MOVE_ROWS = 512
ISSUE_UNROLL = 8
HALF = D_MODEL // 2

T_Q0, T_QI0, T_V0, T_WI0 = 0, Q_WIDTH, Q_WIDTH + IDXQ_WIDTH, Q_WIDTH + IDXQ_WIDTH + KV_WIDTH
T_ROWS = T_WI0 + 16
KN_WIDTH = 256

INT_MIN = -2147483648
HALF_RANGE = 32768
ROUTER_LANES = 128
E0 = N_GROUPS


def _cparams(n_axes, vmem_bytes):
    return pltpu.CompilerParams(
        dimension_semantics=("arbitrary",) * n_axes,
        vmem_limit_bytes=int(min(vmem_bytes, V7X_VMEM_LIMIT_BYTES)),
    )


def _proj_kernel(x_ref, wt_ref, wn_ref, cost_ref, sint_ref, cosn_ref, sgnn_ref,
                 qt_ref, qit_ref, vt_ref, wit_ref, kn_ref):
    xb = x_ref[0].astype(jnp.bfloat16)
    pt = lax.dot_general(wt_ref[...], xb, (((1,), (1,)), ((), ())),
                         preferred_element_type=jnp.float32)
    cos = cost_ref[...]
    sin = sint_ref[...]

    def rope_t(dst_ref, base):
        for h in range(N_HEADS):
            r0 = base + h * HEAD_DIM
            x1 = pt[r0:r0 + ROT_HALF]
            x2 = pt[r0 + ROT_HALF:r0 + ROT_DIMS]
            head = jnp.concatenate([x1 * cos - x2 * sin, x2 * cos + x1 * sin,
                                    pt[r0 + ROT_DIMS:r0 + HEAD_DIM]], axis=0).astype(dst_ref.dtype)
            for jj, cols in enumerate(chunks):
                dst_ref[0, jj, h * HEAD_DIM:(h + 1) * HEAD_DIM, :] = head[:, cols]

    chunks = [slice(jj * ATTN_CHUNK, (jj + 1) * ATTN_CHUNK) for jj in range(PROJ_ROWS // ATTN_CHUNK)]
    rope_t(qt_ref, T_Q0)
    rope_t(qit_ref, T_QI0)
    for jj, cols in enumerate(chunks):
        vt_ref[0, jj] = pt[T_V0:T_V0 + KV_WIDTH, cols].astype(vt_ref.dtype)
        wit_ref[0, jj] = pt[T_WI0:T_WI0 + IDX_HEADS, cols]
    pn = jnp.dot(xb, wn_ref[...], preferred_element_type=jnp.float32)
    kn = pn[:, :KN_WIDTH] * cosn_ref[...] + pn[:, KN_WIDTH:] * sgnn_ref[...]
    kn_ref[0] = kn.astype(kn_ref.dtype)


def _proj_call(x, wt, wn, cos_t, sin_t, cos_n, sgn_n):
    B, S, D = x.shape
    R = PROJ_ROWS
    nt = S // R
    grid = (B, nt)
    out_shape = (
        jax.ShapeDtypeStruct((B, S // ATTN_CHUNK, Q_WIDTH, ATTN_CHUNK), jnp.bfloat16),
        jax.ShapeDtypeStruct((B, S // ATTN_CHUNK, IDXQ_WIDTH, ATTN_CHUNK), jnp.bfloat16),
        jax.ShapeDtypeStruct((B, S // ATTN_CHUNK, KV_WIDTH, ATTN_CHUNK), jnp.bfloat16),
        jax.ShapeDtypeStruct((B, S // ATTN_CHUNK, IDX_HEADS, ATTN_CHUNK), jnp.float32),
        jax.ShapeDtypeStruct((B, S, KN_WIDTH), jnp.bfloat16),
    )
    in_specs = [
        pl.BlockSpec((1, R, D), lambda b, j: (b, j, 0)),
        pl.BlockSpec((T_ROWS, D), lambda b, j: (0, 0)),
        pl.BlockSpec((D, 2 * KN_WIDTH), lambda b, j: (0, 0)),
        pl.BlockSpec((ROT_HALF, R), lambda b, j: (0, j)),
        pl.BlockSpec((ROT_HALF, R), lambda b, j: (0, j)),
        pl.BlockSpec((R, KN_WIDTH), lambda b, j: (j, 0)),
        pl.BlockSpec((R, KN_WIDTH), lambda b, j: (j, 0)),
    ]
    out_specs = (
        pl.BlockSpec((1, R // ATTN_CHUNK, Q_WIDTH, ATTN_CHUNK), lambda b, j: (b, j, 0, 0)),
        pl.BlockSpec((1, R // ATTN_CHUNK, IDXQ_WIDTH, ATTN_CHUNK), lambda b, j: (b, j, 0, 0)),
        pl.BlockSpec((1, R // ATTN_CHUNK, KV_WIDTH, ATTN_CHUNK), lambda b, j: (b, j, 0, 0)),
        pl.BlockSpec((1, R // ATTN_CHUNK, IDX_HEADS, ATTN_CHUNK), lambda b, j: (b, j, 0, 0)),
        pl.BlockSpec((1, R, KN_WIDTH), lambda b, j: (b, j, 0)),
    )
    vmem = 2 * (R * D * 4 + T_ROWS * D * 2 + D * 2 * KN_WIDTH * 2) + 6 * T_ROWS * R * 4 + (8 << 20)
    return pl.pallas_call(
        _proj_kernel, grid=grid, in_specs=in_specs, out_specs=out_specs, out_shape=out_shape,
        compiler_params=_cparams(2, vmem), name="dsa_proj",
    )(x, wt, wn, cos_t, sin_t, cos_n, sgn_n)


def _float_to_key(s):
    b = lax.bitcast_convert_type(s, jnp.int32)
    k = b ^ (lax.shift_right_arithmetic(b, 31) & jnp.int32(0x7FFFFFFF))
    return jnp.where(b == jnp.int32(INT_MIN), jnp.int32(0), k)


def _attn_kernel(qt_ref, qit_ref, wit_ref, kn_ref, vt_ref, o_ref,
                 key_ref, hi_ref, lo_ref, lga_ref, lgb_ref, acc_ref, m_ref, *, seq, top_k):
    C = ATTN_CHUNK
    c = pl.program_id(1)
    nkb = c + 1
    zeros_half = jnp.zeros((HEAD_DIM, C), jnp.bfloat16)
    row_i = lax.broadcasted_iota(jnp.int32, (C, C), 0)
    lane_i = lax.broadcasted_iota(jnp.int32, (C, C), 1)
    causal_in_block = row_i <= lane_i

    def score_body(kb, carry):
        r0 = pl.multiple_of(kb * C, C)
        kix = kn_ref[0, pl.ds(r0, C), KV_WIDTH:KN_WIDTH]
        acc = None
        for h in range(IDX_HEADS):
            rhs = jnp.concatenate([qit_ref[0, 0, h * IDX_DIM:(h + 1) * IDX_DIM, :], zeros_half], axis=0)
            s = jnp.dot(kix, rhs, preferred_element_type=jnp.float32)
            t = jnp.maximum(s, 0.0) * wit_ref[0, 0, h:h + 1, :]
            acc = t if acc is None else acc + t
        score = acc * (IDX_DIM ** -0.5 * IDX_HEADS ** -0.5)
        keys = jnp.where(causal_in_block | (kb != c), _float_to_key(score), jnp.int32(INT_MIN))
        key_ref[kb] = keys
        hi_ref[kb] = lax.shift_right_arithmetic(keys, 16).astype(jnp.int16)
        lo_ref[kb] = ((keys & jnp.int32(0xFFFF)) - jnp.int32(HALF_RANGE)).astype(jnp.int16)
        return carry

    lax.fori_loop(0, nkb, score_body, 0)

    def count(ref, pred):
        packed = ref.dtype == jnp.int16

        def body(kb, part):
            hit = pred(ref[kb], kb)
            if packed:
                words = pltpu.bitcast(jnp.where(hit, jnp.int16(1), jnp.int16(0)), jnp.int32)
            else:
                words = hit.astype(jnp.int32)
            return part + jnp.sum(words.reshape(-1, V7X_SUBLANES, C), axis=0)

        part = lax.fori_loop(0, nkb, body, jnp.zeros((V7X_SUBLANES, C), jnp.int32))
        if packed:
            part = (part & jnp.int32(0xFFFF)) + lax.shift_right_logical(part, 16)
        return jnp.sum(part, axis=0, keepdims=True)

    def kth_largest_16(ref):
        def body(i, ans_u):
            cand_u = ans_u | lax.shift_left(jnp.int32(1), jnp.int32(15) - i)
            cand = (cand_u - jnp.int32(HALF_RANGE)).astype(jnp.int16)
            return jnp.where(count(ref, lambda k, kb: k >= cand) >= top_k, cand_u, ans_u)

        return lax.fori_loop(0, 16, body, jnp.zeros((1, C), jnp.int32)) - jnp.int32(HALF_RANGE)

    t_hi = kth_largest_16(hi_ref)
    t_hi16 = t_hi.astype(jnp.int16)

    def refine_body(kb, carry):
        hi = hi_ref[kb]
        lo_ref[kb] = jnp.where(hi > t_hi16, jnp.int16(HALF_RANGE - 1),
                               jnp.where(hi == t_hi16, lo_ref[kb], jnp.int16(-HALF_RANGE)))
        return carry

    lax.fori_loop(0, nkb, refine_body, 0)
    t_lo = kth_largest_16(lo_ref)
    thr_raw = t_hi * jnp.int32(2 * HALF_RANGE) + (t_lo + jnp.int32(HALF_RANGE))
    thr = jnp.maximum(thr_raw, jnp.int32(INT_MIN + 1))

    t_lo16 = t_lo.astype(jnp.int16)
    n_ge = count(lo_ref, lambda k, kb: k >= t_lo16)
    has_ties = jnp.max(jnp.where(thr_raw != jnp.int32(INT_MIN), n_ge, 0)) > top_k

    @pl.when(has_ties)
    def _():
        need = (top_k - count(key_ref, lambda k, kb: k > thr)).astype(jnp.float32)
        earlier_rows = (row_i > lane_i).astype(jnp.bfloat16)

        def drop_body(kb, seen):
            kk = key_ref[kb]
            tied = kk == thr
            tied_b = jnp.where(tied, 1.0, 0.0).astype(jnp.bfloat16)
            rank = jnp.dot(earlier_rows, tied_b, preferred_element_type=jnp.float32) + seen
            key_ref[kb] = jnp.where(tied & (rank >= need), kk - 1, kk)
            return seen + jnp.sum(tied_b.astype(jnp.float32), axis=0, keepdims=True)

        lax.fori_loop(0, nkb, drop_body, jnp.zeros((1, C), jnp.float32))

    neg_inf = jnp.float32(-jnp.inf)
    m_ref[...] = jnp.full(m_ref.shape, neg_inf, jnp.float32)
    acc_ref[...] = jnp.zeros(acc_ref.shape, jnp.float32)
    T = ATTN_KEY_TILE
    ones_rows = jnp.ones((DEN_ROWS, C), jnp.bfloat16)

    kv_group = N_HEADS // N_KV_HEADS

    def bias_body(kb, carry):
        key_ref[kb] = lax.bitcast_convert_type(jnp.where(key_ref[kb] >= thr, 0.0, neg_inf), jnp.int32)
        return carry

    lax.fori_loop(0, nkb, bias_body, 0)
    n_blocks = seq // C
    key_ref[n_blocks] = lax.bitcast_convert_type(jnp.full((C, C), neg_inf, jnp.float32), jnp.int32)

    def store_logits(kb, dst_ref, h):
        kb_mem = jnp.minimum(kb, nkb - 1)
        kb_bias = jnp.where(kb < nkb, kb, n_blocks)
        r0 = pl.multiple_of(kb_mem * C, C)
        bias = lax.bitcast_convert_type(key_ref[kb_bias], jnp.float32)
        k2 = kn_ref[0, pl.ds(r0, C), 0:KV_WIDTH]
        qh = qt_ref[0, 0, h * HEAD_DIM:(h + 1) * HEAD_DIM, :]
        rhs = jnp.concatenate([qh, zeros_half] if h < kv_group else [zeros_half, qh], axis=0)
        dst_ref[h] = jnp.dot(k2, rhs, preferred_element_type=jnp.float32) + bias

    def softmax_block(kb, src_ref, kb_next, dst_ref):
        kb_mem = jnp.minimum(kb, nkb - 1)
        for h in range(N_HEADS):
            store_logits(kb_next, dst_ref, h)
            g = h // kv_group
            m_old = m_ref[h:h + 1, :]
            m_new = jnp.maximum(m_old, jnp.max(src_ref[h], axis=0, keepdims=True))
            m_safe = jnp.where(m_new == neg_inf, 0.0, m_new)
            p = jnp.concatenate(
                [jnp.exp2(src_ref[h, s * T:(s + 1) * T, :] - m_safe).astype(jnp.bfloat16) for s in range(C // T)],
                axis=0)
            alpha = jnp.exp2(m_old - m_safe)
            vt = vt_ref[0, kb_mem, g * HEAD_DIM:(g + 1) * HEAD_DIM, :]
            pv = jnp.dot(jnp.concatenate([vt, ones_rows], axis=0), p,
                         preferred_element_type=jnp.float32)
            hs = slice(h * ACC_ROWS, (h + 1) * ACC_ROWS)
            acc_ref[hs, :] = alpha * acc_ref[hs, :] + pv
            m_ref[h:h + 1, :] = m_new

    for h in range(N_HEADS):
        store_logits(0, lga_ref, h)

    def attn_body(pair, carry):
        kb = 2 * pair
        softmax_block(kb, lga_ref, kb + 1, lgb_ref)
        softmax_block(kb + 1, lgb_ref, kb + 2, lga_ref)
        return carry

    lax.fori_loop(0, (nkb + 1) // 2, attn_body, 0)
    outs = []
    for h in range(N_HEADS):
        num = acc_ref[h * ACC_ROWS:h * ACC_ROWS + HEAD_DIM, :]
        den = acc_ref[h * ACC_ROWS + HEAD_DIM:h * ACC_ROWS + HEAD_DIM + 1, :]
        outs.append(num / den)
    o_ref[0] = jnp.transpose(jnp.concatenate(outs, axis=0)).astype(o_ref.dtype)


def _attn_call(qt, qit, wit, kn, vt, top_k):
    B, S = kn.shape[:2]
    C = ATTN_CHUNK
    nc = S // C
    kern = functools.partial(_attn_kernel, seq=S, top_k=top_k)
    in_specs = [
        pl.BlockSpec((1, 1, Q_WIDTH, C), lambda b, c: (b, c, 0, 0)),
        pl.BlockSpec((1, 1, IDXQ_WIDTH, C), lambda b, c: (b, c, 0, 0)),
        pl.BlockSpec((1, 1, IDX_HEADS, C), lambda b, c: (b, c, 0, 0)),
        pl.BlockSpec((1, S, KN_WIDTH), lambda b, c: (b, 0, 0)),
        pl.BlockSpec((1, nc, KV_WIDTH, C), lambda b, c: (b, 0, 0, 0)),
    ]
    out_specs = pl.BlockSpec((1, C, Q_WIDTH), lambda b, c: (b, c, 0))
    scratch = [
        pltpu.VMEM((nc + 1, C, C), jnp.int32),
        pltpu.VMEM((nc, C, C), jnp.int16),
        pltpu.VMEM((nc, C, C), jnp.int16),
        pltpu.VMEM((N_HEADS, C, C), jnp.float32),
        pltpu.VMEM((N_HEADS, C, C), jnp.float32),
        pltpu.VMEM((N_HEADS * ACC_ROWS, C), jnp.float32),
        pltpu.VMEM((N_HEADS, C), jnp.float32),
    ]
    vmem = (2 * S * C * 4 + Q_WIDTH * C * 4 + 2 * 2 * (2 * Q_WIDTH * C + S * KN_WIDTH + S * KV_WIDTH + C * Q_WIDTH)
            + 24 * C * C * 4 + (8 << 20))
    return pl.pallas_call(
        kern, grid=(B, nc), in_specs=in_specs, out_specs=out_specs,
        out_shape=jax.ShapeDtypeStruct((B, S, Q_WIDTH), jnp.bfloat16),
        scratch_shapes=scratch, compiler_params=_cparams(2, vmem), name="dsa_attn",
    )(qt, qit, wit, kn, vt)


def _layer_norm(v, g, b):
    mu = jnp.mean(v, axis=-1, keepdims=True)
    d = v - mu
    var = jnp.mean(d * d, axis=-1, keepdims=True)
    return d * lax.rsqrt(var + LN_EPS) * g + b


def _pack_rows(h):
    hi = lax.bitcast_convert_type(h[:, :HALF].astype(jnp.bfloat16).astype(jnp.float32), jnp.int32)
    lo = lax.bitcast_convert_type(h[:, HALF:].astype(jnp.bfloat16).astype(jnp.float32), jnp.int32)
    return (hi & jnp.int32(-65536)) | lax.shift_right_logical(lo, 16)


def _unpack_rows(w):
    hi = lax.bitcast_convert_type(w & jnp.int32(-65536), jnp.float32)
    lo = lax.bitcast_convert_type(lax.shift_left(w, 16), jnp.float32)
    return jnp.concatenate([hi, lo], axis=1)


def _mix_kernel(x_ref, attn_ref, wc_ref, wg_ref, gb_ref, cw_ref, wau_ref, wcu_ref, wo_ref,
                g1_ref, b1_ref, wrh_ref, wrl_ref, rb_ref,
                h_ref, xs_ref, eid_ref, rank_ref, gate_ref, cnt_ref,
                ubuf_ref, *, steps_per_seq):
    R = MIX_ROWS
    i = pl.program_id(0)

    @pl.when(i % steps_per_seq == 0)
    def _():
        ubuf_ref[0:V7X_SUBLANES, :] = jnp.zeros((V7X_SUBLANES, CONV_DIM), jnp.float32)

    x = x_ref[...]
    xb = x.astype(jnp.bfloat16)
    cv = jnp.dot(xb, wc_ref[...], preferred_element_type=jnp.float32)
    u = cv[:, 2 * CONV_DIM:] * cv[:, :CONV_DIM]
    ubuf_ref[V7X_SUBLANES:, :] = u
    u1 = ubuf_ref[V7X_SUBLANES - 1:V7X_SUBLANES - 1 + R, :]
    u2 = ubuf_ref[V7X_SUBLANES - 2:V7X_SUBLANES - 2 + R, :]
    y = cw_ref[0:1, :] * u2 + cw_ref[1:2, :] * u1 + cw_ref[2:3, :] * u
    conv = (cv[:, CONV_DIM:2 * CONV_DIM] * y).astype(jnp.bfloat16)
    ubuf_ref[0:V7X_SUBLANES, :] = u[R - V7X_SUBLANES:, :]
    z = jnp.dot(xb, wg_ref[...], preferred_element_type=jnp.float32) + gb_ref[...]
    gates = 1.0 / (1.0 + jnp.exp(-z))
    au = jnp.dot(attn_ref[...], wau_ref[...], preferred_element_type=jnp.float32)
    cu = jnp.dot(conv, wcu_ref[...], preferred_element_type=jnp.float32)
    merged = gates[:, :D_MODEL] * au + gates[:, D_MODEL:] * cu
    mix = jnp.dot(merged.astype(jnp.bfloat16), wo_ref[...], preferred_element_type=jnp.float32)
    h = _layer_norm(DEEPNORM_ALPHA * x + mix, g1_ref[...], b1_ref[...])
    h_ref[...] = h

    h_hi = h.astype(jnp.bfloat16)
    h_lo = (h - h_hi.astype(jnp.float32)).astype(jnp.bfloat16)
    lg = (jnp.dot(h_hi, wrh_ref[...], preferred_element_type=jnp.float32)
          + jnp.dot(h_lo, wrh_ref[...], preferred_element_type=jnp.float32)
          + jnp.dot(h_hi, wrl_ref[...], preferred_element_type=jnp.float32)) + rb_ref[...]
    lane = lax.broadcasted_iota(jnp.int32, (R, ROUTER_LANES), 1).astype(jnp.float32)
    neg = jnp.float32(-jnp.inf)
    no_lane = jnp.float32(ROUTER_LANES)
    gmask = lane < N_GROUPS
    gl = jnp.where(gmask, lg, neg)
    gmax = jnp.max(gl, axis=1, keepdims=True)
    grp = jnp.min(jnp.where(gl == gmax, lane, no_lane), axis=1, keepdims=True)
    gsum = jnp.sum(jnp.where(gmask, jnp.exp(gl - gmax), 0.0), axis=1, keepdims=True)
    p_grp = 1.0 / gsum
    lo_lane = E0 + grp * EXPERTS_PER_GROUP
    emask = (lane >= lo_lane) & (lane < lo_lane + EXPERTS_PER_GROUP)
    el = jnp.where(emask, lg, neg)
    v1 = jnp.max(el, axis=1, keepdims=True)
    i1 = jnp.min(jnp.where(el == v1, lane, no_lane), axis=1, keepdims=True)
    el2 = jnp.where(lane == i1, neg, el)
    v2 = jnp.max(el2, axis=1, keepdims=True)
    i2 = jnp.min(jnp.where(el2 == v2, lane, no_lane), axis=1, keepdims=True)
    a = jnp.exp(v2 - v1)
    inv = 1.0 / (1.0 + a)
    gate_ref[:, 0:1] = p_grp * inv
    gate_ref[:, 1:2] = p_grp * (a * inv)
    eid_ref[:, 0:1] = (i1 - E0).astype(jnp.int32)
    eid_ref[:, 1:2] = (i2 - E0).astype(jnp.int32)
    oh1 = lane == i1
    oh2 = lane == i2
    oh = (oh1 | oh2).astype(jnp.bfloat16)
    r_i = lax.broadcasted_iota(jnp.int32, (R, R), 0)
    c_i = lax.broadcasted_iota(jnp.int32, (R, R), 1)
    tri = (r_i > c_i).astype(jnp.bfloat16)
    before = jnp.dot(tri, oh, preferred_element_type=jnp.float32)
    rank_ref[:, 0:1] = jnp.sum(jnp.where(oh1, before, 0.0), axis=1, keepdims=True).astype(jnp.int32)
    rank_ref[:, 1:2] = jnp.sum(jnp.where(oh2, before, 0.0), axis=1, keepdims=True).astype(jnp.int32)
    cnt = jnp.sum(oh.astype(jnp.float32), axis=0, keepdims=True)
    cnt_ref[0] = cnt

    seg = jnp.floor((cnt + (ROW_GROUP - 1)) * (1.0 / ROW_GROUP)) * ROW_GROUP
    l_r = lax.broadcasted_iota(jnp.int32, (ROUTER_LANES, ROUTER_LANES), 0)
    l_c = lax.broadcasted_iota(jnp.int32, (ROUTER_LANES, ROUTER_LANES), 1)
    lanes_before = (l_r < l_c).astype(jnp.bfloat16)
    seg8 = jnp.broadcast_to(seg, (V7X_SUBLANES, ROUTER_LANES)).astype(jnp.bfloat16)
    seg_off = jnp.dot(seg8, lanes_before, preferred_element_type=jnp.float32)[0:1, :]
    where_to = before + seg_off
    pos1 = jnp.sum(jnp.where(oh1, where_to, 0.0), axis=1, keepdims=True)
    pos2 = jnp.sum(jnp.where(oh2, where_to, 0.0), axis=1, keepdims=True)
    diag = r_i == c_i
    pos1_row = jnp.sum(jnp.where(diag, pos1, 0.0), axis=0, keepdims=True)
    pos2_row = jnp.sum(jnp.where(diag, pos2, 0.0), axis=0, keepdims=True)
    sorted_row = lax.broadcasted_iota(jnp.int32, (SORTED_ROWS, R), 0).astype(jnp.float32)
    place = ((sorted_row == pos1_row) | (sorted_row == pos2_row)).astype(jnp.bfloat16)
    xs = jnp.dot(place, h_hi, preferred_element_type=jnp.float32)
    xs_ref[0] = _pack_rows(xs)


def _mix_call(x2, attn2, wc, wg, gb, cw, wau, wcu, wo, g1, b1, wrh, wrl, rb, seq):
    N, D = x2.shape
    R = MIX_ROWS
    nt = N // R
    kern = functools.partial(_mix_kernel, steps_per_seq=seq // R)

    def full(a):
        return pl.BlockSpec(a.shape, lambda i: (0,) * a.ndim)

    in_specs = [
        pl.BlockSpec((R, D), lambda i: (i, 0)),
        pl.BlockSpec((R, Q_WIDTH), lambda i: (i, 0)),
        full(wc), full(wg), full(gb), full(cw), full(wau), full(wcu), full(wo),
        full(g1), full(b1), full(wrh), full(wrl), full(rb),
    ]
    out_shape = (
        jax.ShapeDtypeStruct((N, D), jnp.float32),
        jax.ShapeDtypeStruct((nt, SORTED_ROWS, HALF), jnp.int32),
        jax.ShapeDtypeStruct((N, 2), jnp.int32),
        jax.ShapeDtypeStruct((N, 2), jnp.int32),
        jax.ShapeDtypeStruct((N, 2), jnp.float32),
        jax.ShapeDtypeStruct((nt, 1, ROUTER_LANES), jnp.float32),
    )
    out_specs = (
        pl.BlockSpec((R, D), lambda i: (i, 0)),
        pl.BlockSpec((1, SORTED_ROWS, HALF), lambda i: (i, 0, 0)),
        pl.BlockSpec((R, 2), lambda i: (i, 0)),
        pl.BlockSpec((R, 2), lambda i: (i, 0)),
        pl.BlockSpec((R, 2), lambda i: (i, 0)),
        pl.BlockSpec((1, 1, ROUTER_LANES), lambda i: (i, 0, 0)),
    )
    scratch = [pltpu.VMEM((R + V7X_SUBLANES, CONV_DIM), jnp.float32)]
    w_bytes = 2 * (wc.size + wg.size + wau.size + wcu.size + wo.size + wrh.size + wrl.size)
    vmem = (2 * w_bytes + 2 * (R * D * 4 * 2 + R * Q_WIDTH * 2 + SORTED_ROWS * HALF * 4) + 10 * R * 2048 * 4
            + SORTED_ROWS * D * 6 + (6 << 20))
    return pl.pallas_call(
        kern, grid=(nt,), in_specs=in_specs, out_specs=out_specs, out_shape=out_shape,
        scratch_shapes=scratch, compiler_params=_cparams(1, vmem), name="mix_ln_router",
    )(x2, attn2, wc, wg, gb, cw, wau, wcu, wo, g1, b1, wrh, wrl, rb)


def _row_copy(src_ref, s, dst_ref, d, sem):
    return pltpu.make_async_copy(src_ref.at[pl.ds(s, 1), :], dst_ref.at[pl.ds(d, 1), :], sem)


GROUPS_PER_BLOCK = EXPERT_ROWS // ROW_GROUP


def _expert_kernel(be_ref, nb_ref, gsrc_ref, xs_ref, wg_ref, wu_ref, wd_ref, ys_ref,
                   wgb_ref, wub_ref, wdb_ref, xbuf_ref, sems):
    j = pl.program_id(0)
    n_used = nb_ref[0]

    def gather(block, slot):
        for g in range(GROUPS_PER_BLOCK):
            row = pl.multiple_of(gsrc_ref[block * GROUPS_PER_BLOCK + g], ROW_GROUP)
            pltpu.make_async_copy(xs_ref.at[pl.ds(row, ROW_GROUP), :],
                                  xbuf_ref.at[slot, pl.ds(g * ROW_GROUP, ROW_GROUP), :],
                                  sems.at[slot]).start(priority=g % 2)

    def wait_gather(slot):
        pltpu.make_async_copy(xs_ref.at[pl.ds(0, EXPERT_ROWS), :], xbuf_ref.at[slot], sems.at[slot]).wait()

    @pl.when((j == 0) & (n_used > 0))
    def _():
        gather(0, 0)

    @pl.when((j == 0) | (be_ref[j] != be_ref[jnp.maximum(j - 1, 0)]))
    def _():
        wgb_ref[...] = wg_ref[0].astype(jnp.bfloat16)
        wub_ref[...] = wu_ref[0].astype(jnp.bfloat16)
        wdb_ref[...] = wd_ref[0].astype(jnp.bfloat16)

    slot = j % 2

    @pl.when(j + 1 < n_used)
    def _():
        gather(j + 1, 1 - slot)

    @pl.when(j < n_used)
    def _():
        wait_gather(slot)
        xb = _unpack_rows(xbuf_ref[slot]).astype(jnp.bfloat16)
        hg = jnp.dot(xb, wgb_ref[...], preferred_element_type=jnp.float32)
        hu = jnp.dot(xb, wub_ref[...], preferred_element_type=jnp.float32)
        hid = (hg / (1.0 + jnp.exp(-hg))) * hu
        y = jnp.dot(hid.astype(jnp.bfloat16), wdb_ref[...], preferred_element_type=jnp.float32)
        ys_ref[...] = _pack_rows(y)

    @pl.when(j >= nb_ref[0])
    def _():
        ys_ref[...] = jnp.zeros_like(ys_ref)


def _expert_call(block_expert, n_used, group_src, xs_sorted, wg, wu, wd, cap):
    W = xs_sorted.shape[-1]
    RB = EXPERT_ROWS
    nb = cap // RB
    grid_spec = pltpu.PrefetchScalarGridSpec(
        num_scalar_prefetch=3, grid=(nb,),
        in_specs=[
            pl.BlockSpec(memory_space=pl.ANY),
            pl.BlockSpec((1, D_MODEL, D_EXPERT), lambda j, be, nu, gs: (be[j], 0, 0)),
            pl.BlockSpec((1, D_MODEL, D_EXPERT), lambda j, be, nu, gs: (be[j], 0, 0)),
            pl.BlockSpec((1, D_EXPERT, D_MODEL), lambda j, be, nu, gs: (be[j], 0, 0)),
        ],
        out_specs=pl.BlockSpec((RB, W), lambda j, be, nu, gs: (j, 0)),
        scratch_shapes=[pltpu.VMEM((D_MODEL, D_EXPERT), jnp.bfloat16),
                        pltpu.VMEM((D_MODEL, D_EXPERT), jnp.bfloat16),
                        pltpu.VMEM((D_EXPERT, D_MODEL), jnp.bfloat16),
                        pltpu.VMEM((2, RB, W), jnp.int32),
                        pltpu.SemaphoreType.DMA((2,))],
    )
    vmem = (2 * 4 + 2) * 3 * D_MODEL * D_EXPERT + 4 * RB * W * 4 + 8 * RB * D_MODEL * 4 + (6 << 20)
    return pl.pallas_call(
        _expert_kernel, grid_spec=grid_spec,
        out_shape=jax.ShapeDtypeStruct((cap, W), jnp.int32),
        compiler_params=_cparams(1, vmem), name="moe_experts",
    )(block_expert, n_used, group_src, xs_sorted, wg, wu, wd)


def _combine_kernel(dest_ref, h_ref, gate_ref, g2_ref, b2_ref, ys_ref, o_ref, buf_ref, sem):
    R = MOVE_ROWS

    def issue(r, carry):
        _row_copy(ys_ref, dest_ref[0, 0, 2 * r], buf_ref.at[0], r, sem).start(priority=0)
        _row_copy(ys_ref, dest_ref[0, 0, 2 * r + 1], buf_ref.at[1], r, sem).start(priority=1)
        return carry

    lax.fori_loop(0, R, issue, 0, unroll=ISSUE_UNROLL)
    for slot in range(2):
        pltpu.make_async_copy(ys_ref.at[pl.ds(0, R), :], buf_ref.at[slot], sem).wait()
    y0 = _unpack_rows(buf_ref[0])
    y1 = _unpack_rows(buf_ref[1])
    ffn = gate_ref[:, 0:1] * y0 + gate_ref[:, 1:2] * y1
    o_ref[...] = _layer_norm(DEEPNORM_ALPHA * h_ref[...] + ffn, g2_ref[...], b2_ref[...])


def _combine_call(dest3, h, gate, g2, b2, ys):
    N, D = h.shape
    W = ys.shape[1]
    R = MOVE_ROWS
    nt = N // R
    return pl.pallas_call(
        _combine_kernel, grid=(nt,),
        in_specs=[
            pl.BlockSpec((1, 1, 2 * R), lambda i: (i, 0, 0), memory_space=pltpu.SMEM),
            pl.BlockSpec((R, D), lambda i: (i, 0)),
            pl.BlockSpec((R, 2), lambda i: (i, 0)),
            pl.BlockSpec((1, D), lambda i: (0, 0)),
            pl.BlockSpec((1, D), lambda i: (0, 0)),
            pl.BlockSpec(memory_space=pl.ANY),
        ],
        out_specs=pl.BlockSpec((R, D), lambda i: (i, 0)),
        out_shape=jax.ShapeDtypeStruct((N, D), jnp.float32),
        scratch_shapes=[pltpu.VMEM((2, R, W), jnp.int32), pltpu.SemaphoreType.DMA],
        compiler_params=_cparams(1, 4 * R * D * 4 + 2 * R * W * 4 + 8 * R * D * 4 + (4 << 20)),
        name="moe_combine",
    )(dest3, h, gate, g2, b2, ys)


def _rope_tables(seq):
    inv_freq = ROPE_THETA ** (-jnp.arange(ROT_HALF, dtype=jnp.float32) / ROT_HALF)
    ang = jnp.arange(seq, dtype=jnp.int32).astype(jnp.float32)[:, None] * inv_freq[None, :]
    cos, sin = jnp.cos(ang), jnp.sin(ang)
    ones = jnp.ones((seq, HEAD_DIM - ROT_DIMS), jnp.float32)
    zeros = jnp.zeros((seq, HEAD_DIM - ROT_DIMS), jnp.float32)
    c_head = jnp.concatenate([cos, cos, ones], axis=1)
    s_head = jnp.concatenate([-sin, sin, zeros], axis=1)
    reps = KN_WIDTH // HEAD_DIM
    return cos.T, sin.T, jnp.tile(c_head, (1, reps)), jnp.tile(s_head, (1, reps))


def _swap_rot_cols(w):
    d, n = w.shape
    wh = w.reshape(d, n // HEAD_DIM, HEAD_DIM)
    sw = jnp.concatenate([wh[:, :, ROT_HALF:ROT_DIMS], wh[:, :, :ROT_HALF],
                          jnp.zeros((d, n // HEAD_DIM, HEAD_DIM - ROT_DIMS), w.dtype)], axis=2)
    return sw.reshape(d, n)


def _block(x, w_in, gate_bias, w_attn_up, w_conv_up, conv_w, w_out, ln_g, ln_b,
           rg_w, rg_b, re_w, re_b, w_gate_e, w_up_e, w_down_e, ln2_g, ln2_b):
    B, S, D = x.shape
    N = B * S
    top_k = min(TOPK_MAX, S // 4)
    bf = jnp.bfloat16
    o = np.cumsum([0, Q_WIDTH, KV_WIDTH, KV_WIDTH, IDXQ_WIDTH, IDX_DIM, IDX_HEADS,
                   CONV_DIM, CONV_DIM, CONV_DIM, N_BRANCHES * D_MODEL])
    w_q, w_k, w_v, w_qi, w_ki, w_wi = (w_in[:, o[i]:o[i + 1]] for i in range(6))
    w_conv = w_in[:, o[6]:o[9]]
    w_gates = w_in[:, o[9]:o[10]]

    wt = jnp.concatenate([w_q.T * (HEAD_DIM ** -0.5 * LOG2_E), w_qi.T, w_v.T, w_wi.T,
                          jnp.zeros((T_ROWS - T_WI0 - IDX_HEADS, D), w_in.dtype)], axis=0).astype(bf)
    pad = jnp.zeros((D, KN_WIDTH - KV_WIDTH - IDX_DIM), w_in.dtype)
    wn = jnp.concatenate([w_k, w_ki, pad, _swap_rot_cols(w_k), _swap_rot_cols(w_ki), pad], axis=1).astype(bf)
    cos_t, sin_t, cos_n, sgn_n = _rope_tables(S)

    qt, qit, vt, wit, kn = _proj_call(x, wt, wn, cos_t, sin_t, cos_n, sgn_n)
    attn = _attn_call(qt, qit, wit, kn, vt, top_k)

    w_r = jnp.concatenate([rg_w, jnp.transpose(re_w, (1, 0, 2)).reshape(D, N_EXPERTS),
                           jnp.zeros((D, ROUTER_LANES - E0 - N_EXPERTS), rg_w.dtype)], axis=1)
    b_r = jnp.concatenate([rg_b, re_b.reshape(-1),
                           jnp.zeros((ROUTER_LANES - E0 - N_EXPERTS,), rg_b.dtype)])[None, :]
    w_rh = w_r.astype(bf)
    w_rl = (w_r - w_rh.astype(jnp.float32)).astype(bf)

    h, xs_sorted, eid, rank, gate, cnt = _mix_call(
        x.reshape(N, D), attn.reshape(N, Q_WIDTH), w_conv.astype(bf), w_gates.astype(bf),
        gate_bias[None, :], conv_w, w_attn_up.astype(bf), w_conv_up.astype(bf), w_out.astype(bf),
        ln_g[None, :], ln_b[None, :], w_rh, w_rl, b_r, S)

    RB, RG = EXPERT_ROWS, ROW_GROUP
    nt = N // MIX_ROWS
    counts = cnt[:, 0, E0:E0 + N_EXPERTS].astype(jnp.int32)
    seg_len = ((counts + RG - 1) // RG) * RG
    seg_local = jnp.cumsum(seg_len, axis=1) - seg_len
    region = jnp.sum(seg_len, axis=0)
    padded = ((region + RB - 1) // RB) * RB
    pad_end = jnp.cumsum(padded)
    pad_start = pad_end - padded
    seg_start = pad_start[None, :] + jnp.cumsum(seg_len, axis=0) - seg_len
    experts = jnp.arange(N_EXPERTS, dtype=jnp.int32)
    start_tok = jnp.repeat(seg_start, MIX_ROWS, axis=0)
    dest = jnp.sum(jnp.where(eid[:, :, None] == experts[None, None, :], start_tok[:, None, :], 0), axis=-1) + rank
    cap = -(-(N * 2 + nt * N_EXPERTS * (RG - 1) + N_EXPERTS * (RB - 1)) // RB) * RB
    nb = cap // RB
    block_start = jnp.arange(nb, dtype=jnp.int32) * RB
    block_expert = jnp.minimum(jnp.sum((block_start[:, None] >= pad_end[None, :]).astype(jnp.int32), axis=1),
                               N_EXPERTS - 1)
    n_used = (pad_end[-1:] // RB).astype(jnp.int32)
    flat_start = seg_start.T.reshape(-1)
    flat_len = seg_len.T.reshape(-1)
    flat_local = seg_local.T.reshape(-1)
    g_row = jnp.arange(cap // RG, dtype=jnp.int32) * RG
    seg_of = jnp.minimum(jnp.sum(((flat_start + flat_len)[None, :] <= g_row[:, None]).astype(jnp.int32), axis=1),
                         flat_start.shape[0] - 1)
    within = g_row - flat_start[seg_of]
    live = (within >= 0) & (within < flat_len[seg_of])
    group_src = jnp.where(live, (seg_of % nt) * SORTED_ROWS + flat_local[seg_of] + within,
                          SORTED_ROWS - RG).astype(jnp.int32)
    dest3 = dest.reshape(N // MOVE_ROWS, 1, 2 * MOVE_ROWS)

    ys = _expert_call(block_expert, n_used, group_src, xs_sorted.reshape(nt * SORTED_ROWS, HALF),
                      w_gate_e, w_up_e, w_down_e, cap)
    out = _combine_call(dest3, h, gate, ln2_g[None, :], ln2_b[None, :], ys)
    return out.reshape(B, S, D)


def kernel(x, w_in, gate_bias, w_attn_up, w_conv_up, conv_w, w_out, ln1_g, ln1_b, router_group_w,
           router_group_b, router_expert_w, router_expert_b, w_gate_e, w_up_e, w_down_e, ln2_g, ln2_b):
    h = x
    for l in range(DEPTH):
        h = _block(h, w_in[l], gate_bias[l], w_attn_up[l], w_conv_up[l], conv_w[l], w_out[l],
                   ln1_g[l], ln1_b[l], router_group_w[l], router_group_b[l], router_expert_w[l],
                   router_expert_b[l], w_gate_e[l], w_up_e[l], w_down_e[l], ln2_g[l], ln2_b[l])
    return h
```

```python
import functools

import jax
import jax.numpy as jnp
import numpy as np
from jax import lax
from jax.experimental import pallas as pl
from jax.experimental.pallas import tpu as pltpu

D_MODEL = 1024
N_HEADS = 8
N_KV_HEADS = 2
HEAD_DIM = 64
Q_WIDTH = N_HEADS * HEAD_DIM
KV_WIDTH = N_KV_HEADS * HEAD_DIM
ROPE_THETA = 500000.0
ROT_DIMS = HEAD_DIM // 4
ROT_HALF = ROT_DIMS // 2
IDX_HEADS = 8
IDX_DIM = 64
IDXQ_WIDTH = IDX_HEADS * IDX_DIM
TOPK_MAX = 256
CONV_DIM = 512
CONV_WIDTH = 3
N_BRANCHES = 2
N_GROUPS = 4
EXPERTS_PER_GROUP = 8
N_EXPERTS = N_GROUPS * EXPERTS_PER_GROUP
D_EXPERT = 512
LN_EPS = 1e-5
DEPTH = 1
DEEPNORM_ALPHA = (2 * DEPTH) ** 0.25

V7X_LANES = 128
V7X_SUBLANES = 8
V7X_VMEM_LIMIT_BYTES = 56 * 1024 * 1024

PROJ_ROWS = 512
ATTN_CHUNK = 256
ATTN_KEY_TILE = 128
DEN_ROWS = 16
ACC_ROWS = HEAD_DIM + DEN_ROWS
LOG2_E = 1.4426950408889634
MIX_ROWS = 512
EXPERT_ROWS = 256
ROW_GROUP = V7X_SUBLANES
SORTED_ROWS = -(-(2 * MIX_ROWS + N_EXPERTS * (ROW_GROUP - 1) + ROW_GROUP) // 256) * 256
MOVE_ROWS = 512
ISSUE_UNROLL = 8
HALF = D_MODEL // 2

T_Q0, T_QI0, T_V0, T_WI0 = 0, Q_WIDTH, Q_WIDTH + IDXQ_WIDTH, Q_WIDTH + IDXQ_WIDTH + KV_WIDTH
T_ROWS = T_WI0 + 16
KN_WIDTH = 256

INT_MIN = -2147483648
HALF_RANGE = 32768
ROUTER_LANES = 128
E0 = N_GROUPS


def _cparams(n_axes, vmem_bytes):
    return pltpu.CompilerParams(
        dimension_semantics=("arbitrary",) * n_axes,
        vmem_limit_bytes=int(min(vmem_bytes, V7X_VMEM_LIMIT_BYTES)),
    )


def _proj_kernel(x_ref, wt_ref, wn_ref, cost_ref, sint_ref, cosn_ref, sgnn_ref,
                 qt_ref, qit_ref, vt_ref, wit_ref, kn_ref):
    xb = x_ref[0].astype(jnp.bfloat16)
    pt = lax.dot_general(wt_ref[...], xb, (((1,), (1,)), ((), ())),
                         preferred_element_type=jnp.float32)
    cos = cost_ref[...]
    sin = sint_ref[...]

    def rope_t(dst_ref, base):
        for h in range(N_HEADS):
            r0 = base + h * HEAD_DIM
            x1 = pt[r0:r0 + ROT_HALF]
            x2 = pt[r0 + ROT_HALF:r0 + ROT_DIMS]
            head = jnp.concatenate([x1 * cos - x2 * sin, x2 * cos + x1 * sin,
                                    pt[r0 + ROT_DIMS:r0 + HEAD_DIM]], axis=0).astype(dst_ref.dtype)
            for jj, cols in enumerate(chunks):
                dst_ref[0, jj, h * HEAD_DIM:(h + 1) * HEAD_DIM, :] = head[:, cols]

    chunks = [slice(jj * ATTN_CHUNK, (jj + 1) * ATTN_CHUNK) for jj in range(PROJ_ROWS // ATTN_CHUNK)]
    rope_t(qt_ref, T_Q0)
    rope_t(qit_ref, T_QI0)
    for jj, cols in enumerate(chunks):
        vt_ref[0, jj] = pt[T_V0:T_V0 + KV_WIDTH, cols].astype(vt_ref.dtype)
        wit_ref[0, jj] = pt[T_WI0:T_WI0 + IDX_HEADS, cols]
    pn = jnp.dot(xb, wn_ref[...], preferred_element_type=jnp.float32)
    kn = pn[:, :KN_WIDTH] * cosn_ref[...] + pn[:, KN_WIDTH:] * sgnn_ref[...]
    kn_ref[0] = kn.astype(kn_ref.dtype)


def _proj_call(x, wt, wn, cos_t, sin_t, cos_n, sgn_n):
    B, S, D = x.shape
    R = PROJ_ROWS
    nt = S // R
    grid = (B, nt)
    out_shape = (
        jax.ShapeDtypeStruct((B, S // ATTN_CHUNK, Q_WIDTH, ATTN_CHUNK), jnp.bfloat16),
        jax.ShapeDtypeStruct((B, S // ATTN_CHUNK, IDXQ_WIDTH, ATTN_CHUNK), jnp.bfloat16),
        jax.ShapeDtypeStruct((B, S // ATTN_CHUNK, KV_WIDTH, ATTN_CHUNK), jnp.bfloat16),
        jax.ShapeDtypeStruct((B, S // ATTN_CHUNK, IDX_HEADS, ATTN_CHUNK), jnp.float32),
        jax.ShapeDtypeStruct((B, S, KN_WIDTH), jnp.bfloat16),
    )
    in_specs = [
        pl.BlockSpec((1, R, D), lambda b, j: (b, j, 0)),
        pl.BlockSpec((T_ROWS, D), lambda b, j: (0, 0)),
        pl.BlockSpec((D, 2 * KN_WIDTH), lambda b, j: (0, 0)),
        pl.BlockSpec((ROT_HALF, R), lambda b, j: (0, j)),
        pl.BlockSpec((ROT_HALF, R), lambda b, j: (0, j)),
        pl.BlockSpec((R, KN_WIDTH), lambda b, j: (j, 0)),
        pl.BlockSpec((R, KN_WIDTH), lambda b, j: (j, 0)),
    ]
    out_specs = (
        pl.BlockSpec((1, R // ATTN_CHUNK, Q_WIDTH, ATTN_CHUNK), lambda b, j: (b, j, 0, 0)),
        pl.BlockSpec((1, R // ATTN_CHUNK, IDXQ_WIDTH, ATTN_CHUNK), lambda b, j: (b, j, 0, 0)),
        pl.BlockSpec((1, R // ATTN_CHUNK, KV_WIDTH, ATTN_CHUNK), lambda b, j: (b, j, 0, 0)),
        pl.BlockSpec((1, R // ATTN_CHUNK, IDX_HEADS, ATTN_CHUNK), lambda b, j: (b, j, 0, 0)),
        pl.BlockSpec((1, R, KN_WIDTH), lambda b, j: (b, j, 0)),
    )
    vmem = 2 * (R * D * 4 + T_ROWS * D * 2 + D * 2 * KN_WIDTH * 2) + 6 * T_ROWS * R * 4 + (8 << 20)
    return pl.pallas_call(
        _proj_kernel, grid=grid, in_specs=in_specs, out_specs=out_specs, out_shape=out_shape,
        compiler_params=_cparams(2, vmem), name="dsa_proj",
    )(x, wt, wn, cos_t, sin_t, cos_n, sgn_n)


def _float_to_key(s):
    b = lax.bitcast_convert_type(s, jnp.int32)
    k = b ^ (lax.shift_right_arithmetic(b, 31) & jnp.int32(0x7FFFFFFF))
    return jnp.where(b == jnp.int32(INT_MIN), jnp.int32(0), k)


def _attn_kernel(qt_ref, qit_ref, wit_ref, kn_ref, vt_ref, o_ref,
                 key_ref, hi_ref, lo_ref, lga_ref, lgb_ref, acc_ref, m_ref, *, seq, top_k):
    C = ATTN_CHUNK
    c = pl.program_id(1)
    nkb = c + 1
    zeros_half = jnp.zeros((HEAD_DIM, C), jnp.bfloat16)
    row_i = lax.broadcasted_iota(jnp.int32, (C, C), 0)
    lane_i = lax.broadcasted_iota(jnp.int32, (C, C), 1)
    causal_in_block = row_i <= lane_i

    def score_body(kb, carry):
        r0 = pl.multiple_of(kb * C, C)
        kix = kn_ref[0, pl.ds(r0, C), KV_WIDTH:KN_WIDTH]
        acc = None
        for h in range(IDX_HEADS):
            rhs = jnp.concatenate([qit_ref[0, 0, h * IDX_DIM:(h + 1) * IDX_DIM, :], zeros_half], axis=0)
            s = jnp.dot(kix, rhs, preferred_element_type=jnp.float32)
            t = jnp.maximum(s, 0.0) * wit_ref[0, 0, h:h + 1, :]
            acc = t if acc is None else acc + t
        score = acc * (IDX_DIM ** -0.5 * IDX_HEADS ** -0.5)
        keys = jnp.where(causal_in_block | (kb != c), _float_to_key(score), jnp.int32(INT_MIN))
        key_ref[kb] = keys
        hi_ref[kb] = lax.shift_right_arithmetic(keys, 16).astype(jnp.int16)
        lo_ref[kb] = ((keys & jnp.int32(0xFFFF)) - jnp.int32(HALF_RANGE)).astype(jnp.int16)
        return carry

    lax.fori_loop(0, nkb, score_body, 0)

    def count(ref, pred):
        packed = ref.dtype == jnp.int16

        def body(kb, part):
            hit = pred(ref[kb], kb)
            if packed:
                words = pltpu.bitcast(jnp.where(hit, jnp.int16(1), jnp.int16(0)), jnp.int32)
            else:
                words = hit.astype(jnp.int32)
            return part + jnp.sum(words.reshape(-1, V7X_SUBLANES, C), axis=0)

        part = lax.fori_loop(0, nkb, body, jnp.zeros((V7X_SUBLANES, C), jnp.int32))
        if packed:
            part = (part & jnp.int32(0xFFFF)) + lax.shift_right_logical(part, 16)
        return jnp.sum(part, axis=0, keepdims=True)

    def kth_largest_16(ref):
        def body(i, ans_u):
            cand_u = ans_u | lax.shift_left(jnp.int32(1), jnp.int32(15) - i)
            cand = (cand_u - jnp.int32(HALF_RANGE)).astype(jnp.int16)
            return jnp.where(count(ref, lambda k, kb: k >= cand) >= top_k, cand_u, ans_u)

        return lax.fori_loop(0, 16, body, jnp.zeros((1, C), jnp.int32)) - jnp.int32(HALF_RANGE)

    t_hi = kth_largest_16(hi_ref)
    t_hi16 = t_hi.astype(jnp.int16)

    def refine_body(kb, carry):
        hi = hi_ref[kb]
        lo_ref[kb] = jnp.where(hi > t_hi16, jnp.int16(HALF_RANGE - 1),
                               jnp.where(hi == t_hi16, lo_ref[kb], jnp.int16(-HALF_RANGE)))
        return carry

    lax.fori_loop(0, nkb, refine_body, 0)
    t_lo = kth_largest_16(lo_ref)
    thr_raw = t_hi * jnp.int32(2 * HALF_RANGE) + (t_lo + jnp.int32(HALF_RANGE))
    thr = jnp.maximum(thr_raw, jnp.int32(INT_MIN + 1))

    t_lo16 = t_lo.astype(jnp.int16)
    n_ge = count(lo_ref, lambda k, kb: k >= t_lo16)
    has_ties = jnp.max(jnp.where(thr_raw != jnp.int32(INT_MIN), n_ge, 0)) > top_k

    @pl.when(has_ties)
    def _():
        need = (top_k - count(key_ref, lambda k, kb: k > thr)).astype(jnp.float32)
        earlier_rows = (row_i > lane_i).astype(jnp.bfloat16)

        def drop_body(kb, seen):
            kk = key_ref[kb]
            tied = kk == thr
            tied_b = jnp.where(tied, 1.0, 0.0).astype(jnp.bfloat16)
            rank = jnp.dot(earlier_rows, tied_b, preferred_element_type=jnp.float32) + seen
            key_ref[kb] = jnp.where(tied & (rank >= need), kk - 1, kk)
            return seen + jnp.sum(tied_b.astype(jnp.float32), axis=0, keepdims=True)

        lax.fori_loop(0, nkb, drop_body, jnp.zeros((1, C), jnp.float32))

    neg_inf = jnp.float32(-jnp.inf)
    m_ref[...] = jnp.full(m_ref.shape, neg_inf, jnp.float32)
    acc_ref[...] = jnp.zeros(acc_ref.shape, jnp.float32)
    T = ATTN_KEY_TILE
    ones_rows = jnp.ones((DEN_ROWS, C), jnp.bfloat16)

    kv_group = N_HEADS // N_KV_HEADS

    def bias_body(kb, carry):
        key_ref[kb] = lax.bitcast_convert_type(jnp.where(key_ref[kb] >= thr, 0.0, neg_inf), jnp.int32)
        return carry

    lax.fori_loop(0, nkb, bias_body, 0)
    n_blocks = seq // C
    key_ref[n_blocks] = lax.bitcast_convert_type(jnp.full((C, C), neg_inf, jnp.float32), jnp.int32)

    def store_logits(kb, dst_ref, h):
        kb_mem = jnp.minimum(kb, nkb - 1)
        kb_bias = jnp.where(kb < nkb, kb, n_blocks)
        r0 = pl.multiple_of(kb_mem * C, C)
        bias = lax.bitcast_convert_type(key_ref[kb_bias], jnp.float32)
        k2 = kn_ref[0, pl.ds(r0, C), 0:KV_WIDTH]
        qh = qt_ref[0, 0, h * HEAD_DIM:(h + 1) * HEAD_DIM, :]
        rhs = jnp.concatenate([qh, zeros_half] if h < kv_group else [zeros_half, qh], axis=0)
        dst_ref[h] = jnp.dot(k2, rhs, preferred_element_type=jnp.float32) + bias

    def softmax_block(kb, src_ref, kb_next, dst_ref):
        kb_mem = jnp.minimum(kb, nkb - 1)
        for h in range(N_HEADS):
            store_logits(kb_next, dst_ref, h)
            g = h // kv_group
            m_old = m_ref[h:h + 1, :]
            m_new = jnp.maximum(m_old, jnp.max(src_ref[h], axis=0, keepdims=True))
            m_safe = jnp.where(m_new == neg_inf, 0.0, m_new)
            p = jnp.concatenate(
                [jnp.exp2(src_ref[h, s * T:(s + 1) * T, :] - m_safe).astype(jnp.bfloat16) for s in range(C // T)],
                axis=0)
            alpha = jnp.exp2(m_old - m_safe)
            vt = vt_ref[0, kb_mem, g * HEAD_DIM:(g + 1) * HEAD_DIM, :]
            pv = jnp.dot(jnp.concatenate([vt, ones_rows], axis=0), p,
                         preferred_element_type=jnp.float32)
            hs = slice(h * ACC_ROWS, (h + 1) * ACC_ROWS)
            acc_ref[hs, :] = alpha * acc_ref[hs, :] + pv
            m_ref[h:h + 1, :] = m_new

    for h in range(N_HEADS):
        store_logits(0, lga_ref, h)

    def attn_body(pair, carry):
        kb = 2 * pair
        softmax_block(kb, lga_ref, kb + 1, lgb_ref)
        softmax_block(kb + 1, lgb_ref, kb + 2, lga_ref)
        return carry

    lax.fori_loop(0, (nkb + 1) // 2, attn_body, 0)
    outs = []
    for h in range(N_HEADS):
        num = acc_ref[h * ACC_ROWS:h * ACC_ROWS + HEAD_DIM, :]
        den = acc_ref[h * ACC_ROWS + HEAD_DIM:h * ACC_ROWS + HEAD_DIM + 1, :]
        outs.append(num / den)
    o_ref[0] = jnp.transpose(jnp.concatenate(outs, axis=0)).astype(o_ref.dtype)


def _attn_call(qt, qit, wit, kn, vt, top_k):
    B, S = kn.shape[:2]
    C = ATTN_CHUNK
    nc = S // C
    kern = functools.partial(_attn_kernel, seq=S, top_k=top_k)
    in_specs = [
        pl.BlockSpec((1, 1, Q_WIDTH, C), lambda b, c: (b, c, 0, 0)),
        pl.BlockSpec((1, 1, IDXQ_WIDTH, C), lambda b, c: (b, c, 0, 0)),
        pl.BlockSpec((1, 1, IDX_HEADS, C), lambda b, c: (b, c, 0, 0)),
        pl.BlockSpec((1, S, KN_WIDTH), lambda b, c: (b, 0, 0)),
        pl.BlockSpec((1, nc, KV_WIDTH, C), lambda b, c: (b, 0, 0, 0)),
    ]
    out_specs = pl.BlockSpec((1, C, Q_WIDTH), lambda b, c: (b, c, 0))
    scratch = [
        pltpu.VMEM((nc + 1, C, C), jnp.int32),
        pltpu.VMEM((nc, C, C), jnp.int16),
        pltpu.VMEM((nc, C, C), jnp.int16),
        pltpu.VMEM((N_HEADS, C, C), jnp.float32),
        pltpu.VMEM((N_HEADS, C, C), jnp.float32),
        pltpu.VMEM((N_HEADS * ACC_ROWS, C), jnp.float32),
        pltpu.VMEM((N_HEADS, C), jnp.float32),
    ]
    vmem = (2 * S * C * 4 + Q_WIDTH * C * 4 + 2 * 2 * (2 * Q_WIDTH * C + S * KN_WIDTH + S * KV_WIDTH + C * Q_WIDTH)
            + 24 * C * C * 4 + (8 << 20))
    return pl.pallas_call(
        kern, grid=(B, nc), in_specs=in_specs, out_specs=out_specs,
        out_shape=jax.ShapeDtypeStruct((B, S, Q_WIDTH), jnp.bfloat16),
        scratch_shapes=scratch, compiler_params=_cparams(2, vmem), name="dsa_attn",
    )(qt, qit, wit, kn, vt)


def _layer_norm(v, g, b):
    mu = jnp.mean(v, axis=-1, keepdims=True)
    d = v - mu
    var = jnp.mean(d * d, axis=-1, keepdims=True)
    return d * lax.rsqrt(var + LN_EPS) * g + b


def _pack_rows(h):
    hi = lax.bitcast_convert_type(h[:, :HALF].astype(jnp.bfloat16).astype(jnp.float32), jnp.int32)
    lo = lax.bitcast_convert_type(h[:, HALF:].astype(jnp.bfloat16).astype(jnp.float32), jnp.int32)
    return (hi & jnp.int32(-65536)) | lax.shift_right_logical(lo, 16)


def _unpack_rows(w):
    hi = lax.bitcast_convert_type(w & jnp.int32(-65536), jnp.float32)
    lo = lax.bitcast_convert_type(lax.shift_left(w, 16), jnp.float32)
    return jnp.concatenate([hi, lo], axis=1)


def _mix_kernel(x_ref, attn_ref, wc_ref, wg_ref, gb_ref, cw_ref, wau_ref, wcu_ref, wo_ref,
                g1_ref, b1_ref, wrh_ref, wrl_ref, rb_ref,
                h_ref, xs_ref, eid_ref, rank_ref, gate_ref, cnt_ref,
                ubuf_ref, *, steps_per_seq):
    R = MIX_ROWS
    i = pl.program_id(0)

    @pl.when(i % steps_per_seq == 0)
    def _():
        ubuf_ref[0:V7X_SUBLANES, :] = jnp.zeros((V7X_SUBLANES, CONV_DIM), jnp.float32)

    x = x_ref[...]
    xb = x.astype(jnp.bfloat16)
    cv = jnp.dot(xb, wc_ref[...], preferred_element_type=jnp.float32)
    u = cv[:, 2 * CONV_DIM:] * cv[:, :CONV_DIM]
    ubuf_ref[V7X_SUBLANES:, :] = u
    u1 = ubuf_ref[V7X_SUBLANES - 1:V7X_SUBLANES - 1 + R, :]
    u2 = ubuf_ref[V7X_SUBLANES - 2:V7X_SUBLANES - 2 + R, :]
    y = cw_ref[0:1, :] * u2 + cw_ref[1:2, :] * u1 + cw_ref[2:3, :] * u
    conv = (cv[:, CONV_DIM:2 * CONV_DIM] * y).astype(jnp.bfloat16)
    ubuf_ref[0:V7X_SUBLANES, :] = u[R - V7X_SUBLANES:, :]
    z = jnp.dot(xb, wg_ref[...], preferred_element_type=jnp.float32) + gb_ref[...]
    gates = 1.0 / (1.0 + jnp.exp(-z))
    au = jnp.dot(attn_ref[...], wau_ref[...], preferred_element_type=jnp.float32)
    cu = jnp.dot(conv, wcu_ref[...], preferred_element_type=jnp.float32)
    merged = gates[:, :D_MODEL] * au + gates[:, D_MODEL:] * cu
    mix = jnp.dot(merged.astype(jnp.bfloat16), wo_ref[...], preferred_element_type=jnp.float32)
    h = _layer_norm(DEEPNORM_ALPHA * x + mix, g1_ref[...], b1_ref[...])
    h_ref[...] = h

    h_hi = h.astype(jnp.bfloat16)
    h_lo = (h - h_hi.astype(jnp.float32)).astype(jnp.bfloat16)
    lg = (jnp.dot(h_hi, wrh_ref[...], preferred_element_type=jnp.float32)
          + jnp.dot(h_lo, wrh_ref[...], preferred_element_type=jnp.float32)
          + jnp.dot(h_hi, wrl_ref[...], preferred_element_type=jnp.float32)) + rb_ref[...]
    lane = lax.broadcasted_iota(jnp.int32, (R, ROUTER_LANES), 1).astype(jnp.float32)
    neg = jnp.float32(-jnp.inf)
    no_lane = jnp.float32(ROUTER_LANES)
    gmask = lane < N_GROUPS
    gl = jnp.where(gmask, lg, neg)
    gmax = jnp.max(gl, axis=1, keepdims=True)
    grp = jnp.min(jnp.where(gl == gmax, lane, no_lane), axis=1, keepdims=True)
    gsum = jnp.sum(jnp.where(gmask, jnp.exp(gl - gmax), 0.0), axis=1, keepdims=True)
    p_grp = 1.0 / gsum
    lo_lane = E0 + grp * EXPERTS_PER_GROUP
    emask = (lane >= lo_lane) & (lane < lo_lane + EXPERTS_PER_GROUP)
    el = jnp.where(emask, lg, neg)
    v1 = jnp.max(el, axis=1, keepdims=True)
    i1 = jnp.min(jnp.where(el == v1, lane, no_lane), axis=1, keepdims=True)
    el2 = jnp.where(lane == i1, neg, el)
    v2 = jnp.max(el2, axis=1, keepdims=True)
    i2 = jnp.min(jnp.where(el2 == v2, lane, no_lane), axis=1, keepdims=True)
    a = jnp.exp(v2 - v1)
    inv = 1.0 / (1.0 + a)
    gate_ref[:, 0:1] = p_grp * inv
    gate_ref[:, 1:2] = p_grp * (a * inv)
    eid_ref[:, 0:1] = (i1 - E0).astype(jnp.int32)
    eid_ref[:, 1:2] = (i2 - E0).astype(jnp.int32)
    oh1 = lane == i1
    oh2 = lane == i2
    oh = (oh1 | oh2).astype(jnp.bfloat16)
    r_i = lax.broadcasted_iota(jnp.int32, (R, R), 0)
    c_i = lax.broadcasted_iota(jnp.int32, (R, R), 1)
    tri = (r_i > c_i).astype(jnp.bfloat16)
    before = jnp.dot(tri, oh, preferred_element_type=jnp.float32)
    rank_ref[:, 0:1] = jnp.sum(jnp.where(oh1, before, 0.0), axis=1, keepdims=True).astype(jnp.int32)
    rank_ref[:, 1:2] = jnp.sum(jnp.where(oh2, before, 0.0), axis=1, keepdims=True).astype(jnp.int32)
    cnt = jnp.sum(oh.astype(jnp.float32), axis=0, keepdims=True)
    cnt_ref[0] = cnt

    seg = jnp.floor((cnt + (ROW_GROUP - 1)) * (1.0 / ROW_GROUP)) * ROW_GROUP
    l_r = lax.broadcasted_iota(jnp.int32, (ROUTER_LANES, ROUTER_LANES), 0)
    l_c = lax.broadcasted_iota(jnp.int32, (ROUTER_LANES, ROUTER_LANES), 1)
    lanes_before = (l_r < l_c).astype(jnp.bfloat16)
    seg8 = jnp.broadcast_to(seg, (V7X_SUBLANES, ROUTER_LANES)).astype(jnp.bfloat16)
    seg_off = jnp.dot(seg8, lanes_before, preferred_element_type=jnp.float32)[0:1, :]
    where_to = before + seg_off
    pos1 = jnp.sum(jnp.where(oh1, where_to, 0.0), axis=1, keepdims=True)
    pos2 = jnp.sum(jnp.where(oh2, where_to, 0.0), axis=1, keepdims=True)
    diag = r_i == c_i
    pos1_row = jnp.sum(jnp.where(diag, pos1, 0.0), axis=0, keepdims=True)
    pos2_row = jnp.sum(jnp.where(diag, pos2, 0.0), axis=0, keepdims=True)
    sorted_row = lax.broadcasted_iota(jnp.int32, (SORTED_ROWS, R), 0).astype(jnp.float32)
    place = ((sorted_row == pos1_row) | (sorted_row == pos2_row)).astype(jnp.bfloat16)
    xs = jnp.dot(place, h_hi, preferred_element_type=jnp.float32)
    xs_ref[0] = _pack_rows(xs)


def _mix_call(x2, attn2, wc, wg, gb, cw, wau, wcu, wo, g1, b1, wrh, wrl, rb, seq):
    N, D = x2.shape
    R = MIX_ROWS
    nt = N // R
    kern = functools.partial(_mix_kernel, steps_per_seq=seq // R)

    def full(a):
        return pl.BlockSpec(a.shape, lambda i: (0,) * a.ndim)

    in_specs = [
        pl.BlockSpec((R, D), lambda i: (i, 0)),
        pl.BlockSpec((R, Q_WIDTH), lambda i: (i, 0)),
        full(wc), full(wg), full(gb), full(cw), full(wau), full(wcu), full(wo),
        full(g1), full(b1), full(wrh), full(wrl), full(rb),
    ]
    out_shape = (
        jax.ShapeDtypeStruct((N, D), jnp.float32),
        jax.ShapeDtypeStruct((nt, SORTED_ROWS, HALF), jnp.int32),
        jax.ShapeDtypeStruct((N, 2), jnp.int32),
        jax.ShapeDtypeStruct((N, 2), jnp.int32),
        jax.ShapeDtypeStruct((N, 2), jnp.float32),
        jax.ShapeDtypeStruct((nt, 1, ROUTER_LANES), jnp.float32),
    )
    out_specs = (
        pl.BlockSpec((R, D), lambda i: (i, 0)),
        pl.BlockSpec((1, SORTED_ROWS, HALF), lambda i: (i, 0, 0)),
        pl.BlockSpec((R, 2), lambda i: (i, 0)),
        pl.BlockSpec((R, 2), lambda i: (i, 0)),
        pl.BlockSpec((R, 2), lambda i: (i, 0)),
        pl.BlockSpec((1, 1, ROUTER_LANES), lambda i: (i, 0, 0)),
    )
    scratch = [pltpu.VMEM((R + V7X_SUBLANES, CONV_DIM), jnp.float32)]
    w_bytes = 2 * (wc.size + wg.size + wau.size + wcu.size + wo.size + wrh.size + wrl.size)
    vmem = (2 * w_bytes + 2 * (R * D * 4 * 2 + R * Q_WIDTH * 2 + SORTED_ROWS * HALF * 4) + 10 * R * 2048 * 4
            + SORTED_ROWS * D * 6 + (6 << 20))
    return pl.pallas_call(
        kern, grid=(nt,), in_specs=in_specs, out_specs=out_specs, out_shape=out_shape,
        scratch_shapes=scratch, compiler_params=_cparams(1, vmem), name="mix_ln_router",
    )(x2, attn2, wc, wg, gb, cw, wau, wcu, wo, g1, b1, wrh, wrl, rb)


def _row_copy(src_ref, s, dst_ref, d, sem):
    return pltpu.make_async_copy(src_ref.at[pl.ds(s, 1), :], dst_ref.at[pl.ds(d, 1), :], sem)


GROUPS_PER_BLOCK = EXPERT_ROWS // ROW_GROUP


def _expert_kernel(be_ref, nb_ref, gsrc_ref, xs_ref, wg_ref, wu_ref, wd_ref, ys_ref,
                   wgb_ref, wub_ref, wdb_ref, xbuf_ref, sems):
    j = pl.program_id(0)
    n_used = nb_ref[0]

    def gather(block, slot):
        for g in range(GROUPS_PER_BLOCK):
            row = pl.multiple_of(gsrc_ref[block * GROUPS_PER_BLOCK + g], ROW_GROUP)
            pltpu.make_async_copy(xs_ref.at[pl.ds(row, ROW_GROUP), :],
                                  xbuf_ref.at[slot, pl.ds(g * ROW_GROUP, ROW_GROUP), :],
                                  sems.at[slot]).start(priority=g % 2)

    def wait_gather(slot):
        pltpu.make_async_copy(xs_ref.at[pl.ds(0, EXPERT_ROWS), :], xbuf_ref.at[slot], sems.at[slot]).wait()

    @pl.when((j == 0) & (n_used > 0))
    def _():
        gather(0, 0)

    @pl.when((j == 0) | (be_ref[j] != be_ref[jnp.maximum(j - 1, 0)]))
    def _():
        wgb_ref[...] = wg_ref[0].astype(jnp.bfloat16)
        wub_ref[...] = wu_ref[0].astype(jnp.bfloat16)
        wdb_ref[...] = wd_ref[0].astype(jnp.bfloat16)

    slot = j % 2

    @pl.when(j + 1 < n_used)
    def _():
        gather(j + 1, 1 - slot)

    @pl.when(j < n_used)
    def _():
        wait_gather(slot)
        xb = _unpack_rows(xbuf_ref[slot]).astype(jnp.bfloat16)
        hg = jnp.dot(xb, wgb_ref[...], preferred_element_type=jnp.float32)
        hu = jnp.dot(xb, wub_ref[...], preferred_element_type=jnp.float32)
        hid = (hg / (1.0 + jnp.exp(-hg))) * hu
        y = jnp.dot(hid.astype(jnp.bfloat16), wdb_ref[...], preferred_element_type=jnp.float32)
        ys_ref[...] = _pack_rows(y)

    @pl.when(j >= nb_ref[0])
    def _():
        ys_ref[...] = jnp.zeros_like(ys_ref)


def _expert_call(block_expert, n_used, group_src, xs_sorted, wg, wu, wd, cap):
    W = xs_sorted.shape[-1]
    RB = EXPERT_ROWS
    nb = cap // RB
    grid_spec = pltpu.PrefetchScalarGridSpec(
        num_scalar_prefetch=3, grid=(nb,),
        in_specs=[
            pl.BlockSpec(memory_space=pl.ANY),
            pl.BlockSpec((1, D_MODEL, D_EXPERT), lambda j, be, nu, gs: (be[j], 0, 0)),
            pl.BlockSpec((1, D_MODEL, D_EXPERT), lambda j, be, nu, gs: (be[j], 0, 0)),
            pl.BlockSpec((1, D_EXPERT, D_MODEL), lambda j, be, nu, gs: (be[j], 0, 0)),
        ],
        out_specs=pl.BlockSpec((RB, W), lambda j, be, nu, gs: (j, 0)),
        scratch_shapes=[pltpu.VMEM((D_MODEL, D_EXPERT), jnp.bfloat16),
                        pltpu.VMEM((D_MODEL, D_EXPERT), jnp.bfloat16),
                        pltpu.VMEM((D_EXPERT, D_MODEL), jnp.bfloat16),
                        pltpu.VMEM((2, RB, W), jnp.int32),
                        pltpu.SemaphoreType.DMA((2,))],
    )
    vmem = (2 * 4 + 2) * 3 * D_MODEL * D_EXPERT + 4 * RB * W * 4 + 8 * RB * D_MODEL * 4 + (6 << 20)
    return pl.pallas_call(
        _expert_kernel, grid_spec=grid_spec,
        out_shape=jax.ShapeDtypeStruct((cap, W), jnp.int32),
        compiler_params=_cparams(1, vmem), name="moe_experts",
    )(block_expert, n_used, group_src, xs_sorted, wg, wu, wd)


def _combine_kernel(dest_ref, h_ref, gate_ref, g2_ref, b2_ref, ys_ref, o_ref, buf_ref, sem):
    R = MOVE_ROWS

    def issue(r, carry):
        _row_copy(ys_ref, dest_ref[0, 0, 2 * r], buf_ref.at[0], r, sem).start(priority=0)
        _row_copy(ys_ref, dest_ref[0, 0, 2 * r + 1], buf_ref.at[1], r, sem).start(priority=1)
        return carry

    lax.fori_loop(0, R, issue, 0, unroll=ISSUE_UNROLL)
    for slot in range(2):
        pltpu.make_async_copy(ys_ref.at[pl.ds(0, R), :], buf_ref.at[slot], sem).wait()
    y0 = _unpack_rows(buf_ref[0])
    y1 = _unpack_rows(buf_ref[1])
    ffn = gate_ref[:, 0:1] * y0 + gate_ref[:, 1:2] * y1
    o_ref[...] = _layer_norm(DEEPNORM_ALPHA * h_ref[...] + ffn, g2_ref[...], b2_ref[...])


def _combine_call(dest3, h, gate, g2, b2, ys):
    N, D = h.shape
    W = ys.shape[1]
    R = MOVE_ROWS
    nt = N // R
    return pl.pallas_call(
        _combine_kernel, grid=(nt,),
        in_specs=[
            pl.BlockSpec((1, 1, 2 * R), lambda i: (i, 0, 0), memory_space=pltpu.SMEM),
            pl.BlockSpec((R, D), lambda i: (i, 0)),
            pl.BlockSpec((R, 2), lambda i: (i, 0)),
            pl.BlockSpec((1, D), lambda i: (0, 0)),
            pl.BlockSpec((1, D), lambda i: (0, 0)),
            pl.BlockSpec(memory_space=pl.ANY),
        ],
        out_specs=pl.BlockSpec((R, D), lambda i: (i, 0)),
        out_shape=jax.ShapeDtypeStruct((N, D), jnp.float32),
        scratch_shapes=[pltpu.VMEM((2, R, W), jnp.int32), pltpu.SemaphoreType.DMA],
        compiler_params=_cparams(1, 4 * R * D * 4 + 2 * R * W * 4 + 8 * R * D * 4 + (4 << 20)),
        name="moe_combine",
    )(dest3, h, gate, g2, b2, ys)


def _rope_tables(seq):
    inv_freq = ROPE_THETA ** (-jnp.arange(ROT_HALF, dtype=jnp.float32) / ROT_HALF)
    ang = jnp.arange(seq, dtype=jnp.int32).astype(jnp.float32)[:, None] * inv_freq[None, :]
    cos, sin = jnp.cos(ang), jnp.sin(ang)
    ones = jnp.ones((seq, HEAD_DIM - ROT_DIMS), jnp.float32)
    zeros = jnp.zeros((seq, HEAD_DIM - ROT_DIMS), jnp.float32)
    c_head = jnp.concatenate([cos, cos, ones], axis=1)
    s_head = jnp.concatenate([-sin, sin, zeros], axis=1)
    reps = KN_WIDTH // HEAD_DIM
    return cos.T, sin.T, jnp.tile(c_head, (1, reps)), jnp.tile(s_head, (1, reps))


def _swap_rot_cols(w):
    d, n = w.shape
    wh = w.reshape(d, n // HEAD_DIM, HEAD_DIM)
    sw = jnp.concatenate([wh[:, :, ROT_HALF:ROT_DIMS], wh[:, :, :ROT_HALF],
                          jnp.zeros((d, n // HEAD_DIM, HEAD_DIM - ROT_DIMS), w.dtype)], axis=2)
    return sw.reshape(d, n)


def _block(x, w_in, gate_bias, w_attn_up, w_conv_up, conv_w, w_out, ln_g, ln_b,
           rg_w, rg_b, re_w, re_b, w_gate_e, w_up_e, w_down_e, ln2_g, ln2_b):
    B, S, D = x.shape
    N = B * S
    top_k = min(TOPK_MAX, S // 4)
    bf = jnp.bfloat16
    o = np.cumsum([0, Q_WIDTH, KV_WIDTH, KV_WIDTH, IDXQ_WIDTH, IDX_DIM, IDX_HEADS,
                   CONV_DIM, CONV_DIM, CONV_DIM, N_BRANCHES * D_MODEL])
    w_q, w_k, w_v, w_qi, w_ki, w_wi = (w_in[:, o[i]:o[i + 1]] for i in range(6))
    w_conv = w_in[:, o[6]:o[9]]
    w_gates = w_in[:, o[9]:o[10]]

    wt = jnp.concatenate([w_q.T * (HEAD_DIM ** -0.5 * LOG2_E), w_qi.T, w_v.T, w_wi.T,
                          jnp.zeros((T_ROWS - T_WI0 - IDX_HEADS, D), w_in.dtype)], axis=0).astype(bf)
    pad = jnp.zeros((D, KN_WIDTH - KV_WIDTH - IDX_DIM), w_in.dtype)
    wn = jnp.concatenate([w_k, w_ki, pad, _swap_rot_cols(w_k), _swap_rot_cols(w_ki), pad], axis=1).astype(bf)
    cos_t, sin_t, cos_n, sgn_n = _rope_tables(S)

    qt, qit, vt, wit, kn = _proj_call(x, wt, wn, cos_t, sin_t, cos_n, sgn_n)
    attn = _attn_call(qt, qit, wit, kn, vt, top_k)

    w_r = jnp.concatenate([rg_w, jnp.transpose(re_w, (1, 0, 2)).reshape(D, N_EXPERTS),
                           jnp.zeros((D, ROUTER_LANES - E0 - N_EXPERTS), rg_w.dtype)], axis=1)
    b_r = jnp.concatenate([rg_b, re_b.reshape(-1),
                           jnp.zeros((ROUTER_LANES - E0 - N_EXPERTS,), rg_b.dtype)])[None, :]
    w_rh = w_r.astype(bf)
    w_rl = (w_r - w_rh.astype(jnp.float32)).astype(bf)

    h, xs_sorted, eid, rank, gate, cnt = _mix_call(
        x.reshape(N, D), attn.reshape(N, Q_WIDTH), w_conv.astype(bf), w_gates.astype(bf),
        gate_bias[None, :], conv_w, w_attn_up.astype(bf), w_conv_up.astype(bf), w_out.astype(bf),
        ln_g[None, :], ln_b[None, :], w_rh, w_rl, b_r, S)

    RB, RG = EXPERT_ROWS, ROW_GROUP
    nt = N // MIX_ROWS
    counts = cnt[:, 0, E0:E0 + N_EXPERTS].astype(jnp.int32)
    seg_len = ((counts + RG - 1) // RG) * RG
    seg_local = jnp.cumsum(seg_len, axis=1) - seg_len
    region = jnp.sum(seg_len, axis=0)
    padded = ((region + RB - 1) // RB) * RB
    pad_end = jnp.cumsum(padded)
    pad_start = pad_end - padded
    seg_start = pad_start[None, :] + jnp.cumsum(seg_len, axis=0) - seg_len
    experts = jnp.arange(N_EXPERTS, dtype=jnp.int32)
    start_tok = jnp.repeat(seg_start, MIX_ROWS, axis=0)
    dest = jnp.sum(jnp.where(eid[:, :, None] == experts[None, None, :], start_tok[:, None, :], 0), axis=-1) + rank
    cap = -(-(N * 2 + nt * N_EXPERTS * (RG - 1) + N_EXPERTS * (RB - 1)) // RB) * RB
    nb = cap // RB
    block_start = jnp.arange(nb, dtype=jnp.int32) * RB
    block_expert = jnp.minimum(jnp.sum((block_start[:, None] >= pad_end[None, :]).astype(jnp.int32), axis=1),
                               N_EXPERTS - 1)
    n_used = (pad_end[-1:] // RB).astype(jnp.int32)
    flat_start = seg_start.T.reshape(-1)
    flat_end = flat_start + seg_len.T.reshape(-1)
    step_base = (jnp.arange(nt, dtype=jnp.int32) * SORTED_ROWS)[:, None]
    flat_shift = (step_base + seg_local - seg_start).T.reshape(-1)
    g_row = jnp.arange(cap // RG, dtype=jnp.int32) * RG
    ended = flat_end[None, :-1] <= g_row[:, None]

    def lookup(v):
        return v[0] + jnp.sum(jnp.where(ended, (v[1:] - v[:-1])[None, :], 0), axis=1)

    live = (g_row >= lookup(flat_start)) & (g_row < lookup(flat_end))
    group_src = jnp.where(live, g_row + lookup(flat_shift), SORTED_ROWS - RG).astype(jnp.int32)
    dest3 = dest.reshape(N // MOVE_ROWS, 1, 2 * MOVE_ROWS)

    ys = _expert_call(block_expert, n_used, group_src, xs_sorted.reshape(nt * SORTED_ROWS, HALF),
                      w_gate_e, w_up_e, w_down_e, cap)
    out = _combine_call(dest3, h, gate, ln2_g[None, :], ln2_b[None, :], ys)
    return out.reshape(B, S, D)


def kernel(x, w_in, gate_bias, w_attn_up, w_conv_up, conv_w, w_out, ln1_g, ln1_b, router_group_w,
           router_group_b, router_expert_w, router_expert_b, w_gate_e, w_up_e, w_down_e, ln2_g, ln2_b):
    h = x
    for l in range(DEPTH):
        h = _block(h, w_in[l], gate_bias[l], w_attn_up[l], w_conv_up[l], conv_w[l], w_out[l],
                   ln1_g[l], ln1_b[l], router_group_w[l], router_group_b[l], router_expert_w[l],
                   router_expert_b[l], w_gate_e[l], w_up_e[l], w_down_e[l], ln2_g[l], ln2_b[l])
    return h
```

```python
import functools

import jax
import jax.numpy as jnp
import numpy as np
from jax import lax
from jax.experimental import pallas as pl
from jax.experimental.pallas import tpu as pltpu

D_MODEL = 1024
N_HEADS = 8
N_KV_HEADS = 2
HEAD_DIM = 64
Q_WIDTH = N_HEADS * HEAD_DIM
KV_WIDTH = N_KV_HEADS * HEAD_DIM
ROPE_THETA = 500000.0
ROT_DIMS = HEAD_DIM // 4
ROT_HALF = ROT_DIMS // 2
IDX_HEADS = 8
IDX_DIM = 64
IDXQ_WIDTH = IDX_HEADS * IDX_DIM
TOPK_MAX = 256
CONV_DIM = 512
CONV_WIDTH = 3
N_BRANCHES = 2
N_GROUPS = 4
EXPERTS_PER_GROUP = 8
N_EXPERTS = N_GROUPS * EXPERTS_PER_GROUP
D_EXPERT = 512
LN_EPS = 1e-5
DEPTH = 1
DEEPNORM_ALPHA = (2 * DEPTH) ** 0.25

V7X_LANES = 128
V7X_SUBLANES = 8
V7X_VMEM_LIMIT_BYTES = 56 * 1024 * 1024

PROJ_ROWS = 512
ATTN_CHUNK = 256
ATTN_KEY_TILE = 128
DEN_ROWS = 16
ACC_ROWS = HEAD_DIM + DEN_ROWS
LOG2_E = 1.4426950408889634
MIX_ROWS = 512
EXPERT_ROWS = 256
ROW_GROUP = V7X_SUBLANES
SORTED_USED = 2 * MIX_ROWS + N_EXPERTS * (ROW_GROUP - 1)
SORTED_ROWS = -(-(SORTED_USED + ROW_GROUP) // 256) * 256
HALF = D_MODEL // 2

T_Q0, T_QI0, T_V0, T_WI0 = 0, Q_WIDTH, Q_WIDTH + IDXQ_WIDTH, Q_WIDTH + IDXQ_WIDTH + KV_WIDTH
T_ROWS = T_WI0 + 16
KN_WIDTH = 256

INT_MIN = -2147483648
HALF_RANGE = 32768
ROUTER_LANES = 128
E0 = N_GROUPS


def _cparams(n_axes, vmem_bytes):
    return pltpu.CompilerParams(
        dimension_semantics=("arbitrary",) * n_axes,
        vmem_limit_bytes=int(min(vmem_bytes, V7X_VMEM_LIMIT_BYTES)),
    )


def _proj_kernel(x_ref, wt_ref, wn_ref, cost_ref, sint_ref, cosn_ref, sgnn_ref,
                 qt_ref, qit_ref, vt_ref, wit_ref, kn_ref):
    xb = x_ref[0].astype(jnp.bfloat16)
    pt = lax.dot_general(wt_ref[...], xb, (((1,), (1,)), ((), ())),
                         preferred_element_type=jnp.float32)
    cos = cost_ref[...]
    sin = sint_ref[...]

    def rope_t(dst_ref, base):
        for h in range(N_HEADS):
            r0 = base + h * HEAD_DIM
            x1 = pt[r0:r0 + ROT_HALF]
            x2 = pt[r0 + ROT_HALF:r0 + ROT_DIMS]
            head = jnp.concatenate([x1 * cos - x2 * sin, x2 * cos + x1 * sin,
                                    pt[r0 + ROT_DIMS:r0 + HEAD_DIM]], axis=0).astype(dst_ref.dtype)
            for jj, cols in enumerate(chunks):
                dst_ref[0, jj, h * HEAD_DIM:(h + 1) * HEAD_DIM, :] = head[:, cols]

    chunks = [slice(jj * ATTN_CHUNK, (jj + 1) * ATTN_CHUNK) for jj in range(PROJ_ROWS // ATTN_CHUNK)]
    rope_t(qt_ref, T_Q0)
    rope_t(qit_ref, T_QI0)
    for jj, cols in enumerate(chunks):
        vt_ref[0, jj] = pt[T_V0:T_V0 + KV_WIDTH, cols].astype(vt_ref.dtype)
        wit_ref[0, jj] = pt[T_WI0:T_WI0 + IDX_HEADS, cols]
    pn = jnp.dot(xb, wn_ref[...], preferred_element_type=jnp.float32)
    kn = pn[:, :KN_WIDTH] * cosn_ref[...] + pn[:, KN_WIDTH:] * sgnn_ref[...]
    kn_ref[0] = kn.astype(kn_ref.dtype)


def _proj_call(x, wt, wn, cos_t, sin_t, cos_n, sgn_n):
    B, S, D = x.shape
    R = PROJ_ROWS
    nt = S // R
    grid = (B, nt)
    out_shape = (
        jax.ShapeDtypeStruct((B, S // ATTN_CHUNK, Q_WIDTH, ATTN_CHUNK), jnp.bfloat16),
        jax.ShapeDtypeStruct((B, S // ATTN_CHUNK, IDXQ_WIDTH, ATTN_CHUNK), jnp.bfloat16),
        jax.ShapeDtypeStruct((B, S // ATTN_CHUNK, KV_WIDTH, ATTN_CHUNK), jnp.bfloat16),
        jax.ShapeDtypeStruct((B, S // ATTN_CHUNK, IDX_HEADS, ATTN_CHUNK), jnp.float32),
        jax.ShapeDtypeStruct((B, S, KN_WIDTH), jnp.bfloat16),
    )
    in_specs = [
        pl.BlockSpec((1, R, D), lambda b, j: (b, j, 0)),
        pl.BlockSpec((T_ROWS, D), lambda b, j: (0, 0)),
        pl.BlockSpec((D, 2 * KN_WIDTH), lambda b, j: (0, 0)),
        pl.BlockSpec((ROT_HALF, R), lambda b, j: (0, j)),
        pl.BlockSpec((ROT_HALF, R), lambda b, j: (0, j)),
        pl.BlockSpec((R, KN_WIDTH), lambda b, j: (j, 0)),
        pl.BlockSpec((R, KN_WIDTH), lambda b, j: (j, 0)),
    ]
    out_specs = (
        pl.BlockSpec((1, R // ATTN_CHUNK, Q_WIDTH, ATTN_CHUNK), lambda b, j: (b, j, 0, 0)),
        pl.BlockSpec((1, R // ATTN_CHUNK, IDXQ_WIDTH, ATTN_CHUNK), lambda b, j: (b, j, 0, 0)),
        pl.BlockSpec((1, R // ATTN_CHUNK, KV_WIDTH, ATTN_CHUNK), lambda b, j: (b, j, 0, 0)),
        pl.BlockSpec((1, R // ATTN_CHUNK, IDX_HEADS, ATTN_CHUNK), lambda b, j: (b, j, 0, 0)),
        pl.BlockSpec((1, R, KN_WIDTH), lambda b, j: (b, j, 0)),
    )
    vmem = 2 * (R * D * 4 + T_ROWS * D * 2 + D * 2 * KN_WIDTH * 2) + 6 * T_ROWS * R * 4 + (8 << 20)
    return pl.pallas_call(
        _proj_kernel, grid=grid, in_specs=in_specs, out_specs=out_specs, out_shape=out_shape,
        compiler_params=_cparams(2, vmem), name="dsa_proj",
    )(x, wt, wn, cos_t, sin_t, cos_n, sgn_n)


def _float_to_key(s):
    b = lax.bitcast_convert_type(s, jnp.int32)
    k = b ^ (lax.shift_right_arithmetic(b, 31) & jnp.int32(0x7FFFFFFF))
    return jnp.where(b == jnp.int32(INT_MIN), jnp.int32(0), k)


def _attn_kernel(qt_ref, qit_ref, wit_ref, kn_ref, vt_ref, o_ref,
                 key_ref, hi_ref, lo_ref, lga_ref, lgb_ref, acc_ref, m_ref, *, seq, top_k):
    C = ATTN_CHUNK
    c = pl.program_id(1)
    nkb = c + 1
    zeros_half = jnp.zeros((HEAD_DIM, C), jnp.bfloat16)
    row_i = lax.broadcasted_iota(jnp.int32, (C, C), 0)
    lane_i = lax.broadcasted_iota(jnp.int32, (C, C), 1)
    causal_in_block = row_i <= lane_i

    def score_body(kb, carry):
        r0 = pl.multiple_of(kb * C, C)
        kix = kn_ref[0, pl.ds(r0, C), KV_WIDTH:KN_WIDTH]
        acc = None
        for h in range(IDX_HEADS):
            rhs = jnp.concatenate([qit_ref[0, 0, h * IDX_DIM:(h + 1) * IDX_DIM, :], zeros_half], axis=0)
            s = jnp.dot(kix, rhs, preferred_element_type=jnp.float32)
            t = jnp.maximum(s, 0.0) * wit_ref[0, 0, h:h + 1, :]
            acc = t if acc is None else acc + t
        score = acc * (IDX_DIM ** -0.5 * IDX_HEADS ** -0.5)
        keys = jnp.where(causal_in_block | (kb != c), _float_to_key(score), jnp.int32(INT_MIN))
        key_ref[kb] = keys
        hi_ref[kb] = lax.shift_right_arithmetic(keys, 16).astype(jnp.int16)
        lo_ref[kb] = ((keys & jnp.int32(0xFFFF)) - jnp.int32(HALF_RANGE)).astype(jnp.int16)
        return carry

    lax.fori_loop(0, nkb, score_body, 0)

    def count(ref, pred):
        packed = ref.dtype == jnp.int16

        def body(kb, part):
            hit = pred(ref[kb], kb)
            if packed:
                words = pltpu.bitcast(jnp.where(hit, jnp.int16(1), jnp.int16(0)), jnp.int32)
            else:
                words = hit.astype(jnp.int32)
            return part + jnp.sum(words.reshape(-1, V7X_SUBLANES, C), axis=0)

        part = lax.fori_loop(0, nkb, body, jnp.zeros((V7X_SUBLANES, C), jnp.int32))
        if packed:
            part = (part & jnp.int32(0xFFFF)) + lax.shift_right_logical(part, 16)
        return jnp.sum(part, axis=0, keepdims=True)

    def kth_largest_16(ref):
        def body(i, ans_u):
            cand_u = ans_u | lax.shift_left(jnp.int32(1), jnp.int32(15) - i)
            cand = (cand_u - jnp.int32(HALF_RANGE)).astype(jnp.int16)
            return jnp.where(count(ref, lambda k, kb: k >= cand) >= top_k, cand_u, ans_u)

        return lax.fori_loop(0, 16, body, jnp.zeros((1, C), jnp.int32)) - jnp.int32(HALF_RANGE)

    t_hi = kth_largest_16(hi_ref)
    t_hi16 = t_hi.astype(jnp.int16)

    def refine_body(kb, carry):
        hi = hi_ref[kb]
        lo_ref[kb] = jnp.where(hi > t_hi16, jnp.int16(HALF_RANGE - 1),
                               jnp.where(hi == t_hi16, lo_ref[kb], jnp.int16(-HALF_RANGE)))
        return carry

    lax.fori_loop(0, nkb, refine_body, 0)
    t_lo = kth_largest_16(lo_ref)
    thr_raw = t_hi * jnp.int32(2 * HALF_RANGE) + (t_lo + jnp.int32(HALF_RANGE))
    thr = jnp.maximum(thr_raw, jnp.int32(INT_MIN + 1))

    t_lo16 = t_lo.astype(jnp.int16)
    n_ge = count(lo_ref, lambda k, kb: k >= t_lo16)
    has_ties = jnp.max(jnp.where(thr_raw != jnp.int32(INT_MIN), n_ge, 0)) > top_k

    @pl.when(has_ties)
    def _():
        need = (top_k - count(key_ref, lambda k, kb: k > thr)).astype(jnp.float32)
        earlier_rows = (row_i > lane_i).astype(jnp.bfloat16)

        def drop_body(kb, seen):
            kk = key_ref[kb]
            tied = kk == thr
            tied_b = jnp.where(tied, 1.0, 0.0).astype(jnp.bfloat16)
            rank = jnp.dot(earlier_rows, tied_b, preferred_element_type=jnp.float32) + seen
            key_ref[kb] = jnp.where(tied & (rank >= need), kk - 1, kk)
            return seen + jnp.sum(tied_b.astype(jnp.float32), axis=0, keepdims=True)

        lax.fori_loop(0, nkb, drop_body, jnp.zeros((1, C), jnp.float32))

    neg_inf = jnp.float32(-jnp.inf)
    m_ref[...] = jnp.full(m_ref.shape, neg_inf, jnp.float32)
    acc_ref[...] = jnp.zeros(acc_ref.shape, jnp.float32)
    T = ATTN_KEY_TILE
    ones_rows = jnp.ones((DEN_ROWS, C), jnp.bfloat16)

    kv_group = N_HEADS // N_KV_HEADS

    def bias_body(kb, carry):
        key_ref[kb] = lax.bitcast_convert_type(jnp.where(key_ref[kb] >= thr, 0.0, neg_inf), jnp.int32)
        return carry

    lax.fori_loop(0, nkb, bias_body, 0)
    n_blocks = seq // C
    key_ref[n_blocks] = lax.bitcast_convert_type(jnp.full((C, C), neg_inf, jnp.float32), jnp.int32)

    def store_logits(kb, dst_ref, h):
        kb_mem = jnp.minimum(kb, nkb - 1)
        kb_bias = jnp.where(kb < nkb, kb, n_blocks)
        r0 = pl.multiple_of(kb_mem * C, C)
        bias = lax.bitcast_convert_type(key_ref[kb_bias], jnp.float32)
        k2 = kn_ref[0, pl.ds(r0, C), 0:KV_WIDTH]
        qh = qt_ref[0, 0, h * HEAD_DIM:(h + 1) * HEAD_DIM, :]
        rhs = jnp.concatenate([qh, zeros_half] if h < kv_group else [zeros_half, qh], axis=0)
        dst_ref[h] = jnp.dot(k2, rhs, preferred_element_type=jnp.float32) + bias

    def softmax_block(kb, src_ref, kb_next, dst_ref):
        kb_mem = jnp.minimum(kb, nkb - 1)
        for h in range(N_HEADS):
            store_logits(kb_next, dst_ref, h)
            g = h // kv_group
            m_old = m_ref[h:h + 1, :]
            m_new = jnp.maximum(m_old, jnp.max(src_ref[h], axis=0, keepdims=True))
            m_safe = jnp.where(m_new == neg_inf, 0.0, m_new)
            p = jnp.concatenate(
                [jnp.exp2(src_ref[h, s * T:(s + 1) * T, :] - m_safe).astype(jnp.bfloat16) for s in range(C // T)],
                axis=0)
            alpha = jnp.exp2(m_old - m_safe)
            vt = vt_ref[0, kb_mem, g * HEAD_DIM:(g + 1) * HEAD_DIM, :]
            pv = jnp.dot(jnp.concatenate([vt, ones_rows], axis=0), p,
                         preferred_element_type=jnp.float32)
            hs = slice(h * ACC_ROWS, (h + 1) * ACC_ROWS)
            acc_ref[hs, :] = alpha * acc_ref[hs, :] + pv
            m_ref[h:h + 1, :] = m_new

    for h in range(N_HEADS):
        store_logits(0, lga_ref, h)

    def attn_body(pair, carry):
        kb = 2 * pair
        softmax_block(kb, lga_ref, kb + 1, lgb_ref)
        softmax_block(kb + 1, lgb_ref, kb + 2, lga_ref)
        return carry

    lax.fori_loop(0, (nkb + 1) // 2, attn_body, 0)
    outs = []
    for h in range(N_HEADS):
        num = acc_ref[h * ACC_ROWS:h * ACC_ROWS + HEAD_DIM, :]
        den = acc_ref[h * ACC_ROWS + HEAD_DIM:h * ACC_ROWS + HEAD_DIM + 1, :]
        outs.append(num / den)
    o_ref[0] = jnp.transpose(jnp.concatenate(outs, axis=0)).astype(o_ref.dtype)


def _attn_call(qt, qit, wit, kn, vt, top_k):
    B, S = kn.shape[:2]
    C = ATTN_CHUNK
    nc = S // C
    kern = functools.partial(_attn_kernel, seq=S, top_k=top_k)
    in_specs = [
        pl.BlockSpec((1, 1, Q_WIDTH, C), lambda b, c: (b, c, 0, 0)),
        pl.BlockSpec((1, 1, IDXQ_WIDTH, C), lambda b, c: (b, c, 0, 0)),
        pl.BlockSpec((1, 1, IDX_HEADS, C), lambda b, c: (b, c, 0, 0)),
        pl.BlockSpec((1, S, KN_WIDTH), lambda b, c: (b, 0, 0)),
        pl.BlockSpec((1, nc, KV_WIDTH, C), lambda b, c: (b, 0, 0, 0)),
    ]
    out_specs = pl.BlockSpec((1, C, Q_WIDTH), lambda b, c: (b, c, 0))
    scratch = [
        pltpu.VMEM((nc + 1, C, C), jnp.int32),
        pltpu.VMEM((nc, C, C), jnp.int16),
        pltpu.VMEM((nc, C, C), jnp.int16),
        pltpu.VMEM((N_HEADS, C, C), jnp.float32),
        pltpu.VMEM((N_HEADS, C, C), jnp.float32),
        pltpu.VMEM((N_HEADS * ACC_ROWS, C), jnp.float32),
        pltpu.VMEM((N_HEADS, C), jnp.float32),
    ]
    vmem = (2 * S * C * 4 + Q_WIDTH * C * 4 + 2 * 2 * (2 * Q_WIDTH * C + S * KN_WIDTH + S * KV_WIDTH + C * Q_WIDTH)
            + 24 * C * C * 4 + (8 << 20))
    return pl.pallas_call(
        kern, grid=(B, nc), in_specs=in_specs, out_specs=out_specs,
        out_shape=jax.ShapeDtypeStruct((B, S, Q_WIDTH), jnp.bfloat16),
        scratch_shapes=scratch, compiler_params=_cparams(2, vmem), name="dsa_attn",
    )(qt, qit, wit, kn, vt)


def _layer_norm(v, g, b):
    mu = jnp.mean(v, axis=-1, keepdims=True)
    d = v - mu
    var = jnp.mean(d * d, axis=-1, keepdims=True)
    return d * lax.rsqrt(var + LN_EPS) * g + b


def _pack_rows(h):
    hi = lax.bitcast_convert_type(h[:, :HALF].astype(jnp.bfloat16).astype(jnp.float32), jnp.int32)
    lo = lax.bitcast_convert_type(h[:, HALF:].astype(jnp.bfloat16).astype(jnp.float32), jnp.int32)
    return (hi & jnp.int32(-65536)) | lax.shift_right_logical(lo, 16)


def _unpack_rows(w):
    hi = lax.bitcast_convert_type(w & jnp.int32(-65536), jnp.float32)
    lo = lax.bitcast_convert_type(lax.shift_left(w, 16), jnp.float32)
    return jnp.concatenate([hi, lo], axis=1)


def _mix_kernel(x_ref, attn_ref, wc_ref, wg_ref, gb_ref, cw_ref, wau_ref, wcu_ref, wo_ref,
                g1_ref, b1_ref, wrh_ref, wrl_ref, rb_ref,
                h_ref, xs_ref, pos_ref, gate_ref, cnt_ref,
                ubuf_ref, *, steps_per_seq):
    R = MIX_ROWS
    i = pl.program_id(0)

    @pl.when(i % steps_per_seq == 0)
    def _():
        ubuf_ref[0:V7X_SUBLANES, :] = jnp.zeros((V7X_SUBLANES, CONV_DIM), jnp.float32)

    x = x_ref[...]
    xb = x.astype(jnp.bfloat16)
    cv = jnp.dot(xb, wc_ref[...], preferred_element_type=jnp.float32)
    u = cv[:, 2 * CONV_DIM:] * cv[:, :CONV_DIM]
    ubuf_ref[V7X_SUBLANES:, :] = u
    u1 = ubuf_ref[V7X_SUBLANES - 1:V7X_SUBLANES - 1 + R, :]
    u2 = ubuf_ref[V7X_SUBLANES - 2:V7X_SUBLANES - 2 + R, :]
    y = cw_ref[0:1, :] * u2 + cw_ref[1:2, :] * u1 + cw_ref[2:3, :] * u
    conv = (cv[:, CONV_DIM:2 * CONV_DIM] * y).astype(jnp.bfloat16)
    ubuf_ref[0:V7X_SUBLANES, :] = u[R - V7X_SUBLANES:, :]
    z = jnp.dot(xb, wg_ref[...], preferred_element_type=jnp.float32) + gb_ref[...]
    gates = 1.0 / (1.0 + jnp.exp(-z))
    au = jnp.dot(attn_ref[...], wau_ref[...], preferred_element_type=jnp.float32)
    cu = jnp.dot(conv, wcu_ref[...], preferred_element_type=jnp.float32)
    merged = gates[:, :D_MODEL] * au + gates[:, D_MODEL:] * cu
    mix = jnp.dot(merged.astype(jnp.bfloat16), wo_ref[...], preferred_element_type=jnp.float32)
    h = _layer_norm(DEEPNORM_ALPHA * x + mix, g1_ref[...], b1_ref[...])
    h_ref[...] = h

    h_hi = h.astype(jnp.bfloat16)
    h_lo = (h - h_hi.astype(jnp.float32)).astype(jnp.bfloat16)
    lg = (jnp.dot(h_hi, wrh_ref[...], preferred_element_type=jnp.float32)
          + jnp.dot(h_lo, wrh_ref[...], preferred_element_type=jnp.float32)
          + jnp.dot(h_hi, wrl_ref[...], preferred_element_type=jnp.float32)) + rb_ref[...]
    lane = lax.broadcasted_iota(jnp.int32, (R, ROUTER_LANES), 1).astype(jnp.float32)
    neg = jnp.float32(-jnp.inf)
    no_lane = jnp.float32(ROUTER_LANES)
    gmask = lane < N_GROUPS
    gl = jnp.where(gmask, lg, neg)
    gmax = jnp.max(gl, axis=1, keepdims=True)
    grp = jnp.min(jnp.where(gl == gmax, lane, no_lane), axis=1, keepdims=True)
    gsum = jnp.sum(jnp.where(gmask, jnp.exp(gl - gmax), 0.0), axis=1, keepdims=True)
    p_grp = 1.0 / gsum
    lo_lane = E0 + grp * EXPERTS_PER_GROUP
    emask = (lane >= lo_lane) & (lane < lo_lane + EXPERTS_PER_GROUP)
    el = jnp.where(emask, lg, neg)
    v1 = jnp.max(el, axis=1, keepdims=True)
    i1 = jnp.min(jnp.where(el == v1, lane, no_lane), axis=1, keepdims=True)
    el2 = jnp.where(lane == i1, neg, el)
    v2 = jnp.max(el2, axis=1, keepdims=True)
    i2 = jnp.min(jnp.where(el2 == v2, lane, no_lane), axis=1, keepdims=True)
    a = jnp.exp(v2 - v1)
    inv = 1.0 / (1.0 + a)
    gate_ref[:, 0:1] = p_grp * inv
    gate_ref[:, 1:2] = p_grp * (a * inv)
    oh1 = lane == i1
    oh2 = lane == i2
    oh = (oh1 | oh2).astype(jnp.bfloat16)
    r_i = lax.broadcasted_iota(jnp.int32, (R, R), 0)
    c_i = lax.broadcasted_iota(jnp.int32, (R, R), 1)
    tri = (r_i > c_i).astype(jnp.bfloat16)
    before = jnp.dot(tri, oh, preferred_element_type=jnp.float32)
    cnt = jnp.sum(oh.astype(jnp.float32), axis=0, keepdims=True)
    cnt_ref[0] = cnt

    seg = jnp.floor((cnt + (ROW_GROUP - 1)) * (1.0 / ROW_GROUP)) * ROW_GROUP
    l_r = lax.broadcasted_iota(jnp.int32, (ROUTER_LANES, ROUTER_LANES), 0)
    l_c = lax.broadcasted_iota(jnp.int32, (ROUTER_LANES, ROUTER_LANES), 1)
    lanes_before = (l_r < l_c).astype(jnp.bfloat16)
    seg8 = jnp.broadcast_to(seg, (V7X_SUBLANES, ROUTER_LANES)).astype(jnp.bfloat16)
    seg_off = jnp.dot(seg8, lanes_before, preferred_element_type=jnp.float32)[0:1, :]
    where_to = before + seg_off
    pos1 = jnp.sum(jnp.where(oh1, where_to, 0.0), axis=1, keepdims=True)
    pos2 = jnp.sum(jnp.where(oh2, where_to, 0.0), axis=1, keepdims=True)
    pos_ref[:, 0:1] = pos1.astype(jnp.int32)
    pos_ref[:, 1:2] = pos2.astype(jnp.int32)
    diag = r_i == c_i
    pos1_row = jnp.sum(jnp.where(diag, pos1, 0.0), axis=0, keepdims=True)
    pos2_row = jnp.sum(jnp.where(diag, pos2, 0.0), axis=0, keepdims=True)
    sorted_row = lax.broadcasted_iota(jnp.int32, (SORTED_ROWS, R), 0).astype(jnp.float32)
    place = ((sorted_row == pos1_row) | (sorted_row == pos2_row)).astype(jnp.bfloat16)
    xs = jnp.dot(place, h_hi, preferred_element_type=jnp.float32)
    xs_ref[0] = _pack_rows(xs)


def _mix_call(x2, attn2, wc, wg, gb, cw, wau, wcu, wo, g1, b1, wrh, wrl, rb, seq):
    N, D = x2.shape
    R = MIX_ROWS
    nt = N // R
    kern = functools.partial(_mix_kernel, steps_per_seq=seq // R)

    def full(a):
        return pl.BlockSpec(a.shape, lambda i: (0,) * a.ndim)

    in_specs = [
        pl.BlockSpec((R, D), lambda i: (i, 0)),
        pl.BlockSpec((R, Q_WIDTH), lambda i: (i, 0)),
        full(wc), full(wg), full(gb), full(cw), full(wau), full(wcu), full(wo),
        full(g1), full(b1), full(wrh), full(wrl), full(rb),
    ]
    out_shape = (
        jax.ShapeDtypeStruct((N, D), jnp.float32),
        jax.ShapeDtypeStruct((nt, SORTED_ROWS, HALF), jnp.int32),
        jax.ShapeDtypeStruct((N, 2), jnp.int32),
        jax.ShapeDtypeStruct((N, 2), jnp.float32),
        jax.ShapeDtypeStruct((nt, 1, ROUTER_LANES), jnp.float32),
    )
    out_specs = (
        pl.BlockSpec((R, D), lambda i: (i, 0)),
        pl.BlockSpec((1, SORTED_ROWS, HALF), lambda i: (i, 0, 0)),
        pl.BlockSpec((R, 2), lambda i: (i, 0)),
        pl.BlockSpec((R, 2), lambda i: (i, 0)),
        pl.BlockSpec((1, 1, ROUTER_LANES), lambda i: (i, 0, 0)),
    )
    scratch = [pltpu.VMEM((R + V7X_SUBLANES, CONV_DIM), jnp.float32)]
    w_bytes = 2 * (wc.size + wg.size + wau.size + wcu.size + wo.size + wrh.size + wrl.size)
    vmem = (2 * w_bytes + 2 * (R * D * 4 * 2 + R * Q_WIDTH * 2 + SORTED_ROWS * HALF * 4) + 10 * R * 2048 * 4
            + SORTED_ROWS * D * 6 + (6 << 20))
    return pl.pallas_call(
        kern, grid=(nt,), in_specs=in_specs, out_specs=out_specs, out_shape=out_shape,
        scratch_shapes=scratch, compiler_params=_cparams(1, vmem), name="mix_ln_router",
    )(x2, attn2, wc, wg, gb, cw, wau, wcu, wo, g1, b1, wrh, wrl, rb)


GROUPS_PER_BLOCK = EXPERT_ROWS // ROW_GROUP


def _expert_kernel(be_ref, nb_ref, gsrc_ref, gdst_ref, xs_ref, wg_ref, wu_ref, wd_ref, ys_ref,
                   wgb_ref, wub_ref, wdb_ref, xbuf_ref, ybuf_ref, gsems, ssems):
    del xs_ref
    j = pl.program_id(0)
    n_used = nb_ref[0]
    slot = j % 2

    def gather(block, to_slot):
        for g in range(GROUPS_PER_BLOCK):
            row = pl.multiple_of(gsrc_ref[block * GROUPS_PER_BLOCK + g], ROW_GROUP)
            pltpu.make_async_copy(ys_ref.at[pl.ds(row, ROW_GROUP), :],
                                  xbuf_ref.at[to_slot, pl.ds(g * ROW_GROUP, ROW_GROUP), :],
                                  gsems.at[to_slot]).start(priority=g % 2)

    def scatter(block, from_slot):
        for g in range(GROUPS_PER_BLOCK):
            row = pl.multiple_of(gdst_ref[block * GROUPS_PER_BLOCK + g], ROW_GROUP)
            pltpu.make_async_copy(ybuf_ref.at[from_slot, pl.ds(g * ROW_GROUP, ROW_GROUP), :],
                                  ys_ref.at[pl.ds(row, ROW_GROUP), :],
                                  ssems.at[from_slot]).start(priority=g % 2)

    def wait_gather(s):
        pltpu.make_async_copy(ys_ref.at[pl.ds(0, EXPERT_ROWS), :], xbuf_ref.at[s], gsems.at[s]).wait()

    def wait_scatter(s):
        pltpu.make_async_copy(ybuf_ref.at[s], ys_ref.at[pl.ds(0, EXPERT_ROWS), :], ssems.at[s]).wait()

    @pl.when((j == 0) & (n_used > 0))
    def _():
        gather(0, 0)

    @pl.when((j == 0) | (be_ref[j] != be_ref[jnp.maximum(j - 1, 0)]))
    def _():
        wgb_ref[...] = wg_ref[0].astype(jnp.bfloat16)
        wub_ref[...] = wu_ref[0].astype(jnp.bfloat16)
        wdb_ref[...] = wd_ref[0].astype(jnp.bfloat16)

    @pl.when(j + 1 < n_used)
    def _():
        gather(j + 1, 1 - slot)

    @pl.when((j >= 2) & (j < n_used))
    def _():
        wait_scatter(slot)

    @pl.when(j < n_used)
    def _():
        wait_gather(slot)
        xb = _unpack_rows(xbuf_ref[slot]).astype(jnp.bfloat16)
        hg = jnp.dot(xb, wgb_ref[...], preferred_element_type=jnp.float32)
        hu = jnp.dot(xb, wub_ref[...], preferred_element_type=jnp.float32)
        hid = (hg / (1.0 + jnp.exp(-hg))) * hu
        y = jnp.dot(hid.astype(jnp.bfloat16), wdb_ref[...], preferred_element_type=jnp.float32)
        ybuf_ref[slot] = _pack_rows(y)
        scatter(j, slot)

    @pl.when(j == n_used - 1)
    def _():
        wait_scatter(slot)

    @pl.when((j == n_used - 1) & (j >= 1))
    def _():
        wait_scatter(1 - slot)


def _expert_call(block_expert, n_used, group_src, group_dst, xs_sorted, wg, wu, wd, n_blocks):
    W = xs_sorted.shape[-1]
    RB = EXPERT_ROWS
    grid_spec = pltpu.PrefetchScalarGridSpec(
        num_scalar_prefetch=4, grid=(n_blocks,),
        in_specs=[
            pl.BlockSpec(memory_space=pl.ANY),
            pl.BlockSpec((1, D_MODEL, D_EXPERT), lambda j, be, nu, gs, gd: (be[j], 0, 0)),
            pl.BlockSpec((1, D_MODEL, D_EXPERT), lambda j, be, nu, gs, gd: (be[j], 0, 0)),
            pl.BlockSpec((1, D_EXPERT, D_MODEL), lambda j, be, nu, gs, gd: (be[j], 0, 0)),
        ],
        out_specs=pl.BlockSpec(memory_space=pl.ANY),
        scratch_shapes=[pltpu.VMEM((D_MODEL, D_EXPERT), jnp.bfloat16),
                        pltpu.VMEM((D_MODEL, D_EXPERT), jnp.bfloat16),
                        pltpu.VMEM((D_EXPERT, D_MODEL), jnp.bfloat16),
                        pltpu.VMEM((2, RB, W), jnp.int32),
                        pltpu.VMEM((2, RB, W), jnp.int32),
                        pltpu.SemaphoreType.DMA((2,)),
                        pltpu.SemaphoreType.DMA((2,))],
    )
    vmem = (2 * 4 + 2) * 3 * D_MODEL * D_EXPERT + 4 * RB * W * 4 + 8 * RB * D_MODEL * 4 + (6 << 20)
    return pl.pallas_call(
        _expert_kernel, grid_spec=grid_spec,
        out_shape=jax.ShapeDtypeStruct(xs_sorted.shape, xs_sorted.dtype),
        input_output_aliases={4: 0},
        compiler_params=_cparams(1, vmem), name="moe_experts",
    )(block_expert, n_used, group_src, group_dst, xs_sorted, wg, wu, wd)


def _combine_kernel(ys_ref, pos_ref, gate_ref, h_ref, g2_ref, b2_ref, o_ref):
    R = MIX_ROWS
    y = _unpack_rows(ys_ref[0]).astype(jnp.bfloat16)
    col = lax.broadcasted_iota(jnp.int32, (R, SORTED_ROWS), 1)
    pick = (jnp.where(col == pos_ref[:, 0:1], gate_ref[:, 0:1], 0.0)
            + jnp.where(col == pos_ref[:, 1:2], gate_ref[:, 1:2], 0.0)).astype(jnp.bfloat16)
    ffn = jnp.dot(pick, y, preferred_element_type=jnp.float32)
    o_ref[...] = _layer_norm(DEEPNORM_ALPHA * h_ref[...] + ffn, g2_ref[...], b2_ref[...])


def _combine_call(ys_sorted, pos, gate, h, g2, b2):
    N, D = h.shape
    R = MIX_ROWS
    nt = N // R
    W = ys_sorted.shape[-1]
    return pl.pallas_call(
        _combine_kernel, grid=(nt,),
        in_specs=[
            pl.BlockSpec((1, SORTED_ROWS, W), lambda i: (i, 0, 0)),
            pl.BlockSpec((R, 2), lambda i: (i, 0)),
            pl.BlockSpec((R, 2), lambda i: (i, 0)),
            pl.BlockSpec((R, D), lambda i: (i, 0)),
            pl.BlockSpec((1, D), lambda i: (0, 0)),
            pl.BlockSpec((1, D), lambda i: (0, 0)),
        ],
        out_specs=pl.BlockSpec((R, D), lambda i: (i, 0)),
        out_shape=jax.ShapeDtypeStruct((N, D), jnp.float32),
        compiler_params=_cparams(1, 2 * (SORTED_ROWS * W * 4 + 2 * R * D * 4) + SORTED_ROWS * D * 8
                                 + 3 * R * SORTED_ROWS * 4 + (6 << 20)),
        name="moe_combine",
    )(ys_sorted, pos, gate, h, g2, b2)


def _rope_tables(seq):
    inv_freq = ROPE_THETA ** (-jnp.arange(ROT_HALF, dtype=jnp.float32) / ROT_HALF)
    ang = jnp.arange(seq, dtype=jnp.int32).astype(jnp.float32)[:, None] * inv_freq[None, :]
    cos, sin = jnp.cos(ang), jnp.sin(ang)
    ones = jnp.ones((seq, HEAD_DIM - ROT_DIMS), jnp.float32)
    zeros = jnp.zeros((seq, HEAD_DIM - ROT_DIMS), jnp.float32)
    c_head = jnp.concatenate([cos, cos, ones], axis=1)
    s_head = jnp.concatenate([-sin, sin, zeros], axis=1)
    reps = KN_WIDTH // HEAD_DIM
    return cos.T, sin.T, jnp.tile(c_head, (1, reps)), jnp.tile(s_head, (1, reps))


def _swap_rot_cols(w):
    d, n = w.shape
    wh = w.reshape(d, n // HEAD_DIM, HEAD_DIM)
    sw = jnp.concatenate([wh[:, :, ROT_HALF:ROT_DIMS], wh[:, :, :ROT_HALF],
                          jnp.zeros((d, n // HEAD_DIM, HEAD_DIM - ROT_DIMS), w.dtype)], axis=2)
    return sw.reshape(d, n)


def _block(x, w_in, gate_bias, w_attn_up, w_conv_up, conv_w, w_out, ln_g, ln_b,
           rg_w, rg_b, re_w, re_b, w_gate_e, w_up_e, w_down_e, ln2_g, ln2_b):
    B, S, D = x.shape
    N = B * S
    top_k = min(TOPK_MAX, S // 4)
    bf = jnp.bfloat16
    o = np.cumsum([0, Q_WIDTH, KV_WIDTH, KV_WIDTH, IDXQ_WIDTH, IDX_DIM, IDX_HEADS,
                   CONV_DIM, CONV_DIM, CONV_DIM, N_BRANCHES * D_MODEL])
    w_q, w_k, w_v, w_qi, w_ki, w_wi = (w_in[:, o[i]:o[i + 1]] for i in range(6))
    w_conv = w_in[:, o[6]:o[9]]
    w_gates = w_in[:, o[9]:o[10]]

    wt = jnp.concatenate([w_q.T * (HEAD_DIM ** -0.5 * LOG2_E), w_qi.T, w_v.T, w_wi.T,
                          jnp.zeros((T_ROWS - T_WI0 - IDX_HEADS, D), w_in.dtype)], axis=0).astype(bf)
    pad = jnp.zeros((D, KN_WIDTH - KV_WIDTH - IDX_DIM), w_in.dtype)
    wn = jnp.concatenate([w_k, w_ki, pad, _swap_rot_cols(w_k), _swap_rot_cols(w_ki), pad], axis=1).astype(bf)
    cos_t, sin_t, cos_n, sgn_n = _rope_tables(S)

    qt, qit, vt, wit, kn = _proj_call(x, wt, wn, cos_t, sin_t, cos_n, sgn_n)
    attn = _attn_call(qt, qit, wit, kn, vt, top_k)

    w_r = jnp.concatenate([rg_w, jnp.transpose(re_w, (1, 0, 2)).reshape(D, N_EXPERTS),
                           jnp.zeros((D, ROUTER_LANES - E0 - N_EXPERTS), rg_w.dtype)], axis=1)
    b_r = jnp.concatenate([rg_b, re_b.reshape(-1),
                           jnp.zeros((ROUTER_LANES - E0 - N_EXPERTS,), rg_b.dtype)])[None, :]
    w_rh = w_r.astype(bf)
    w_rl = (w_r - w_rh.astype(jnp.float32)).astype(bf)

    h, xs_sorted, pos, gate, cnt = _mix_call(
        x.reshape(N, D), attn.reshape(N, Q_WIDTH), w_conv.astype(bf), w_gates.astype(bf),
        gate_bias[None, :], conv_w, w_attn_up.astype(bf), w_conv_up.astype(bf), w_out.astype(bf),
        ln_g[None, :], ln_b[None, :], w_rh, w_rl, b_r, S)

    RB, RG = EXPERT_ROWS, ROW_GROUP
    nt = N // MIX_ROWS
    counts = cnt[:, 0, E0:E0 + N_EXPERTS].astype(jnp.int32)
    seg_len = ((counts + RG - 1) // RG) * RG
    seg_local = jnp.cumsum(seg_len, axis=1) - seg_len
    region = jnp.sum(seg_len, axis=0)
    padded = ((region + RB - 1) // RB) * RB
    pad_end = jnp.cumsum(padded)
    pad_start = pad_end - padded
    seg_start = pad_start[None, :] + jnp.cumsum(seg_len, axis=0) - seg_len
    cap = -(-(N * 2 + nt * N_EXPERTS * (RG - 1) + N_EXPERTS * (RB - 1)) // RB) * RB
    nb = cap // RB
    block_start = jnp.arange(nb, dtype=jnp.int32) * RB
    block_expert = jnp.minimum(jnp.sum((block_start[:, None] >= pad_end[None, :]).astype(jnp.int32), axis=1),
                               N_EXPERTS - 1)
    n_used = (pad_end[-1:] // RB).astype(jnp.int32)
    flat_start = seg_start.T.reshape(-1)
    flat_end = flat_start + seg_len.T.reshape(-1)
    step_base = (jnp.arange(nt, dtype=jnp.int32) * SORTED_ROWS)[:, None]
    flat_shift = (step_base + seg_local - seg_start).T.reshape(-1)
    g_idx = jnp.arange(cap // RG, dtype=jnp.int32)
    g_row = g_idx * RG
    ended = flat_end[None, :-1] <= g_row[:, None]

    def lookup(v):
        return v[0] + jnp.sum(jnp.where(ended, (v[1:] - v[:-1])[None, :], 0), axis=1)

    live = (g_row >= lookup(flat_start)) & (g_row < lookup(flat_end))
    home = g_row + lookup(flat_shift)
    spare_per_step = (SORTED_ROWS - SORTED_USED) // RG
    spare = ((g_idx // GROUPS_PER_BLOCK) % 2) * GROUPS_PER_BLOCK + g_idx % GROUPS_PER_BLOCK
    assert 2 * GROUPS_PER_BLOCK <= (nt - 1) * spare_per_step, "not enough spare groups for block padding"
    spare_row = (1 + spare // spare_per_step) * SORTED_ROWS + SORTED_USED + (spare % spare_per_step) * RG
    group_src = jnp.where(live, home, SORTED_ROWS - RG).astype(jnp.int32)
    group_dst = jnp.where(live, home, spare_row).astype(jnp.int32)

    ys_sorted = _expert_call(block_expert, n_used, group_src, group_dst,
                             xs_sorted.reshape(nt * SORTED_ROWS, HALF), w_gate_e, w_up_e, w_down_e, nb)
    out = _combine_call(ys_sorted.reshape(nt, SORTED_ROWS, HALF), pos, gate, h, ln2_g[None, :], ln2_b[None, :])
    return out.reshape(B, S, D)


def kernel(x, w_in, gate_bias, w_attn_up, w_conv_up, conv_w, w_out, ln1_g, ln1_b, router_group_w,
           router_group_b, router_expert_w, router_expert_b, w_gate_e, w_up_e, w_down_e, ln2_g, ln2_b):
    h = x
    for l in range(DEPTH):
        h = _block(h, w_in[l], gate_bias[l], w_attn_up[l], w_conv_up[l], conv_w[l], w_out[l],
                   ln1_g[l], ln1_b[l], router_group_w[l], router_group_b[l], router_expert_w[l],
                   router_expert_b[l], w_gate_e[l], w_up_e[l], w_down_e[l], ln2_g[l], ln2_b[l])
    return h
```

```python
import functools

import jax
import jax.numpy as jnp
import numpy as np
from jax import lax
from jax.experimental import pallas as pl
from jax.experimental.pallas import tpu as pltpu

D_MODEL = 1024
N_HEADS = 8
N_KV_HEADS = 2
HEAD_DIM = 64
Q_WIDTH = N_HEADS * HEAD_DIM
KV_WIDTH = N_KV_HEADS * HEAD_DIM
ROPE_THETA = 500000.0
ROT_DIMS = HEAD_DIM // 4
ROT_HALF = ROT_DIMS // 2
IDX_HEADS = 8
IDX_DIM = 64
IDXQ_WIDTH = IDX_HEADS * IDX_DIM
TOPK_MAX = 256
CONV_DIM = 512
CONV_WIDTH = 3
N_BRANCHES = 2
N_GROUPS = 4
EXPERTS_PER_GROUP = 8
N_EXPERTS = N_GROUPS * EXPERTS_PER_GROUP
D_EXPERT = 512
LN_EPS = 1e-5
DEPTH = 1
DEEPNORM_ALPHA = (2 * DEPTH) ** 0.25

V7X_LANES = 128
V7X_SUBLANES = 8
V7X_VMEM_LIMIT_BYTES = 56 * 1024 * 1024

PROJ_ROWS = 512
ATTN_CHUNK = 256
ATTN_KEY_TILE = 128
DEN_ROWS = 16
ACC_ROWS = HEAD_DIM + DEN_ROWS
LOG2_E = 1.4426950408889634
MIX_ROWS = 512
EXPERT_ROWS = 256
ROW_GROUP = V7X_SUBLANES
SORTED_USED = 2 * MIX_ROWS + N_EXPERTS * (ROW_GROUP - 1)
SORTED_ROWS = -(-(SORTED_USED + ROW_GROUP) // 256) * 256
HALF = D_MODEL // 2

T_Q0, T_QI0, T_V0, T_WI0 = 0, Q_WIDTH, Q_WIDTH + IDXQ_WIDTH, Q_WIDTH + IDXQ_WIDTH + KV_WIDTH
T_ROWS = T_WI0 + 16
KN_WIDTH = 256

INT_MIN = -2147483648
HALF_RANGE = 32768
ROUTER_LANES = 128
E0 = N_GROUPS


def _cparams(n_axes, vmem_bytes):
    return pltpu.CompilerParams(
        dimension_semantics=("arbitrary",) * n_axes,
        vmem_limit_bytes=int(min(vmem_bytes, V7X_VMEM_LIMIT_BYTES)),
    )


def _proj_kernel(x_ref, wt_ref, wn_ref, cost_ref, sint_ref, cosn_ref, sgnn_ref,
                 qt_ref, qit_ref, vt_ref, wit_ref, kn_ref):
    xb = x_ref[0].astype(jnp.bfloat16)
    pt = lax.dot_general(wt_ref[...], xb, (((1,), (1,)), ((), ())),
                         preferred_element_type=jnp.float32)
    cos = cost_ref[...]
    sin = sint_ref[...]

    def rope_t(dst_ref, base):
        for h in range(N_HEADS):
            r0 = base + h * HEAD_DIM
            x1 = pt[r0:r0 + ROT_HALF]
            x2 = pt[r0 + ROT_HALF:r0 + ROT_DIMS]
            head = jnp.concatenate([x1 * cos - x2 * sin, x2 * cos + x1 * sin,
                                    pt[r0 + ROT_DIMS:r0 + HEAD_DIM]], axis=0).astype(dst_ref.dtype)
            for jj, cols in enumerate(chunks):
                dst_ref[0, jj, h * HEAD_DIM:(h + 1) * HEAD_DIM, :] = head[:, cols]

    chunks = [slice(jj * ATTN_CHUNK, (jj + 1) * ATTN_CHUNK) for jj in range(PROJ_ROWS // ATTN_CHUNK)]
    rope_t(qt_ref, T_Q0)
    rope_t(qit_ref, T_QI0)
    for jj, cols in enumerate(chunks):
        vt_ref[0, jj] = pt[T_V0:T_V0 + KV_WIDTH, cols].astype(vt_ref.dtype)
        wit_ref[0, jj] = pt[T_WI0:T_WI0 + IDX_HEADS, cols]
    pn = jnp.dot(xb, wn_ref[...], preferred_element_type=jnp.float32)
    kn = pn[:, :KN_WIDTH] * cosn_ref[...] + pn[:, KN_WIDTH:] * sgnn_ref[...]
    kn_ref[0] = kn.astype(kn_ref.dtype)


def _proj_call(x, wt, wn, cos_t, sin_t, cos_n, sgn_n):
    B, S, D = x.shape
    R = PROJ_ROWS
    nt = S // R
    grid = (B, nt)
    out_shape = (
        jax.ShapeDtypeStruct((B, S // ATTN_CHUNK, Q_WIDTH, ATTN_CHUNK), jnp.bfloat16),
        jax.ShapeDtypeStruct((B, S // ATTN_CHUNK, IDXQ_WIDTH, ATTN_CHUNK), jnp.bfloat16),
        jax.ShapeDtypeStruct((B, S // ATTN_CHUNK, KV_WIDTH, ATTN_CHUNK), jnp.bfloat16),
        jax.ShapeDtypeStruct((B, S // ATTN_CHUNK, IDX_HEADS, ATTN_CHUNK), jnp.float32),
        jax.ShapeDtypeStruct((B, S, KN_WIDTH), jnp.bfloat16),
    )
    in_specs = [
        pl.BlockSpec((1, R, D), lambda b, j: (b, j, 0)),
        pl.BlockSpec((T_ROWS, D), lambda b, j: (0, 0)),
        pl.BlockSpec((D, 2 * KN_WIDTH), lambda b, j: (0, 0)),
        pl.BlockSpec((ROT_HALF, R), lambda b, j: (0, j)),
        pl.BlockSpec((ROT_HALF, R), lambda b, j: (0, j)),
        pl.BlockSpec((R, KN_WIDTH), lambda b, j: (j, 0)),
        pl.BlockSpec((R, KN_WIDTH), lambda b, j: (j, 0)),
    ]
    out_specs = (
        pl.BlockSpec((1, R // ATTN_CHUNK, Q_WIDTH, ATTN_CHUNK), lambda b, j: (b, j, 0, 0)),
        pl.BlockSpec((1, R // ATTN_CHUNK, IDXQ_WIDTH, ATTN_CHUNK), lambda b, j: (b, j, 0, 0)),
        pl.BlockSpec((1, R // ATTN_CHUNK, KV_WIDTH, ATTN_CHUNK), lambda b, j: (b, j, 0, 0)),
        pl.BlockSpec((1, R // ATTN_CHUNK, IDX_HEADS, ATTN_CHUNK), lambda b, j: (b, j, 0, 0)),
        pl.BlockSpec((1, R, KN_WIDTH), lambda b, j: (b, j, 0)),
    )
    vmem = 2 * (R * D * 4 + T_ROWS * D * 2 + D * 2 * KN_WIDTH * 2) + 6 * T_ROWS * R * 4 + (8 << 20)
    return pl.pallas_call(
        _proj_kernel, grid=grid, in_specs=in_specs, out_specs=out_specs, out_shape=out_shape,
        compiler_params=_cparams(2, vmem), name="dsa_proj",
    )(x, wt, wn, cos_t, sin_t, cos_n, sgn_n)


def _float_to_key(s):
    b = lax.bitcast_convert_type(s, jnp.int32)
    k = b ^ (lax.shift_right_arithmetic(b, 31) & jnp.int32(0x7FFFFFFF))
    return jnp.where(b == jnp.int32(INT_MIN), jnp.int32(0), k)


def _attn_kernel(qt_ref, qit_ref, wit_ref, kn_ref, vt_ref, o_ref,
                 key_ref, hi_ref, lo_ref, lga_ref, lgb_ref, acc_ref, m_ref, *, seq, top_k):
    C = ATTN_CHUNK
    c = pl.program_id(1)
    nkb = c + 1
    zeros_half = jnp.zeros((HEAD_DIM, C), jnp.bfloat16)
    row_i = lax.broadcasted_iota(jnp.int32, (C, C), 0)
    lane_i = lax.broadcasted_iota(jnp.int32, (C, C), 1)
    causal_in_block = row_i <= lane_i

    def score_block(kb):
        r0 = pl.multiple_of(kb * C, C)
        kix = kn_ref[0, pl.ds(r0, C), KV_WIDTH:KN_WIDTH]
        acc = None
        for h in range(IDX_HEADS):
            rhs = jnp.concatenate([qit_ref[0, 0, h * IDX_DIM:(h + 1) * IDX_DIM, :], zeros_half], axis=0)
            s = jnp.dot(kix, rhs, preferred_element_type=jnp.float32)
            t = jnp.maximum(s, 0.0) * wit_ref[0, 0, h:h + 1, :]
            acc = t if acc is None else acc + t
        score = acc * (IDX_DIM ** -0.5 * IDX_HEADS ** -0.5)
        keys = jnp.where(causal_in_block | (kb != c), _float_to_key(score), jnp.int32(INT_MIN))
        key_ref[kb] = keys
        hi_ref[kb] = lax.shift_right_arithmetic(keys, 16).astype(jnp.int16)
        lo_ref[kb] = ((keys & jnp.int32(0xFFFF)) - jnp.int32(HALF_RANGE)).astype(jnp.int16)

    def score_body(pair, carry):
        score_block(2 * pair)
        score_block(2 * pair + 1)
        return carry

    lax.fori_loop(0, nkb // 2, score_body, 0)

    @pl.when(nkb % 2 == 1)
    def _():
        score_block(nkb - 1)

    def count(ref, pred):
        packed = ref.dtype == jnp.int16

        def body(kb, part):
            hit = pred(ref[kb], kb)
            if packed:
                words = pltpu.bitcast(jnp.where(hit, jnp.int16(1), jnp.int16(0)), jnp.int32)
            else:
                words = hit.astype(jnp.int32)
            return part + jnp.sum(words.reshape(-1, V7X_SUBLANES, C), axis=0)

        part = lax.fori_loop(0, nkb, body, jnp.zeros((V7X_SUBLANES, C), jnp.int32))
        if packed:
            part = (part & jnp.int32(0xFFFF)) + lax.shift_right_logical(part, 16)
        return jnp.sum(part, axis=0, keepdims=True)

    def kth_largest_16(ref):
        def body(i, ans_u):
            cand_u = ans_u | lax.shift_left(jnp.int32(1), jnp.int32(15) - i)
            cand = (cand_u - jnp.int32(HALF_RANGE)).astype(jnp.int16)
            return jnp.where(count(ref, lambda k, kb: k >= cand) >= top_k, cand_u, ans_u)

        return lax.fori_loop(0, 16, body, jnp.zeros((1, C), jnp.int32)) - jnp.int32(HALF_RANGE)

    t_hi = kth_largest_16(hi_ref)
    t_hi16 = t_hi.astype(jnp.int16)

    def refine_body(kb, carry):
        hi = hi_ref[kb]
        lo_ref[kb] = jnp.where(hi > t_hi16, jnp.int16(HALF_RANGE - 1),
                               jnp.where(hi == t_hi16, lo_ref[kb], jnp.int16(-HALF_RANGE)))
        return carry

    lax.fori_loop(0, nkb, refine_body, 0)
    t_lo = kth_largest_16(lo_ref)
    thr_raw = t_hi * jnp.int32(2 * HALF_RANGE) + (t_lo + jnp.int32(HALF_RANGE))
    thr = jnp.maximum(thr_raw, jnp.int32(INT_MIN + 1))

    t_lo16 = t_lo.astype(jnp.int16)
    n_ge = count(lo_ref, lambda k, kb: k >= t_lo16)
    has_ties = jnp.max(jnp.where(thr_raw != jnp.int32(INT_MIN), n_ge, 0)) > top_k

    @pl.when(has_ties)
    def _():
        need = (top_k - count(key_ref, lambda k, kb: k > thr)).astype(jnp.float32)
        earlier_rows = (row_i > lane_i).astype(jnp.bfloat16)

        def drop_body(kb, seen):
            kk = key_ref[kb]
            tied = kk == thr
            tied_b = jnp.where(tied, 1.0, 0.0).astype(jnp.bfloat16)
            rank = jnp.dot(earlier_rows, tied_b, preferred_element_type=jnp.float32) + seen
            key_ref[kb] = jnp.where(tied & (rank >= need), kk - 1, kk)
            return seen + jnp.sum(tied_b.astype(jnp.float32), axis=0, keepdims=True)

        lax.fori_loop(0, nkb, drop_body, jnp.zeros((1, C), jnp.float32))

    neg_inf = jnp.float32(-jnp.inf)
    m_ref[...] = jnp.full(m_ref.shape, neg_inf, jnp.float32)
    acc_ref[...] = jnp.zeros(acc_ref.shape, jnp.float32)
    T = ATTN_KEY_TILE
    ones_rows = jnp.ones((DEN_ROWS, C), jnp.bfloat16)

    kv_group = N_HEADS // N_KV_HEADS

    def bias_body(kb, carry):
        key_ref[kb] = lax.bitcast_convert_type(jnp.where(key_ref[kb] >= thr, 0.0, neg_inf), jnp.int32)
        return carry

    lax.fori_loop(0, nkb, bias_body, 0)

    def store_logits(kb, dst_ref, h):
        kb = jnp.minimum(kb, nkb - 1)
        r0 = pl.multiple_of(kb * C, C)
        bias = lax.bitcast_convert_type(key_ref[kb], jnp.float32)
        k2 = kn_ref[0, pl.ds(r0, C), 0:KV_WIDTH]
        qh = qt_ref[0, 0, h * HEAD_DIM:(h + 1) * HEAD_DIM, :]
        rhs = jnp.concatenate([qh, zeros_half] if h < kv_group else [zeros_half, qh], axis=0)
        dst_ref[h] = jnp.dot(k2, rhs, preferred_element_type=jnp.float32) + bias

    def softmax_block(kb, src_ref, dst_ref):
        for h in range(N_HEADS):
            if dst_ref is not None:
                store_logits(kb + 1, dst_ref, h)
            g = h // kv_group
            m_old = m_ref[h:h + 1, :]
            m_new = jnp.maximum(m_old, jnp.max(src_ref[h], axis=0, keepdims=True))
            m_safe = jnp.where(m_new == neg_inf, 0.0, m_new)
            p = jnp.concatenate(
                [jnp.exp2(src_ref[h, s * T:(s + 1) * T, :] - m_safe).astype(jnp.bfloat16) for s in range(C // T)],
                axis=0)
            alpha = jnp.exp2(m_old - m_safe)
            vt = vt_ref[0, kb, g * HEAD_DIM:(g + 1) * HEAD_DIM, :]
            pv = jnp.dot(jnp.concatenate([vt, ones_rows], axis=0), p,
                         preferred_element_type=jnp.float32)
            hs = slice(h * ACC_ROWS, (h + 1) * ACC_ROWS)
            acc_ref[hs, :] = alpha * acc_ref[hs, :] + pv
            m_ref[h:h + 1, :] = m_new

    for h in range(N_HEADS):
        store_logits(0, lga_ref, h)

    def attn_body(pair, carry):
        kb = 2 * pair
        softmax_block(kb, lga_ref, lgb_ref)
        softmax_block(kb + 1, lgb_ref, lga_ref)
        return carry

    lax.fori_loop(0, nkb // 2, attn_body, 0)

    @pl.when(nkb % 2 == 1)
    def _():
        softmax_block(nkb - 1, lga_ref, None)
    outs = []
    for h in range(N_HEADS):
        num = acc_ref[h * ACC_ROWS:h * ACC_ROWS + HEAD_DIM, :]
        den = acc_ref[h * ACC_ROWS + HEAD_DIM:h * ACC_ROWS + HEAD_DIM + 1, :]
        outs.append(num / den)
    o_ref[0] = jnp.transpose(jnp.concatenate(outs, axis=0)).astype(o_ref.dtype)


def _attn_call(qt, qit, wit, kn, vt, top_k):
    B, S = kn.shape[:2]
    C = ATTN_CHUNK
    nc = S // C
    kern = functools.partial(_attn_kernel, seq=S, top_k=top_k)
    in_specs = [
        pl.BlockSpec((1, 1, Q_WIDTH, C), lambda b, c: (b, c, 0, 0)),
        pl.BlockSpec((1, 1, IDXQ_WIDTH, C), lambda b, c: (b, c, 0, 0)),
        pl.BlockSpec((1, 1, IDX_HEADS, C), lambda b, c: (b, c, 0, 0)),
        pl.BlockSpec((1, S, KN_WIDTH), lambda b, c: (b, 0, 0)),
        pl.BlockSpec((1, nc, KV_WIDTH, C), lambda b, c: (b, 0, 0, 0)),
    ]
    out_specs = pl.BlockSpec((1, C, Q_WIDTH), lambda b, c: (b, c, 0))
    scratch = [
        pltpu.VMEM((nc, C, C), jnp.int32),
        pltpu.VMEM((nc, C, C), jnp.int16),
        pltpu.VMEM((nc, C, C), jnp.int16),
        pltpu.VMEM((N_HEADS, C, C), jnp.float32),
        pltpu.VMEM((N_HEADS, C, C), jnp.float32),
        pltpu.VMEM((N_HEADS * ACC_ROWS, C), jnp.float32),
        pltpu.VMEM((N_HEADS, C), jnp.float32),
    ]
    vmem = (2 * S * C * 4 + Q_WIDTH * C * 4 + 2 * 2 * (2 * Q_WIDTH * C + S * KN_WIDTH + S * KV_WIDTH + C * Q_WIDTH)
            + 24 * C * C * 4 + (8 << 20))
    return pl.pallas_call(
        kern, grid=(B, nc), in_specs=in_specs, out_specs=out_specs,
        out_shape=jax.ShapeDtypeStruct((B, S, Q_WIDTH), jnp.bfloat16),
        scratch_shapes=scratch, compiler_params=_cparams(2, vmem), name="dsa_attn",
    )(qt, qit, wit, kn, vt)


def _layer_norm(v, g, b):
    mu = jnp.mean(v, axis=-1, keepdims=True)
    d = v - mu
    var = jnp.mean(d * d, axis=-1, keepdims=True)
    return d * lax.rsqrt(var + LN_EPS) * g + b


def _pack_rows(h):
    hi = lax.bitcast_convert_type(h[:, :HALF].astype(jnp.bfloat16).astype(jnp.float32), jnp.int32)
    lo = lax.bitcast_convert_type(h[:, HALF:].astype(jnp.bfloat16).astype(jnp.float32), jnp.int32)
    return (hi & jnp.int32(-65536)) | lax.shift_right_logical(lo, 16)


def _unpack_rows(w):
    hi = lax.bitcast_convert_type(w & jnp.int32(-65536), jnp.float32)
    lo = lax.bitcast_convert_type(lax.shift_left(w, 16), jnp.float32)
    return jnp.concatenate([hi, lo], axis=1)


def _mix_kernel(x_ref, attn_ref, wc_ref, wg_ref, gb_ref, cw_ref, wau_ref, wcu_ref, wo_ref,
                g1_ref, b1_ref, wrh_ref, wrl_ref, rb_ref,
                h_ref, xs_ref, pos_ref, gate_ref, cnt_ref,
                ubuf_ref, *, steps_per_seq):
    R = MIX_ROWS
    i = pl.program_id(0)

    @pl.when(i % steps_per_seq == 0)
    def _():
        ubuf_ref[0:V7X_SUBLANES, :] = jnp.zeros((V7X_SUBLANES, CONV_DIM), jnp.float32)

    x = x_ref[...]
    xb = x.astype(jnp.bfloat16)
    cv = jnp.dot(xb, wc_ref[...], preferred_element_type=jnp.float32)
    u = cv[:, 2 * CONV_DIM:] * cv[:, :CONV_DIM]
    ubuf_ref[V7X_SUBLANES:, :] = u
    u1 = ubuf_ref[V7X_SUBLANES - 1:V7X_SUBLANES - 1 + R, :]
    u2 = ubuf_ref[V7X_SUBLANES - 2:V7X_SUBLANES - 2 + R, :]
    y = cw_ref[0:1, :] * u2 + cw_ref[1:2, :] * u1 + cw_ref[2:3, :] * u
    conv = (cv[:, CONV_DIM:2 * CONV_DIM] * y).astype(jnp.bfloat16)
    ubuf_ref[0:V7X_SUBLANES, :] = u[R - V7X_SUBLANES:, :]
    z = jnp.dot(xb, wg_ref[...], preferred_element_type=jnp.float32) + gb_ref[...]
    gates = 1.0 / (1.0 + jnp.exp(-z))
    au = jnp.dot(attn_ref[...], wau_ref[...], preferred_element_type=jnp.float32)
    cu = jnp.dot(conv, wcu_ref[...], preferred_element_type=jnp.float32)
    merged = gates[:, :D_MODEL] * au + gates[:, D_MODEL:] * cu
    mix = jnp.dot(merged.astype(jnp.bfloat16), wo_ref[...], preferred_element_type=jnp.float32)
    h = _layer_norm(DEEPNORM_ALPHA * x + mix, g1_ref[...], b1_ref[...])
    h_ref[...] = h

    h_hi = h.astype(jnp.bfloat16)
    h_lo = (h - h_hi.astype(jnp.float32)).astype(jnp.bfloat16)
    lg = (jnp.dot(h_hi, wrh_ref[...], preferred_element_type=jnp.float32)
          + jnp.dot(h_lo, wrh_ref[...], preferred_element_type=jnp.float32)
          + jnp.dot(h_hi, wrl_ref[...], preferred_element_type=jnp.float32)) + rb_ref[...]
    lane = lax.broadcasted_iota(jnp.int32, (R, ROUTER_LANES), 1).astype(jnp.float32)
    neg = jnp.float32(-jnp.inf)
    no_lane = jnp.float32(ROUTER_LANES)
    gmask = lane < N_GROUPS
    gl = jnp.where(gmask, lg, neg)
    gmax = jnp.max(gl, axis=1, keepdims=True)
    grp = jnp.min(jnp.where(gl == gmax, lane, no_lane), axis=1, keepdims=True)
    gsum = jnp.sum(jnp.where(gmask, jnp.exp(gl - gmax), 0.0), axis=1, keepdims=True)
    p_grp = 1.0 / gsum
    lo_lane = E0 + grp * EXPERTS_PER_GROUP
    emask = (lane >= lo_lane) & (lane < lo_lane + EXPERTS_PER_GROUP)
    el = jnp.where(emask, lg, neg)
    v1 = jnp.max(el, axis=1, keepdims=True)
    i1 = jnp.min(jnp.where(el == v1, lane, no_lane), axis=1, keepdims=True)
    el2 = jnp.where(lane == i1, neg, el)
    v2 = jnp.max(el2, axis=1, keepdims=True)
    i2 = jnp.min(jnp.where(el2 == v2, lane, no_lane), axis=1, keepdims=True)
    a = jnp.exp(v2 - v1)
    inv = 1.0 / (1.0 + a)
    gate_ref[:, 0:1] = p_grp * inv
    gate_ref[:, 1:2] = p_grp * (a * inv)
    oh1 = lane == i1
    oh2 = lane == i2
    oh = (oh1 | oh2).astype(jnp.bfloat16)
    r_i = lax.broadcasted_iota(jnp.int32, (R, R), 0)
    c_i = lax.broadcasted_iota(jnp.int32, (R, R), 1)
    tri = (r_i > c_i).astype(jnp.bfloat16)
    before = jnp.dot(tri, oh, preferred_element_type=jnp.float32)
    cnt = jnp.sum(oh.astype(jnp.float32), axis=0, keepdims=True)
    cnt_ref[0] = cnt

    seg = jnp.floor((cnt + (ROW_GROUP - 1)) * (1.0 / ROW_GROUP)) * ROW_GROUP
    l_r = lax.broadcasted_iota(jnp.int32, (ROUTER_LANES, ROUTER_LANES), 0)
    l_c = lax.broadcasted_iota(jnp.int32, (ROUTER_LANES, ROUTER_LANES), 1)
    lanes_before = (l_r < l_c).astype(jnp.bfloat16)
    seg8 = jnp.broadcast_to(seg, (V7X_SUBLANES, ROUTER_LANES)).astype(jnp.bfloat16)
    seg_off = jnp.dot(seg8, lanes_before, preferred_element_type=jnp.float32)[0:1, :]
    where_to = before + seg_off
    pos1 = jnp.sum(jnp.where(oh1, where_to, 0.0), axis=1, keepdims=True)
    pos2 = jnp.sum(jnp.where(oh2, where_to, 0.0), axis=1, keepdims=True)
    pos_ref[:, 0:1] = pos1.astype(jnp.int32)
    pos_ref[:, 1:2] = pos2.astype(jnp.int32)
    diag = r_i == c_i
    pos1_row = jnp.sum(jnp.where(diag, pos1, 0.0), axis=0, keepdims=True)
    pos2_row = jnp.sum(jnp.where(diag, pos2, 0.0), axis=0, keepdims=True)
    sorted_row = lax.broadcasted_iota(jnp.int32, (SORTED_ROWS, R), 0).astype(jnp.float32)
    place = ((sorted_row == pos1_row) | (sorted_row == pos2_row)).astype(jnp.bfloat16)
    xs = jnp.dot(place, h_hi, preferred_element_type=jnp.float32)
    xs_ref[0] = _pack_rows(xs)


def _mix_call(x2, attn2, wc, wg, gb, cw, wau, wcu, wo, g1, b1, wrh, wrl, rb, seq):
    N, D = x2.shape
    R = MIX_ROWS
    nt = N // R
    kern = functools.partial(_mix_kernel, steps_per_seq=seq // R)

    def full(a):
        return pl.BlockSpec(a.shape, lambda i: (0,) * a.ndim)

    in_specs = [
        pl.BlockSpec((R, D), lambda i: (i, 0)),
        pl.BlockSpec((R, Q_WIDTH), lambda i: (i, 0)),
        full(wc), full(wg), full(gb), full(cw), full(wau), full(wcu), full(wo),
        full(g1), full(b1), full(wrh), full(wrl), full(rb),
    ]
    out_shape = (
        jax.ShapeDtypeStruct((N, D), jnp.float32),
        jax.ShapeDtypeStruct((nt, SORTED_ROWS, HALF), jnp.int32),
        jax.ShapeDtypeStruct((N, 2), jnp.int32),
        jax.ShapeDtypeStruct((N, 2), jnp.float32),
        jax.ShapeDtypeStruct((nt, 1, ROUTER_LANES), jnp.float32),
    )
    out_specs = (
        pl.BlockSpec((R, D), lambda i: (i, 0)),
        pl.BlockSpec((1, SORTED_ROWS, HALF), lambda i: (i, 0, 0)),
        pl.BlockSpec((R, 2), lambda i: (i, 0)),
        pl.BlockSpec((R, 2), lambda i: (i, 0)),
        pl.BlockSpec((1, 1, ROUTER_LANES), lambda i: (i, 0, 0)),
    )
    scratch = [pltpu.VMEM((R + V7X_SUBLANES, CONV_DIM), jnp.float32)]
    w_bytes = 2 * (wc.size + wg.size + wau.size + wcu.size + wo.size + wrh.size + wrl.size)
    vmem = (2 * w_bytes + 2 * (R * D * 4 * 2 + R * Q_WIDTH * 2 + SORTED_ROWS * HALF * 4) + 10 * R * 2048 * 4
            + SORTED_ROWS * D * 6 + (6 << 20))
    return pl.pallas_call(
        kern, grid=(nt,), in_specs=in_specs, out_specs=out_specs, out_shape=out_shape,
        scratch_shapes=scratch, compiler_params=_cparams(1, vmem), name="mix_ln_router",
    )(x2, attn2, wc, wg, gb, cw, wau, wcu, wo, g1, b1, wrh, wrl, rb)


GROUPS_PER_BLOCK = EXPERT_ROWS // ROW_GROUP


def _expert_kernel(be_ref, nb_ref, gsrc_ref, gdst_ref, xs_ref, wg_ref, wu_ref, wd_ref, ys_ref,
                   wgb_ref, wub_ref, wdb_ref, xbuf_ref, ybuf_ref, gsems, ssems):
    del xs_ref
    j = pl.program_id(0)
    n_used = nb_ref[0]
    slot = j % 2

    def gather(block, to_slot):
        for g in range(GROUPS_PER_BLOCK):
            row = pl.multiple_of(gsrc_ref[block * GROUPS_PER_BLOCK + g], ROW_GROUP)
            pltpu.make_async_copy(ys_ref.at[pl.ds(row, ROW_GROUP), :],
                                  xbuf_ref.at[to_slot, pl.ds(g * ROW_GROUP, ROW_GROUP), :],
                                  gsems.at[to_slot]).start(priority=g % 2)

    def scatter(block, from_slot):
        for g in range(GROUPS_PER_BLOCK):
            row = pl.multiple_of(gdst_ref[block * GROUPS_PER_BLOCK + g], ROW_GROUP)
            pltpu.make_async_copy(ybuf_ref.at[from_slot, pl.ds(g * ROW_GROUP, ROW_GROUP), :],
                                  ys_ref.at[pl.ds(row, ROW_GROUP), :],
                                  ssems.at[from_slot]).start(priority=g % 2)

    def wait_gather(s):
        pltpu.make_async_copy(ys_ref.at[pl.ds(0, EXPERT_ROWS), :], xbuf_ref.at[s], gsems.at[s]).wait()

    def wait_scatter(s):
        pltpu.make_async_copy(ybuf_ref.at[s], ys_ref.at[pl.ds(0, EXPERT_ROWS), :], ssems.at[s]).wait()

    @pl.when((j == 0) & (n_used > 0))
    def _():
        gather(0, 0)

    @pl.when((j == 0) | (be_ref[j] != be_ref[jnp.maximum(j - 1, 0)]))
    def _():
        wgb_ref[...] = wg_ref[0].astype(jnp.bfloat16)
        wub_ref[...] = wu_ref[0].astype(jnp.bfloat16)
        wdb_ref[...] = wd_ref[0].astype(jnp.bfloat16)

    @pl.when(j + 1 < n_used)
    def _():
        gather(j + 1, 1 - slot)

    @pl.when((j >= 2) & (j < n_used))
    def _():
        wait_scatter(slot)

    @pl.when(j < n_used)
    def _():
        wait_gather(slot)
        xb = _unpack_rows(xbuf_ref[slot]).astype(jnp.bfloat16)
        hg = jnp.dot(xb, wgb_ref[...], preferred_element_type=jnp.float32)
        hu = jnp.dot(xb, wub_ref[...], preferred_element_type=jnp.float32)
        hid = (hg / (1.0 + jnp.exp(-hg))) * hu
        y = jnp.dot(hid.astype(jnp.bfloat16), wdb_ref[...], preferred_element_type=jnp.float32)
        ybuf_ref[slot] = _pack_rows(y)
        scatter(j, slot)

    @pl.when(j == n_used - 1)
    def _():
        wait_scatter(slot)

    @pl.when((j == n_used - 1) & (j >= 1))
    def _():
        wait_scatter(1 - slot)


def _expert_call(block_expert, n_used, group_src, group_dst, xs_sorted, wg, wu, wd, n_blocks):
    W = xs_sorted.shape[-1]
    RB = EXPERT_ROWS
    grid_spec = pltpu.PrefetchScalarGridSpec(
        num_scalar_prefetch=4, grid=(n_blocks,),
        in_specs=[
            pl.BlockSpec(memory_space=pl.ANY),
            pl.BlockSpec((1, D_MODEL, D_EXPERT), lambda j, be, nu, gs, gd: (be[j], 0, 0)),
            pl.BlockSpec((1, D_MODEL, D_EXPERT), lambda j, be, nu, gs, gd: (be[j], 0, 0)),
            pl.BlockSpec((1, D_EXPERT, D_MODEL), lambda j, be, nu, gs, gd: (be[j], 0, 0)),
        ],
        out_specs=pl.BlockSpec(memory_space=pl.ANY),
        scratch_shapes=[pltpu.VMEM((D_MODEL, D_EXPERT), jnp.bfloat16),
                        pltpu.VMEM((D_MODEL, D_EXPERT), jnp.bfloat16),
                        pltpu.VMEM((D_EXPERT, D_MODEL), jnp.bfloat16),
                        pltpu.VMEM((2, RB, W), jnp.int32),
                        pltpu.VMEM((2, RB, W), jnp.int32),
                        pltpu.SemaphoreType.DMA((2,)),
                        pltpu.SemaphoreType.DMA((2,))],
    )
    vmem = (2 * 4 + 2) * 3 * D_MODEL * D_EXPERT + 4 * RB * W * 4 + 8 * RB * D_MODEL * 4 + (6 << 20)
    return pl.pallas_call(
        _expert_kernel, grid_spec=grid_spec,
        out_shape=jax.ShapeDtypeStruct(xs_sorted.shape, xs_sorted.dtype),
        input_output_aliases={4: 0},
        compiler_params=_cparams(1, vmem), name="moe_experts",
    )(block_expert, n_used, group_src, group_dst, xs_sorted, wg, wu, wd)


def _combine_kernel(ys_ref, pos_ref, gate_ref, h_ref, g2_ref, b2_ref, o_ref):
    R = MIX_ROWS
    y = _unpack_rows(ys_ref[0]).astype(jnp.bfloat16)
    col = lax.broadcasted_iota(jnp.int32, (R, SORTED_ROWS), 1)
    pick = (jnp.where(col == pos_ref[:, 0:1], gate_ref[:, 0:1], 0.0)
            + jnp.where(col == pos_ref[:, 1:2], gate_ref[:, 1:2], 0.0)).astype(jnp.bfloat16)
    ffn = jnp.dot(pick, y, preferred_element_type=jnp.float32)
    o_ref[...] = _layer_norm(DEEPNORM_ALPHA * h_ref[...] + ffn, g2_ref[...], b2_ref[...])


def _combine_call(ys_sorted, pos, gate, h, g2, b2):
    N, D = h.shape
    R = MIX_ROWS
    nt = N // R
    W = ys_sorted.shape[-1]
    return pl.pallas_call(
        _combine_kernel, grid=(nt,),
        in_specs=[
            pl.BlockSpec((1, SORTED_ROWS, W), lambda i: (i, 0, 0)),
            pl.BlockSpec((R, 2), lambda i: (i, 0)),
            pl.BlockSpec((R, 2), lambda i: (i, 0)),
            pl.BlockSpec((R, D), lambda i: (i, 0)),
            pl.BlockSpec((1, D), lambda i: (0, 0)),
            pl.BlockSpec((1, D), lambda i: (0, 0)),
        ],
        out_specs=pl.BlockSpec((R, D), lambda i: (i, 0)),
        out_shape=jax.ShapeDtypeStruct((N, D), jnp.float32),
        compiler_params=_cparams(1, 2 * (SORTED_ROWS * W * 4 + 2 * R * D * 4) + SORTED_ROWS * D * 8
                                 + 3 * R * SORTED_ROWS * 4 + (6 << 20)),
        name="moe_combine",
    )(ys_sorted, pos, gate, h, g2, b2)


def _rope_tables(seq):
    inv_freq = ROPE_THETA ** (-jnp.arange(ROT_HALF, dtype=jnp.float32) / ROT_HALF)
    ang = jnp.arange(seq, dtype=jnp.int32).astype(jnp.float32)[:, None] * inv_freq[None, :]
    cos, sin = jnp.cos(ang), jnp.sin(ang)
    ones = jnp.ones((seq, HEAD_DIM - ROT_DIMS), jnp.float32)
    zeros = jnp.zeros((seq, HEAD_DIM - ROT_DIMS), jnp.float32)
    c_head = jnp.concatenate([cos, cos, ones], axis=1)
    s_head = jnp.concatenate([-sin, sin, zeros], axis=1)
    reps = KN_WIDTH // HEAD_DIM
    return cos.T, sin.T, jnp.tile(c_head, (1, reps)), jnp.tile(s_head, (1, reps))


def _swap_rot_cols(w):
    d, n = w.shape
    wh = w.reshape(d, n // HEAD_DIM, HEAD_DIM)
    sw = jnp.concatenate([wh[:, :, ROT_HALF:ROT_DIMS], wh[:, :, :ROT_HALF],
                          jnp.zeros((d, n // HEAD_DIM, HEAD_DIM - ROT_DIMS), w.dtype)], axis=2)
    return sw.reshape(d, n)


def _block(x, w_in, gate_bias, w_attn_up, w_conv_up, conv_w, w_out, ln_g, ln_b,
           rg_w, rg_b, re_w, re_b, w_gate_e, w_up_e, w_down_e, ln2_g, ln2_b):
    B, S, D = x.shape
    N = B * S
    top_k = min(TOPK_MAX, S // 4)
    bf = jnp.bfloat16
    o = np.cumsum([0, Q_WIDTH, KV_WIDTH, KV_WIDTH, IDXQ_WIDTH, IDX_DIM, IDX_HEADS,
                   CONV_DIM, CONV_DIM, CONV_DIM, N_BRANCHES * D_MODEL])
    w_q, w_k, w_v, w_qi, w_ki, w_wi = (w_in[:, o[i]:o[i + 1]] for i in range(6))
    w_conv = w_in[:, o[6]:o[9]]
    w_gates = w_in[:, o[9]:o[10]]

    wt = jnp.concatenate([w_q.T * (HEAD_DIM ** -0.5 * LOG2_E), w_qi.T, w_v.T, w_wi.T,
                          jnp.zeros((T_ROWS - T_WI0 - IDX_HEADS, D), w_in.dtype)], axis=0).astype(bf)
    pad = jnp.zeros((D, KN_WIDTH - KV_WIDTH - IDX_DIM), w_in.dtype)
    wn = jnp.concatenate([w_k, w_ki, pad, _swap_rot_cols(w_k), _swap_rot_cols(w_ki), pad], axis=1).astype(bf)
    cos_t, sin_t, cos_n, sgn_n = _rope_tables(S)

    qt, qit, vt, wit, kn = _proj_call(x, wt, wn, cos_t, sin_t, cos_n, sgn_n)
    attn = _attn_call(qt, qit, wit, kn, vt, top_k)

    w_r = jnp.concatenate([rg_w, jnp.transpose(re_w, (1, 0, 2)).reshape(D, N_EXPERTS),
                           jnp.zeros((D, ROUTER_LANES - E0 - N_EXPERTS), rg_w.dtype)], axis=1)
    b_r = jnp.concatenate([rg_b, re_b.reshape(-1),
                           jnp.zeros((ROUTER_LANES - E0 - N_EXPERTS,), rg_b.dtype)])[None, :]
    w_rh = w_r.astype(bf)
    w_rl = (w_r - w_rh.astype(jnp.float32)).astype(bf)

    h, xs_sorted, pos, gate, cnt = _mix_call(
        x.reshape(N, D), attn.reshape(N, Q_WIDTH), w_conv.astype(bf), w_gates.astype(bf),
        gate_bias[None, :], conv_w, w_attn_up.astype(bf), w_conv_up.astype(bf), w_out.astype(bf),
        ln_g[None, :], ln_b[None, :], w_rh, w_rl, b_r, S)

    RB, RG = EXPERT_ROWS, ROW_GROUP
    nt = N // MIX_ROWS
    counts = cnt[:, 0, E0:E0 + N_EXPERTS].astype(jnp.int32)
    seg_len = ((counts + RG - 1) // RG) * RG
    seg_local = jnp.cumsum(seg_len, axis=1) - seg_len
    region = jnp.sum(seg_len, axis=0)
    padded = ((region + RB - 1) // RB) * RB
    pad_end = jnp.cumsum(padded)
    pad_start = pad_end - padded
    seg_start = pad_start[None, :] + jnp.cumsum(seg_len, axis=0) - seg_len
    cap = -(-(N * 2 + nt * N_EXPERTS * (RG - 1) + N_EXPERTS * (RB - 1)) // RB) * RB
    nb = cap // RB
    block_start = jnp.arange(nb, dtype=jnp.int32) * RB
    block_expert = jnp.minimum(jnp.sum((block_start[:, None] >= pad_end[None, :]).astype(jnp.int32), axis=1),
                               N_EXPERTS - 1)
    n_used = (pad_end[-1:] // RB).astype(jnp.int32)
    flat_start = seg_start.T.reshape(-1)
    flat_end = flat_start + seg_len.T.reshape(-1)
    step_base = (jnp.arange(nt, dtype=jnp.int32) * SORTED_ROWS)[:, None]
    flat_shift = (step_base + seg_local - seg_start).T.reshape(-1)
    g_idx = jnp.arange(cap // RG, dtype=jnp.int32)
    g_row = g_idx * RG
    ended = flat_end[None, :-1] <= g_row[:, None]

    def lookup(v):
        return v[0] + jnp.sum(jnp.where(ended, (v[1:] - v[:-1])[None, :], 0), axis=1)

    live = (g_row >= lookup(flat_start)) & (g_row < lookup(flat_end))
    home = g_row + lookup(flat_shift)
    spare_per_step = (SORTED_ROWS - SORTED_USED) // RG
    spare = ((g_idx // GROUPS_PER_BLOCK) % 2) * GROUPS_PER_BLOCK + g_idx % GROUPS_PER_BLOCK
    assert 2 * GROUPS_PER_BLOCK <= (nt - 1) * spare_per_step, "not enough spare groups for block padding"
    spare_row = (1 + spare // spare_per_step) * SORTED_ROWS + SORTED_USED + (spare % spare_per_step) * RG
    group_src = jnp.where(live, home, SORTED_ROWS - RG).astype(jnp.int32)
    group_dst = jnp.where(live, home, spare_row).astype(jnp.int32)

    ys_sorted = _expert_call(block_expert, n_used, group_src, group_dst,
                             xs_sorted.reshape(nt * SORTED_ROWS, HALF), w_gate_e, w_up_e, w_down_e, nb)
    out = _combine_call(ys_sorted.reshape(nt, SORTED_ROWS, HALF), pos, gate, h, ln2_g[None, :], ln2_b[None, :])
    return out.reshape(B, S, D)


def kernel(x, w_in, gate_bias, w_attn_up, w_conv_up, conv_w, w_out, ln1_g, ln1_b, router_group_w,
           router_group_b, router_expert_w, router_expert_b, w_gate_e, w_up_e, w_down_e, ln2_g, ln2_b):
    h = x
    for l in range(DEPTH):
        h = _block(h, w_in[l], gate_bias[l], w_attn_up[l], w_conv_up[l], conv_w[l], w_out[l],
                   ln1_g[l], ln1_b[l], router_group_w[l], router_group_b[l], router_expert_w[l],
                   router_expert_b[l], w_gate_e[l], w_up_e[l], w_down_e[l], ln2_g[l], ln2_b[l])
    return h
```

```python
import functools

import jax
import jax.numpy as jnp
import numpy as np
from jax import lax
from jax.experimental import pallas as pl
from jax.experimental.pallas import tpu as pltpu

D_MODEL = 1024
N_HEADS = 8
N_KV_HEADS = 2
HEAD_DIM = 64
Q_WIDTH = N_HEADS * HEAD_DIM
KV_WIDTH = N_KV_HEADS * HEAD_DIM
ROPE_THETA = 500000.0
ROT_DIMS = HEAD_DIM // 4
ROT_HALF = ROT_DIMS // 2
IDX_HEADS = 8
IDX_DIM = 64
IDXQ_WIDTH = IDX_HEADS * IDX_DIM
TOPK_MAX = 256
CONV_DIM = 512
CONV_WIDTH = 3
N_BRANCHES = 2
N_GROUPS = 4
EXPERTS_PER_GROUP = 8
N_EXPERTS = N_GROUPS * EXPERTS_PER_GROUP
D_EXPERT = 512
LN_EPS = 1e-5
DEPTH = 1
DEEPNORM_ALPHA = (2 * DEPTH) ** 0.25

V7X_LANES = 128
V7X_SUBLANES = 8
V7X_VMEM_LIMIT_BYTES = 56 * 1024 * 1024

PROJ_ROWS = 512
ATTN_CHUNK = 256
ATTN_KEY_TILE = 128
DEN_ROWS = 16
ACC_ROWS = HEAD_DIM + DEN_ROWS
LOG2_E = 1.4426950408889634
MIX_ROWS = 512
EXPERT_ROWS = 256
ROW_GROUP = V7X_SUBLANES
SORTED_USED = 2 * MIX_ROWS + N_EXPERTS * (ROW_GROUP - 1)
SORTED_ROWS = -(-(SORTED_USED + ROW_GROUP) // 256) * 256
HALF = D_MODEL // 2

T_Q0, T_QI0, T_V0, T_WI0 = 0, Q_WIDTH, Q_WIDTH + IDXQ_WIDTH, Q_WIDTH + IDXQ_WIDTH + KV_WIDTH
T_ROWS = T_WI0 + 16
KN_WIDTH = 256

INT_MIN = -2147483648
HALF_RANGE = 32768
ROUTER_LANES = 128
E0 = N_GROUPS


def _cparams(n_axes, vmem_bytes):
    return pltpu.CompilerParams(
        dimension_semantics=("arbitrary",) * n_axes,
        vmem_limit_bytes=int(min(vmem_bytes, V7X_VMEM_LIMIT_BYTES)),
    )


def _proj_kernel(x_ref, wt_ref, wn_ref, cost_ref, sint_ref, cosn_ref, sgnn_ref,
                 qt_ref, qit_ref, vt_ref, wit_ref, kn_ref):
    xb = x_ref[0].astype(jnp.bfloat16)
    pt = lax.dot_general(wt_ref[...], xb, (((1,), (1,)), ((), ())),
                         preferred_element_type=jnp.float32)
    cos = cost_ref[...]
    sin = sint_ref[...]

    def rope_t(dst_ref, base):
        for h in range(N_HEADS):
            r0 = base + h * HEAD_DIM
            x1 = pt[r0:r0 + ROT_HALF]
            x2 = pt[r0 + ROT_HALF:r0 + ROT_DIMS]
            head = jnp.concatenate([x1 * cos - x2 * sin, x2 * cos + x1 * sin,
                                    pt[r0 + ROT_DIMS:r0 + HEAD_DIM]], axis=0).astype(dst_ref.dtype)
            for jj, cols in enumerate(chunks):
                dst_ref[0, jj, h * HEAD_DIM:(h + 1) * HEAD_DIM, :] = head[:, cols]

    chunks = [slice(jj * ATTN_CHUNK, (jj + 1) * ATTN_CHUNK) for jj in range(PROJ_ROWS // ATTN_CHUNK)]
    rope_t(qt_ref, T_Q0)
    rope_t(qit_ref, T_QI0)
    for jj, cols in enumerate(chunks):
        vt_ref[0, jj] = pt[T_V0:T_V0 + KV_WIDTH, cols].astype(vt_ref.dtype)
        wit_ref[0, jj] = pt[T_WI0:T_WI0 + IDX_HEADS, cols]
    pn = jnp.dot(xb, wn_ref[...], preferred_element_type=jnp.float32)
    kn = pn[:, :KN_WIDTH] * cosn_ref[...] + pn[:, KN_WIDTH:] * sgnn_ref[...]
    kn_ref[0] = kn.astype(kn_ref.dtype)


def _proj_call(x, wt, wn, cos_t, sin_t, cos_n, sgn_n):
    B, S, D = x.shape
    R = PROJ_ROWS
    nt = S // R
    grid = (B, nt)
    out_shape = (
        jax.ShapeDtypeStruct((B, S // ATTN_CHUNK, Q_WIDTH, ATTN_CHUNK), jnp.bfloat16),
        jax.ShapeDtypeStruct((B, S // ATTN_CHUNK, IDXQ_WIDTH, ATTN_CHUNK), jnp.bfloat16),
        jax.ShapeDtypeStruct((B, S // ATTN_CHUNK, KV_WIDTH, ATTN_CHUNK), jnp.bfloat16),
        jax.ShapeDtypeStruct((B, S // ATTN_CHUNK, IDX_HEADS, ATTN_CHUNK), jnp.float32),
        jax.ShapeDtypeStruct((B, S, KN_WIDTH), jnp.bfloat16),
    )
    in_specs = [
        pl.BlockSpec((1, R, D), lambda b, j: (b, j, 0)),
        pl.BlockSpec((T_ROWS, D), lambda b, j: (0, 0)),
        pl.BlockSpec((D, 2 * KN_WIDTH), lambda b, j: (0, 0)),
        pl.BlockSpec((ROT_HALF, R), lambda b, j: (0, j)),
        pl.BlockSpec((ROT_HALF, R), lambda b, j: (0, j)),
        pl.BlockSpec((R, KN_WIDTH), lambda b, j: (j, 0)),
        pl.BlockSpec((R, KN_WIDTH), lambda b, j: (j, 0)),
    ]
    out_specs = (
        pl.BlockSpec((1, R // ATTN_CHUNK, Q_WIDTH, ATTN_CHUNK), lambda b, j: (b, j, 0, 0)),
        pl.BlockSpec((1, R // ATTN_CHUNK, IDXQ_WIDTH, ATTN_CHUNK), lambda b, j: (b, j, 0, 0)),
        pl.BlockSpec((1, R // ATTN_CHUNK, KV_WIDTH, ATTN_CHUNK), lambda b, j: (b, j, 0, 0)),
        pl.BlockSpec((1, R // ATTN_CHUNK, IDX_HEADS, ATTN_CHUNK), lambda b, j: (b, j, 0, 0)),
        pl.BlockSpec((1, R, KN_WIDTH), lambda b, j: (b, j, 0)),
    )
    vmem = 2 * (R * D * 4 + T_ROWS * D * 2 + D * 2 * KN_WIDTH * 2) + 6 * T_ROWS * R * 4 + (8 << 20)
    return pl.pallas_call(
        _proj_kernel, grid=grid, in_specs=in_specs, out_specs=out_specs, out_shape=out_shape,
        compiler_params=_cparams(2, vmem), name="dsa_proj",
    )(x, wt, wn, cos_t, sin_t, cos_n, sgn_n)


def _float_to_key(s):
    b = lax.bitcast_convert_type(s, jnp.int32)
    k = b ^ (lax.shift_right_arithmetic(b, 31) & jnp.int32(0x7FFFFFFF))
    return jnp.where(b == jnp.int32(INT_MIN), jnp.int32(0), k)


def _attn_kernel(qt_ref, qit_ref, wit_ref, kn_ref, vt_ref, o_ref,
                 key_ref, hi_ref, lo_ref, lga_ref, lgb_ref, acc_ref, m_ref, *, seq, top_k):
    C = ATTN_CHUNK
    c = pl.program_id(1)
    nkb = c + 1
    zeros_half = jnp.zeros((HEAD_DIM, C), jnp.bfloat16)
    row_i = lax.broadcasted_iota(jnp.int32, (C, C), 0)
    lane_i = lax.broadcasted_iota(jnp.int32, (C, C), 1)
    causal_in_block = row_i <= lane_i

    def score_block(kb):
        r0 = pl.multiple_of(kb * C, C)
        kix = kn_ref[0, pl.ds(r0, C), KV_WIDTH:KN_WIDTH]
        acc = None
        for h in range(IDX_HEADS):
            rhs = jnp.concatenate([qit_ref[0, 0, h * IDX_DIM:(h + 1) * IDX_DIM, :], zeros_half], axis=0)
            s = jnp.dot(kix, rhs, preferred_element_type=jnp.float32)
            t = jnp.maximum(s, 0.0) * wit_ref[0, 0, h:h + 1, :]
            acc = t if acc is None else acc + t
        score = acc * (IDX_DIM ** -0.5 * IDX_HEADS ** -0.5)
        keys = jnp.where(causal_in_block | (kb != c), _float_to_key(score), jnp.int32(INT_MIN))
        key_ref[kb] = keys
        hi_ref[kb] = lax.shift_right_arithmetic(keys, 16).astype(jnp.int16)
        lo_ref[kb] = ((keys & jnp.int32(0xFFFF)) - jnp.int32(HALF_RANGE)).astype(jnp.int16)

    def score_body(pair, carry):
        score_block(2 * pair)
        score_block(2 * pair + 1)
        return carry

    lax.fori_loop(0, nkb // 2, score_body, 0)

    @pl.when(nkb % 2 == 1)
    def _():
        score_block(nkb - 1)

    def count(ref, pred, blocks=None):
        packed = ref.dtype == jnp.int16

        def body(kb, part):
            hit = pred(ref[kb], kb)
            if packed:
                words = pltpu.bitcast(jnp.where(hit, jnp.int16(1), jnp.int16(0)), jnp.int32)
            else:
                words = hit.astype(jnp.int32)
            return part + jnp.sum(words.reshape(-1, V7X_SUBLANES, C), axis=0)

        part = jnp.zeros((V7X_SUBLANES, C), jnp.int32)
        if blocks is None:
            part = lax.fori_loop(0, nkb, body, part)
        else:
            for kb in range(blocks):
                part = body(kb, part)
        if packed:
            part = (part & jnp.int32(0xFFFF)) + lax.shift_right_logical(part, 16)
        return jnp.sum(part, axis=0, keepdims=True)

    def kth_largest_16(ref, blocks):
        def body(i, ans_u):
            cand_u = ans_u | lax.shift_left(jnp.int32(1), jnp.int32(15) - i)
            cand = (cand_u - jnp.int32(HALF_RANGE)).astype(jnp.int16)
            return jnp.where(count(ref, lambda k, kb: k >= cand, blocks) >= top_k, cand_u, ans_u)

        return lax.fori_loop(0, 16, body, jnp.zeros((1, C), jnp.int32)) - jnp.int32(HALF_RANGE)

    def threshold(blocks):
        t_hi = kth_largest_16(hi_ref, blocks)
        t_hi16 = t_hi.astype(jnp.int16)
        for kb in range(blocks):
            hi = hi_ref[kb]
            lo_ref[kb] = jnp.where(hi > t_hi16, jnp.int16(HALF_RANGE - 1),
                                   jnp.where(hi == t_hi16, lo_ref[kb], jnp.int16(-HALF_RANGE)))
        t_lo = kth_largest_16(lo_ref, blocks)
        t_lo16 = t_lo.astype(jnp.int16)
        return t_hi, t_lo, count(lo_ref, lambda k, kb: k >= t_lo16, blocks)

    t_hi, t_lo, n_ge = lax.switch(c, [functools.partial(threshold, n + 1) for n in range(seq // C)])
    thr_raw = t_hi * jnp.int32(2 * HALF_RANGE) + (t_lo + jnp.int32(HALF_RANGE))
    thr = jnp.maximum(thr_raw, jnp.int32(INT_MIN + 1))

    has_ties = jnp.max(jnp.where(thr_raw != jnp.int32(INT_MIN), n_ge, 0)) > top_k

    @pl.when(has_ties)
    def _():
        need = (top_k - count(key_ref, lambda k, kb: k > thr)).astype(jnp.float32)
        earlier_rows = (row_i > lane_i).astype(jnp.bfloat16)

        def drop_body(kb, seen):
            kk = key_ref[kb]
            tied = kk == thr
            tied_b = jnp.where(tied, 1.0, 0.0).astype(jnp.bfloat16)
            rank = jnp.dot(earlier_rows, tied_b, preferred_element_type=jnp.float32) + seen
            key_ref[kb] = jnp.where(tied & (rank >= need), kk - 1, kk)
            return seen + jnp.sum(tied_b.astype(jnp.float32), axis=0, keepdims=True)

        lax.fori_loop(0, nkb, drop_body, jnp.zeros((1, C), jnp.float32))

    neg_inf = jnp.float32(-jnp.inf)
    m_ref[...] = jnp.full(m_ref.shape, neg_inf, jnp.float32)
    acc_ref[...] = jnp.zeros(acc_ref.shape, jnp.float32)
    T = ATTN_KEY_TILE
    ones_rows = jnp.ones((DEN_ROWS, C), jnp.bfloat16)

    kv_group = N_HEADS // N_KV_HEADS

    def bias_body(kb, carry):
        key_ref[kb] = lax.bitcast_convert_type(jnp.where(key_ref[kb] >= thr, 0.0, neg_inf), jnp.int32)
        return carry

    lax.fori_loop(0, nkb, bias_body, 0)

    def store_logits(kb, dst_ref, h):
        kb = jnp.minimum(kb, nkb - 1)
        r0 = pl.multiple_of(kb * C, C)
        bias = lax.bitcast_convert_type(key_ref[kb], jnp.float32)
        k2 = kn_ref[0, pl.ds(r0, C), 0:KV_WIDTH]
        qh = qt_ref[0, 0, h * HEAD_DIM:(h + 1) * HEAD_DIM, :]
        rhs = jnp.concatenate([qh, zeros_half] if h < kv_group else [zeros_half, qh], axis=0)
        dst_ref[h] = jnp.dot(k2, rhs, preferred_element_type=jnp.float32) + bias

    def softmax_block(kb, src_ref, dst_ref):
        for h in range(N_HEADS):
            if dst_ref is not None:
                store_logits(kb + 1, dst_ref, h)
            g = h // kv_group
            m_old = m_ref[h:h + 1, :]
            m_new = jnp.maximum(m_old, jnp.max(src_ref[h], axis=0, keepdims=True))
            m_safe = jnp.where(m_new == neg_inf, 0.0, m_new)
            p = jnp.concatenate(
                [jnp.exp2(src_ref[h, s * T:(s + 1) * T, :] - m_safe).astype(jnp.bfloat16) for s in range(C // T)],
                axis=0)
            alpha = jnp.exp2(m_old - m_safe)
            vt = vt_ref[0, kb, g * HEAD_DIM:(g + 1) * HEAD_DIM, :]
            pv = jnp.dot(jnp.concatenate([vt, ones_rows], axis=0), p,
                         preferred_element_type=jnp.float32)
            hs = slice(h * ACC_ROWS, (h + 1) * ACC_ROWS)
            acc_ref[hs, :] = alpha * acc_ref[hs, :] + pv
            m_ref[h:h + 1, :] = m_new

    for h in range(N_HEADS):
        store_logits(0, lga_ref, h)

    def attn_body(pair, carry):
        kb = 2 * pair
        softmax_block(kb, lga_ref, lgb_ref)
        softmax_block(kb + 1, lgb_ref, lga_ref)
        return carry

    lax.fori_loop(0, nkb // 2, attn_body, 0)

    @pl.when(nkb % 2 == 1)
    def _():
        softmax_block(nkb - 1, lga_ref, None)
    outs = []
    for h in range(N_HEADS):
        num = acc_ref[h * ACC_ROWS:h * ACC_ROWS + HEAD_DIM, :]
        den = acc_ref[h * ACC_ROWS + HEAD_DIM:h * ACC_ROWS + HEAD_DIM + 1, :]
        outs.append(num / den)
    o_ref[0] = jnp.transpose(jnp.concatenate(outs, axis=0)).astype(o_ref.dtype)


def _attn_call(qt, qit, wit, kn, vt, top_k):
    B, S = kn.shape[:2]
    C = ATTN_CHUNK
    nc = S // C
    kern = functools.partial(_attn_kernel, seq=S, top_k=top_k)
    in_specs = [
        pl.BlockSpec((1, 1, Q_WIDTH, C), lambda b, c: (b, c, 0, 0)),
        pl.BlockSpec((1, 1, IDXQ_WIDTH, C), lambda b, c: (b, c, 0, 0)),
        pl.BlockSpec((1, 1, IDX_HEADS, C), lambda b, c: (b, c, 0, 0)),
        pl.BlockSpec((1, S, KN_WIDTH), lambda b, c: (b, 0, 0)),
        pl.BlockSpec((1, nc, KV_WIDTH, C), lambda b, c: (b, 0, 0, 0)),
    ]
    out_specs = pl.BlockSpec((1, C, Q_WIDTH), lambda b, c: (b, c, 0))
    scratch = [
        pltpu.VMEM((nc, C, C), jnp.int32),
        pltpu.VMEM((nc, C, C), jnp.int16),
        pltpu.VMEM((nc, C, C), jnp.int16),
        pltpu.VMEM((N_HEADS, C, C), jnp.float32),
        pltpu.VMEM((N_HEADS, C, C), jnp.float32),
        pltpu.VMEM((N_HEADS * ACC_ROWS, C), jnp.float32),
        pltpu.VMEM((N_HEADS, C), jnp.float32),
    ]
    vmem = (2 * S * C * 4 + Q_WIDTH * C * 4 + 2 * 2 * (2 * Q_WIDTH * C + S * KN_WIDTH + S * KV_WIDTH + C * Q_WIDTH)
            + 24 * C * C * 4 + (8 << 20))
    return pl.pallas_call(
        kern, grid=(B, nc), in_specs=in_specs, out_specs=out_specs,
        out_shape=jax.ShapeDtypeStruct((B, S, Q_WIDTH), jnp.bfloat16),
        scratch_shapes=scratch, compiler_params=_cparams(2, vmem), name="dsa_attn",
    )(qt, qit, wit, kn, vt)


def _layer_norm(v, g, b):
    mu = jnp.mean(v, axis=-1, keepdims=True)
    d = v - mu
    var = jnp.mean(d * d, axis=-1, keepdims=True)
    return d * lax.rsqrt(var + LN_EPS) * g + b


def _pack_rows(h):
    hi = lax.bitcast_convert_type(h[:, :HALF].astype(jnp.bfloat16).astype(jnp.float32), jnp.int32)
    lo = lax.bitcast_convert_type(h[:, HALF:].astype(jnp.bfloat16).astype(jnp.float32), jnp.int32)
    return (hi & jnp.int32(-65536)) | lax.shift_right_logical(lo, 16)


def _unpack_rows(w):
    hi = lax.bitcast_convert_type(w & jnp.int32(-65536), jnp.float32)
    lo = lax.bitcast_convert_type(lax.shift_left(w, 16), jnp.float32)
    return jnp.concatenate([hi, lo], axis=1)


def _mix_kernel(x_ref, attn_ref, wc_ref, wg_ref, gb_ref, cw_ref, wau_ref, wcu_ref, wo_ref,
                g1_ref, b1_ref, wrh_ref, wrl_ref, rb_ref,
                h_ref, xs_ref, pos_ref, gate_ref, cnt_ref,
                ubuf_ref, *, steps_per_seq):
    R = MIX_ROWS
    i = pl.program_id(0)

    @pl.when(i % steps_per_seq == 0)
    def _():
        ubuf_ref[0:V7X_SUBLANES, :] = jnp.zeros((V7X_SUBLANES, CONV_DIM), jnp.float32)

    x = x_ref[...]
    xb = x.astype(jnp.bfloat16)
    cv = jnp.dot(xb, wc_ref[...], preferred_element_type=jnp.float32)
    u = cv[:, 2 * CONV_DIM:] * cv[:, :CONV_DIM]
    ubuf_ref[V7X_SUBLANES:, :] = u
    u1 = ubuf_ref[V7X_SUBLANES - 1:V7X_SUBLANES - 1 + R, :]
    u2 = ubuf_ref[V7X_SUBLANES - 2:V7X_SUBLANES - 2 + R, :]
    y = cw_ref[0:1, :] * u2 + cw_ref[1:2, :] * u1 + cw_ref[2:3, :] * u
    conv = (cv[:, CONV_DIM:2 * CONV_DIM] * y).astype(jnp.bfloat16)
    ubuf_ref[0:V7X_SUBLANES, :] = u[R - V7X_SUBLANES:, :]
    z = jnp.dot(xb, wg_ref[...], preferred_element_type=jnp.float32) + gb_ref[...]
    gates = 1.0 / (1.0 + jnp.exp(-z))
    au = jnp.dot(attn_ref[...], wau_ref[...], preferred_element_type=jnp.float32)
    cu = jnp.dot(conv, wcu_ref[...], preferred_element_type=jnp.float32)
    merged = gates[:, :D_MODEL] * au + gates[:, D_MODEL:] * cu
    mix = jnp.dot(merged.astype(jnp.bfloat16), wo_ref[...], preferred_element_type=jnp.float32)
    h = _layer_norm(DEEPNORM_ALPHA * x + mix, g1_ref[...], b1_ref[...])
    h_ref[...] = h

    h_hi = h.astype(jnp.bfloat16)
    h_lo = (h - h_hi.astype(jnp.float32)).astype(jnp.bfloat16)
    lg = (jnp.dot(h_hi, wrh_ref[...], preferred_element_type=jnp.float32)
          + jnp.dot(h_lo, wrh_ref[...], preferred_element_type=jnp.float32)
          + jnp.dot(h_hi, wrl_ref[...], preferred_element_type=jnp.float32)) + rb_ref[...]
    lane = lax.broadcasted_iota(jnp.int32, (R, ROUTER_LANES), 1).astype(jnp.float32)
    neg = jnp.float32(-jnp.inf)
    no_lane = jnp.float32(ROUTER_LANES)
    gmask = lane < N_GROUPS
    gl = jnp.where(gmask, lg, neg)
    gmax = jnp.max(gl, axis=1, keepdims=True)
    grp = jnp.min(jnp.where(gl == gmax, lane, no_lane), axis=1, keepdims=True)
    gsum = jnp.sum(jnp.where(gmask, jnp.exp(gl - gmax), 0.0), axis=1, keepdims=True)
    p_grp = 1.0 / gsum
    lo_lane = E0 + grp * EXPERTS_PER_GROUP
    emask = (lane >= lo_lane) & (lane < lo_lane + EXPERTS_PER_GROUP)
    el = jnp.where(emask, lg, neg)
    v1 = jnp.max(el, axis=1, keepdims=True)
    i1 = jnp.min(jnp.where(el == v1, lane, no_lane), axis=1, keepdims=True)
    el2 = jnp.where(lane == i1, neg, el)
    v2 = jnp.max(el2, axis=1, keepdims=True)
    i2 = jnp.min(jnp.where(el2 == v2, lane, no_lane), axis=1, keepdims=True)
    a = jnp.exp(v2 - v1)
    inv = 1.0 / (1.0 + a)
    gate_ref[:, 0:1] = p_grp * inv
    gate_ref[:, 1:2] = p_grp * (a * inv)
    oh1 = lane == i1
    oh2 = lane == i2
    oh = (oh1 | oh2).astype(jnp.bfloat16)
    r_i = lax.broadcasted_iota(jnp.int32, (R, R), 0)
    c_i = lax.broadcasted_iota(jnp.int32, (R, R), 1)
    tri = (r_i > c_i).astype(jnp.bfloat16)
    before = jnp.dot(tri, oh, preferred_element_type=jnp.float32)
    cnt = jnp.sum(oh.astype(jnp.float32), axis=0, keepdims=True)
    cnt_ref[0] = cnt

    seg = jnp.floor((cnt + (ROW_GROUP - 1)) * (1.0 / ROW_GROUP)) * ROW_GROUP
    l_r = lax.broadcasted_iota(jnp.int32, (ROUTER_LANES, ROUTER_LANES), 0)
    l_c = lax.broadcasted_iota(jnp.int32, (ROUTER_LANES, ROUTER_LANES), 1)
    lanes_before = (l_r < l_c).astype(jnp.bfloat16)
    seg8 = jnp.broadcast_to(seg, (V7X_SUBLANES, ROUTER_LANES)).astype(jnp.bfloat16)
    seg_off = jnp.dot(seg8, lanes_before, preferred_element_type=jnp.float32)[0:1, :]
    where_to = before + seg_off
    pos1 = jnp.sum(jnp.where(oh1, where_to, 0.0), axis=1, keepdims=True)
    pos2 = jnp.sum(jnp.where(oh2, where_to, 0.0), axis=1, keepdims=True)
    pos_ref[:, 0:1] = pos1.astype(jnp.int32)
    pos_ref[:, 1:2] = pos2.astype(jnp.int32)
    diag = r_i == c_i
    pos1_row = jnp.sum(jnp.where(diag, pos1, 0.0), axis=0, keepdims=True)
    pos2_row = jnp.sum(jnp.where(diag, pos2, 0.0), axis=0, keepdims=True)
    sorted_row = lax.broadcasted_iota(jnp.int32, (SORTED_ROWS, R), 0).astype(jnp.float32)
    place = ((sorted_row == pos1_row) | (sorted_row == pos2_row)).astype(jnp.bfloat16)
    xs = jnp.dot(place, h_hi, preferred_element_type=jnp.float32)
    xs_ref[0] = _pack_rows(xs)


def _mix_call(x2, attn2, wc, wg, gb, cw, wau, wcu, wo, g1, b1, wrh, wrl, rb, seq):
    N, D = x2.shape
    R = MIX_ROWS
    nt = N // R
    kern = functools.partial(_mix_kernel, steps_per_seq=seq // R)

    def full(a):
        return pl.BlockSpec(a.shape, lambda i: (0,) * a.ndim)

    in_specs = [
        pl.BlockSpec((R, D), lambda i: (i, 0)),
        pl.BlockSpec((R, Q_WIDTH), lambda i: (i, 0)),
        full(wc), full(wg), full(gb), full(cw), full(wau), full(wcu), full(wo),
        full(g1), full(b1), full(wrh), full(wrl), full(rb),
    ]
    out_shape = (
        jax.ShapeDtypeStruct((N, D), jnp.float32),
        jax.ShapeDtypeStruct((nt, SORTED_ROWS, HALF), jnp.int32),
        jax.ShapeDtypeStruct((N, 2), jnp.int32),
        jax.ShapeDtypeStruct((N, 2), jnp.float32),
        jax.ShapeDtypeStruct((nt, 1, ROUTER_LANES), jnp.float32),
    )
    out_specs = (
        pl.BlockSpec((R, D), lambda i: (i, 0)),
        pl.BlockSpec((1, SORTED_ROWS, HALF), lambda i: (i, 0, 0)),
        pl.BlockSpec((R, 2), lambda i: (i, 0)),
        pl.BlockSpec((R, 2), lambda i: (i, 0)),
        pl.BlockSpec((1, 1, ROUTER_LANES), lambda i: (i, 0, 0)),
    )
    scratch = [pltpu.VMEM((R + V7X_SUBLANES, CONV_DIM), jnp.float32)]
    w_bytes = 2 * (wc.size + wg.size + wau.size + wcu.size + wo.size + wrh.size + wrl.size)
    vmem = (2 * w_bytes + 2 * (R * D * 4 * 2 + R * Q_WIDTH * 2 + SORTED_ROWS * HALF * 4) + 10 * R * 2048 * 4
            + SORTED_ROWS * D * 6 + (6 << 20))
    return pl.pallas_call(
        kern, grid=(nt,), in_specs=in_specs, out_specs=out_specs, out_shape=out_shape,
        scratch_shapes=scratch, compiler_params=_cparams(1, vmem), name="mix_ln_router",
    )(x2, attn2, wc, wg, gb, cw, wau, wcu, wo, g1, b1, wrh, wrl, rb)


GROUPS_PER_BLOCK = EXPERT_ROWS // ROW_GROUP


def _expert_kernel(be_ref, nb_ref, gsrc_ref, gdst_ref, xs_ref, wg_ref, wu_ref, wd_ref, ys_ref,
                   wgb_ref, wub_ref, wdb_ref, xbuf_ref, ybuf_ref, gsems, ssems):
    del xs_ref
    j = pl.program_id(0)
    n_used = nb_ref[0]
    slot = j % 2

    def gather(block, to_slot):
        for g in range(GROUPS_PER_BLOCK):
            row = pl.multiple_of(gsrc_ref[block * GROUPS_PER_BLOCK + g], ROW_GROUP)
            pltpu.make_async_copy(ys_ref.at[pl.ds(row, ROW_GROUP), :],
                                  xbuf_ref.at[to_slot, pl.ds(g * ROW_GROUP, ROW_GROUP), :],
                                  gsems.at[to_slot]).start(priority=g % 2)

    def scatter(block, from_slot):
        for g in range(GROUPS_PER_BLOCK):
            row = pl.multiple_of(gdst_ref[block * GROUPS_PER_BLOCK + g], ROW_GROUP)
            pltpu.make_async_copy(ybuf_ref.at[from_slot, pl.ds(g * ROW_GROUP, ROW_GROUP), :],
                                  ys_ref.at[pl.ds(row, ROW_GROUP), :],
                                  ssems.at[from_slot]).start(priority=g % 2)

    def wait_gather(s):
        pltpu.make_async_copy(ys_ref.at[pl.ds(0, EXPERT_ROWS), :], xbuf_ref.at[s], gsems.at[s]).wait()

    def wait_scatter(s):
        pltpu.make_async_copy(ybuf_ref.at[s], ys_ref.at[pl.ds(0, EXPERT_ROWS), :], ssems.at[s]).wait()

    @pl.when((j == 0) & (n_used > 0))
    def _():
        gather(0, 0)

    @pl.when((j == 0) | (be_ref[j] != be_ref[jnp.maximum(j - 1, 0)]))
    def _():
        wgb_ref[...] = wg_ref[0].astype(jnp.bfloat16)
        wub_ref[...] = wu_ref[0].astype(jnp.bfloat16)
        wdb_ref[...] = wd_ref[0].astype(jnp.bfloat16)

    @pl.when(j + 1 < n_used)
    def _():
        gather(j + 1, 1 - slot)

    @pl.when((j >= 2) & (j < n_used))
    def _():
        wait_scatter(slot)

    @pl.when(j < n_used)
    def _():
        wait_gather(slot)
        xb = _unpack_rows(xbuf_ref[slot]).astype(jnp.bfloat16)
        hg = jnp.dot(xb, wgb_ref[...], preferred_element_type=jnp.float32)
        hu = jnp.dot(xb, wub_ref[...], preferred_element_type=jnp.float32)
        hid = (hg / (1.0 + jnp.exp(-hg))) * hu
        y = jnp.dot(hid.astype(jnp.bfloat16), wdb_ref[...], preferred_element_type=jnp.float32)
        ybuf_ref[slot] = _pack_rows(y)
        scatter(j, slot)

    @pl.when(j == n_used - 1)
    def _():
        wait_scatter(slot)

    @pl.when((j == n_used - 1) & (j >= 1))
    def _():
        wait_scatter(1 - slot)


def _expert_call(block_expert, n_used, group_src, group_dst, xs_sorted, wg, wu, wd, n_blocks):
    W = xs_sorted.shape[-1]
    RB = EXPERT_ROWS
    grid_spec = pltpu.PrefetchScalarGridSpec(
        num_scalar_prefetch=4, grid=(n_blocks,),
        in_specs=[
            pl.BlockSpec(memory_space=pl.ANY),
            pl.BlockSpec((1, D_MODEL, D_EXPERT), lambda j, be, nu, gs, gd: (be[j], 0, 0)),
            pl.BlockSpec((1, D_MODEL, D_EXPERT), lambda j, be, nu, gs, gd: (be[j], 0, 0)),
            pl.BlockSpec((1, D_EXPERT, D_MODEL), lambda j, be, nu, gs, gd: (be[j], 0, 0)),
        ],
        out_specs=pl.BlockSpec(memory_space=pl.ANY),
        scratch_shapes=[pltpu.VMEM((D_MODEL, D_EXPERT), jnp.bfloat16),
                        pltpu.VMEM((D_MODEL, D_EXPERT), jnp.bfloat16),
                        pltpu.VMEM((D_EXPERT, D_MODEL), jnp.bfloat16),
                        pltpu.VMEM((2, RB, W), jnp.int32),
                        pltpu.VMEM((2, RB, W), jnp.int32),
                        pltpu.SemaphoreType.DMA((2,)),
                        pltpu.SemaphoreType.DMA((2,))],
    )
    vmem = (2 * 4 + 2) * 3 * D_MODEL * D_EXPERT + 4 * RB * W * 4 + 8 * RB * D_MODEL * 4 + (6 << 20)
    return pl.pallas_call(
        _expert_kernel, grid_spec=grid_spec,
        out_shape=jax.ShapeDtypeStruct(xs_sorted.shape, xs_sorted.dtype),
        input_output_aliases={4: 0},
        compiler_params=_cparams(1, vmem), name="moe_experts",
    )(block_expert, n_used, group_src, group_dst, xs_sorted, wg, wu, wd)


def _combine_kernel(ys_ref, pos_ref, gate_ref, h_ref, g2_ref, b2_ref, o_ref):
    R = MIX_ROWS
    y = _unpack_rows(ys_ref[0]).astype(jnp.bfloat16)
    col = lax.broadcasted_iota(jnp.int32, (R, SORTED_ROWS), 1)
    pick = (jnp.where(col == pos_ref[:, 0:1], gate_ref[:, 0:1], 0.0)
            + jnp.where(col == pos_ref[:, 1:2], gate_ref[:, 1:2], 0.0)).astype(jnp.bfloat16)
    ffn = jnp.dot(pick, y, preferred_element_type=jnp.float32)
    o_ref[...] = _layer_norm(DEEPNORM_ALPHA * h_ref[...] + ffn, g2_ref[...], b2_ref[...])


def _combine_call(ys_sorted, pos, gate, h, g2, b2):
    N, D = h.shape
    R = MIX_ROWS
    nt = N // R
    W = ys_sorted.shape[-1]
    return pl.pallas_call(
        _combine_kernel, grid=(nt,),
        in_specs=[
            pl.BlockSpec((1, SORTED_ROWS, W), lambda i: (i, 0, 0)),
            pl.BlockSpec((R, 2), lambda i: (i, 0)),
            pl.BlockSpec((R, 2), lambda i: (i, 0)),
            pl.BlockSpec((R, D), lambda i: (i, 0)),
            pl.BlockSpec((1, D), lambda i: (0, 0)),
            pl.BlockSpec((1, D), lambda i: (0, 0)),
        ],
        out_specs=pl.BlockSpec((R, D), lambda i: (i, 0)),
        out_shape=jax.ShapeDtypeStruct((N, D), jnp.float32),
        compiler_params=_cparams(1, 2 * (SORTED_ROWS * W * 4 + 2 * R * D * 4) + SORTED_ROWS * D * 8
                                 + 3 * R * SORTED_ROWS * 4 + (6 << 20)),
        name="moe_combine",
    )(ys_sorted, pos, gate, h, g2, b2)


def _rope_tables(seq):
    inv_freq = ROPE_THETA ** (-jnp.arange(ROT_HALF, dtype=jnp.float32) / ROT_HALF)
    ang = jnp.arange(seq, dtype=jnp.int32).astype(jnp.float32)[:, None] * inv_freq[None, :]
    cos, sin = jnp.cos(ang), jnp.sin(ang)
    ones = jnp.ones((seq, HEAD_DIM - ROT_DIMS), jnp.float32)
    zeros = jnp.zeros((seq, HEAD_DIM - ROT_DIMS), jnp.float32)
    c_head = jnp.concatenate([cos, cos, ones], axis=1)
    s_head = jnp.concatenate([-sin, sin, zeros], axis=1)
    reps = KN_WIDTH // HEAD_DIM
    return cos.T, sin.T, jnp.tile(c_head, (1, reps)), jnp.tile(s_head, (1, reps))


def _swap_rot_cols(w):
    d, n = w.shape
    wh = w.reshape(d, n // HEAD_DIM, HEAD_DIM)
    sw = jnp.concatenate([wh[:, :, ROT_HALF:ROT_DIMS], wh[:, :, :ROT_HALF],
                          jnp.zeros((d, n // HEAD_DIM, HEAD_DIM - ROT_DIMS), w.dtype)], axis=2)
    return sw.reshape(d, n)


def _block(x, w_in, gate_bias, w_attn_up, w_conv_up, conv_w, w_out, ln_g, ln_b,
           rg_w, rg_b, re_w, re_b, w_gate_e, w_up_e, w_down_e, ln2_g, ln2_b):
    B, S, D = x.shape
    N = B * S
    top_k = min(TOPK_MAX, S // 4)
    bf = jnp.bfloat16
    o = np.cumsum([0, Q_WIDTH, KV_WIDTH, KV_WIDTH, IDXQ_WIDTH, IDX_DIM, IDX_HEADS,
                   CONV_DIM, CONV_DIM, CONV_DIM, N_BRANCHES * D_MODEL])
    w_q, w_k, w_v, w_qi, w_ki, w_wi = (w_in[:, o[i]:o[i + 1]] for i in range(6))
    w_conv = w_in[:, o[6]:o[9]]
    w_gates = w_in[:, o[9]:o[10]]

    wt = jnp.concatenate([w_q.T * (HEAD_DIM ** -0.5 * LOG2_E), w_qi.T, w_v.T, w_wi.T,
                          jnp.zeros((T_ROWS - T_WI0 - IDX_HEADS, D), w_in.dtype)], axis=0).astype(bf)
    pad = jnp.zeros((D, KN_WIDTH - KV_WIDTH - IDX_DIM), w_in.dtype)
    wn = jnp.concatenate([w_k, w_ki, pad, _swap_rot_cols(w_k), _swap_rot_cols(w_ki), pad], axis=1).astype(bf)
    cos_t, sin_t, cos_n, sgn_n = _rope_tables(S)

    qt, qit, vt, wit, kn = _proj_call(x, wt, wn, cos_t, sin_t, cos_n, sgn_n)
    attn = _attn_call(qt, qit, wit, kn, vt, top_k)

    w_r = jnp.concatenate([rg_w, jnp.transpose(re_w, (1, 0, 2)).reshape(D, N_EXPERTS),
                           jnp.zeros((D, ROUTER_LANES - E0 - N_EXPERTS), rg_w.dtype)], axis=1)
    b_r = jnp.concatenate([rg_b, re_b.reshape(-1),
                           jnp.zeros((ROUTER_LANES - E0 - N_EXPERTS,), rg_b.dtype)])[None, :]
    w_rh = w_r.astype(bf)
    w_rl = (w_r - w_rh.astype(jnp.float32)).astype(bf)

    h, xs_sorted, pos, gate, cnt = _mix_call(
        x.reshape(N, D), attn.reshape(N, Q_WIDTH), w_conv.astype(bf), w_gates.astype(bf),
        gate_bias[None, :], conv_w, w_attn_up.astype(bf), w_conv_up.astype(bf), w_out.astype(bf),
        ln_g[None, :], ln_b[None, :], w_rh, w_rl, b_r, S)

    RB, RG = EXPERT_ROWS, ROW_GROUP
    nt = N // MIX_ROWS
    counts = cnt[:, 0, E0:E0 + N_EXPERTS].astype(jnp.int32)
    seg_len = ((counts + RG - 1) // RG) * RG
    seg_local = jnp.cumsum(seg_len, axis=1) - seg_len
    region = jnp.sum(seg_len, axis=0)
    padded = ((region + RB - 1) // RB) * RB
    pad_end = jnp.cumsum(padded)
    pad_start = pad_end - padded
    seg_start = pad_start[None, :] + jnp.cumsum(seg_len, axis=0) - seg_len
    cap = -(-(N * 2 + nt * N_EXPERTS * (RG - 1) + N_EXPERTS * (RB - 1)) // RB) * RB
    nb = cap // RB
    block_start = jnp.arange(nb, dtype=jnp.int32) * RB
    block_expert = jnp.minimum(jnp.sum((block_start[:, None] >= pad_end[None, :]).astype(jnp.int32), axis=1),
                               N_EXPERTS - 1)
    n_used = (pad_end[-1:] // RB).astype(jnp.int32)
    flat_start = seg_start.T.reshape(-1)
    flat_end = flat_start + seg_len.T.reshape(-1)
    step_base = (jnp.arange(nt, dtype=jnp.int32) * SORTED_ROWS)[:, None]
    flat_shift = (step_base + seg_local - seg_start).T.reshape(-1)
    g_idx = jnp.arange(cap // RG, dtype=jnp.int32)
    g_row = g_idx * RG
    ended = flat_end[None, :-1] <= g_row[:, None]

    def lookup(v):
        return v[0] + jnp.sum(jnp.where(ended, (v[1:] - v[:-1])[None, :], 0), axis=1)

    live = (g_row >= lookup(flat_start)) & (g_row < lookup(flat_end))
    home = g_row + lookup(flat_shift)
    spare_per_step = (SORTED_ROWS - SORTED_USED) // RG
    spare = ((g_idx // GROUPS_PER_BLOCK) % 2) * GROUPS_PER_BLOCK + g_idx % GROUPS_PER_BLOCK
    assert 2 * GROUPS_PER_BLOCK <= (nt - 1) * spare_per_step, "not enough spare groups for block padding"
    spare_row = (1 + spare // spare_per_step) * SORTED_ROWS + SORTED_USED + (spare % spare_per_step) * RG
    group_src = jnp.where(live, home, SORTED_ROWS - RG).astype(jnp.int32)
    group_dst = jnp.where(live, home, spare_row).astype(jnp.int32)

    ys_sorted = _expert_call(block_expert, n_used, group_src, group_dst,
                             xs_sorted.reshape(nt * SORTED_ROWS, HALF), w_gate_e, w_up_e, w_down_e, nb)
    out = _combine_call(ys_sorted.reshape(nt, SORTED_ROWS, HALF), pos, gate, h, ln2_g[None, :], ln2_b[None, :])
    return out.reshape(B, S, D)


def kernel(x, w_in, gate_bias, w_attn_up, w_conv_up, conv_w, w_out, ln1_g, ln1_b, router_group_w,
           router_group_b, router_expert_w, router_expert_b, w_gate_e, w_up_e, w_down_e, ln2_g, ln2_b):
    h = x
    for l in range(DEPTH):
        h = _block(h, w_in[l], gate_bias[l], w_attn_up[l], w_conv_up[l], conv_w[l], w_out[l],
                   ln1_g[l], ln1_b[l], router_group_w[l], router_group_b[l], router_expert_w[l],
                   router_expert_b[l], w_gate_e[l], w_up_e[l], w_down_e[l], ln2_g[l], ln2_b[l])
    return h
```

```python
import functools

import jax
import jax.numpy as jnp
import numpy as np
from jax import lax
from jax.experimental import pallas as pl
from jax.experimental.pallas import tpu as pltpu

D_MODEL = 1024
N_HEADS = 8
N_KV_HEADS = 2
HEAD_DIM = 64
Q_WIDTH = N_HEADS * HEAD_DIM
KV_WIDTH = N_KV_HEADS * HEAD_DIM
ROPE_THETA = 500000.0
ROT_DIMS = HEAD_DIM // 4
ROT_HALF = ROT_DIMS // 2
IDX_HEADS = 8
IDX_DIM = 64
IDXQ_WIDTH = IDX_HEADS * IDX_DIM
TOPK_MAX = 256
CONV_DIM = 512
CONV_WIDTH = 3
N_BRANCHES = 2
N_GROUPS = 4
EXPERTS_PER_GROUP = 8
N_EXPERTS = N_GROUPS * EXPERTS_PER_GROUP
D_EXPERT = 512
LN_EPS = 1e-5
DEPTH = 1
DEEPNORM_ALPHA = (2 * DEPTH) ** 0.25

V7X_LANES = 128
V7X_SUBLANES = 8
V7X_VMEM_LIMIT_BYTES = 56 * 1024 * 1024

PROJ_ROWS = 512
ATTN_CHUNK = 256
ATTN_KEY_TILE = 128
DEN_ROWS = 16
ACC_ROWS = HEAD_DIM + DEN_ROWS
LOG2_E = 1.4426950408889634
MIX_ROWS = 512
EXPERT_ROWS = 512
ROW_GROUP = V7X_SUBLANES
SORTED_USED = 2 * MIX_ROWS + N_EXPERTS * (ROW_GROUP - 1)
SORTED_ROWS = -(-(SORTED_USED + ROW_GROUP) // 256) * 256
HALF = D_MODEL // 2

T_Q0, T_QI0, T_V0, T_WI0 = 0, Q_WIDTH, Q_WIDTH + IDXQ_WIDTH, Q_WIDTH + IDXQ_WIDTH + KV_WIDTH
T_ROWS = T_WI0 + 16
KN_WIDTH = 256

INT_MIN = -2147483648
HALF_RANGE = 32768
ROUTER_LANES = 128
E0 = N_GROUPS


def _cparams(n_axes, vmem_bytes):
    return pltpu.CompilerParams(
        dimension_semantics=("arbitrary",) * n_axes,
        vmem_limit_bytes=int(min(vmem_bytes, V7X_VMEM_LIMIT_BYTES)),
    )


def _proj_kernel(x_ref, wt_ref, wn_ref, cost_ref, sint_ref, cosn_ref, sgnn_ref,
                 qt_ref, qit_ref, vt_ref, wit_ref, kn_ref):
    xb = x_ref[0].astype(jnp.bfloat16)
    pt = lax.dot_general(wt_ref[...], xb, (((1,), (1,)), ((), ())),
                         preferred_element_type=jnp.float32)
    cos = cost_ref[...]
    sin = sint_ref[...]

    def rope_t(dst_ref, base):
        for h in range(N_HEADS):
            r0 = base + h * HEAD_DIM
            x1 = pt[r0:r0 + ROT_HALF]
            x2 = pt[r0 + ROT_HALF:r0 + ROT_DIMS]
            head = jnp.concatenate([x1 * cos - x2 * sin, x2 * cos + x1 * sin,
                                    pt[r0 + ROT_DIMS:r0 + HEAD_DIM]], axis=0).astype(dst_ref.dtype)
            for jj, cols in enumerate(chunks):
                dst_ref[0, jj, h * HEAD_DIM:(h + 1) * HEAD_DIM, :] = head[:, cols]

    chunks = [slice(jj * ATTN_CHUNK, (jj + 1) * ATTN_CHUNK) for jj in range(PROJ_ROWS // ATTN_CHUNK)]
    rope_t(qt_ref, T_Q0)
    rope_t(qit_ref, T_QI0)
    for jj, cols in enumerate(chunks):
        vt_ref[0, jj] = pt[T_V0:T_V0 + KV_WIDTH, cols].astype(vt_ref.dtype)
        wit_ref[0, jj] = pt[T_WI0:T_WI0 + IDX_HEADS, cols]
    pn = jnp.dot(xb, wn_ref[...], preferred_element_type=jnp.float32)
    kn = pn[:, :KN_WIDTH] * cosn_ref[...] + pn[:, KN_WIDTH:] * sgnn_ref[...]
    kn_ref[0] = kn.astype(kn_ref.dtype)


def _proj_call(x, wt, wn, cos_t, sin_t, cos_n, sgn_n):
    B, S, D = x.shape
    R = PROJ_ROWS
    nt = S // R
    grid = (B, nt)
    out_shape = (
        jax.ShapeDtypeStruct((B, S // ATTN_CHUNK, Q_WIDTH, ATTN_CHUNK), jnp.bfloat16),
        jax.ShapeDtypeStruct((B, S // ATTN_CHUNK, IDXQ_WIDTH, ATTN_CHUNK), jnp.bfloat16),
        jax.ShapeDtypeStruct((B, S // ATTN_CHUNK, KV_WIDTH, ATTN_CHUNK), jnp.bfloat16),
        jax.ShapeDtypeStruct((B, S // ATTN_CHUNK, IDX_HEADS, ATTN_CHUNK), jnp.float32),
        jax.ShapeDtypeStruct((B, S, KN_WIDTH), jnp.bfloat16),
    )
    in_specs = [
        pl.BlockSpec((1, R, D), lambda b, j: (b, j, 0)),
        pl.BlockSpec((T_ROWS, D), lambda b, j: (0, 0)),
        pl.BlockSpec((D, 2 * KN_WIDTH), lambda b, j: (0, 0)),
        pl.BlockSpec((ROT_HALF, R), lambda b, j: (0, j)),
        pl.BlockSpec((ROT_HALF, R), lambda b, j: (0, j)),
        pl.BlockSpec((R, KN_WIDTH), lambda b, j: (j, 0)),
        pl.BlockSpec((R, KN_WIDTH), lambda b, j: (j, 0)),
    ]
    out_specs = (
        pl.BlockSpec((1, R // ATTN_CHUNK, Q_WIDTH, ATTN_CHUNK), lambda b, j: (b, j, 0, 0)),
        pl.BlockSpec((1, R // ATTN_CHUNK, IDXQ_WIDTH, ATTN_CHUNK), lambda b, j: (b, j, 0, 0)),
        pl.BlockSpec((1, R // ATTN_CHUNK, KV_WIDTH, ATTN_CHUNK), lambda b, j: (b, j, 0, 0)),
        pl.BlockSpec((1, R // ATTN_CHUNK, IDX_HEADS, ATTN_CHUNK), lambda b, j: (b, j, 0, 0)),
        pl.BlockSpec((1, R, KN_WIDTH), lambda b, j: (b, j, 0)),
    )
    vmem = 2 * (R * D * 4 + T_ROWS * D * 2 + D * 2 * KN_WIDTH * 2) + 6 * T_ROWS * R * 4 + (8 << 20)
    return pl.pallas_call(
        _proj_kernel, grid=grid, in_specs=in_specs, out_specs=out_specs, out_shape=out_shape,
        compiler_params=_cparams(2, vmem), name="dsa_proj",
    )(x, wt, wn, cos_t, sin_t, cos_n, sgn_n)


def _float_to_key(s):
    b = lax.bitcast_convert_type(s, jnp.int32)
    k = b ^ (lax.shift_right_arithmetic(b, 31) & jnp.int32(0x7FFFFFFF))
    return jnp.where(b == jnp.int32(INT_MIN), jnp.int32(0), k)


def _attn_kernel(qt_ref, qit_ref, wit_ref, kn_ref, vt_ref, o_ref,
                 key_ref, hi_ref, lo_ref, lga_ref, lgb_ref, acc_ref, m_ref, *, seq, top_k):
    C = ATTN_CHUNK
    c = pl.program_id(1)
    nkb = c + 1
    zeros_half = jnp.zeros((HEAD_DIM, C), jnp.bfloat16)
    row_i = lax.broadcasted_iota(jnp.int32, (C, C), 0)
    lane_i = lax.broadcasted_iota(jnp.int32, (C, C), 1)
    causal_in_block = row_i <= lane_i

    def score_block(kb):
        r0 = pl.multiple_of(kb * C, C)
        kix = kn_ref[0, pl.ds(r0, C), KV_WIDTH:KN_WIDTH]
        acc = None
        for h in range(IDX_HEADS):
            rhs = jnp.concatenate([qit_ref[0, 0, h * IDX_DIM:(h + 1) * IDX_DIM, :], zeros_half], axis=0)
            s = jnp.dot(kix, rhs, preferred_element_type=jnp.float32)
            t = jnp.maximum(s, 0.0) * wit_ref[0, 0, h:h + 1, :]
            acc = t if acc is None else acc + t
        score = acc * (IDX_DIM ** -0.5 * IDX_HEADS ** -0.5)
        keys = jnp.where(causal_in_block | (kb != c), _float_to_key(score), jnp.int32(INT_MIN))
        key_ref[kb] = keys
        hi_ref[kb] = lax.shift_right_arithmetic(keys, 16).astype(jnp.int16)
        lo_ref[kb] = ((keys & jnp.int32(0xFFFF)) - jnp.int32(HALF_RANGE)).astype(jnp.int16)

    def score_body(pair, carry):
        score_block(2 * pair)
        score_block(2 * pair + 1)
        return carry

    lax.fori_loop(0, nkb // 2, score_body, 0)

    @pl.when(nkb % 2 == 1)
    def _():
        score_block(nkb - 1)

    def count(ref, pred, blocks=None):
        packed = ref.dtype == jnp.int16

        def body(kb, part):
            hit = pred(ref[kb], kb)
            if packed:
                words = pltpu.bitcast(jnp.where(hit, jnp.int16(1), jnp.int16(0)), jnp.int32)
            else:
                words = hit.astype(jnp.int32)
            return part + jnp.sum(words.reshape(-1, V7X_SUBLANES, C), axis=0)

        part = jnp.zeros((V7X_SUBLANES, C), jnp.int32)
        if blocks is None:
            part = lax.fori_loop(0, nkb, body, part)
        else:
            for kb in range(blocks):
                part = body(kb, part)
        if packed:
            part = (part & jnp.int32(0xFFFF)) + lax.shift_right_logical(part, 16)
        return jnp.sum(part, axis=0, keepdims=True)

    def kth_largest_16(ref, blocks):
        def body(i, ans_u):
            cand_u = ans_u | lax.shift_left(jnp.int32(1), jnp.int32(15) - i)
            cand = (cand_u - jnp.int32(HALF_RANGE)).astype(jnp.int16)
            return jnp.where(count(ref, lambda k, kb: k >= cand, blocks) >= top_k, cand_u, ans_u)

        return lax.fori_loop(0, 16, body, jnp.zeros((1, C), jnp.int32)) - jnp.int32(HALF_RANGE)

    def threshold(blocks):
        t_hi = kth_largest_16(hi_ref, blocks)
        t_hi16 = t_hi.astype(jnp.int16)
        for kb in range(blocks):
            hi = hi_ref[kb]
            lo_ref[kb] = jnp.where(hi > t_hi16, jnp.int16(HALF_RANGE - 1),
                                   jnp.where(hi == t_hi16, lo_ref[kb], jnp.int16(-HALF_RANGE)))
        t_lo = kth_largest_16(lo_ref, blocks)
        t_lo16 = t_lo.astype(jnp.int16)
        return t_hi, t_lo, count(lo_ref, lambda k, kb: k >= t_lo16, blocks)

    t_hi, t_lo, n_ge = lax.switch(c, [functools.partial(threshold, n + 1) for n in range(seq // C)])
    thr_raw = t_hi * jnp.int32(2 * HALF_RANGE) + (t_lo + jnp.int32(HALF_RANGE))
    thr = jnp.maximum(thr_raw, jnp.int32(INT_MIN + 1))

    has_ties = jnp.max(jnp.where(thr_raw != jnp.int32(INT_MIN), n_ge, 0)) > top_k

    @pl.when(has_ties)
    def _():
        need = (top_k - count(key_ref, lambda k, kb: k > thr)).astype(jnp.float32)
        earlier_rows = (row_i > lane_i).astype(jnp.bfloat16)

        def drop_body(kb, seen):
            kk = key_ref[kb]
            tied = kk == thr
            tied_b = jnp.where(tied, 1.0, 0.0).astype(jnp.bfloat16)
            rank = jnp.dot(earlier_rows, tied_b, preferred_element_type=jnp.float32) + seen
            key_ref[kb] = jnp.where(tied & (rank >= need), kk - 1, kk)
            return seen + jnp.sum(tied_b.astype(jnp.float32), axis=0, keepdims=True)

        lax.fori_loop(0, nkb, drop_body, jnp.zeros((1, C), jnp.float32))

    neg_inf = jnp.float32(-jnp.inf)
    m_ref[...] = jnp.full(m_ref.shape, neg_inf, jnp.float32)
    acc_ref[...] = jnp.zeros(acc_ref.shape, jnp.float32)
    T = ATTN_KEY_TILE
    ones_rows = jnp.ones((DEN_ROWS, C), jnp.bfloat16)

    kv_group = N_HEADS // N_KV_HEADS

    def bias_body(kb, carry):
        key_ref[kb] = lax.bitcast_convert_type(jnp.where(key_ref[kb] >= thr, 0.0, neg_inf), jnp.int32)
        return carry

    lax.fori_loop(0, nkb, bias_body, 0)

    def store_logits(kb, dst_ref, h):
        kb = jnp.minimum(kb, nkb - 1)
        r0 = pl.multiple_of(kb * C, C)
        bias = lax.bitcast_convert_type(key_ref[kb], jnp.float32)
        k2 = kn_ref[0, pl.ds(r0, C), 0:KV_WIDTH]
        qh = qt_ref[0, 0, h * HEAD_DIM:(h + 1) * HEAD_DIM, :]
        rhs = jnp.concatenate([qh, zeros_half] if h < kv_group else [zeros_half, qh], axis=0)
        dst_ref[h] = jnp.dot(k2, rhs, preferred_element_type=jnp.float32) + bias

    def softmax_block(kb, src_ref, dst_ref):
        for h in range(N_HEADS):
            if dst_ref is not None:
                store_logits(kb + 1, dst_ref, h)
            g = h // kv_group
            m_old = m_ref[h:h + 1, :]
            m_new = jnp.maximum(m_old, jnp.max(src_ref[h], axis=0, keepdims=True))
            m_safe = jnp.where(m_new == neg_inf, 0.0, m_new)
            p = jnp.concatenate(
                [jnp.exp2(src_ref[h, s * T:(s + 1) * T, :] - m_safe).astype(jnp.bfloat16) for s in range(C // T)],
                axis=0)
            alpha = jnp.exp2(m_old - m_safe)
            vt = vt_ref[0, kb, g * HEAD_DIM:(g + 1) * HEAD_DIM, :]
            pv = jnp.dot(jnp.concatenate([vt, ones_rows], axis=0), p,
                         preferred_element_type=jnp.float32)
            hs = slice(h * ACC_ROWS, (h + 1) * ACC_ROWS)
            acc_ref[hs, :] = alpha * acc_ref[hs, :] + pv
            m_ref[h:h + 1, :] = m_new

    for h in range(N_HEADS):
        store_logits(0, lga_ref, h)

    def attn_body(pair, carry):
        kb = 2 * pair
        softmax_block(kb, lga_ref, lgb_ref)
        softmax_block(kb + 1, lgb_ref, lga_ref)
        return carry

    lax.fori_loop(0, nkb // 2, attn_body, 0)

    @pl.when(nkb % 2 == 1)
    def _():
        softmax_block(nkb - 1, lga_ref, None)
    outs = []
    for h in range(N_HEADS):
        num = acc_ref[h * ACC_ROWS:h * ACC_ROWS + HEAD_DIM, :]
        den = acc_ref[h * ACC_ROWS + HEAD_DIM:h * ACC_ROWS + HEAD_DIM + 1, :]
        outs.append(num / den)
    o_ref[0] = jnp.transpose(jnp.concatenate(outs, axis=0)).astype(o_ref.dtype)


def _attn_call(qt, qit, wit, kn, vt, top_k):
    B, S = kn.shape[:2]
    C = ATTN_CHUNK
    nc = S // C
    kern = functools.partial(_attn_kernel, seq=S, top_k=top_k)
    in_specs = [
        pl.BlockSpec((1, 1, Q_WIDTH, C), lambda b, c: (b, c, 0, 0)),
        pl.BlockSpec((1, 1, IDXQ_WIDTH, C), lambda b, c: (b, c, 0, 0)),
        pl.BlockSpec((1, 1, IDX_HEADS, C), lambda b, c: (b, c, 0, 0)),
        pl.BlockSpec((1, S, KN_WIDTH), lambda b, c: (b, 0, 0)),
        pl.BlockSpec((1, nc, KV_WIDTH, C), lambda b, c: (b, 0, 0, 0)),
    ]
    out_specs = pl.BlockSpec((1, C, Q_WIDTH), lambda b, c: (b, c, 0))
    scratch = [
        pltpu.VMEM((nc, C, C), jnp.int32),
        pltpu.VMEM((nc, C, C), jnp.int16),
        pltpu.VMEM((nc, C, C), jnp.int16),
        pltpu.VMEM((N_HEADS, C, C), jnp.float32),
        pltpu.VMEM((N_HEADS, C, C), jnp.float32),
        pltpu.VMEM((N_HEADS * ACC_ROWS, C), jnp.float32),
        pltpu.VMEM((N_HEADS, C), jnp.float32),
    ]
    vmem = (2 * S * C * 4 + Q_WIDTH * C * 4 + 2 * 2 * (2 * Q_WIDTH * C + S * KN_WIDTH + S * KV_WIDTH + C * Q_WIDTH)
            + 24 * C * C * 4 + (8 << 20))
    return pl.pallas_call(
        kern, grid=(B, nc), in_specs=in_specs, out_specs=out_specs,
        out_shape=jax.ShapeDtypeStruct((B, S, Q_WIDTH), jnp.bfloat16),
        scratch_shapes=scratch, compiler_params=_cparams(2, vmem), name="dsa_attn",
    )(qt, qit, wit, kn, vt)


def _layer_norm(v, g, b):
    mu = jnp.mean(v, axis=-1, keepdims=True)
    d = v - mu
    var = jnp.mean(d * d, axis=-1, keepdims=True)
    return d * lax.rsqrt(var + LN_EPS) * g + b


def _pack_rows(h):
    hi = lax.bitcast_convert_type(h[:, :HALF].astype(jnp.bfloat16).astype(jnp.float32), jnp.int32)
    lo = lax.bitcast_convert_type(h[:, HALF:].astype(jnp.bfloat16).astype(jnp.float32), jnp.int32)
    return (hi & jnp.int32(-65536)) | lax.shift_right_logical(lo, 16)


def _unpack_rows(w):
    hi = lax.bitcast_convert_type(w & jnp.int32(-65536), jnp.float32)
    lo = lax.bitcast_convert_type(lax.shift_left(w, 16), jnp.float32)
    return jnp.concatenate([hi, lo], axis=1)


def _mix_kernel(x_ref, attn_ref, wc_ref, wg_ref, gb_ref, cw_ref, wau_ref, wcu_ref, wo_ref,
                g1_ref, b1_ref, wrh_ref, wrl_ref, rb_ref,
                h_ref, xs_ref, pos_ref, gate_ref, cnt_ref,
                ubuf_ref, *, steps_per_seq):
    R = MIX_ROWS
    i = pl.program_id(0)

    @pl.when(i % steps_per_seq == 0)
    def _():
        ubuf_ref[0:V7X_SUBLANES, :] = jnp.zeros((V7X_SUBLANES, CONV_DIM), jnp.float32)

    x = x_ref[...]
    xb = x.astype(jnp.bfloat16)
    cv = jnp.dot(xb, wc_ref[...], preferred_element_type=jnp.float32)
    u = cv[:, 2 * CONV_DIM:] * cv[:, :CONV_DIM]
    ubuf_ref[V7X_SUBLANES:, :] = u
    u1 = ubuf_ref[V7X_SUBLANES - 1:V7X_SUBLANES - 1 + R, :]
    u2 = ubuf_ref[V7X_SUBLANES - 2:V7X_SUBLANES - 2 + R, :]
    y = cw_ref[0:1, :] * u2 + cw_ref[1:2, :] * u1 + cw_ref[2:3, :] * u
    conv = (cv[:, CONV_DIM:2 * CONV_DIM] * y).astype(jnp.bfloat16)
    ubuf_ref[0:V7X_SUBLANES, :] = u[R - V7X_SUBLANES:, :]
    z = jnp.dot(xb, wg_ref[...], preferred_element_type=jnp.float32) + gb_ref[...]
    gates = 1.0 / (1.0 + jnp.exp(-z))
    au = jnp.dot(attn_ref[...], wau_ref[...], preferred_element_type=jnp.float32)
    cu = jnp.dot(conv, wcu_ref[...], preferred_element_type=jnp.float32)
    merged = gates[:, :D_MODEL] * au + gates[:, D_MODEL:] * cu
    mix = jnp.dot(merged.astype(jnp.bfloat16), wo_ref[...], preferred_element_type=jnp.float32)
    h = _layer_norm(DEEPNORM_ALPHA * x + mix, g1_ref[...], b1_ref[...])
    h_ref[...] = h

    h_hi = h.astype(jnp.bfloat16)
    h_lo = (h - h_hi.astype(jnp.float32)).astype(jnp.bfloat16)
    lg = (jnp.dot(h_hi, wrh_ref[...], preferred_element_type=jnp.float32)
          + jnp.dot(h_lo, wrh_ref[...], preferred_element_type=jnp.float32)
          + jnp.dot(h_hi, wrl_ref[...], preferred_element_type=jnp.float32)) + rb_ref[...]
    lane = lax.broadcasted_iota(jnp.int32, (R, ROUTER_LANES), 1).astype(jnp.float32)
    neg = jnp.float32(-jnp.inf)
    no_lane = jnp.float32(ROUTER_LANES)
    gmask = lane < N_GROUPS
    gl = jnp.where(gmask, lg, neg)
    gmax = jnp.max(gl, axis=1, keepdims=True)
    grp = jnp.min(jnp.where(gl == gmax, lane, no_lane), axis=1, keepdims=True)
    gsum = jnp.sum(jnp.where(gmask, jnp.exp(gl - gmax), 0.0), axis=1, keepdims=True)
    p_grp = 1.0 / gsum
    lo_lane = E0 + grp * EXPERTS_PER_GROUP
    emask = (lane >= lo_lane) & (lane < lo_lane + EXPERTS_PER_GROUP)
    el = jnp.where(emask, lg, neg)
    v1 = jnp.max(el, axis=1, keepdims=True)
    i1 = jnp.min(jnp.where(el == v1, lane, no_lane), axis=1, keepdims=True)
    el2 = jnp.where(lane == i1, neg, el)
    v2 = jnp.max(el2, axis=1, keepdims=True)
    i2 = jnp.min(jnp.where(el2 == v2, lane, no_lane), axis=1, keepdims=True)
    a = jnp.exp(v2 - v1)
    inv = 1.0 / (1.0 + a)
    gate_ref[:, 0:1] = p_grp * inv
    gate_ref[:, 1:2] = p_grp * (a * inv)
    oh1 = lane == i1
    oh2 = lane == i2
    oh = (oh1 | oh2).astype(jnp.bfloat16)
    r_i = lax.broadcasted_iota(jnp.int32, (R, R), 0)
    c_i = lax.broadcasted_iota(jnp.int32, (R, R), 1)
    tri = (r_i > c_i).astype(jnp.bfloat16)
    before = jnp.dot(tri, oh, preferred_element_type=jnp.float32)
    cnt = jnp.sum(oh.astype(jnp.float32), axis=0, keepdims=True)
    cnt_ref[0] = cnt

    seg = jnp.floor((cnt + (ROW_GROUP - 1)) * (1.0 / ROW_GROUP)) * ROW_GROUP
    l_r = lax.broadcasted_iota(jnp.int32, (ROUTER_LANES, ROUTER_LANES), 0)
    l_c = lax.broadcasted_iota(jnp.int32, (ROUTER_LANES, ROUTER_LANES), 1)
    lanes_before = (l_r < l_c).astype(jnp.bfloat16)
    seg8 = jnp.broadcast_to(seg, (V7X_SUBLANES, ROUTER_LANES)).astype(jnp.bfloat16)
    seg_off = jnp.dot(seg8, lanes_before, preferred_element_type=jnp.float32)[0:1, :]
    where_to = before + seg_off
    pos1 = jnp.sum(jnp.where(oh1, where_to, 0.0), axis=1, keepdims=True)
    pos2 = jnp.sum(jnp.where(oh2, where_to, 0.0), axis=1, keepdims=True)
    pos_ref[:, 0:1] = pos1.astype(jnp.int32)
    pos_ref[:, 1:2] = pos2.astype(jnp.int32)
    diag = r_i == c_i
    pos1_row = jnp.sum(jnp.where(diag, pos1, 0.0), axis=0, keepdims=True)
    pos2_row = jnp.sum(jnp.where(diag, pos2, 0.0), axis=0, keepdims=True)
    sorted_row = lax.broadcasted_iota(jnp.int32, (SORTED_ROWS, R), 0).astype(jnp.float32)
    place = ((sorted_row == pos1_row) | (sorted_row == pos2_row)).astype(jnp.bfloat16)
    xs = jnp.dot(place, h_hi, preferred_element_type=jnp.float32)
    xs_ref[0] = _pack_rows(xs)


def _mix_call(x2, attn2, wc, wg, gb, cw, wau, wcu, wo, g1, b1, wrh, wrl, rb, seq):
    N, D = x2.shape
    R = MIX_ROWS
    nt = N // R
    kern = functools.partial(_mix_kernel, steps_per_seq=seq // R)

    def full(a):
        return pl.BlockSpec(a.shape, lambda i: (0,) * a.ndim)

    in_specs = [
        pl.BlockSpec((R, D), lambda i: (i, 0)),
        pl.BlockSpec((R, Q_WIDTH), lambda i: (i, 0)),
        full(wc), full(wg), full(gb), full(cw), full(wau), full(wcu), full(wo),
        full(g1), full(b1), full(wrh), full(wrl), full(rb),
    ]
    out_shape = (
        jax.ShapeDtypeStruct((N, D), jnp.float32),
        jax.ShapeDtypeStruct((nt, SORTED_ROWS, HALF), jnp.int32),
        jax.ShapeDtypeStruct((N, 2), jnp.int32),
        jax.ShapeDtypeStruct((N, 2), jnp.float32),
        jax.ShapeDtypeStruct((nt, 1, ROUTER_LANES), jnp.float32),
    )
    out_specs = (
        pl.BlockSpec((R, D), lambda i: (i, 0)),
        pl.BlockSpec((1, SORTED_ROWS, HALF), lambda i: (i, 0, 0)),
        pl.BlockSpec((R, 2), lambda i: (i, 0)),
        pl.BlockSpec((R, 2), lambda i: (i, 0)),
        pl.BlockSpec((1, 1, ROUTER_LANES), lambda i: (i, 0, 0)),
    )
    scratch = [pltpu.VMEM((R + V7X_SUBLANES, CONV_DIM), jnp.float32)]
    w_bytes = 2 * (wc.size + wg.size + wau.size + wcu.size + wo.size + wrh.size + wrl.size)
    vmem = (2 * w_bytes + 2 * (R * D * 4 * 2 + R * Q_WIDTH * 2 + SORTED_ROWS * HALF * 4) + 10 * R * 2048 * 4
            + SORTED_ROWS * D * 6 + (6 << 20))
    return pl.pallas_call(
        kern, grid=(nt,), in_specs=in_specs, out_specs=out_specs, out_shape=out_shape,
        scratch_shapes=scratch, compiler_params=_cparams(1, vmem), name="mix_ln_router",
    )(x2, attn2, wc, wg, gb, cw, wau, wcu, wo, g1, b1, wrh, wrl, rb)


GROUPS_PER_BLOCK = EXPERT_ROWS // ROW_GROUP


def _expert_kernel(be_ref, nb_ref, gsrc_ref, gdst_ref, xs_ref, wg_ref, wu_ref, wd_ref, ys_ref,
                   wgb_ref, wub_ref, wdb_ref, xbuf_ref, ybuf_ref, gsems, ssems):
    del xs_ref
    j = pl.program_id(0)
    n_used = nb_ref[0]
    slot = j % 2

    def gather(block, to_slot):
        for g in range(GROUPS_PER_BLOCK):
            row = pl.multiple_of(gsrc_ref[block * GROUPS_PER_BLOCK + g], ROW_GROUP)
            pltpu.make_async_copy(ys_ref.at[pl.ds(row, ROW_GROUP), :],
                                  xbuf_ref.at[to_slot, pl.ds(g * ROW_GROUP, ROW_GROUP), :],
                                  gsems.at[to_slot]).start(priority=g % 2)

    def scatter(block, from_slot):
        for g in range(GROUPS_PER_BLOCK):
            row = pl.multiple_of(gdst_ref[block * GROUPS_PER_BLOCK + g], ROW_GROUP)
            pltpu.make_async_copy(ybuf_ref.at[from_slot, pl.ds(g * ROW_GROUP, ROW_GROUP), :],
                                  ys_ref.at[pl.ds(row, ROW_GROUP), :],
                                  ssems.at[from_slot]).start(priority=g % 2)

    def wait_gather(s):
        pltpu.make_async_copy(ys_ref.at[pl.ds(0, EXPERT_ROWS), :], xbuf_ref.at[s], gsems.at[s]).wait()

    def wait_scatter(s):
        pltpu.make_async_copy(ybuf_ref.at[s], ys_ref.at[pl.ds(0, EXPERT_ROWS), :], ssems.at[s]).wait()

    @pl.when((j == 0) & (n_used > 0))
    def _():
        gather(0, 0)

    @pl.when((j == 0) | (be_ref[j] != be_ref[jnp.maximum(j - 1, 0)]))
    def _():
        wgb_ref[...] = wg_ref[0].astype(jnp.bfloat16)
        wub_ref[...] = wu_ref[0].astype(jnp.bfloat16)
        wdb_ref[...] = wd_ref[0].astype(jnp.bfloat16)

    @pl.when(j + 1 < n_used)
    def _():
        gather(j + 1, 1 - slot)

    @pl.when((j >= 2) & (j < n_used))
    def _():
        wait_scatter(slot)

    @pl.when(j < n_used)
    def _():
        wait_gather(slot)
        xb = _unpack_rows(xbuf_ref[slot]).astype(jnp.bfloat16)
        hg = jnp.dot(xb, wgb_ref[...], preferred_element_type=jnp.float32)
        hu = jnp.dot(xb, wub_ref[...], preferred_element_type=jnp.float32)
        hid = (hg / (1.0 + jnp.exp(-hg))) * hu
        y = jnp.dot(hid.astype(jnp.bfloat16), wdb_ref[...], preferred_element_type=jnp.float32)
        ybuf_ref[slot] = _pack_rows(y)
        scatter(j, slot)

    @pl.when(j == n_used - 1)
    def _():
        wait_scatter(slot)

    @pl.when((j == n_used - 1) & (j >= 1))
    def _():
        wait_scatter(1 - slot)


def _expert_call(block_expert, n_used, group_src, group_dst, xs_sorted, wg, wu, wd, n_blocks):
    W = xs_sorted.shape[-1]
    RB = EXPERT_ROWS
    grid_spec = pltpu.PrefetchScalarGridSpec(
        num_scalar_prefetch=4, grid=(n_blocks,),
        in_specs=[
            pl.BlockSpec(memory_space=pl.ANY),
            pl.BlockSpec((1, D_MODEL, D_EXPERT), lambda j, be, nu, gs, gd: (be[j], 0, 0)),
            pl.BlockSpec((1, D_MODEL, D_EXPERT), lambda j, be, nu, gs, gd: (be[j], 0, 0)),
            pl.BlockSpec((1, D_EXPERT, D_MODEL), lambda j, be, nu, gs, gd: (be[j], 0, 0)),
        ],
        out_specs=pl.BlockSpec(memory_space=pl.ANY),
        scratch_shapes=[pltpu.VMEM((D_MODEL, D_EXPERT), jnp.bfloat16),
                        pltpu.VMEM((D_MODEL, D_EXPERT), jnp.bfloat16),
                        pltpu.VMEM((D_EXPERT, D_MODEL), jnp.bfloat16),
                        pltpu.VMEM((2, RB, W), jnp.int32),
                        pltpu.VMEM((2, RB, W), jnp.int32),
                        pltpu.SemaphoreType.DMA((2,)),
                        pltpu.SemaphoreType.DMA((2,))],
    )
    vmem = (2 * 4 + 2) * 3 * D_MODEL * D_EXPERT + 4 * RB * W * 4 + 8 * RB * D_MODEL * 4 + (6 << 20)
    return pl.pallas_call(
        _expert_kernel, grid_spec=grid_spec,
        out_shape=jax.ShapeDtypeStruct(xs_sorted.shape, xs_sorted.dtype),
        input_output_aliases={4: 0},
        compiler_params=_cparams(1, vmem), name="moe_experts",
    )(block_expert, n_used, group_src, group_dst, xs_sorted, wg, wu, wd)


def _combine_kernel(ys_ref, pos_ref, gate_ref, h_ref, g2_ref, b2_ref, o_ref):
    R = MIX_ROWS
    y = _unpack_rows(ys_ref[0]).astype(jnp.bfloat16)
    col = lax.broadcasted_iota(jnp.int32, (R, SORTED_ROWS), 1)
    pick = (jnp.where(col == pos_ref[:, 0:1], gate_ref[:, 0:1], 0.0)
            + jnp.where(col == pos_ref[:, 1:2], gate_ref[:, 1:2], 0.0)).astype(jnp.bfloat16)
    ffn = jnp.dot(pick, y, preferred_element_type=jnp.float32)
    o_ref[...] = _layer_norm(DEEPNORM_ALPHA * h_ref[...] + ffn, g2_ref[...], b2_ref[...])


def _combine_call(ys_sorted, pos, gate, h, g2, b2):
    N, D = h.shape
    R = MIX_ROWS
    nt = N // R
    W = ys_sorted.shape[-1]
    return pl.pallas_call(
        _combine_kernel, grid=(nt,),
        in_specs=[
            pl.BlockSpec((1, SORTED_ROWS, W), lambda i: (i, 0, 0)),
            pl.BlockSpec((R, 2), lambda i: (i, 0)),
            pl.BlockSpec((R, 2), lambda i: (i, 0)),
            pl.BlockSpec((R, D), lambda i: (i, 0)),
            pl.BlockSpec((1, D), lambda i: (0, 0)),
            pl.BlockSpec((1, D), lambda i: (0, 0)),
        ],
        out_specs=pl.BlockSpec((R, D), lambda i: (i, 0)),
        out_shape=jax.ShapeDtypeStruct((N, D), jnp.float32),
        compiler_params=_cparams(1, 2 * (SORTED_ROWS * W * 4 + 2 * R * D * 4) + SORTED_ROWS * D * 8
                                 + 3 * R * SORTED_ROWS * 4 + (6 << 20)),
        name="moe_combine",
    )(ys_sorted, pos, gate, h, g2, b2)


def _rope_tables(seq):
    inv_freq = ROPE_THETA ** (-jnp.arange(ROT_HALF, dtype=jnp.float32) / ROT_HALF)
    ang = jnp.arange(seq, dtype=jnp.int32).astype(jnp.float32)[:, None] * inv_freq[None, :]
    cos, sin = jnp.cos(ang), jnp.sin(ang)
    ones = jnp.ones((seq, HEAD_DIM - ROT_DIMS), jnp.float32)
    zeros = jnp.zeros((seq, HEAD_DIM - ROT_DIMS), jnp.float32)
    c_head = jnp.concatenate([cos, cos, ones], axis=1)
    s_head = jnp.concatenate([-sin, sin, zeros], axis=1)
    reps = KN_WIDTH // HEAD_DIM
    return cos.T, sin.T, jnp.tile(c_head, (1, reps)), jnp.tile(s_head, (1, reps))


def _swap_rot_cols(w):
    d, n = w.shape
    wh = w.reshape(d, n // HEAD_DIM, HEAD_DIM)
    sw = jnp.concatenate([wh[:, :, ROT_HALF:ROT_DIMS], wh[:, :, :ROT_HALF],
                          jnp.zeros((d, n // HEAD_DIM, HEAD_DIM - ROT_DIMS), w.dtype)], axis=2)
    return sw.reshape(d, n)


def _block(x, w_in, gate_bias, w_attn_up, w_conv_up, conv_w, w_out, ln_g, ln_b,
           rg_w, rg_b, re_w, re_b, w_gate_e, w_up_e, w_down_e, ln2_g, ln2_b):
    B, S, D = x.shape
    N = B * S
    top_k = min(TOPK_MAX, S // 4)
    bf = jnp.bfloat16
    o = np.cumsum([0, Q_WIDTH, KV_WIDTH, KV_WIDTH, IDXQ_WIDTH, IDX_DIM, IDX_HEADS,
                   CONV_DIM, CONV_DIM, CONV_DIM, N_BRANCHES * D_MODEL])
    w_q, w_k, w_v, w_qi, w_ki, w_wi = (w_in[:, o[i]:o[i + 1]] for i in range(6))
    w_conv = w_in[:, o[6]:o[9]]
    w_gates = w_in[:, o[9]:o[10]]

    wt = jnp.concatenate([w_q.T * (HEAD_DIM ** -0.5 * LOG2_E), w_qi.T, w_v.T, w_wi.T,
                          jnp.zeros((T_ROWS - T_WI0 - IDX_HEADS, D), w_in.dtype)], axis=0).astype(bf)
    pad = jnp.zeros((D, KN_WIDTH - KV_WIDTH - IDX_DIM), w_in.dtype)
    wn = jnp.concatenate([w_k, w_ki, pad, _swap_rot_cols(w_k), _swap_rot_cols(w_ki), pad], axis=1).astype(bf)
    cos_t, sin_t, cos_n, sgn_n = _rope_tables(S)

    qt, qit, vt, wit, kn = _proj_call(x, wt, wn, cos_t, sin_t, cos_n, sgn_n)
    attn = _attn_call(qt, qit, wit, kn, vt, top_k)

    w_r = jnp.concatenate([rg_w, jnp.transpose(re_w, (1, 0, 2)).reshape(D, N_EXPERTS),
                           jnp.zeros((D, ROUTER_LANES - E0 - N_EXPERTS), rg_w.dtype)], axis=1)
    b_r = jnp.concatenate([rg_b, re_b.reshape(-1),
                           jnp.zeros((ROUTER_LANES - E0 - N_EXPERTS,), rg_b.dtype)])[None, :]
    w_rh = w_r.astype(bf)
    w_rl = (w_r - w_rh.astype(jnp.float32)).astype(bf)

    h, xs_sorted, pos, gate, cnt = _mix_call(
        x.reshape(N, D), attn.reshape(N, Q_WIDTH), w_conv.astype(bf), w_gates.astype(bf),
        gate_bias[None, :], conv_w, w_attn_up.astype(bf), w_conv_up.astype(bf), w_out.astype(bf),
        ln_g[None, :], ln_b[None, :], w_rh, w_rl, b_r, S)

    RB, RG = EXPERT_ROWS, ROW_GROUP
    nt = N // MIX_ROWS
    counts = cnt[:, 0, E0:E0 + N_EXPERTS].astype(jnp.int32)
    seg_len = ((counts + RG - 1) // RG) * RG
    seg_local = jnp.cumsum(seg_len, axis=1) - seg_len
    region = jnp.sum(seg_len, axis=0)
    padded = ((region + RB - 1) // RB) * RB
    pad_end = jnp.cumsum(padded)
    pad_start = pad_end - padded
    seg_start = pad_start[None, :] + jnp.cumsum(seg_len, axis=0) - seg_len
    cap = -(-(N * 2 + nt * N_EXPERTS * (RG - 1) + N_EXPERTS * (RB - 1)) // RB) * RB
    nb = cap // RB
    block_start = jnp.arange(nb, dtype=jnp.int32) * RB
    block_expert = jnp.minimum(jnp.sum((block_start[:, None] >= pad_end[None, :]).astype(jnp.int32), axis=1),
                               N_EXPERTS - 1)
    n_used = (pad_end[-1:] // RB).astype(jnp.int32)
    flat_start = seg_start.T.reshape(-1)
    flat_end = flat_start + seg_len.T.reshape(-1)
    step_base = (jnp.arange(nt, dtype=jnp.int32) * SORTED_ROWS)[:, None]
    flat_shift = (step_base + seg_local - seg_start).T.reshape(-1)
    g_idx = jnp.arange(cap // RG, dtype=jnp.int32)
    g_row = g_idx * RG
    ended = flat_end[None, :-1] <= g_row[:, None]

    def lookup(v):
        return v[0] + jnp.sum(jnp.where(ended, (v[1:] - v[:-1])[None, :], 0), axis=1)

    live = (g_row >= lookup(flat_start)) & (g_row < lookup(flat_end))
    home = g_row + lookup(flat_shift)
    spare_per_step = (SORTED_ROWS - SORTED_USED) // RG
    spare = ((g_idx // GROUPS_PER_BLOCK) % 2) * GROUPS_PER_BLOCK + g_idx % GROUPS_PER_BLOCK
    assert 2 * GROUPS_PER_BLOCK <= (nt - 1) * spare_per_step, "not enough spare groups for block padding"
    spare_row = (1 + spare // spare_per_step) * SORTED_ROWS + SORTED_USED + (spare % spare_per_step) * RG
    group_src = jnp.where(live, home, SORTED_ROWS - RG).astype(jnp.int32)
    group_dst = jnp.where(live, home, spare_row).astype(jnp.int32)

    ys_sorted = _expert_call(block_expert, n_used, group_src, group_dst,
                             xs_sorted.reshape(nt * SORTED_ROWS, HALF), w_gate_e, w_up_e, w_down_e, nb)
    out = _combine_call(ys_sorted.reshape(nt, SORTED_ROWS, HALF), pos, gate, h, ln2_g[None, :], ln2_b[None, :])
    return out.reshape(B, S, D)


def kernel(x, w_in, gate_bias, w_attn_up, w_conv_up, conv_w, w_out, ln1_g, ln1_b, router_group_w,
           router_group_b, router_expert_w, router_expert_b, w_gate_e, w_up_e, w_down_e, ln2_g, ln2_b):
    h = x
    for l in range(DEPTH):
        h = _block(h, w_in[l], gate_bias[l], w_attn_up[l], w_conv_up[l], conv_w[l], w_out[l],
                   ln1_g[l], ln1_b[l], router_group_w[l], router_group_b[l], router_expert_w[l],
                   router_expert_b[l], w_gate_e[l], w_up_e[l], w_down_e[l], ln2_g[l], ln2_b[l])
    return h
```

```python
import functools

import jax
import jax.numpy as jnp
import numpy as np
from jax import lax
from jax.experimental import pallas as pl
from jax.experimental.pallas import tpu as pltpu

D_MODEL = 1024
N_HEADS = 8
N_KV_HEADS = 2
HEAD_DIM = 64
Q_WIDTH = N_HEADS * HEAD_DIM
KV_WIDTH = N_KV_HEADS * HEAD_DIM
ROPE_THETA = 500000.0
ROT_DIMS = HEAD_DIM // 4
ROT_HALF = ROT_DIMS // 2
IDX_HEADS = 8
IDX_DIM = 64
IDXQ_WIDTH = IDX_HEADS * IDX_DIM
TOPK_MAX = 256
CONV_DIM = 512
CONV_WIDTH = 3
N_BRANCHES = 2
N_GROUPS = 4
EXPERTS_PER_GROUP = 8
N_EXPERTS = N_GROUPS * EXPERTS_PER_GROUP
D_EXPERT = 512
LN_EPS = 1e-5
DEPTH = 1
DEEPNORM_ALPHA = (2 * DEPTH) ** 0.25

V7X_LANES = 128
V7X_SUBLANES = 8
V7X_VMEM_LIMIT_BYTES = 56 * 1024 * 1024

PROJ_ROWS = 512
ATTN_CHUNK = 256
ATTN_KEY_TILE = 128
DEN_ROWS = 16
ACC_ROWS = HEAD_DIM + DEN_ROWS
LOG2_E = 1.4426950408889634
MIX_ROWS = 512
EXPERT_ROWS = 512
ROW_GROUP = V7X_SUBLANES
SORTED_USED = 2 * MIX_ROWS + N_EXPERTS * (ROW_GROUP - 1)
SORTED_ROWS = -(-(SORTED_USED + ROW_GROUP) // 256) * 256
HALF = D_MODEL // 2

T_Q0, T_QI0, T_V0, T_WI0 = 0, Q_WIDTH, Q_WIDTH + IDXQ_WIDTH, Q_WIDTH + IDXQ_WIDTH + KV_WIDTH
T_ROWS = T_WI0 + 16
KN_WIDTH = 256

INT_MIN = -2147483648
HALF_RANGE = 32768
ROUTER_LANES = 128
E0 = N_GROUPS


def _cparams(n_axes, vmem_bytes):
    return pltpu.CompilerParams(
        dimension_semantics=("arbitrary",) * n_axes,
        vmem_limit_bytes=int(min(vmem_bytes, V7X_VMEM_LIMIT_BYTES)),
    )


def _proj_kernel(x_ref, wt_ref, wn_ref, cost_ref, sint_ref, cosn_ref, sgnn_ref,
                 qt_ref, qit_ref, vt_ref, wit_ref, kn_ref):
    xb = x_ref[0].astype(jnp.bfloat16)
    pt = lax.dot_general(wt_ref[...], xb, (((1,), (1,)), ((), ())),
                         preferred_element_type=jnp.float32)
    cos = cost_ref[...]
    sin = sint_ref[...]

    def rope_t(dst_ref, base):
        for h in range(N_HEADS):
            r0 = base + h * HEAD_DIM
            x1 = pt[r0:r0 + ROT_HALF]
            x2 = pt[r0 + ROT_HALF:r0 + ROT_DIMS]
            head = jnp.concatenate([x1 * cos - x2 * sin, x2 * cos + x1 * sin,
                                    pt[r0 + ROT_DIMS:r0 + HEAD_DIM]], axis=0).astype(dst_ref.dtype)
            for jj, cols in enumerate(chunks):
                dst_ref[0, jj, h * HEAD_DIM:(h + 1) * HEAD_DIM, :] = head[:, cols]

    chunks = [slice(jj * ATTN_CHUNK, (jj + 1) * ATTN_CHUNK) for jj in range(PROJ_ROWS // ATTN_CHUNK)]
    rope_t(qt_ref, T_Q0)
    rope_t(qit_ref, T_QI0)
    for jj, cols in enumerate(chunks):
        vt_ref[0, jj] = pt[T_V0:T_V0 + KV_WIDTH, cols].astype(vt_ref.dtype)
        wit_ref[0, jj] = pt[T_WI0:T_WI0 + IDX_HEADS, cols]
    pn = jnp.dot(xb, wn_ref[...], preferred_element_type=jnp.float32)
    kn = pn[:, :KN_WIDTH] * cosn_ref[...] + pn[:, KN_WIDTH:] * sgnn_ref[...]
    kn_ref[0] = kn.astype(kn_ref.dtype)


def _proj_call(x, wt, wn, cos_t, sin_t, cos_n, sgn_n):
    B, S, D = x.shape
    R = PROJ_ROWS
    nt = S // R
    grid = (B, nt)
    out_shape = (
        jax.ShapeDtypeStruct((B, S // ATTN_CHUNK, Q_WIDTH, ATTN_CHUNK), jnp.bfloat16),
        jax.ShapeDtypeStruct((B, S // ATTN_CHUNK, IDXQ_WIDTH, ATTN_CHUNK), jnp.bfloat16),
        jax.ShapeDtypeStruct((B, S // ATTN_CHUNK, KV_WIDTH, ATTN_CHUNK), jnp.bfloat16),
        jax.ShapeDtypeStruct((B, S // ATTN_CHUNK, IDX_HEADS, ATTN_CHUNK), jnp.float32),
        jax.ShapeDtypeStruct((B, S, KN_WIDTH), jnp.bfloat16),
    )
    in_specs = [
        pl.BlockSpec((1, R, D), lambda b, j: (b, j, 0)),
        pl.BlockSpec((T_ROWS, D), lambda b, j: (0, 0)),
        pl.BlockSpec((D, 2 * KN_WIDTH), lambda b, j: (0, 0)),
        pl.BlockSpec((ROT_HALF, R), lambda b, j: (0, j)),
        pl.BlockSpec((ROT_HALF, R), lambda b, j: (0, j)),
        pl.BlockSpec((R, KN_WIDTH), lambda b, j: (j, 0)),
        pl.BlockSpec((R, KN_WIDTH), lambda b, j: (j, 0)),
    ]
    out_specs = (
        pl.BlockSpec((1, R // ATTN_CHUNK, Q_WIDTH, ATTN_CHUNK), lambda b, j: (b, j, 0, 0)),
        pl.BlockSpec((1, R // ATTN_CHUNK, IDXQ_WIDTH, ATTN_CHUNK), lambda b, j: (b, j, 0, 0)),
        pl.BlockSpec((1, R // ATTN_CHUNK, KV_WIDTH, ATTN_CHUNK), lambda b, j: (b, j, 0, 0)),
        pl.BlockSpec((1, R // ATTN_CHUNK, IDX_HEADS, ATTN_CHUNK), lambda b, j: (b, j, 0, 0)),
        pl.BlockSpec((1, R, KN_WIDTH), lambda b, j: (b, j, 0)),
    )
    vmem = 2 * (R * D * 4 + T_ROWS * D * 2 + D * 2 * KN_WIDTH * 2) + 6 * T_ROWS * R * 4 + (8 << 20)
    return pl.pallas_call(
        _proj_kernel, grid=grid, in_specs=in_specs, out_specs=out_specs, out_shape=out_shape,
        compiler_params=_cparams(2, vmem), name="dsa_proj",
    )(x, wt, wn, cos_t, sin_t, cos_n, sgn_n)


def _float_to_key(s):
    b = lax.bitcast_convert_type(s, jnp.int32)
    k = b ^ (lax.shift_right_arithmetic(b, 31) & jnp.int32(0x7FFFFFFF))
    return jnp.where(b == jnp.int32(INT_MIN), jnp.int32(0), k)


def _attn_kernel(qt_ref, qit_ref, wit_ref, kn_ref, vt_ref, o_ref,
                 key_ref, hi_ref, lo_ref, lga_ref, lgb_ref, acc_ref, m_ref, *, seq, top_k):
    C = ATTN_CHUNK
    c = pl.program_id(1)
    nkb = c + 1
    zeros_half = jnp.zeros((HEAD_DIM, C), jnp.bfloat16)
    row_i = lax.broadcasted_iota(jnp.int32, (C, C), 0)
    lane_i = lax.broadcasted_iota(jnp.int32, (C, C), 1)
    causal_in_block = row_i <= lane_i

    def score_block(kb):
        r0 = pl.multiple_of(kb * C, C)
        kix = kn_ref[0, pl.ds(r0, C), KV_WIDTH:KN_WIDTH]
        acc = None
        for h in range(IDX_HEADS):
            rhs = jnp.concatenate([qit_ref[0, 0, h * IDX_DIM:(h + 1) * IDX_DIM, :], zeros_half], axis=0)
            s = jnp.dot(kix, rhs, preferred_element_type=jnp.float32)
            t = jnp.maximum(s, 0.0) * wit_ref[0, 0, h:h + 1, :]
            acc = t if acc is None else acc + t
        score = acc * (IDX_DIM ** -0.5 * IDX_HEADS ** -0.5)
        keys = jnp.where(causal_in_block | (kb != c), _float_to_key(score), jnp.int32(INT_MIN))
        key_ref[kb] = keys
        hi_ref[kb] = lax.shift_right_arithmetic(keys, 16).astype(jnp.int16)
        lo_ref[kb] = ((keys & jnp.int32(0xFFFF)) - jnp.int32(HALF_RANGE)).astype(jnp.int16)

    def score_body(pair, carry):
        score_block(2 * pair)
        score_block(2 * pair + 1)
        return carry

    lax.fori_loop(0, nkb // 2, score_body, 0)

    @pl.when(nkb % 2 == 1)
    def _():
        score_block(nkb - 1)

    def count(ref, pred, blocks=None):
        packed = ref.dtype == jnp.int16

        def body(kb, part):
            hit = pred(ref[kb], kb)
            if packed:
                words = pltpu.bitcast(jnp.where(hit, jnp.int16(1), jnp.int16(0)), jnp.int32)
            else:
                words = hit.astype(jnp.int32)
            return part + jnp.sum(words.reshape(-1, V7X_SUBLANES, C), axis=0)

        part = jnp.zeros((V7X_SUBLANES, C), jnp.int32)
        if blocks is None:
            part = lax.fori_loop(0, nkb, body, part)
        else:
            for kb in range(blocks):
                part = body(kb, part)
        if packed:
            part = (part & jnp.int32(0xFFFF)) + lax.shift_right_logical(part, 16)
        return jnp.sum(part, axis=0, keepdims=True)

    def kth_largest_16(ref, blocks):
        def body(i, ans_u):
            cand_u = ans_u | lax.shift_left(jnp.int32(1), jnp.int32(15) - i)
            cand = (cand_u - jnp.int32(HALF_RANGE)).astype(jnp.int16)
            return jnp.where(count(ref, lambda k, kb: k >= cand, blocks) >= top_k, cand_u, ans_u)

        return lax.fori_loop(0, 16, body, jnp.zeros((1, C), jnp.int32)) - jnp.int32(HALF_RANGE)

    def threshold(blocks):
        t_hi = kth_largest_16(hi_ref, blocks)
        t_hi16 = t_hi.astype(jnp.int16)
        for kb in range(blocks):
            hi = hi_ref[kb]
            lo_ref[kb] = jnp.where(hi > t_hi16, jnp.int16(HALF_RANGE - 1),
                                   jnp.where(hi == t_hi16, lo_ref[kb], jnp.int16(-HALF_RANGE)))
        t_lo = kth_largest_16(lo_ref, blocks)
        t_lo16 = t_lo.astype(jnp.int16)
        return t_hi, t_lo, count(lo_ref, lambda k, kb: k >= t_lo16, blocks)

    t_hi, t_lo, n_ge = lax.switch(c, [functools.partial(threshold, n + 1) for n in range(seq // C)])
    thr_raw = t_hi * jnp.int32(2 * HALF_RANGE) + (t_lo + jnp.int32(HALF_RANGE))
    thr = jnp.maximum(thr_raw, jnp.int32(INT_MIN + 1))

    has_ties = jnp.max(jnp.where(thr_raw != jnp.int32(INT_MIN), n_ge, 0)) > top_k

    @pl.when(has_ties)
    def _():
        need = (top_k - count(key_ref, lambda k, kb: k > thr)).astype(jnp.float32)
        earlier_rows = (row_i > lane_i).astype(jnp.bfloat16)

        def drop_body(kb, seen):
            kk = key_ref[kb]
            tied = kk == thr
            tied_b = jnp.where(tied, 1.0, 0.0).astype(jnp.bfloat16)
            rank = jnp.dot(earlier_rows, tied_b, preferred_element_type=jnp.float32) + seen
            key_ref[kb] = jnp.where(tied & (rank >= need), kk - 1, kk)
            return seen + jnp.sum(tied_b.astype(jnp.float32), axis=0, keepdims=True)

        lax.fori_loop(0, nkb, drop_body, jnp.zeros((1, C), jnp.float32))

    neg_inf = jnp.float32(-jnp.inf)
    m_ref[...] = jnp.full(m_ref.shape, neg_inf, jnp.float32)
    acc_ref[...] = jnp.zeros(acc_ref.shape, jnp.float32)
    T = ATTN_KEY_TILE
    ones_rows = jnp.ones((DEN_ROWS, C), jnp.bfloat16)

    kv_group = N_HEADS // N_KV_HEADS

    def bias_body(kb, carry):
        key_ref[kb] = lax.bitcast_convert_type(jnp.where(key_ref[kb] >= thr, 0.0, neg_inf), jnp.int32)
        return carry

    lax.fori_loop(0, nkb, bias_body, 0)

    def store_logits(kb, dst_ref, h):
        kb = jnp.minimum(kb, nkb - 1)
        r0 = pl.multiple_of(kb * C, C)
        bias = lax.bitcast_convert_type(key_ref[kb], jnp.float32)
        k2 = kn_ref[0, pl.ds(r0, C), 0:KV_WIDTH]
        qh = qt_ref[0, 0, h * HEAD_DIM:(h + 1) * HEAD_DIM, :]
        rhs = jnp.concatenate([qh, zeros_half] if h < kv_group else [zeros_half, qh], axis=0)
        dst_ref[h] = jnp.dot(k2, rhs, preferred_element_type=jnp.float32) + bias

    def softmax_block(kb, src_ref, dst_ref):
        for h in range(N_HEADS):
            if dst_ref is not None:
                store_logits(kb + 1, dst_ref, h)
            g = h // kv_group
            m_old = m_ref[h:h + 1, :]
            m_new = jnp.maximum(m_old, jnp.max(src_ref[h], axis=0, keepdims=True))
            m_safe = jnp.where(m_new == neg_inf, 0.0, m_new)
            p = jnp.concatenate(
                [jnp.exp2(src_ref[h, s * T:(s + 1) * T, :] - m_safe).astype(jnp.bfloat16) for s in range(C // T)],
                axis=0)
            alpha = jnp.exp2(m_old - m_safe)
            vt = vt_ref[0, kb, g * HEAD_DIM:(g + 1) * HEAD_DIM, :]
            pv = jnp.dot(jnp.concatenate([vt, ones_rows], axis=0), p,
                         preferred_element_type=jnp.float32)
            hs = slice(h * ACC_ROWS, (h + 1) * ACC_ROWS)
            acc_ref[hs, :] = alpha * acc_ref[hs, :] + pv
            m_ref[h:h + 1, :] = m_new

    for h in range(N_HEADS):
        store_logits(0, lga_ref, h)

    def attn_body(pair, carry):
        kb = 2 * pair
        softmax_block(kb, lga_ref, lgb_ref)
        softmax_block(kb + 1, lgb_ref, lga_ref)
        return carry

    lax.fori_loop(0, nkb // 2, attn_body, 0)

    @pl.when(nkb % 2 == 1)
    def _():
        softmax_block(nkb - 1, lga_ref, None)
    outs = []
    for h in range(N_HEADS):
        num = acc_ref[h * ACC_ROWS:h * ACC_ROWS + HEAD_DIM, :]
        den = acc_ref[h * ACC_ROWS + HEAD_DIM:h * ACC_ROWS + HEAD_DIM + 1, :]
        outs.append(num / den)
    o_ref[0] = jnp.transpose(jnp.concatenate(outs, axis=0)).astype(o_ref.dtype)


def _attn_call(qt, qit, wit, kn, vt, top_k):
    B, S = kn.shape[:2]
    C = ATTN_CHUNK
    nc = S // C
    kern = functools.partial(_attn_kernel, seq=S, top_k=top_k)
    in_specs = [
        pl.BlockSpec((1, 1, Q_WIDTH, C), lambda b, c: (b, c, 0, 0)),
        pl.BlockSpec((1, 1, IDXQ_WIDTH, C), lambda b, c: (b, c, 0, 0)),
        pl.BlockSpec((1, 1, IDX_HEADS, C), lambda b, c: (b, c, 0, 0)),
        pl.BlockSpec((1, S, KN_WIDTH), lambda b, c: (b, 0, 0)),
        pl.BlockSpec((1, nc, KV_WIDTH, C), lambda b, c: (b, 0, 0, 0)),
    ]
    out_specs = pl.BlockSpec((1, C, Q_WIDTH), lambda b, c: (b, c, 0))
    scratch = [
        pltpu.VMEM((nc, C, C), jnp.int32),
        pltpu.VMEM((nc, C, C), jnp.int16),
        pltpu.VMEM((nc, C, C), jnp.int16),
        pltpu.VMEM((N_HEADS, C, C), jnp.float32),
        pltpu.VMEM((N_HEADS, C, C), jnp.float32),
        pltpu.VMEM((N_HEADS * ACC_ROWS, C), jnp.float32),
        pltpu.VMEM((N_HEADS, C), jnp.float32),
    ]
    vmem = (2 * S * C * 4 + Q_WIDTH * C * 4 + 2 * 2 * (2 * Q_WIDTH * C + S * KN_WIDTH + S * KV_WIDTH + C * Q_WIDTH)
            + 24 * C * C * 4 + (8 << 20))
    return pl.pallas_call(
        kern, grid=(B, nc), in_specs=in_specs, out_specs=out_specs,
        out_shape=jax.ShapeDtypeStruct((B, S, Q_WIDTH), jnp.bfloat16),
        scratch_shapes=scratch, compiler_params=_cparams(2, vmem), name="dsa_attn",
    )(qt, qit, wit, kn, vt)


def _layer_norm(v, g, b):
    mu = jnp.mean(v, axis=-1, keepdims=True)
    d = v - mu
    var = jnp.mean(d * d, axis=-1, keepdims=True)
    return d * lax.rsqrt(var + LN_EPS) * g + b


def _pack_rows(h):
    hi = lax.bitcast_convert_type(h[:, :HALF].astype(jnp.bfloat16).astype(jnp.float32), jnp.int32)
    lo = lax.bitcast_convert_type(h[:, HALF:].astype(jnp.bfloat16).astype(jnp.float32), jnp.int32)
    return (hi & jnp.int32(-65536)) | lax.shift_right_logical(lo, 16)


def _unpack_rows(w):
    hi = lax.bitcast_convert_type(w & jnp.int32(-65536), jnp.float32)
    lo = lax.bitcast_convert_type(lax.shift_left(w, 16), jnp.float32)
    return jnp.concatenate([hi, lo], axis=1)


def _mix_kernel(x_ref, attn_ref, wc_ref, wg_ref, gb_ref, cw_ref, wau_ref, wcu_ref, wo_ref,
                g1_ref, b1_ref, wrh_ref, wrl_ref, rb_ref,
                h_ref, xs_ref, pos_ref, gate_ref, cnt_ref,
                ubuf_ref, *, steps_per_seq):
    R = MIX_ROWS
    i = pl.program_id(0)

    @pl.when(i % steps_per_seq == 0)
    def _():
        ubuf_ref[0:V7X_SUBLANES, :] = jnp.zeros((V7X_SUBLANES, CONV_DIM), jnp.float32)

    x = x_ref[...]
    xb = x.astype(jnp.bfloat16)
    cv = jnp.dot(xb, wc_ref[...], preferred_element_type=jnp.float32)
    u = cv[:, 2 * CONV_DIM:] * cv[:, :CONV_DIM]
    ubuf_ref[V7X_SUBLANES:, :] = u
    u1 = ubuf_ref[V7X_SUBLANES - 1:V7X_SUBLANES - 1 + R, :]
    u2 = ubuf_ref[V7X_SUBLANES - 2:V7X_SUBLANES - 2 + R, :]
    y = cw_ref[0:1, :] * u2 + cw_ref[1:2, :] * u1 + cw_ref[2:3, :] * u
    conv = (cv[:, CONV_DIM:2 * CONV_DIM] * y).astype(jnp.bfloat16)
    ubuf_ref[0:V7X_SUBLANES, :] = u[R - V7X_SUBLANES:, :]
    z = jnp.dot(xb, wg_ref[...], preferred_element_type=jnp.float32) + gb_ref[...]
    gates = 1.0 / (1.0 + jnp.exp(-z))
    au = jnp.dot(attn_ref[...], wau_ref[...], preferred_element_type=jnp.float32)
    cu = jnp.dot(conv, wcu_ref[...], preferred_element_type=jnp.float32)
    merged = gates[:, :D_MODEL] * au + gates[:, D_MODEL:] * cu
    mix = jnp.dot(merged.astype(jnp.bfloat16), wo_ref[...], preferred_element_type=jnp.float32)
    h = _layer_norm(DEEPNORM_ALPHA * x + mix, g1_ref[...], b1_ref[...])
    h_ref[...] = h

    h_hi = h.astype(jnp.bfloat16)
    h_lo = (h - h_hi.astype(jnp.float32)).astype(jnp.bfloat16)
    lg = (jnp.dot(h_hi, wrh_ref[...], preferred_element_type=jnp.float32)
          + jnp.dot(h_lo, wrh_ref[...], preferred_element_type=jnp.float32)
          + jnp.dot(h_hi, wrl_ref[...], preferred_element_type=jnp.float32)) + rb_ref[...]
    lane = lax.broadcasted_iota(jnp.int32, (R, ROUTER_LANES), 1).astype(jnp.float32)
    neg = jnp.float32(-jnp.inf)
    no_lane = jnp.float32(ROUTER_LANES)
    gmask = lane < N_GROUPS
    gl = jnp.where(gmask, lg, neg)
    gmax = jnp.max(gl, axis=1, keepdims=True)
    grp = jnp.min(jnp.where(gl == gmax, lane, no_lane), axis=1, keepdims=True)
    gsum = jnp.sum(jnp.where(gmask, jnp.exp(gl - gmax), 0.0), axis=1, keepdims=True)
    p_grp = 1.0 / gsum
    lo_lane = E0 + grp * EXPERTS_PER_GROUP
    emask = (lane >= lo_lane) & (lane < lo_lane + EXPERTS_PER_GROUP)
    el = jnp.where(emask, lg, neg)
    v1 = jnp.max(el, axis=1, keepdims=True)
    i1 = jnp.min(jnp.where(el == v1, lane, no_lane), axis=1, keepdims=True)
    el2 = jnp.where(lane == i1, neg, el)
    v2 = jnp.max(el2, axis=1, keepdims=True)
    i2 = jnp.min(jnp.where(el2 == v2, lane, no_lane), axis=1, keepdims=True)
    a = jnp.exp(v2 - v1)
    inv = 1.0 / (1.0 + a)
    gate_ref[:, 0:1] = p_grp * inv
    gate_ref[:, 1:2] = p_grp * (a * inv)
    oh1 = lane == i1
    oh2 = lane == i2
    oh = (oh1 | oh2).astype(jnp.bfloat16)
    r_i = lax.broadcasted_iota(jnp.int32, (R, R), 0)
    c_i = lax.broadcasted_iota(jnp.int32, (R, R), 1)
    tri = (r_i > c_i).astype(jnp.bfloat16)
    before = jnp.dot(tri, oh, preferred_element_type=jnp.float32)
    cnt = jnp.sum(oh.astype(jnp.float32), axis=0, keepdims=True)
    cnt_ref[0] = cnt

    seg = jnp.floor((cnt + (ROW_GROUP - 1)) * (1.0 / ROW_GROUP)) * ROW_GROUP
    l_r = lax.broadcasted_iota(jnp.int32, (ROUTER_LANES, ROUTER_LANES), 0)
    l_c = lax.broadcasted_iota(jnp.int32, (ROUTER_LANES, ROUTER_LANES), 1)
    lanes_before = (l_r < l_c).astype(jnp.bfloat16)
    seg8 = jnp.broadcast_to(seg, (V7X_SUBLANES, ROUTER_LANES)).astype(jnp.bfloat16)
    seg_off = jnp.dot(seg8, lanes_before, preferred_element_type=jnp.float32)[0:1, :]
    where_to = before + seg_off
    pos1 = jnp.sum(jnp.where(oh1, where_to, 0.0), axis=1, keepdims=True)
    pos2 = jnp.sum(jnp.where(oh2, where_to, 0.0), axis=1, keepdims=True)
    pos_ref[:, 0:1] = pos1.astype(jnp.int32)
    pos_ref[:, 1:2] = pos2.astype(jnp.int32)
    diag = r_i == c_i
    pos1_row = jnp.sum(jnp.where(diag, pos1, 0.0), axis=0, keepdims=True)
    pos2_row = jnp.sum(jnp.where(diag, pos2, 0.0), axis=0, keepdims=True)
    sorted_row = lax.broadcasted_iota(jnp.int32, (SORTED_ROWS, R), 0).astype(jnp.float32)
    place = ((sorted_row == pos1_row) | (sorted_row == pos2_row)).astype(jnp.bfloat16)
    xs = jnp.dot(place, h_hi, preferred_element_type=jnp.float32)
    xs_ref[0] = _pack_rows(xs)


def _mix_call(x2, attn2, wc, wg, gb, cw, wau, wcu, wo, g1, b1, wrh, wrl, rb, seq):
    N, D = x2.shape
    R = MIX_ROWS
    nt = N // R
    kern = functools.partial(_mix_kernel, steps_per_seq=seq // R)

    def full(a):
        return pl.BlockSpec(a.shape, lambda i: (0,) * a.ndim)

    in_specs = [
        pl.BlockSpec((R, D), lambda i: (i, 0)),
        pl.BlockSpec((R, Q_WIDTH), lambda i: (i, 0)),
        full(wc), full(wg), full(gb), full(cw), full(wau), full(wcu), full(wo),
        full(g1), full(b1), full(wrh), full(wrl), full(rb),
    ]
    out_shape = (
        jax.ShapeDtypeStruct((N, D), jnp.float32),
        jax.ShapeDtypeStruct((nt, SORTED_ROWS, HALF), jnp.int32),
        jax.ShapeDtypeStruct((N, 2), jnp.int32),
        jax.ShapeDtypeStruct((N, 2), jnp.float32),
        jax.ShapeDtypeStruct((nt, 1, ROUTER_LANES), jnp.float32),
    )
    out_specs = (
        pl.BlockSpec((R, D), lambda i: (i, 0)),
        pl.BlockSpec((1, SORTED_ROWS, HALF), lambda i: (i, 0, 0)),
        pl.BlockSpec((R, 2), lambda i: (i, 0)),
        pl.BlockSpec((R, 2), lambda i: (i, 0)),
        pl.BlockSpec((1, 1, ROUTER_LANES), lambda i: (i, 0, 0)),
    )
    scratch = [pltpu.VMEM((R + V7X_SUBLANES, CONV_DIM), jnp.float32)]
    w_bytes = 2 * (wc.size + wg.size + wau.size + wcu.size + wo.size + wrh.size + wrl.size)
    vmem = (2 * w_bytes + 2 * (R * D * 4 * 2 + R * Q_WIDTH * 2 + SORTED_ROWS * HALF * 4) + 10 * R * 2048 * 4
            + SORTED_ROWS * D * 6 + (6 << 20))
    return pl.pallas_call(
        kern, grid=(nt,), in_specs=in_specs, out_specs=out_specs, out_shape=out_shape,
        scratch_shapes=scratch, compiler_params=_cparams(1, vmem), name="mix_ln_router",
    )(x2, attn2, wc, wg, gb, cw, wau, wcu, wo, g1, b1, wrh, wrl, rb)


GROUPS_PER_BLOCK = EXPERT_ROWS // ROW_GROUP


def _expert_kernel(be_ref, nb_ref, gsrc_ref, gdst_ref, xs_ref, wg_ref, wu_ref, wd_ref, ys_ref,
                   wgb_ref, wub_ref, wdb_ref, xbuf_ref, ybuf_ref, gsems, ssems):
    del xs_ref
    j = pl.program_id(0)
    n_used = nb_ref[0]
    slot = j % 2

    def gather(block, to_slot):
        for g in range(GROUPS_PER_BLOCK):
            row = pl.multiple_of(gsrc_ref[block * GROUPS_PER_BLOCK + g], ROW_GROUP)
            pltpu.make_async_copy(ys_ref.at[pl.ds(row, ROW_GROUP), :],
                                  xbuf_ref.at[to_slot, pl.ds(g * ROW_GROUP, ROW_GROUP), :],
                                  gsems.at[to_slot]).start(priority=g % 2)

    def scatter(block, from_slot):
        for g in range(GROUPS_PER_BLOCK):
            row = pl.multiple_of(gdst_ref[block * GROUPS_PER_BLOCK + g], ROW_GROUP)
            pltpu.make_async_copy(ybuf_ref.at[from_slot, pl.ds(g * ROW_GROUP, ROW_GROUP), :],
                                  ys_ref.at[pl.ds(row, ROW_GROUP), :],
                                  ssems.at[from_slot]).start(priority=g % 2)

    def wait_gather(s):
        pltpu.make_async_copy(ys_ref.at[pl.ds(0, EXPERT_ROWS), :], xbuf_ref.at[s], gsems.at[s]).wait()

    def wait_scatter(s):
        pltpu.make_async_copy(ybuf_ref.at[s], ys_ref.at[pl.ds(0, EXPERT_ROWS), :], ssems.at[s]).wait()

    @pl.when((j == 0) & (n_used > 0))
    def _():
        gather(0, 0)

    @pl.when((j == 0) | (be_ref[j] != be_ref[jnp.maximum(j - 1, 0)]))
    def _():
        wgb_ref[...] = wg_ref[0].astype(jnp.bfloat16)
        wub_ref[...] = wu_ref[0].astype(jnp.bfloat16)
        wdb_ref[...] = wd_ref[0].astype(jnp.bfloat16)

    @pl.when(j + 1 < n_used)
    def _():
        gather(j + 1, 1 - slot)

    @pl.when((j >= 2) & (j < n_used))
    def _():
        wait_scatter(slot)

    @pl.when(j < n_used)
    def _():
        wait_gather(slot)
        xb = _unpack_rows(xbuf_ref[slot]).astype(jnp.bfloat16)
        hg = jnp.dot(xb, wgb_ref[...], preferred_element_type=jnp.float32)
        hu = jnp.dot(xb, wub_ref[...], preferred_element_type=jnp.float32)
        hid = (hg / (1.0 + jnp.exp(-hg))) * hu
        y = jnp.dot(hid.astype(jnp.bfloat16), wdb_ref[...], preferred_element_type=jnp.float32)
        ybuf_ref[slot] = _pack_rows(y)
        scatter(j, slot)

    @pl.when(j == n_used - 1)
    def _():
        wait_scatter(slot)

    @pl.when((j == n_used - 1) & (j >= 1))
    def _():
        wait_scatter(1 - slot)


def _expert_call(block_expert, n_used, group_src, group_dst, xs_sorted, wg, wu, wd, n_blocks):
    W = xs_sorted.shape[-1]
    RB = EXPERT_ROWS
    grid_spec = pltpu.PrefetchScalarGridSpec(
        num_scalar_prefetch=4, grid=(n_blocks,),
        in_specs=[
            pl.BlockSpec(memory_space=pl.ANY),
            pl.BlockSpec((1, D_MODEL, D_EXPERT), lambda j, be, nu, gs, gd: (be[j], 0, 0)),
            pl.BlockSpec((1, D_MODEL, D_EXPERT), lambda j, be, nu, gs, gd: (be[j], 0, 0)),
            pl.BlockSpec((1, D_EXPERT, D_MODEL), lambda j, be, nu, gs, gd: (be[j], 0, 0)),
        ],
        out_specs=pl.BlockSpec(memory_space=pl.ANY),
        scratch_shapes=[pltpu.VMEM((D_MODEL, D_EXPERT), jnp.bfloat16),
                        pltpu.VMEM((D_MODEL, D_EXPERT), jnp.bfloat16),
                        pltpu.VMEM((D_EXPERT, D_MODEL), jnp.bfloat16),
                        pltpu.VMEM((2, RB, W), jnp.int32),
                        pltpu.VMEM((2, RB, W), jnp.int32),
                        pltpu.SemaphoreType.DMA((2,)),
                        pltpu.SemaphoreType.DMA((2,))],
    )
    vmem = (2 * 4 + 2) * 3 * D_MODEL * D_EXPERT + 4 * RB * W * 4 + 8 * RB * D_MODEL * 4 + (6 << 20)
    return pl.pallas_call(
        _expert_kernel, grid_spec=grid_spec,
        out_shape=jax.ShapeDtypeStruct(xs_sorted.shape, xs_sorted.dtype),
        input_output_aliases={4: 0},
        compiler_params=_cparams(1, vmem), name="moe_experts",
    )(block_expert, n_used, group_src, group_dst, xs_sorted, wg, wu, wd)


def _combine_kernel(ys_ref, pos_ref, gate_ref, h_ref, g2_ref, b2_ref, o_ref):
    R = MIX_ROWS
    y = _unpack_rows(ys_ref[0]).astype(jnp.bfloat16)
    col = lax.broadcasted_iota(jnp.int32, (R, SORTED_ROWS), 1)
    pick = (jnp.where(col == pos_ref[:, 0:1], gate_ref[:, 0:1], 0.0)
            + jnp.where(col == pos_ref[:, 1:2], gate_ref[:, 1:2], 0.0)).astype(jnp.bfloat16)
    ffn = jnp.dot(pick, y, preferred_element_type=jnp.float32)
    o_ref[...] = _layer_norm(DEEPNORM_ALPHA * h_ref[...] + ffn, g2_ref[...], b2_ref[...])


def _combine_call(ys_sorted, pos, gate, h, g2, b2):
    N, D = h.shape
    R = MIX_ROWS
    nt = N // R
    W = ys_sorted.shape[-1]
    return pl.pallas_call(
        _combine_kernel, grid=(nt,),
        in_specs=[
            pl.BlockSpec((1, SORTED_ROWS, W), lambda i: (i, 0, 0)),
            pl.BlockSpec((R, 2), lambda i: (i, 0)),
            pl.BlockSpec((R, 2), lambda i: (i, 0)),
            pl.BlockSpec((R, D), lambda i: (i, 0)),
            pl.BlockSpec((1, D), lambda i: (0, 0)),
            pl.BlockSpec((1, D), lambda i: (0, 0)),
        ],
        out_specs=pl.BlockSpec((R, D), lambda i: (i, 0)),
        out_shape=jax.ShapeDtypeStruct((N, D), jnp.float32),
        compiler_params=_cparams(1, 2 * (SORTED_ROWS * W * 4 + 2 * R * D * 4) + SORTED_ROWS * D * 8
                                 + 3 * R * SORTED_ROWS * 4 + (6 << 20)),
        name="moe_combine",
    )(ys_sorted, pos, gate, h, g2, b2)


def _rope_tables(seq):
    inv_freq = ROPE_THETA ** (-jnp.arange(ROT_HALF, dtype=jnp.float32) / ROT_HALF)
    ang = jnp.arange(seq, dtype=jnp.int32).astype(jnp.float32)[:, None] * inv_freq[None, :]
    cos, sin = jnp.cos(ang), jnp.sin(ang)
    ones = jnp.ones((seq, HEAD_DIM - ROT_DIMS), jnp.float32)
    zeros = jnp.zeros((seq, HEAD_DIM - ROT_DIMS), jnp.float32)
    c_head = jnp.concatenate([cos, cos, ones], axis=1)
    s_head = jnp.concatenate([-sin, sin, zeros], axis=1)
    reps = KN_WIDTH // HEAD_DIM
    return cos.T, sin.T, jnp.tile(c_head, (1, reps)), jnp.tile(s_head, (1, reps))


def _swap_rot_cols(w):
    d, n = w.shape
    wh = w.reshape(d, n // HEAD_DIM, HEAD_DIM)
    sw = jnp.concatenate([wh[:, :, ROT_HALF:ROT_DIMS], wh[:, :, :ROT_HALF],
                          jnp.zeros((d, n // HEAD_DIM, HEAD_DIM - ROT_DIMS), w.dtype)], axis=2)
    return sw.reshape(d, n)


def _block(x, w_in, gate_bias, w_attn_up, w_conv_up, conv_w, w_out, ln_g, ln_b,
           rg_w, rg_b, re_w, re_b, w_gate_e, w_up_e, w_down_e, ln2_g, ln2_b):
    B, S, D = x.shape
    N = B * S
    top_k = min(TOPK_MAX, S // 4)
    bf = jnp.bfloat16
    o = np.cumsum([0, Q_WIDTH, KV_WIDTH, KV_WIDTH, IDXQ_WIDTH, IDX_DIM, IDX_HEADS,
                   CONV_DIM, CONV_DIM, CONV_DIM, N_BRANCHES * D_MODEL])
    w_q, w_k, w_v, w_qi, w_ki, w_wi = (w_in[:, o[i]:o[i + 1]] for i in range(6))
    w_conv = w_in[:, o[6]:o[9]]
    w_gates = w_in[:, o[9]:o[10]]

    wt = jnp.concatenate([w_q.T * (HEAD_DIM ** -0.5 * LOG2_E), w_qi.T, w_v.T, w_wi.T,
                          jnp.zeros((T_ROWS - T_WI0 - IDX_HEADS, D), w_in.dtype)], axis=0).astype(bf)
    pad = jnp.zeros((D, KN_WIDTH - KV_WIDTH - IDX_DIM), w_in.dtype)
    wn = jnp.concatenate([w_k, w_ki, pad, _swap_rot_cols(w_k), _swap_rot_cols(w_ki), pad], axis=1).astype(bf)
    cos_t, sin_t, cos_n, sgn_n = _rope_tables(S)

    qt, qit, vt, wit, kn = _proj_call(x, wt, wn, cos_t, sin_t, cos_n, sgn_n)
    attn = _attn_call(qt, qit, wit, kn, vt, top_k)

    w_r = jnp.concatenate([rg_w, jnp.transpose(re_w, (1, 0, 2)).reshape(D, N_EXPERTS),
                           jnp.zeros((D, ROUTER_LANES - E0 - N_EXPERTS), rg_w.dtype)], axis=1)
    b_r = jnp.concatenate([rg_b, re_b.reshape(-1),
                           jnp.zeros((ROUTER_LANES - E0 - N_EXPERTS,), rg_b.dtype)])[None, :]
    w_rh = w_r.astype(bf)
    w_rl = (w_r - w_rh.astype(jnp.float32)).astype(bf)

    h, xs_sorted, pos, gate, cnt = _mix_call(
        x.reshape(N, D), attn.reshape(N, Q_WIDTH), w_conv.astype(bf), w_gates.astype(bf),
        gate_bias[None, :], conv_w, w_attn_up.astype(bf), w_conv_up.astype(bf), w_out.astype(bf),
        ln_g[None, :], ln_b[None, :], w_rh, w_rl, b_r, S)

    RB, RG = EXPERT_ROWS, ROW_GROUP
    nt = N // MIX_ROWS
    counts = cnt[:, 0, E0:E0 + N_EXPERTS].astype(jnp.int32)
    seg_len = ((counts + RG - 1) // RG) * RG
    seg_local = jnp.cumsum(seg_len, axis=1) - seg_len
    region = jnp.sum(seg_len, axis=0)
    padded = ((region + RB - 1) // RB) * RB
    pad_end = jnp.cumsum(padded)
    pad_start = pad_end - padded
    seg_start = pad_start[None, :] + jnp.cumsum(seg_len, axis=0) - seg_len
    cap = -(-(N * 2 + nt * N_EXPERTS * (RG - 1) + N_EXPERTS * (RB - 1)) // RB) * RB
    nb = cap // RB
    block_start = jnp.arange(nb, dtype=jnp.int32) * RB
    block_expert = jnp.minimum(jnp.sum((block_start[:, None] >= pad_end[None, :]).astype(jnp.int32), axis=1),
                               N_EXPERTS - 1)
    n_used = (pad_end[-1:] // RB).astype(jnp.int32)
    step_base = (jnp.arange(nt, dtype=jnp.int32) * SORTED_ROWS)[:, None]
    g_idx = jnp.arange(cap // RG, dtype=jnp.int32)
    g_row = (g_idx * RG).reshape(nb, GROUPS_PER_BLOCK)
    seg_end_b = (seg_start + seg_len).T[block_expert]
    seg_start_b = seg_start.T[block_expert]
    seg_shift_b = (step_base + seg_local - seg_start).T[block_expert]
    ended = seg_end_b[:, None, :-1] <= g_row[:, :, None]

    def lookup(v):
        return v[:, None, 0] + jnp.sum(jnp.where(ended, (v[:, 1:] - v[:, :-1])[:, None, :], 0), axis=2)

    live = ((g_row >= lookup(seg_start_b)) & (g_row < lookup(seg_end_b))).reshape(-1)
    home = (g_row + lookup(seg_shift_b)).reshape(-1)
    spare_per_step = (SORTED_ROWS - SORTED_USED) // RG
    spare = ((g_idx // GROUPS_PER_BLOCK) % 2) * GROUPS_PER_BLOCK + g_idx % GROUPS_PER_BLOCK
    assert 2 * GROUPS_PER_BLOCK <= (nt - 1) * spare_per_step, "not enough spare groups for block padding"
    spare_row = (1 + spare // spare_per_step) * SORTED_ROWS + SORTED_USED + (spare % spare_per_step) * RG
    group_src = jnp.where(live, home, SORTED_ROWS - RG).astype(jnp.int32)
    group_dst = jnp.where(live, home, spare_row).astype(jnp.int32)

    ys_sorted = _expert_call(block_expert, n_used, group_src, group_dst,
                             xs_sorted.reshape(nt * SORTED_ROWS, HALF), w_gate_e, w_up_e, w_down_e, nb)
    out = _combine_call(ys_sorted.reshape(nt, SORTED_ROWS, HALF), pos, gate, h, ln2_g[None, :], ln2_b[None, :])
    return out.reshape(B, S, D)


def kernel(x, w_in, gate_bias, w_attn_up, w_conv_up, conv_w, w_out, ln1_g, ln1_b, router_group_w,
           router_group_b, router_expert_w, router_expert_b, w_gate_e, w_up_e, w_down_e, ln2_g, ln2_b):
    h = x
    for l in range(DEPTH):
        h = _block(h, w_in[l], gate_bias[l], w_attn_up[l], w_conv_up[l], conv_w[l], w_out[l],
                   ln1_g[l], ln1_b[l], router_group_w[l], router_group_b[l], router_expert_w[l],
                   router_expert_b[l], w_gate_e[l], w_up_e[l], w_down_e[l], ln2_g[l], ln2_b[l])
    return h
```

```python
import functools

import jax
import jax.numpy as jnp
import numpy as np
from jax import lax
from jax.experimental import pallas as pl
from jax.experimental.pallas import tpu as pltpu

D_MODEL = 1024
N_HEADS = 8
N_KV_HEADS = 2
HEAD_DIM = 64
Q_WIDTH = N_HEADS * HEAD_DIM
KV_WIDTH = N_KV_HEADS * HEAD_DIM
ROPE_THETA = 500000.0
ROT_DIMS = HEAD_DIM // 4
ROT_HALF = ROT_DIMS // 2
IDX_HEADS = 8
IDX_DIM = 64
IDXQ_WIDTH = IDX_HEADS * IDX_DIM
TOPK_MAX = 256
CONV_DIM = 512
CONV_WIDTH = 3
N_BRANCHES = 2
N_GROUPS = 4
EXPERTS_PER_GROUP = 8
N_EXPERTS = N_GROUPS * EXPERTS_PER_GROUP
D_EXPERT = 512
LN_EPS = 1e-5
DEPTH = 1
DEEPNORM_ALPHA = (2 * DEPTH) ** 0.25

V7X_SUBLANES = 8
V7X_VMEM_LIMIT_BYTES = 56 * 1024 * 1024

PROJ_ROWS = 512
ATTN_CHUNK = 256
ATTN_KEY_TILE = 128
DEN_ROWS = 16
ACC_ROWS = HEAD_DIM + DEN_ROWS
LOG2_E = 1.4426950408889634
MIX_ROWS = 512
EXPERT_ROWS = 512
ROW_GROUP = V7X_SUBLANES
SORTED_USED = 2 * MIX_ROWS + N_EXPERTS * (ROW_GROUP - 1)
SORTED_ROWS = -(-(SORTED_USED + ROW_GROUP) // 256) * 256
HALF = D_MODEL // 2

T_Q0, T_QI0, T_V0, T_WI0 = 0, Q_WIDTH, Q_WIDTH + IDXQ_WIDTH, Q_WIDTH + IDXQ_WIDTH + KV_WIDTH
T_ROWS = T_WI0 + 16
KN_WIDTH = 256

INT_MIN = -2147483648
HALF_RANGE = 32768
ROUTER_LANES = 128
E0 = N_GROUPS


def _cparams(n_axes, vmem_bytes):
    return pltpu.CompilerParams(
        dimension_semantics=("arbitrary",) * n_axes,
        vmem_limit_bytes=int(min(vmem_bytes, V7X_VMEM_LIMIT_BYTES)),
    )


def _proj_kernel(x_ref, wt_ref, wn_ref, cost_ref, sint_ref, cosn_ref, sgnn_ref,
                 qt_ref, qit_ref, vt_ref, wit_ref, kn_ref):
    xb = x_ref[0].astype(jnp.bfloat16)
    pt = lax.dot_general(wt_ref[...], xb, (((1,), (1,)), ((), ())),
                         preferred_element_type=jnp.float32)
    cos = cost_ref[...]
    sin = sint_ref[...]

    def rope_t(dst_ref, base):
        for h in range(N_HEADS):
            r0 = base + h * HEAD_DIM
            x1 = pt[r0:r0 + ROT_HALF]
            x2 = pt[r0 + ROT_HALF:r0 + ROT_DIMS]
            head = jnp.concatenate([x1 * cos - x2 * sin, x2 * cos + x1 * sin,
                                    pt[r0 + ROT_DIMS:r0 + HEAD_DIM]], axis=0).astype(dst_ref.dtype)
            for jj, cols in enumerate(chunks):
                dst_ref[0, jj, h * HEAD_DIM:(h + 1) * HEAD_DIM, :] = head[:, cols]

    chunks = [slice(jj * ATTN_CHUNK, (jj + 1) * ATTN_CHUNK) for jj in range(PROJ_ROWS // ATTN_CHUNK)]
    rope_t(qt_ref, T_Q0)
    rope_t(qit_ref, T_QI0)
    for jj, cols in enumerate(chunks):
        vt_ref[0, jj] = pt[T_V0:T_V0 + KV_WIDTH, cols].astype(vt_ref.dtype)
        wit_ref[0, jj] = pt[T_WI0:T_WI0 + IDX_HEADS, cols]
    pn = jnp.dot(xb, wn_ref[...], preferred_element_type=jnp.float32)
    kn = pn[:, :KN_WIDTH] * cosn_ref[...] + pn[:, KN_WIDTH:] * sgnn_ref[...]
    kn_ref[0] = kn.astype(kn_ref.dtype)


def _proj_call(x, wt, wn, cos_t, sin_t, cos_n, sgn_n):
    B, S, D = x.shape
    R = PROJ_ROWS
    nt = S // R
    grid = (B, nt)
    out_shape = (
        jax.ShapeDtypeStruct((B, S // ATTN_CHUNK, Q_WIDTH, ATTN_CHUNK), jnp.bfloat16),
        jax.ShapeDtypeStruct((B, S // ATTN_CHUNK, IDXQ_WIDTH, ATTN_CHUNK), jnp.bfloat16),
        jax.ShapeDtypeStruct((B, S // ATTN_CHUNK, KV_WIDTH, ATTN_CHUNK), jnp.bfloat16),
        jax.ShapeDtypeStruct((B, S // ATTN_CHUNK, IDX_HEADS, ATTN_CHUNK), jnp.float32),
        jax.ShapeDtypeStruct((B, S, KN_WIDTH), jnp.bfloat16),
    )
    in_specs = [
        pl.BlockSpec((1, R, D), lambda b, j: (b, j, 0)),
        pl.BlockSpec((T_ROWS, D), lambda b, j: (0, 0)),
        pl.BlockSpec((D, 2 * KN_WIDTH), lambda b, j: (0, 0)),
        pl.BlockSpec((ROT_HALF, R), lambda b, j: (0, j)),
        pl.BlockSpec((ROT_HALF, R), lambda b, j: (0, j)),
        pl.BlockSpec((R, KN_WIDTH), lambda b, j: (j, 0)),
        pl.BlockSpec((R, KN_WIDTH), lambda b, j: (j, 0)),
    ]
    out_specs = (
        pl.BlockSpec((1, R // ATTN_CHUNK, Q_WIDTH, ATTN_CHUNK), lambda b, j: (b, j, 0, 0)),
        pl.BlockSpec((1, R // ATTN_CHUNK, IDXQ_WIDTH, ATTN_CHUNK), lambda b, j: (b, j, 0, 0)),
        pl.BlockSpec((1, R // ATTN_CHUNK, KV_WIDTH, ATTN_CHUNK), lambda b, j: (b, j, 0, 0)),
        pl.BlockSpec((1, R // ATTN_CHUNK, IDX_HEADS, ATTN_CHUNK), lambda b, j: (b, j, 0, 0)),
        pl.BlockSpec((1, R, KN_WIDTH), lambda b, j: (b, j, 0)),
    )
    vmem = 2 * (R * D * 4 + T_ROWS * D * 2 + D * 2 * KN_WIDTH * 2) + 6 * T_ROWS * R * 4 + (8 << 20)
    return pl.pallas_call(
        _proj_kernel, grid=grid, in_specs=in_specs, out_specs=out_specs, out_shape=out_shape,
        compiler_params=_cparams(2, vmem), name="dsa_proj",
    )(x, wt, wn, cos_t, sin_t, cos_n, sgn_n)


def _float_to_key(s):
    b = lax.bitcast_convert_type(s, jnp.int32)
    k = b ^ (lax.shift_right_arithmetic(b, 31) & jnp.int32(0x7FFFFFFF))
    return jnp.where(b == jnp.int32(INT_MIN), jnp.int32(0), k)


def _attn_kernel(qt_ref, qit_ref, wit_ref, kn_ref, vt_ref, o_ref,
                 key_ref, hi_ref, lo_ref, lga_ref, lgb_ref, acc_ref, m_ref, *, seq, top_k):
    C = ATTN_CHUNK
    c = pl.program_id(1)
    nkb = c + 1
    zeros_half = jnp.zeros((HEAD_DIM, C), jnp.bfloat16)
    row_i = lax.broadcasted_iota(jnp.int32, (C, C), 0)
    lane_i = lax.broadcasted_iota(jnp.int32, (C, C), 1)
    causal_in_block = row_i <= lane_i

    def score_block(kb):
        r0 = pl.multiple_of(kb * C, C)
        kix = kn_ref[0, pl.ds(r0, C), KV_WIDTH:KN_WIDTH]
        acc = None
        for h in range(IDX_HEADS):
            rhs = jnp.concatenate([qit_ref[0, 0, h * IDX_DIM:(h + 1) * IDX_DIM, :], zeros_half], axis=0)
            s = jnp.dot(kix, rhs, preferred_element_type=jnp.float32)
            t = jnp.maximum(s, 0.0) * wit_ref[0, 0, h:h + 1, :]
            acc = t if acc is None else acc + t
        score = acc * (IDX_DIM ** -0.5 * IDX_HEADS ** -0.5)
        keys = jnp.where(causal_in_block | (kb != c), _float_to_key(score), jnp.int32(INT_MIN))
        key_ref[kb] = keys
        hi_ref[kb] = lax.shift_right_arithmetic(keys, 16).astype(jnp.int16)
        lo_ref[kb] = ((keys & jnp.int32(0xFFFF)) - jnp.int32(HALF_RANGE)).astype(jnp.int16)

    def score_body(pair, carry):
        score_block(2 * pair)
        score_block(2 * pair + 1)
        return carry

    lax.fori_loop(0, nkb // 2, score_body, 0)

    @pl.when(nkb % 2 == 1)
    def _():
        score_block(nkb - 1)

    def count(ref, pred, blocks=None):
        packed = ref.dtype == jnp.int16

        def body(kb, part):
            hit = pred(ref[kb], kb)
            if packed:
                words = pltpu.bitcast(jnp.where(hit, jnp.int16(1), jnp.int16(0)), jnp.int32)
            else:
                words = hit.astype(jnp.int32)
            return part + jnp.sum(words.reshape(-1, V7X_SUBLANES, C), axis=0)

        part = jnp.zeros((V7X_SUBLANES, C), jnp.int32)
        if blocks is None:
            part = lax.fori_loop(0, nkb, body, part)
        else:
            for kb in range(blocks):
                part = body(kb, part)
        if packed:
            part = (part & jnp.int32(0xFFFF)) + lax.shift_right_logical(part, 16)
        return jnp.sum(part, axis=0, keepdims=True)

    def kth_largest_16(ref, blocks):
        def body(i, ans_u):
            cand_u = ans_u | lax.shift_left(jnp.int32(1), jnp.int32(15) - i)
            cand = (cand_u - jnp.int32(HALF_RANGE)).astype(jnp.int16)
            return jnp.where(count(ref, lambda k, kb: k >= cand, blocks) >= top_k, cand_u, ans_u)

        return lax.fori_loop(0, 16, body, jnp.zeros((1, C), jnp.int32)) - jnp.int32(HALF_RANGE)

    def threshold(blocks):
        t_hi = kth_largest_16(hi_ref, blocks)
        t_hi16 = t_hi.astype(jnp.int16)
        for kb in range(blocks):
            hi = hi_ref[kb]
            lo_ref[kb] = jnp.where(hi > t_hi16, jnp.int16(HALF_RANGE - 1),
                                   jnp.where(hi == t_hi16, lo_ref[kb], jnp.int16(-HALF_RANGE)))
        t_lo = kth_largest_16(lo_ref, blocks)
        t_lo16 = t_lo.astype(jnp.int16)
        return t_hi, t_lo, count(lo_ref, lambda k, kb: k >= t_lo16, blocks)

    t_hi, t_lo, n_ge = lax.switch(c, [functools.partial(threshold, n + 1) for n in range(seq // C)])
    thr_raw = t_hi * jnp.int32(2 * HALF_RANGE) + (t_lo + jnp.int32(HALF_RANGE))
    thr = jnp.maximum(thr_raw, jnp.int32(INT_MIN + 1))

    has_ties = jnp.max(jnp.where(thr_raw != jnp.int32(INT_MIN), n_ge, 0)) > top_k

    @pl.when(has_ties)
    def _():
        need = (top_k - count(key_ref, lambda k, kb: k > thr)).astype(jnp.float32)
        earlier_rows = (row_i > lane_i).astype(jnp.bfloat16)

        def drop_body(kb, seen):
            kk = key_ref[kb]
            tied = kk == thr
            tied_b = jnp.where(tied, 1.0, 0.0).astype(jnp.bfloat16)
            rank = jnp.dot(earlier_rows, tied_b, preferred_element_type=jnp.float32) + seen
            key_ref[kb] = jnp.where(tied & (rank >= need), kk - 1, kk)
            return seen + jnp.sum(tied_b.astype(jnp.float32), axis=0, keepdims=True)

        lax.fori_loop(0, nkb, drop_body, jnp.zeros((1, C), jnp.float32))

    neg_inf = jnp.float32(-jnp.inf)
    m_ref[...] = jnp.full(m_ref.shape, neg_inf, jnp.float32)
    acc_ref[...] = jnp.zeros(acc_ref.shape, jnp.float32)
    T = ATTN_KEY_TILE
    ones_rows = jnp.ones((DEN_ROWS, C), jnp.bfloat16)

    kv_group = N_HEADS // N_KV_HEADS

    def bias_body(kb, carry):
        key_ref[kb] = lax.bitcast_convert_type(jnp.where(key_ref[kb] >= thr, 0.0, neg_inf), jnp.int32)
        return carry

    lax.fori_loop(0, nkb, bias_body, 0)

    def store_logits(kb, dst_ref, h):
        kb = jnp.minimum(kb, nkb - 1)
        r0 = pl.multiple_of(kb * C, C)
        bias = lax.bitcast_convert_type(key_ref[kb], jnp.float32)
        k2 = kn_ref[0, pl.ds(r0, C), 0:KV_WIDTH]
        qh = qt_ref[0, 0, h * HEAD_DIM:(h + 1) * HEAD_DIM, :]
        rhs = jnp.concatenate([qh, zeros_half] if h < kv_group else [zeros_half, qh], axis=0)
        dst_ref[h] = jnp.dot(k2, rhs, preferred_element_type=jnp.float32) + bias

    def softmax_block(kb, src_ref, dst_ref):
        for h in range(N_HEADS):
            if dst_ref is not None:
                store_logits(kb + 1, dst_ref, h)
            g = h // kv_group
            m_old = m_ref[h:h + 1, :]
            m_new = jnp.maximum(m_old, jnp.max(src_ref[h], axis=0, keepdims=True))
            m_safe = jnp.where(m_new == neg_inf, 0.0, m_new)
            p = jnp.concatenate(
                [jnp.exp2(src_ref[h, s * T:(s + 1) * T, :] - m_safe).astype(jnp.bfloat16) for s in range(C // T)],
                axis=0)
            alpha = jnp.exp2(m_old - m_safe)
            vt = vt_ref[0, kb, g * HEAD_DIM:(g + 1) * HEAD_DIM, :]
            pv = jnp.dot(jnp.concatenate([vt, ones_rows], axis=0), p,
                         preferred_element_type=jnp.float32)
            hs = slice(h * ACC_ROWS, (h + 1) * ACC_ROWS)
            acc_ref[hs, :] = alpha * acc_ref[hs, :] + pv
            m_ref[h:h + 1, :] = m_new

    for h in range(N_HEADS):
        store_logits(0, lga_ref, h)

    def attn_body(pair, carry):
        kb = 2 * pair
        softmax_block(kb, lga_ref, lgb_ref)
        softmax_block(kb + 1, lgb_ref, lga_ref)
        return carry

    lax.fori_loop(0, nkb // 2, attn_body, 0)

    @pl.when(nkb % 2 == 1)
    def _():
        softmax_block(nkb - 1, lga_ref, None)
    outs = []
    for h in range(N_HEADS):
        num = acc_ref[h * ACC_ROWS:h * ACC_ROWS + HEAD_DIM, :]
        den = acc_ref[h * ACC_ROWS + HEAD_DIM:h * ACC_ROWS + HEAD_DIM + 1, :]
        outs.append(num / den)
    o_ref[0] = jnp.transpose(jnp.concatenate(outs, axis=0)).astype(o_ref.dtype)


def _attn_call(qt, qit, wit, kn, vt, top_k):
    B, S = kn.shape[:2]
    C = ATTN_CHUNK
    nc = S // C
    kern = functools.partial(_attn_kernel, seq=S, top_k=top_k)
    in_specs = [
        pl.BlockSpec((1, 1, Q_WIDTH, C), lambda b, c: (b, c, 0, 0)),
        pl.BlockSpec((1, 1, IDXQ_WIDTH, C), lambda b, c: (b, c, 0, 0)),
        pl.BlockSpec((1, 1, IDX_HEADS, C), lambda b, c: (b, c, 0, 0)),
        pl.BlockSpec((1, S, KN_WIDTH), lambda b, c: (b, 0, 0)),
        pl.BlockSpec((1, nc, KV_WIDTH, C), lambda b, c: (b, 0, 0, 0)),
    ]
    out_specs = pl.BlockSpec((1, C, Q_WIDTH), lambda b, c: (b, c, 0))
    scratch = [
        pltpu.VMEM((nc, C, C), jnp.int32),
        pltpu.VMEM((nc, C, C), jnp.int16),
        pltpu.VMEM((nc, C, C), jnp.int16),
        pltpu.VMEM((N_HEADS, C, C), jnp.float32),
        pltpu.VMEM((N_HEADS, C, C), jnp.float32),
        pltpu.VMEM((N_HEADS * ACC_ROWS, C), jnp.float32),
        pltpu.VMEM((N_HEADS, C), jnp.float32),
    ]
    vmem = (2 * S * C * 4 + Q_WIDTH * C * 4 + 2 * 2 * (2 * Q_WIDTH * C + S * KN_WIDTH + S * KV_WIDTH + C * Q_WIDTH)
            + 24 * C * C * 4 + (8 << 20))
    return pl.pallas_call(
        kern, grid=(B, nc), in_specs=in_specs, out_specs=out_specs,
        out_shape=jax.ShapeDtypeStruct((B, S, Q_WIDTH), jnp.bfloat16),
        scratch_shapes=scratch, compiler_params=_cparams(2, vmem), name="dsa_attn",
    )(qt, qit, wit, kn, vt)


def _layer_norm(v, g, b):
    mu = jnp.mean(v, axis=-1, keepdims=True)
    d = v - mu
    var = jnp.mean(d * d, axis=-1, keepdims=True)
    return d * lax.rsqrt(var + LN_EPS) * g + b


def _pack_rows(h):
    hi = lax.bitcast_convert_type(h[:, :HALF].astype(jnp.bfloat16).astype(jnp.float32), jnp.int32)
    lo = lax.bitcast_convert_type(h[:, HALF:].astype(jnp.bfloat16).astype(jnp.float32), jnp.int32)
    return (hi & jnp.int32(-65536)) | lax.shift_right_logical(lo, 16)


def _unpack_rows(w):
    hi = lax.bitcast_convert_type(w & jnp.int32(-65536), jnp.float32)
    lo = lax.bitcast_convert_type(lax.shift_left(w, 16), jnp.float32)
    return jnp.concatenate([hi, lo], axis=1)


def _mix_kernel(x_ref, attn_ref, wc_ref, wg_ref, gb_ref, cw_ref, wau_ref, wcu_ref, wo_ref,
                g1_ref, b1_ref, wrh_ref, wrl_ref, rb_ref,
                h_ref, xs_ref, pos_ref, gate_ref, cnt_ref,
                ubuf_ref, *, steps_per_seq):
    R = MIX_ROWS
    i = pl.program_id(0)

    @pl.when(i % steps_per_seq == 0)
    def _():
        ubuf_ref[0:V7X_SUBLANES, :] = jnp.zeros((V7X_SUBLANES, CONV_DIM), jnp.float32)

    x = x_ref[...]
    xb = x.astype(jnp.bfloat16)
    cv = jnp.dot(xb, wc_ref[...], preferred_element_type=jnp.float32)
    u = cv[:, 2 * CONV_DIM:] * cv[:, :CONV_DIM]
    ubuf_ref[V7X_SUBLANES:, :] = u
    u1 = ubuf_ref[V7X_SUBLANES - 1:V7X_SUBLANES - 1 + R, :]
    u2 = ubuf_ref[V7X_SUBLANES - 2:V7X_SUBLANES - 2 + R, :]
    y = cw_ref[0:1, :] * u2 + cw_ref[1:2, :] * u1 + cw_ref[2:3, :] * u
    conv = (cv[:, CONV_DIM:2 * CONV_DIM] * y).astype(jnp.bfloat16)
    ubuf_ref[0:V7X_SUBLANES, :] = u[R - V7X_SUBLANES:, :]
    z = jnp.dot(xb, wg_ref[...], preferred_element_type=jnp.float32) + gb_ref[...]
    gates = 1.0 / (1.0 + jnp.exp(-z))
    au = jnp.dot(attn_ref[...], wau_ref[...], preferred_element_type=jnp.float32)
    cu = jnp.dot(conv, wcu_ref[...], preferred_element_type=jnp.float32)
    merged = gates[:, :D_MODEL] * au + gates[:, D_MODEL:] * cu
    mix = jnp.dot(merged.astype(jnp.bfloat16), wo_ref[...], preferred_element_type=jnp.float32)
    h = _layer_norm(DEEPNORM_ALPHA * x + mix, g1_ref[...], b1_ref[...])
    h_ref[...] = h

    h_hi = h.astype(jnp.bfloat16)
    h_lo = (h - h_hi.astype(jnp.float32)).astype(jnp.bfloat16)
    lg = (jnp.dot(h_hi, wrh_ref[...], preferred_element_type=jnp.float32)
          + jnp.dot(h_lo, wrh_ref[...], preferred_element_type=jnp.float32)
          + jnp.dot(h_hi, wrl_ref[...], preferred_element_type=jnp.float32)) + rb_ref[...]
    lane = lax.broadcasted_iota(jnp.int32, (R, ROUTER_LANES), 1).astype(jnp.float32)
    neg = jnp.float32(-jnp.inf)
    no_lane = jnp.float32(ROUTER_LANES)
    gmask = lane < N_GROUPS
    gl = jnp.where(gmask, lg, neg)
    gmax = jnp.max(gl, axis=1, keepdims=True)
    grp = jnp.min(jnp.where(gl == gmax, lane, no_lane), axis=1, keepdims=True)
    gsum = jnp.sum(jnp.where(gmask, jnp.exp(gl - gmax), 0.0), axis=1, keepdims=True)
    p_grp = 1.0 / gsum
    lo_lane = E0 + grp * EXPERTS_PER_GROUP
    emask = (lane >= lo_lane) & (lane < lo_lane + EXPERTS_PER_GROUP)
    el = jnp.where(emask, lg, neg)
    v1 = jnp.max(el, axis=1, keepdims=True)
    i1 = jnp.min(jnp.where(el == v1, lane, no_lane), axis=1, keepdims=True)
    el2 = jnp.where(lane == i1, neg, el)
    v2 = jnp.max(el2, axis=1, keepdims=True)
    i2 = jnp.min(jnp.where(el2 == v2, lane, no_lane), axis=1, keepdims=True)
    a = jnp.exp(v2 - v1)
    inv = 1.0 / (1.0 + a)
    gate_ref[:, 0:1] = p_grp * inv
    gate_ref[:, 1:2] = p_grp * (a * inv)
    oh1 = lane == i1
    oh2 = lane == i2
    oh = (oh1 | oh2).astype(jnp.bfloat16)
    r_i = lax.broadcasted_iota(jnp.int32, (R, R), 0)
    c_i = lax.broadcasted_iota(jnp.int32, (R, R), 1)
    tri = (r_i > c_i).astype(jnp.bfloat16)
    before = jnp.dot(tri, oh, preferred_element_type=jnp.float32)
    cnt = jnp.sum(oh.astype(jnp.float32), axis=0, keepdims=True)
    cnt_ref[0] = cnt

    seg = jnp.floor((cnt + (ROW_GROUP - 1)) * (1.0 / ROW_GROUP)) * ROW_GROUP
    l_r = lax.broadcasted_iota(jnp.int32, (ROUTER_LANES, ROUTER_LANES), 0)
    l_c = lax.broadcasted_iota(jnp.int32, (ROUTER_LANES, ROUTER_LANES), 1)
    lanes_before = (l_r < l_c).astype(jnp.bfloat16)
    seg8 = jnp.broadcast_to(seg, (V7X_SUBLANES, ROUTER_LANES)).astype(jnp.bfloat16)
    seg_off = jnp.dot(seg8, lanes_before, preferred_element_type=jnp.float32)[0:1, :]
    where_to = before + seg_off
    pos1 = jnp.sum(jnp.where(oh1, where_to, 0.0), axis=1, keepdims=True)
    pos2 = jnp.sum(jnp.where(oh2, where_to, 0.0), axis=1, keepdims=True)
    pos_ref[:, 0:1] = pos1.astype(jnp.int32)
    pos_ref[:, 1:2] = pos2.astype(jnp.int32)
    diag = r_i == c_i
    pos1_row = jnp.sum(jnp.where(diag, pos1, 0.0), axis=0, keepdims=True)
    pos2_row = jnp.sum(jnp.where(diag, pos2, 0.0), axis=0, keepdims=True)
    sorted_row = lax.broadcasted_iota(jnp.int32, (SORTED_ROWS, R), 0).astype(jnp.float32)
    place = ((sorted_row == pos1_row) | (sorted_row == pos2_row)).astype(jnp.bfloat16)
    xs = jnp.dot(place, h_hi, preferred_element_type=jnp.float32)
    xs_ref[0] = _pack_rows(xs)


def _mix_call(x2, attn2, wc, wg, gb, cw, wau, wcu, wo, g1, b1, wrh, wrl, rb, seq):
    N, D = x2.shape
    R = MIX_ROWS
    nt = N // R
    kern = functools.partial(_mix_kernel, steps_per_seq=seq // R)

    def full(a):
        return pl.BlockSpec(a.shape, lambda i: (0,) * a.ndim)

    in_specs = [
        pl.BlockSpec((R, D), lambda i: (i, 0)),
        pl.BlockSpec((R, Q_WIDTH), lambda i: (i, 0)),
        full(wc), full(wg), full(gb), full(cw), full(wau), full(wcu), full(wo),
        full(g1), full(b1), full(wrh), full(wrl), full(rb),
    ]
    out_shape = (
        jax.ShapeDtypeStruct((N, D), jnp.float32),
        jax.ShapeDtypeStruct((nt, SORTED_ROWS, HALF), jnp.int32),
        jax.ShapeDtypeStruct((N, 2), jnp.int32),
        jax.ShapeDtypeStruct((N, 2), jnp.float32),
        jax.ShapeDtypeStruct((nt, 1, ROUTER_LANES), jnp.float32),
    )
    out_specs = (
        pl.BlockSpec((R, D), lambda i: (i, 0)),
        pl.BlockSpec((1, SORTED_ROWS, HALF), lambda i: (i, 0, 0)),
        pl.BlockSpec((R, 2), lambda i: (i, 0)),
        pl.BlockSpec((R, 2), lambda i: (i, 0)),
        pl.BlockSpec((1, 1, ROUTER_LANES), lambda i: (i, 0, 0)),
    )
    scratch = [pltpu.VMEM((R + V7X_SUBLANES, CONV_DIM), jnp.float32)]
    w_bytes = 2 * (wc.size + wg.size + wau.size + wcu.size + wo.size + wrh.size + wrl.size)
    vmem = (2 * w_bytes + 2 * (R * D * 4 * 2 + R * Q_WIDTH * 2 + SORTED_ROWS * HALF * 4) + 10 * R * 2048 * 4
            + SORTED_ROWS * D * 6 + (6 << 20))
    return pl.pallas_call(
        kern, grid=(nt,), in_specs=in_specs, out_specs=out_specs, out_shape=out_shape,
        scratch_shapes=scratch, compiler_params=_cparams(1, vmem), name="mix_ln_router",
    )(x2, attn2, wc, wg, gb, cw, wau, wcu, wo, g1, b1, wrh, wrl, rb)


GROUPS_PER_BLOCK = EXPERT_ROWS // ROW_GROUP


def _expert_kernel(be_ref, nb_ref, gsrc_ref, gdst_ref, xs_ref, wg_ref, wu_ref, wd_ref, ys_ref,
                   wgb_ref, wub_ref, wdb_ref, xbuf_ref, ybuf_ref, gsems, ssems):
    del xs_ref
    j = pl.program_id(0)
    n_used = nb_ref[0]
    slot = j % 2

    def gather(block, to_slot):
        for g in range(GROUPS_PER_BLOCK):
            row = pl.multiple_of(gsrc_ref[block * GROUPS_PER_BLOCK + g], ROW_GROUP)
            pltpu.make_async_copy(ys_ref.at[pl.ds(row, ROW_GROUP), :],
                                  xbuf_ref.at[to_slot, pl.ds(g * ROW_GROUP, ROW_GROUP), :],
                                  gsems.at[to_slot]).start(priority=g % 2)

    def scatter(block, from_slot):
        for g in range(GROUPS_PER_BLOCK):
            row = pl.multiple_of(gdst_ref[block * GROUPS_PER_BLOCK + g], ROW_GROUP)
            pltpu.make_async_copy(ybuf_ref.at[from_slot, pl.ds(g * ROW_GROUP, ROW_GROUP), :],
                                  ys_ref.at[pl.ds(row, ROW_GROUP), :],
                                  ssems.at[from_slot]).start(priority=g % 2)

    def wait_gather(s):
        pltpu.make_async_copy(ys_ref.at[pl.ds(0, EXPERT_ROWS), :], xbuf_ref.at[s], gsems.at[s]).wait()

    def wait_scatter(s):
        pltpu.make_async_copy(ybuf_ref.at[s], ys_ref.at[pl.ds(0, EXPERT_ROWS), :], ssems.at[s]).wait()

    @pl.when((j == 0) & (n_used > 0))
    def _():
        gather(0, 0)

    @pl.when((j == 0) | (be_ref[j] != be_ref[jnp.maximum(j - 1, 0)]))
    def _():
        wgb_ref[...] = wg_ref[0].astype(jnp.bfloat16)
        wub_ref[...] = wu_ref[0].astype(jnp.bfloat16)
        wdb_ref[...] = wd_ref[0].astype(jnp.bfloat16)

    @pl.when(j + 1 < n_used)
    def _():
        gather(j + 1, 1 - slot)

    @pl.when((j >= 2) & (j < n_used))
    def _():
        wait_scatter(slot)

    @pl.when(j < n_used)
    def _():
        wait_gather(slot)
        xb = _unpack_rows(xbuf_ref[slot]).astype(jnp.bfloat16)
        hg = jnp.dot(xb, wgb_ref[...], preferred_element_type=jnp.float32)
        hu = jnp.dot(xb, wub_ref[...], preferred_element_type=jnp.float32)
        hid = (hg / (1.0 + jnp.exp(-hg))) * hu
        y = jnp.dot(hid.astype(jnp.bfloat16), wdb_ref[...], preferred_element_type=jnp.float32)
        ybuf_ref[slot] = _pack_rows(y)
        scatter(j, slot)

    @pl.when(j == n_used - 1)
    def _():
        wait_scatter(slot)

    @pl.when((j == n_used - 1) & (j >= 1))
    def _():
        wait_scatter(1 - slot)


def _expert_call(block_expert, n_used, group_src, group_dst, xs_sorted, wg, wu, wd, n_blocks):
    W = xs_sorted.shape[-1]
    RB = EXPERT_ROWS
    grid_spec = pltpu.PrefetchScalarGridSpec(
        num_scalar_prefetch=4, grid=(n_blocks,),
        in_specs=[
            pl.BlockSpec(memory_space=pl.ANY),
            pl.BlockSpec((1, D_MODEL, D_EXPERT), lambda j, be, nu, gs, gd: (be[j], 0, 0)),
            pl.BlockSpec((1, D_MODEL, D_EXPERT), lambda j, be, nu, gs, gd: (be[j], 0, 0)),
            pl.BlockSpec((1, D_EXPERT, D_MODEL), lambda j, be, nu, gs, gd: (be[j], 0, 0)),
        ],
        out_specs=pl.BlockSpec(memory_space=pl.ANY),
        scratch_shapes=[pltpu.VMEM((D_MODEL, D_EXPERT), jnp.bfloat16),
                        pltpu.VMEM((D_MODEL, D_EXPERT), jnp.bfloat16),
                        pltpu.VMEM((D_EXPERT, D_MODEL), jnp.bfloat16),
                        pltpu.VMEM((2, RB, W), jnp.int32),
                        pltpu.VMEM((2, RB, W), jnp.int32),
                        pltpu.SemaphoreType.DMA((2,)),
                        pltpu.SemaphoreType.DMA((2,))],
    )
    vmem = (2 * 4 + 2) * 3 * D_MODEL * D_EXPERT + 4 * RB * W * 4 + 8 * RB * D_MODEL * 4 + (6 << 20)
    return pl.pallas_call(
        _expert_kernel, grid_spec=grid_spec,
        out_shape=jax.ShapeDtypeStruct(xs_sorted.shape, xs_sorted.dtype),
        input_output_aliases={4: 0},
        compiler_params=_cparams(1, vmem), name="moe_experts",
    )(block_expert, n_used, group_src, group_dst, xs_sorted, wg, wu, wd)


def _combine_kernel(ys_ref, pos_ref, gate_ref, h_ref, g2_ref, b2_ref, o_ref):
    R = MIX_ROWS
    y = _unpack_rows(ys_ref[0]).astype(jnp.bfloat16)
    col = lax.broadcasted_iota(jnp.int32, (R, SORTED_ROWS), 1)
    pick = (jnp.where(col == pos_ref[:, 0:1], gate_ref[:, 0:1], 0.0)
            + jnp.where(col == pos_ref[:, 1:2], gate_ref[:, 1:2], 0.0)).astype(jnp.bfloat16)
    ffn = jnp.dot(pick, y, preferred_element_type=jnp.float32)
    o_ref[...] = _layer_norm(DEEPNORM_ALPHA * h_ref[...] + ffn, g2_ref[...], b2_ref[...])


def _combine_call(ys_sorted, pos, gate, h, g2, b2):
    N, D = h.shape
    R = MIX_ROWS
    nt = N // R
    W = ys_sorted.shape[-1]
    return pl.pallas_call(
        _combine_kernel, grid=(nt,),
        in_specs=[
            pl.BlockSpec((1, SORTED_ROWS, W), lambda i: (i, 0, 0)),
            pl.BlockSpec((R, 2), lambda i: (i, 0)),
            pl.BlockSpec((R, 2), lambda i: (i, 0)),
            pl.BlockSpec((R, D), lambda i: (i, 0)),
            pl.BlockSpec((1, D), lambda i: (0, 0)),
            pl.BlockSpec((1, D), lambda i: (0, 0)),
        ],
        out_specs=pl.BlockSpec((R, D), lambda i: (i, 0)),
        out_shape=jax.ShapeDtypeStruct((N, D), jnp.float32),
        compiler_params=_cparams(1, 2 * (SORTED_ROWS * W * 4 + 2 * R * D * 4) + SORTED_ROWS * D * 8
                                 + 3 * R * SORTED_ROWS * 4 + (6 << 20)),
        name="moe_combine",
    )(ys_sorted, pos, gate, h, g2, b2)


def _rope_tables(seq):
    inv_freq = ROPE_THETA ** (-jnp.arange(ROT_HALF, dtype=jnp.float32) / ROT_HALF)
    ang = jnp.arange(seq, dtype=jnp.int32).astype(jnp.float32)[:, None] * inv_freq[None, :]
    cos, sin = jnp.cos(ang), jnp.sin(ang)
    ones = jnp.ones((seq, HEAD_DIM - ROT_DIMS), jnp.float32)
    zeros = jnp.zeros((seq, HEAD_DIM - ROT_DIMS), jnp.float32)
    c_head = jnp.concatenate([cos, cos, ones], axis=1)
    s_head = jnp.concatenate([-sin, sin, zeros], axis=1)
    reps = KN_WIDTH // HEAD_DIM
    return cos.T, sin.T, jnp.tile(c_head, (1, reps)), jnp.tile(s_head, (1, reps))


def _swap_rot_cols(w):
    d, n = w.shape
    wh = w.reshape(d, n // HEAD_DIM, HEAD_DIM)
    sw = jnp.concatenate([wh[:, :, ROT_HALF:ROT_DIMS], wh[:, :, :ROT_HALF],
                          jnp.zeros((d, n // HEAD_DIM, HEAD_DIM - ROT_DIMS), w.dtype)], axis=2)
    return sw.reshape(d, n)


def _block(x, w_in, gate_bias, w_attn_up, w_conv_up, conv_w, w_out, ln_g, ln_b,
           rg_w, rg_b, re_w, re_b, w_gate_e, w_up_e, w_down_e, ln2_g, ln2_b):
    B, S, D = x.shape
    N = B * S
    top_k = min(TOPK_MAX, S // 4)
    bf = jnp.bfloat16
    o = np.cumsum([0, Q_WIDTH, KV_WIDTH, KV_WIDTH, IDXQ_WIDTH, IDX_DIM, IDX_HEADS,
                   CONV_DIM, CONV_DIM, CONV_DIM, N_BRANCHES * D_MODEL])
    w_q, w_k, w_v, w_qi, w_ki, w_wi = (w_in[:, o[i]:o[i + 1]] for i in range(6))
    w_conv = w_in[:, o[6]:o[9]]
    w_gates = w_in[:, o[9]:o[10]]

    wt = jnp.concatenate([w_q.T * (HEAD_DIM ** -0.5 * LOG2_E), w_qi.T, w_v.T, w_wi.T,
                          jnp.zeros((T_ROWS - T_WI0 - IDX_HEADS, D), w_in.dtype)], axis=0).astype(bf)
    pad = jnp.zeros((D, KN_WIDTH - KV_WIDTH - IDX_DIM), w_in.dtype)
    wn = jnp.concatenate([w_k, w_ki, pad, _swap_rot_cols(w_k), _swap_rot_cols(w_ki), pad], axis=1).astype(bf)
    cos_t, sin_t, cos_n, sgn_n = _rope_tables(S)

    qt, qit, vt, wit, kn = _proj_call(x, wt, wn, cos_t, sin_t, cos_n, sgn_n)
    attn = _attn_call(qt, qit, wit, kn, vt, top_k)

    w_r = jnp.concatenate([rg_w, jnp.transpose(re_w, (1, 0, 2)).reshape(D, N_EXPERTS),
                           jnp.zeros((D, ROUTER_LANES - E0 - N_EXPERTS), rg_w.dtype)], axis=1)
    b_r = jnp.concatenate([rg_b, re_b.reshape(-1),
                           jnp.zeros((ROUTER_LANES - E0 - N_EXPERTS,), rg_b.dtype)])[None, :]
    w_rh = w_r.astype(bf)
    w_rl = (w_r - w_rh.astype(jnp.float32)).astype(bf)

    h, xs_sorted, pos, gate, cnt = _mix_call(
        x.reshape(N, D), attn.reshape(N, Q_WIDTH), w_conv.astype(bf), w_gates.astype(bf),
        gate_bias[None, :], conv_w, w_attn_up.astype(bf), w_conv_up.astype(bf), w_out.astype(bf),
        ln_g[None, :], ln_b[None, :], w_rh, w_rl, b_r, S)

    RB, RG = EXPERT_ROWS, ROW_GROUP
    nt = N // MIX_ROWS
    counts = cnt[:, 0, E0:E0 + N_EXPERTS].astype(jnp.int32)
    seg_len = ((counts + RG - 1) // RG) * RG
    seg_local = jnp.cumsum(seg_len, axis=1) - seg_len
    region = jnp.sum(seg_len, axis=0)
    padded = ((region + RB - 1) // RB) * RB
    pad_end = jnp.cumsum(padded)
    pad_start = pad_end - padded
    seg_start = pad_start[None, :] + jnp.cumsum(seg_len, axis=0) - seg_len
    cap = -(-(N * 2 + nt * N_EXPERTS * (RG - 1) + N_EXPERTS * (RB - 1)) // RB) * RB
    nb = cap // RB
    block_start = jnp.arange(nb, dtype=jnp.int32) * RB
    block_expert = jnp.minimum(jnp.sum((block_start[:, None] >= pad_end[None, :]).astype(jnp.int32), axis=1),
                               N_EXPERTS - 1)
    n_used = (pad_end[-1:] // RB).astype(jnp.int32)
    step_base = (jnp.arange(nt, dtype=jnp.int32) * SORTED_ROWS)[:, None]
    g_idx = jnp.arange(cap // RG, dtype=jnp.int32)
    g_row = (g_idx * RG).reshape(nb, GROUPS_PER_BLOCK)
    seg_end_b = (seg_start + seg_len).T[block_expert]
    seg_start_b = seg_start.T[block_expert]
    seg_shift_b = (step_base + seg_local - seg_start).T[block_expert]
    ended = seg_end_b[:, None, :-1] <= g_row[:, :, None]

    def lookup(v):
        return v[:, None, 0] + jnp.sum(jnp.where(ended, (v[:, 1:] - v[:, :-1])[:, None, :], 0), axis=2)

    live = ((g_row >= lookup(seg_start_b)) & (g_row < lookup(seg_end_b))).reshape(-1)
    home = (g_row + lookup(seg_shift_b)).reshape(-1)
    spare_per_step = (SORTED_ROWS - SORTED_USED) // RG
    spare = ((g_idx // GROUPS_PER_BLOCK) % 2) * GROUPS_PER_BLOCK + g_idx % GROUPS_PER_BLOCK
    assert 2 * GROUPS_PER_BLOCK <= (nt - 1) * spare_per_step, "not enough spare groups for block padding"
    spare_row = (1 + spare // spare_per_step) * SORTED_ROWS + SORTED_USED + (spare % spare_per_step) * RG
    group_src = jnp.where(live, home, SORTED_ROWS - RG).astype(jnp.int32)
    group_dst = jnp.where(live, home, spare_row).astype(jnp.int32)

    ys_sorted = _expert_call(block_expert, n_used, group_src, group_dst,
                             xs_sorted.reshape(nt * SORTED_ROWS, HALF), w_gate_e, w_up_e, w_down_e, nb)
    out = _combine_call(ys_sorted.reshape(nt, SORTED_ROWS, HALF), pos, gate, h, ln2_g[None, :], ln2_b[None, :])
    return out.reshape(B, S, D)


def kernel(x, w_in, gate_bias, w_attn_up, w_conv_up, conv_w, w_out, ln1_g, ln1_b, router_group_w,
           router_group_b, router_expert_w, router_expert_b, w_gate_e, w_up_e, w_down_e, ln2_g, ln2_b):
    h = x
    for l in range(DEPTH):
        h = _block(h, w_in[l], gate_bias[l], w_attn_up[l], w_conv_up[l], conv_w[l], w_out[l],
                   ln1_g[l], ln1_b[l], router_group_w[l], router_group_b[l], router_expert_w[l],
                   router_expert_b[l], w_gate_e[l], w_up_e[l], w_down_e[l], ln2_g[l], ln2_b[l])
    return h
```

```python
import functools

import jax
import jax.numpy as jnp
import numpy as np
from jax import lax
from jax.experimental import pallas as pl
from jax.experimental.pallas import tpu as pltpu

D_MODEL = 1024
N_HEADS = 8
N_KV_HEADS = 2
HEAD_DIM = 64
Q_WIDTH = N_HEADS * HEAD_DIM
KV_WIDTH = N_KV_HEADS * HEAD_DIM
ROPE_THETA = 500000.0
ROT_DIMS = HEAD_DIM // 4
ROT_HALF = ROT_DIMS // 2
IDX_HEADS = 8
IDX_DIM = 64
IDXQ_WIDTH = IDX_HEADS * IDX_DIM
TOPK_MAX = 256
CONV_DIM = 512
CONV_WIDTH = 3
N_BRANCHES = 2
N_GROUPS = 4
EXPERTS_PER_GROUP = 8
N_EXPERTS = N_GROUPS * EXPERTS_PER_GROUP
D_EXPERT = 512
LN_EPS = 1e-5
DEPTH = 1
DEEPNORM_ALPHA = (2 * DEPTH) ** 0.25

V7X_SUBLANES = 8
V7X_VMEM_LIMIT_BYTES = 56 * 1024 * 1024

PROJ_ROWS = 512
ATTN_CHUNK = 256
ATTN_PAIR = 2
ATTN_KEY_TILE = 128
DEN_ROWS = 16
ACC_ROWS = HEAD_DIM + DEN_ROWS
LOG2_E = 1.4426950408889634
MIX_ROWS = 512
EXPERT_ROWS = 512
ROW_GROUP = V7X_SUBLANES
SORTED_USED = 2 * MIX_ROWS + N_EXPERTS * (ROW_GROUP - 1)
SORTED_ROWS = -(-(SORTED_USED + ROW_GROUP) // 256) * 256
HALF = D_MODEL // 2

T_Q0, T_QI0, T_V0, T_WI0 = 0, Q_WIDTH, Q_WIDTH + IDXQ_WIDTH, Q_WIDTH + IDXQ_WIDTH + KV_WIDTH
T_ROWS = T_WI0 + 16
KN_WIDTH = 256

INT_MIN = -2147483648
HALF_RANGE = 32768
ROUTER_LANES = 128
E0 = N_GROUPS


def _cparams(n_axes, vmem_bytes):
    return pltpu.CompilerParams(
        dimension_semantics=("arbitrary",) * n_axes,
        vmem_limit_bytes=int(min(vmem_bytes, V7X_VMEM_LIMIT_BYTES)),
    )


def _proj_kernel(x_ref, wt_ref, wn_ref, cost_ref, sint_ref, cosn_ref, sgnn_ref,
                 qt_ref, qit_ref, vt_ref, wit_ref, kn_ref):
    xb = x_ref[0].astype(jnp.bfloat16)
    pt = lax.dot_general(wt_ref[...], xb, (((1,), (1,)), ((), ())),
                         preferred_element_type=jnp.float32)
    cos = cost_ref[...]
    sin = sint_ref[...]

    def rope_t(dst_ref, base):
        for h in range(N_HEADS):
            r0 = base + h * HEAD_DIM
            x1 = pt[r0:r0 + ROT_HALF]
            x2 = pt[r0 + ROT_HALF:r0 + ROT_DIMS]
            head = jnp.concatenate([x1 * cos - x2 * sin, x2 * cos + x1 * sin,
                                    pt[r0 + ROT_DIMS:r0 + HEAD_DIM]], axis=0).astype(dst_ref.dtype)
            for jj, cols in enumerate(chunks):
                dst_ref[0, jj, h * HEAD_DIM:(h + 1) * HEAD_DIM, :] = head[:, cols]

    chunks = [slice(jj * ATTN_CHUNK, (jj + 1) * ATTN_CHUNK) for jj in range(PROJ_ROWS // ATTN_CHUNK)]
    rope_t(qt_ref, T_Q0)
    rope_t(qit_ref, T_QI0)
    for jj, cols in enumerate(chunks):
        vt_ref[0, jj] = pt[T_V0:T_V0 + KV_WIDTH, cols].astype(vt_ref.dtype)
        wit_ref[0, jj] = pt[T_WI0:T_WI0 + IDX_HEADS, cols]
    pn = jnp.dot(xb, wn_ref[...], preferred_element_type=jnp.float32)
    kn = pn[:, :KN_WIDTH] * cosn_ref[...] + pn[:, KN_WIDTH:] * sgnn_ref[...]
    kn_ref[0] = kn.astype(kn_ref.dtype)


def _proj_call(x, wt, wn, cos_t, sin_t, cos_n, sgn_n):
    B, S, D = x.shape
    R = PROJ_ROWS
    nt = S // R
    grid = (B, nt)
    out_shape = (
        jax.ShapeDtypeStruct((B, S // ATTN_CHUNK, Q_WIDTH, ATTN_CHUNK), jnp.bfloat16),
        jax.ShapeDtypeStruct((B, S // ATTN_CHUNK, IDXQ_WIDTH, ATTN_CHUNK), jnp.bfloat16),
        jax.ShapeDtypeStruct((B, S // ATTN_CHUNK, KV_WIDTH, ATTN_CHUNK), jnp.bfloat16),
        jax.ShapeDtypeStruct((B, S // ATTN_CHUNK, IDX_HEADS, ATTN_CHUNK), jnp.float32),
        jax.ShapeDtypeStruct((B, S, KN_WIDTH), jnp.bfloat16),
    )
    in_specs = [
        pl.BlockSpec((1, R, D), lambda b, j: (b, j, 0)),
        pl.BlockSpec((T_ROWS, D), lambda b, j: (0, 0)),
        pl.BlockSpec((D, 2 * KN_WIDTH), lambda b, j: (0, 0)),
        pl.BlockSpec((ROT_HALF, R), lambda b, j: (0, j)),
        pl.BlockSpec((ROT_HALF, R), lambda b, j: (0, j)),
        pl.BlockSpec((R, KN_WIDTH), lambda b, j: (j, 0)),
        pl.BlockSpec((R, KN_WIDTH), lambda b, j: (j, 0)),
    ]
    out_specs = (
        pl.BlockSpec((1, R // ATTN_CHUNK, Q_WIDTH, ATTN_CHUNK), lambda b, j: (b, j, 0, 0)),
        pl.BlockSpec((1, R // ATTN_CHUNK, IDXQ_WIDTH, ATTN_CHUNK), lambda b, j: (b, j, 0, 0)),
        pl.BlockSpec((1, R // ATTN_CHUNK, KV_WIDTH, ATTN_CHUNK), lambda b, j: (b, j, 0, 0)),
        pl.BlockSpec((1, R // ATTN_CHUNK, IDX_HEADS, ATTN_CHUNK), lambda b, j: (b, j, 0, 0)),
        pl.BlockSpec((1, R, KN_WIDTH), lambda b, j: (b, j, 0)),
    )
    vmem = 2 * (R * D * 4 + T_ROWS * D * 2 + D * 2 * KN_WIDTH * 2) + 6 * T_ROWS * R * 4 + (8 << 20)
    return pl.pallas_call(
        _proj_kernel, grid=grid, in_specs=in_specs, out_specs=out_specs, out_shape=out_shape,
        compiler_params=_cparams(2, vmem), name="dsa_proj",
    )(x, wt, wn, cos_t, sin_t, cos_n, sgn_n)


def _float_to_key(s):
    b = lax.bitcast_convert_type(s, jnp.int32)
    k = b ^ (lax.shift_right_arithmetic(b, 31) & jnp.int32(0x7FFFFFFF))
    return jnp.where(b == jnp.int32(INT_MIN), jnp.int32(0), k)


def _attn_kernel(qt_ref, qit_ref, wit_ref, kn_ref, vt_ref, o_ref,
                 key_ref, hi_ref, lo_ref, lga_ref, lgb_ref, acc_ref, m_ref, *, seq, top_k):
    C = ATTN_CHUNK
    E = range(ATTN_PAIR)
    c = pl.program_id(1)
    nkb = c + 1
    zeros_half = jnp.zeros((HEAD_DIM, C), jnp.bfloat16)
    row_i = lax.broadcasted_iota(jnp.int32, (C, C), 0)
    lane_i = lax.broadcasted_iota(jnp.int32, (C, C), 1)
    causal_in_block = row_i <= lane_i

    def score_block(e, kb):
        r0 = pl.multiple_of(kb * C, C)
        kix = kn_ref[e, pl.ds(r0, C), KV_WIDTH:KN_WIDTH]
        acc = None
        for h in range(IDX_HEADS):
            rhs = jnp.concatenate([qit_ref[e, 0, h * IDX_DIM:(h + 1) * IDX_DIM, :], zeros_half], axis=0)
            s = jnp.dot(kix, rhs, preferred_element_type=jnp.float32)
            t = jnp.maximum(s, 0.0) * wit_ref[e, 0, h:h + 1, :]
            acc = t if acc is None else acc + t
        score = acc * (IDX_DIM ** -0.5 * IDX_HEADS ** -0.5)
        keys = jnp.where(causal_in_block | (kb != c), _float_to_key(score), jnp.int32(INT_MIN))
        key_ref[e, kb] = keys
        hi_ref[e, kb] = lax.shift_right_arithmetic(keys, 16).astype(jnp.int16)
        lo_ref[e, kb] = ((keys & jnp.int32(0xFFFF)) - jnp.int32(HALF_RANGE)).astype(jnp.int16)

    def score_body(kb, carry):
        for e in E:
            score_block(e, kb)
        return carry

    lax.fori_loop(0, nkb, score_body, 0)

    def count(ref, pred, blocks=None):
        packed = ref.dtype == jnp.int16

        def body(kb, part):
            hit = pred(ref[kb], kb)
            if packed:
                words = pltpu.bitcast(jnp.where(hit, jnp.int16(1), jnp.int16(0)), jnp.int32)
            else:
                words = hit.astype(jnp.int32)
            return part + jnp.sum(words.reshape(-1, V7X_SUBLANES, C), axis=0)

        part = jnp.zeros((V7X_SUBLANES, C), jnp.int32)
        if blocks is None:
            part = lax.fori_loop(0, nkb, body, part)
        else:
            for kb in range(blocks):
                part = body(kb, part)
        if packed:
            part = (part & jnp.int32(0xFFFF)) + lax.shift_right_logical(part, 16)
        return jnp.sum(part, axis=0, keepdims=True)

    def kth_largest_16(ref, blocks):
        def body(i, ans_u):
            cand_u = ans_u | lax.shift_left(jnp.int32(1), jnp.int32(15) - i)
            cand = (cand_u - jnp.int32(HALF_RANGE)).astype(jnp.int16)
            return jnp.where(count(ref, lambda k, kb: k >= cand, blocks) >= top_k, cand_u, ans_u)

        return lax.fori_loop(0, 16, body, jnp.zeros((1, C), jnp.int32)) - jnp.int32(HALF_RANGE)

    def threshold(blocks, e):
        hi_e, lo_e = hi_ref.at[e], lo_ref.at[e]
        t_hi = kth_largest_16(hi_e, blocks)
        t_hi16 = t_hi.astype(jnp.int16)
        for kb in range(blocks):
            hi = hi_e[kb]
            lo_e[kb] = jnp.where(hi > t_hi16, jnp.int16(HALF_RANGE - 1),
                                 jnp.where(hi == t_hi16, lo_e[kb], jnp.int16(-HALF_RANGE)))
        t_lo = kth_largest_16(lo_e, blocks)
        t_lo16 = t_lo.astype(jnp.int16)
        return t_hi, t_lo, count(lo_e, lambda k, kb: k >= t_lo16, blocks)

    neg_inf = jnp.float32(-jnp.inf)
    thrs = []
    for e in E:
        t_hi, t_lo, n_ge = lax.switch(c, [functools.partial(threshold, n + 1, e) for n in range(seq // C)])
        thr_raw = t_hi * jnp.int32(2 * HALF_RANGE) + (t_lo + jnp.int32(HALF_RANGE))
        thr = jnp.maximum(thr_raw, jnp.int32(INT_MIN + 1))
        thrs.append(thr)
        key_e = key_ref.at[e]

        has_ties = jnp.max(jnp.where(thr_raw != jnp.int32(INT_MIN), n_ge, 0)) > top_k

        @pl.when(has_ties)
        def _(thr=thr, key_e=key_e):
            need = (top_k - count(key_e, lambda k, kb: k > thr)).astype(jnp.float32)
            earlier_rows = (row_i > lane_i).astype(jnp.bfloat16)

            def drop_body(kb, seen):
                kk = key_e[kb]
                tied = kk == thr
                tied_b = jnp.where(tied, 1.0, 0.0).astype(jnp.bfloat16)
                rank = jnp.dot(earlier_rows, tied_b, preferred_element_type=jnp.float32) + seen
                key_e[kb] = jnp.where(tied & (rank >= need), kk - 1, kk)
                return seen + jnp.sum(tied_b.astype(jnp.float32), axis=0, keepdims=True)

            lax.fori_loop(0, nkb, drop_body, jnp.zeros((1, C), jnp.float32))

    m_ref[...] = jnp.full(m_ref.shape, neg_inf, jnp.float32)
    acc_ref[...] = jnp.zeros(acc_ref.shape, jnp.float32)
    T = ATTN_KEY_TILE
    ones_rows = jnp.ones((DEN_ROWS, C), jnp.bfloat16)

    kv_group = N_HEADS // N_KV_HEADS

    def bias_body(kb, carry):
        for e in E:
            key_ref[e, kb] = lax.bitcast_convert_type(jnp.where(key_ref[e, kb] >= thrs[e], 0.0, neg_inf), jnp.int32)
        return carry

    lax.fori_loop(0, nkb, bias_body, 0)

    def store_logits(e, kb, dst_ref, h):
        kb = jnp.minimum(kb, nkb - 1)
        r0 = pl.multiple_of(kb * C, C)
        bias = lax.bitcast_convert_type(key_ref[e, kb], jnp.float32)
        k2 = kn_ref[e, pl.ds(r0, C), 0:KV_WIDTH]
        qh = qt_ref[e, 0, h * HEAD_DIM:(h + 1) * HEAD_DIM, :]
        rhs = jnp.concatenate([qh, zeros_half] if h < kv_group else [zeros_half, qh], axis=0)
        dst_ref[e, h] = jnp.dot(k2, rhs, preferred_element_type=jnp.float32) + bias

    def softmax_block(kb, src_ref, dst_ref):
        for h in range(N_HEADS):
            for e in E:
                if dst_ref is not None:
                    store_logits(e, kb + 1, dst_ref, h)
                g = h // kv_group
                m_old = m_ref[e, h:h + 1, :]
                m_new = jnp.maximum(m_old, jnp.max(src_ref[e, h], axis=0, keepdims=True))
                m_safe = jnp.where(m_new == neg_inf, 0.0, m_new)
                p = jnp.concatenate(
                    [jnp.exp2(src_ref[e, h, s * T:(s + 1) * T, :] - m_safe).astype(jnp.bfloat16)
                     for s in range(C // T)], axis=0)
                alpha = jnp.exp2(m_old - m_safe)
                vt = vt_ref[e, kb, g * HEAD_DIM:(g + 1) * HEAD_DIM, :]
                pv = jnp.dot(jnp.concatenate([vt, ones_rows], axis=0), p,
                             preferred_element_type=jnp.float32)
                hs = slice(h * ACC_ROWS, (h + 1) * ACC_ROWS)
                acc_ref[e, hs, :] = alpha * acc_ref[e, hs, :] + pv
                m_ref[e, h:h + 1, :] = m_new

    for h in range(N_HEADS):
        for e in E:
            store_logits(e, 0, lga_ref, h)

    def attn_body(pair, carry):
        kb = 2 * pair
        softmax_block(kb, lga_ref, lgb_ref)
        softmax_block(kb + 1, lgb_ref, lga_ref)
        return carry

    lax.fori_loop(0, nkb // 2, attn_body, 0)

    @pl.when(nkb % 2 == 1)
    def _():
        softmax_block(nkb - 1, lga_ref, None)

    for e in E:
        outs = []
        for h in range(N_HEADS):
            num = acc_ref[e, h * ACC_ROWS:h * ACC_ROWS + HEAD_DIM, :]
            den = acc_ref[e, h * ACC_ROWS + HEAD_DIM:h * ACC_ROWS + HEAD_DIM + 1, :]
            outs.append(num / den)
        o_ref[e] = jnp.transpose(jnp.concatenate(outs, axis=0)).astype(o_ref.dtype)


def _attn_call(qt, qit, wit, kn, vt, top_k):
    B, S = kn.shape[:2]
    C = ATTN_CHUNK
    P = ATTN_PAIR
    nc = S // C
    kern = functools.partial(_attn_kernel, seq=S, top_k=top_k)
    in_specs = [
        pl.BlockSpec((P, 1, Q_WIDTH, C), lambda b, c: (b, c, 0, 0)),
        pl.BlockSpec((P, 1, IDXQ_WIDTH, C), lambda b, c: (b, c, 0, 0)),
        pl.BlockSpec((P, 1, IDX_HEADS, C), lambda b, c: (b, c, 0, 0)),
        pl.BlockSpec((P, S, KN_WIDTH), lambda b, c: (b, 0, 0)),
        pl.BlockSpec((P, nc, KV_WIDTH, C), lambda b, c: (b, 0, 0, 0)),
    ]
    out_specs = pl.BlockSpec((P, C, Q_WIDTH), lambda b, c: (b, c, 0))
    scratch = [
        pltpu.VMEM((P, nc, C, C), jnp.int32),
        pltpu.VMEM((P, nc, C, C), jnp.int16),
        pltpu.VMEM((P, nc, C, C), jnp.int16),
        pltpu.VMEM((P, N_HEADS, C, C), jnp.float32),
        pltpu.VMEM((P, N_HEADS, C, C), jnp.float32),
        pltpu.VMEM((P, N_HEADS * ACC_ROWS, C), jnp.float32),
        pltpu.VMEM((P, N_HEADS, C), jnp.float32),
    ]
    vmem = P * (2 * S * C * 4 + Q_WIDTH * C * 4 + 2 * 2 * (2 * Q_WIDTH * C + S * KN_WIDTH + S * KV_WIDTH + C * Q_WIDTH)
                + 2 * N_HEADS * C * C * 4) + 24 * C * C * 4 + (8 << 20)
    return pl.pallas_call(
        kern, grid=(B // P, nc), in_specs=in_specs, out_specs=out_specs,
        out_shape=jax.ShapeDtypeStruct((B, S, Q_WIDTH), jnp.bfloat16),
        scratch_shapes=scratch, compiler_params=_cparams(2, vmem), name="dsa_attn",
    )(qt, qit, wit, kn, vt)


def _layer_norm(v, g, b):
    mu = jnp.mean(v, axis=-1, keepdims=True)
    d = v - mu
    var = jnp.mean(d * d, axis=-1, keepdims=True)
    return d * lax.rsqrt(var + LN_EPS) * g + b


def _pack_rows(h):
    hi = lax.bitcast_convert_type(h[:, :HALF].astype(jnp.bfloat16).astype(jnp.float32), jnp.int32)
    lo = lax.bitcast_convert_type(h[:, HALF:].astype(jnp.bfloat16).astype(jnp.float32), jnp.int32)
    return (hi & jnp.int32(-65536)) | lax.shift_right_logical(lo, 16)


def _unpack_rows(w):
    hi = lax.bitcast_convert_type(w & jnp.int32(-65536), jnp.float32)
    lo = lax.bitcast_convert_type(lax.shift_left(w, 16), jnp.float32)
    return jnp.concatenate([hi, lo], axis=1)


def _mix_kernel(x_ref, attn_ref, wc_ref, wg_ref, gb_ref, cw_ref, wau_ref, wcu_ref, wo_ref,
                g1_ref, b1_ref, wrh_ref, wrl_ref, rb_ref,
                h_ref, xs_ref, pos_ref, gate_ref, cnt_ref,
                ubuf_ref, *, steps_per_seq):
    R = MIX_ROWS
    i = pl.program_id(0)

    @pl.when(i % steps_per_seq == 0)
    def _():
        ubuf_ref[0:V7X_SUBLANES, :] = jnp.zeros((V7X_SUBLANES, CONV_DIM), jnp.float32)

    x = x_ref[...]
    xb = x.astype(jnp.bfloat16)
    cv = jnp.dot(xb, wc_ref[...], preferred_element_type=jnp.float32)
    u = cv[:, 2 * CONV_DIM:] * cv[:, :CONV_DIM]
    ubuf_ref[V7X_SUBLANES:, :] = u
    u1 = ubuf_ref[V7X_SUBLANES - 1:V7X_SUBLANES - 1 + R, :]
    u2 = ubuf_ref[V7X_SUBLANES - 2:V7X_SUBLANES - 2 + R, :]
    y = cw_ref[0:1, :] * u2 + cw_ref[1:2, :] * u1 + cw_ref[2:3, :] * u
    conv = (cv[:, CONV_DIM:2 * CONV_DIM] * y).astype(jnp.bfloat16)
    ubuf_ref[0:V7X_SUBLANES, :] = u[R - V7X_SUBLANES:, :]
    z = jnp.dot(xb, wg_ref[...], preferred_element_type=jnp.float32) + gb_ref[...]
    gates = 1.0 / (1.0 + jnp.exp(-z))
    au = jnp.dot(attn_ref[...], wau_ref[...], preferred_element_type=jnp.float32)
    cu = jnp.dot(conv, wcu_ref[...], preferred_element_type=jnp.float32)
    merged = gates[:, :D_MODEL] * au + gates[:, D_MODEL:] * cu
    mix = jnp.dot(merged.astype(jnp.bfloat16), wo_ref[...], preferred_element_type=jnp.float32)
    h = _layer_norm(DEEPNORM_ALPHA * x + mix, g1_ref[...], b1_ref[...])
    h_ref[...] = h

    h_hi = h.astype(jnp.bfloat16)
    h_lo = (h - h_hi.astype(jnp.float32)).astype(jnp.bfloat16)
    lg = (jnp.dot(h_hi, wrh_ref[...], preferred_element_type=jnp.float32)
          + jnp.dot(h_lo, wrh_ref[...], preferred_element_type=jnp.float32)
          + jnp.dot(h_hi, wrl_ref[...], preferred_element_type=jnp.float32)) + rb_ref[...]
    lane = lax.broadcasted_iota(jnp.int32, (R, ROUTER_LANES), 1).astype(jnp.float32)
    neg = jnp.float32(-jnp.inf)
    no_lane = jnp.float32(ROUTER_LANES)
    gmask = lane < N_GROUPS
    gl = jnp.where(gmask, lg, neg)
    gmax = jnp.max(gl, axis=1, keepdims=True)
    grp = jnp.min(jnp.where(gl == gmax, lane, no_lane), axis=1, keepdims=True)
    gsum = jnp.sum(jnp.where(gmask, jnp.exp(gl - gmax), 0.0), axis=1, keepdims=True)
    p_grp = 1.0 / gsum
    lo_lane = E0 + grp * EXPERTS_PER_GROUP
    emask = (lane >= lo_lane) & (lane < lo_lane + EXPERTS_PER_GROUP)
    el = jnp.where(emask, lg, neg)
    v1 = jnp.max(el, axis=1, keepdims=True)
    i1 = jnp.min(jnp.where(el == v1, lane, no_lane), axis=1, keepdims=True)
    el2 = jnp.where(lane == i1, neg, el)
    v2 = jnp.max(el2, axis=1, keepdims=True)
    i2 = jnp.min(jnp.where(el2 == v2, lane, no_lane), axis=1, keepdims=True)
    a = jnp.exp(v2 - v1)
    inv = 1.0 / (1.0 + a)
    gate_ref[:, 0:1] = p_grp * inv
    gate_ref[:, 1:2] = p_grp * (a * inv)
    oh1 = lane == i1
    oh2 = lane == i2
    oh = (oh1 | oh2).astype(jnp.bfloat16)
    r_i = lax.broadcasted_iota(jnp.int32, (R, R), 0)
    c_i = lax.broadcasted_iota(jnp.int32, (R, R), 1)
    tri = (r_i > c_i).astype(jnp.bfloat16)
    before = jnp.dot(tri, oh, preferred_element_type=jnp.float32)
    cnt = jnp.sum(oh.astype(jnp.float32), axis=0, keepdims=True)
    cnt_ref[0] = cnt

    seg = jnp.floor((cnt + (ROW_GROUP - 1)) * (1.0 / ROW_GROUP)) * ROW_GROUP
    l_r = lax.broadcasted_iota(jnp.int32, (ROUTER_LANES, ROUTER_LANES), 0)
    l_c = lax.broadcasted_iota(jnp.int32, (ROUTER_LANES, ROUTER_LANES), 1)
    lanes_before = (l_r < l_c).astype(jnp.bfloat16)
    seg8 = jnp.broadcast_to(seg, (V7X_SUBLANES, ROUTER_LANES)).astype(jnp.bfloat16)
    seg_off = jnp.dot(seg8, lanes_before, preferred_element_type=jnp.float32)[0:1, :]
    where_to = before + seg_off
    pos1 = jnp.sum(jnp.where(oh1, where_to, 0.0), axis=1, keepdims=True)
    pos2 = jnp.sum(jnp.where(oh2, where_to, 0.0), axis=1, keepdims=True)
    pos_ref[:, 0:1] = pos1.astype(jnp.int32)
    pos_ref[:, 1:2] = pos2.astype(jnp.int32)
    diag = r_i == c_i
    pos1_row = jnp.sum(jnp.where(diag, pos1, 0.0), axis=0, keepdims=True)
    pos2_row = jnp.sum(jnp.where(diag, pos2, 0.0), axis=0, keepdims=True)
    sorted_row = lax.broadcasted_iota(jnp.int32, (SORTED_ROWS, R), 0).astype(jnp.float32)
    place = ((sorted_row == pos1_row) | (sorted_row == pos2_row)).astype(jnp.bfloat16)
    xs = jnp.dot(place, h_hi, preferred_element_type=jnp.float32)
    xs_ref[0] = _pack_rows(xs)


def _mix_call(x2, attn2, wc, wg, gb, cw, wau, wcu, wo, g1, b1, wrh, wrl, rb, seq):
    N, D = x2.shape
    R = MIX_ROWS
    nt = N // R
    kern = functools.partial(_mix_kernel, steps_per_seq=seq // R)

    def full(a):
        return pl.BlockSpec(a.shape, lambda i: (0,) * a.ndim)

    in_specs = [
        pl.BlockSpec((R, D), lambda i: (i, 0)),
        pl.BlockSpec((R, Q_WIDTH), lambda i: (i, 0)),
        full(wc), full(wg), full(gb), full(cw), full(wau), full(wcu), full(wo),
        full(g1), full(b1), full(wrh), full(wrl), full(rb),
    ]
    out_shape = (
        jax.ShapeDtypeStruct((N, D), jnp.float32),
        jax.ShapeDtypeStruct((nt, SORTED_ROWS, HALF), jnp.int32),
        jax.ShapeDtypeStruct((N, 2), jnp.int32),
        jax.ShapeDtypeStruct((N, 2), jnp.float32),
        jax.ShapeDtypeStruct((nt, 1, ROUTER_LANES), jnp.float32),
    )
    out_specs = (
        pl.BlockSpec((R, D), lambda i: (i, 0)),
        pl.BlockSpec((1, SORTED_ROWS, HALF), lambda i: (i, 0, 0)),
        pl.BlockSpec((R, 2), lambda i: (i, 0)),
        pl.BlockSpec((R, 2), lambda i: (i, 0)),
        pl.BlockSpec((1, 1, ROUTER_LANES), lambda i: (i, 0, 0)),
    )
    scratch = [pltpu.VMEM((R + V7X_SUBLANES, CONV_DIM), jnp.float32)]
    w_bytes = 2 * (wc.size + wg.size + wau.size + wcu.size + wo.size + wrh.size + wrl.size)
    vmem = (2 * w_bytes + 2 * (R * D * 4 * 2 + R * Q_WIDTH * 2 + SORTED_ROWS * HALF * 4) + 10 * R * 2048 * 4
            + SORTED_ROWS * D * 6 + (6 << 20))
    return pl.pallas_call(
        kern, grid=(nt,), in_specs=in_specs, out_specs=out_specs, out_shape=out_shape,
        scratch_shapes=scratch, compiler_params=_cparams(1, vmem), name="mix_ln_router",
    )(x2, attn2, wc, wg, gb, cw, wau, wcu, wo, g1, b1, wrh, wrl, rb)


GROUPS_PER_BLOCK = EXPERT_ROWS // ROW_GROUP


def _expert_kernel(be_ref, nb_ref, gsrc_ref, gdst_ref, xs_ref, wg_ref, wu_ref, wd_ref, ys_ref,
                   wgb_ref, wub_ref, wdb_ref, xbuf_ref, ybuf_ref, gsems, ssems):
    del xs_ref
    j = pl.program_id(0)
    n_used = nb_ref[0]
    slot = j % 2

    def gather(block, to_slot):
        for g in range(GROUPS_PER_BLOCK):
            row = pl.multiple_of(gsrc_ref[block * GROUPS_PER_BLOCK + g], ROW_GROUP)
            pltpu.make_async_copy(ys_ref.at[pl.ds(row, ROW_GROUP), :],
                                  xbuf_ref.at[to_slot, pl.ds(g * ROW_GROUP, ROW_GROUP), :],
                                  gsems.at[to_slot]).start(priority=g % 2)

    def scatter(block, from_slot):
        for g in range(GROUPS_PER_BLOCK):
            row = pl.multiple_of(gdst_ref[block * GROUPS_PER_BLOCK + g], ROW_GROUP)
            pltpu.make_async_copy(ybuf_ref.at[from_slot, pl.ds(g * ROW_GROUP, ROW_GROUP), :],
                                  ys_ref.at[pl.ds(row, ROW_GROUP), :],
                                  ssems.at[from_slot]).start(priority=g % 2)

    def wait_gather(s):
        pltpu.make_async_copy(ys_ref.at[pl.ds(0, EXPERT_ROWS), :], xbuf_ref.at[s], gsems.at[s]).wait()

    def wait_scatter(s):
        pltpu.make_async_copy(ybuf_ref.at[s], ys_ref.at[pl.ds(0, EXPERT_ROWS), :], ssems.at[s]).wait()

    @pl.when((j == 0) & (n_used > 0))
    def _():
        gather(0, 0)

    @pl.when((j == 0) | (be_ref[j] != be_ref[jnp.maximum(j - 1, 0)]))
    def _():
        wgb_ref[...] = wg_ref[0].astype(jnp.bfloat16)
        wub_ref[...] = wu_ref[0].astype(jnp.bfloat16)
        wdb_ref[...] = wd_ref[0].astype(jnp.bfloat16)

    @pl.when(j + 1 < n_used)
    def _():
        gather(j + 1, 1 - slot)

    @pl.when((j >= 2) & (j < n_used))
    def _():
        wait_scatter(slot)

    @pl.when(j < n_used)
    def _():
        wait_gather(slot)
        xb = _unpack_rows(xbuf_ref[slot]).astype(jnp.bfloat16)
        hg = jnp.dot(xb, wgb_ref[...], preferred_element_type=jnp.float32)
        hu = jnp.dot(xb, wub_ref[...], preferred_element_type=jnp.float32)
        hid = (hg / (1.0 + jnp.exp(-hg))) * hu
        y = jnp.dot(hid.astype(jnp.bfloat16), wdb_ref[...], preferred_element_type=jnp.float32)
        ybuf_ref[slot] = _pack_rows(y)
        scatter(j, slot)

    @pl.when(j == n_used - 1)
    def _():
        wait_scatter(slot)

    @pl.when((j == n_used - 1) & (j >= 1))
    def _():
        wait_scatter(1 - slot)


def _expert_call(block_expert, n_used, group_src, group_dst, xs_sorted, wg, wu, wd, n_blocks):
    W = xs_sorted.shape[-1]
    RB = EXPERT_ROWS
    grid_spec = pltpu.PrefetchScalarGridSpec(
        num_scalar_prefetch=4, grid=(n_blocks,),
        in_specs=[
            pl.BlockSpec(memory_space=pl.ANY),
            pl.BlockSpec((1, D_MODEL, D_EXPERT), lambda j, be, nu, gs, gd: (be[j], 0, 0)),
            pl.BlockSpec((1, D_MODEL, D_EXPERT), lambda j, be, nu, gs, gd: (be[j], 0, 0)),
            pl.BlockSpec((1, D_EXPERT, D_MODEL), lambda j, be, nu, gs, gd: (be[j], 0, 0)),
        ],
        out_specs=pl.BlockSpec(memory_space=pl.ANY),
        scratch_shapes=[pltpu.VMEM((D_MODEL, D_EXPERT), jnp.bfloat16),
                        pltpu.VMEM((D_MODEL, D_EXPERT), jnp.bfloat16),
                        pltpu.VMEM((D_EXPERT, D_MODEL), jnp.bfloat16),
                        pltpu.VMEM((2, RB, W), jnp.int32),
                        pltpu.VMEM((2, RB, W), jnp.int32),
                        pltpu.SemaphoreType.DMA((2,)),
                        pltpu.SemaphoreType.DMA((2,))],
    )
    vmem = (2 * 4 + 2) * 3 * D_MODEL * D_EXPERT + 4 * RB * W * 4 + 8 * RB * D_MODEL * 4 + (6 << 20)
    return pl.pallas_call(
        _expert_kernel, grid_spec=grid_spec,
        out_shape=jax.ShapeDtypeStruct(xs_sorted.shape, xs_sorted.dtype),
        input_output_aliases={4: 0},
        compiler_params=_cparams(1, vmem), name="moe_experts",
    )(block_expert, n_used, group_src, group_dst, xs_sorted, wg, wu, wd)


def _combine_kernel(ys_ref, pos_ref, gate_ref, h_ref, g2_ref, b2_ref, o_ref):
    R = MIX_ROWS
    y = _unpack_rows(ys_ref[0]).astype(jnp.bfloat16)
    col = lax.broadcasted_iota(jnp.int32, (R, SORTED_ROWS), 1)
    pick = (jnp.where(col == pos_ref[:, 0:1], gate_ref[:, 0:1], 0.0)
            + jnp.where(col == pos_ref[:, 1:2], gate_ref[:, 1:2], 0.0)).astype(jnp.bfloat16)
    ffn = jnp.dot(pick, y, preferred_element_type=jnp.float32)
    o_ref[...] = _layer_norm(DEEPNORM_ALPHA * h_ref[...] + ffn, g2_ref[...], b2_ref[...])


def _combine_call(ys_sorted, pos, gate, h, g2, b2):
    N, D = h.shape
    R = MIX_ROWS
    nt = N // R
    W = ys_sorted.shape[-1]
    return pl.pallas_call(
        _combine_kernel, grid=(nt,),
        in_specs=[
            pl.BlockSpec((1, SORTED_ROWS, W), lambda i: (i, 0, 0)),
            pl.BlockSpec((R, 2), lambda i: (i, 0)),
            pl.BlockSpec((R, 2), lambda i: (i, 0)),
            pl.BlockSpec((R, D), lambda i: (i, 0)),
            pl.BlockSpec((1, D), lambda i: (0, 0)),
            pl.BlockSpec((1, D), lambda i: (0, 0)),
        ],
        out_specs=pl.BlockSpec((R, D), lambda i: (i, 0)),
        out_shape=jax.ShapeDtypeStruct((N, D), jnp.float32),
        compiler_params=_cparams(1, 2 * (SORTED_ROWS * W * 4 + 2 * R * D * 4) + SORTED_ROWS * D * 8
                                 + 3 * R * SORTED_ROWS * 4 + (6 << 20)),
        name="moe_combine",
    )(ys_sorted, pos, gate, h, g2, b2)


def _rope_tables(seq):
    inv_freq = ROPE_THETA ** (-jnp.arange(ROT_HALF, dtype=jnp.float32) / ROT_HALF)
    ang = jnp.arange(seq, dtype=jnp.int32).astype(jnp.float32)[:, None] * inv_freq[None, :]
    cos, sin = jnp.cos(ang), jnp.sin(ang)
    ones = jnp.ones((seq, HEAD_DIM - ROT_DIMS), jnp.float32)
    zeros = jnp.zeros((seq, HEAD_DIM - ROT_DIMS), jnp.float32)
    c_head = jnp.concatenate([cos, cos, ones], axis=1)
    s_head = jnp.concatenate([-sin, sin, zeros], axis=1)
    reps = KN_WIDTH // HEAD_DIM
    return cos.T, sin.T, jnp.tile(c_head, (1, reps)), jnp.tile(s_head, (1, reps))


def _swap_rot_cols(w):
    d, n = w.shape
    wh = w.reshape(d, n // HEAD_DIM, HEAD_DIM)
    sw = jnp.concatenate([wh[:, :, ROT_HALF:ROT_DIMS], wh[:, :, :ROT_HALF],
                          jnp.zeros((d, n // HEAD_DIM, HEAD_DIM - ROT_DIMS), w.dtype)], axis=2)
    return sw.reshape(d, n)


def _block(x, w_in, gate_bias, w_attn_up, w_conv_up, conv_w, w_out, ln_g, ln_b,
           rg_w, rg_b, re_w, re_b, w_gate_e, w_up_e, w_down_e, ln2_g, ln2_b):
    B, S, D = x.shape
    N = B * S
    top_k = min(TOPK_MAX, S // 4)
    bf = jnp.bfloat16
    o = np.cumsum([0, Q_WIDTH, KV_WIDTH, KV_WIDTH, IDXQ_WIDTH, IDX_DIM, IDX_HEADS,
                   CONV_DIM, CONV_DIM, CONV_DIM, N_BRANCHES * D_MODEL])
    w_q, w_k, w_v, w_qi, w_ki, w_wi = (w_in[:, o[i]:o[i + 1]] for i in range(6))
    w_conv = w_in[:, o[6]:o[9]]
    w_gates = w_in[:, o[9]:o[10]]

    wt = jnp.concatenate([w_q.T * (HEAD_DIM ** -0.5 * LOG2_E), w_qi.T, w_v.T, w_wi.T,
                          jnp.zeros((T_ROWS - T_WI0 - IDX_HEADS, D), w_in.dtype)], axis=0).astype(bf)
    pad = jnp.zeros((D, KN_WIDTH - KV_WIDTH - IDX_DIM), w_in.dtype)
    wn = jnp.concatenate([w_k, w_ki, pad, _swap_rot_cols(w_k), _swap_rot_cols(w_ki), pad], axis=1).astype(bf)
    cos_t, sin_t, cos_n, sgn_n = _rope_tables(S)

    qt, qit, vt, wit, kn = _proj_call(x, wt, wn, cos_t, sin_t, cos_n, sgn_n)
    attn = _attn_call(qt, qit, wit, kn, vt, top_k)

    w_r = jnp.concatenate([rg_w, jnp.transpose(re_w, (1, 0, 2)).reshape(D, N_EXPERTS),
                           jnp.zeros((D, ROUTER_LANES - E0 - N_EXPERTS), rg_w.dtype)], axis=1)
    b_r = jnp.concatenate([rg_b, re_b.reshape(-1),
                           jnp.zeros((ROUTER_LANES - E0 - N_EXPERTS,), rg_b.dtype)])[None, :]
    w_rh = w_r.astype(bf)
    w_rl = (w_r - w_rh.astype(jnp.float32)).astype(bf)

    h, xs_sorted, pos, gate, cnt = _mix_call(
        x.reshape(N, D), attn.reshape(N, Q_WIDTH), w_conv.astype(bf), w_gates.astype(bf),
        gate_bias[None, :], conv_w, w_attn_up.astype(bf), w_conv_up.astype(bf), w_out.astype(bf),
        ln_g[None, :], ln_b[None, :], w_rh, w_rl, b_r, S)

    RB, RG = EXPERT_ROWS, ROW_GROUP
    nt = N // MIX_ROWS
    counts = cnt[:, 0, E0:E0 + N_EXPERTS].astype(jnp.int32)
    seg_len = ((counts + RG - 1) // RG) * RG
    seg_local = jnp.cumsum(seg_len, axis=1) - seg_len
    region = jnp.sum(seg_len, axis=0)
    padded = ((region + RB - 1) // RB) * RB
    pad_end = jnp.cumsum(padded)
    pad_start = pad_end - padded
    seg_start = pad_start[None, :] + jnp.cumsum(seg_len, axis=0) - seg_len
    cap = -(-(N * 2 + nt * N_EXPERTS * (RG - 1) + N_EXPERTS * (RB - 1)) // RB) * RB
    nb = cap // RB
    block_start = jnp.arange(nb, dtype=jnp.int32) * RB
    block_expert = jnp.minimum(jnp.sum((block_start[:, None] >= pad_end[None, :]).astype(jnp.int32), axis=1),
                               N_EXPERTS - 1)
    n_used = (pad_end[-1:] // RB).astype(jnp.int32)
    step_base = (jnp.arange(nt, dtype=jnp.int32) * SORTED_ROWS)[:, None]
    g_idx = jnp.arange(cap // RG, dtype=jnp.int32)
    g_row = (g_idx * RG).reshape(nb, GROUPS_PER_BLOCK)
    seg_end_b = (seg_start + seg_len).T[block_expert]
    seg_start_b = seg_start.T[block_expert]
    seg_shift_b = (step_base + seg_local - seg_start).T[block_expert]
    ended = seg_end_b[:, None, :-1] <= g_row[:, :, None]

    def lookup(v):
        return v[:, None, 0] + jnp.sum(jnp.where(ended, (v[:, 1:] - v[:, :-1])[:, None, :], 0), axis=2)

    live = ((g_row >= lookup(seg_start_b)) & (g_row < lookup(seg_end_b))).reshape(-1)
    home = (g_row + lookup(seg_shift_b)).reshape(-1)
    spare_per_step = (SORTED_ROWS - SORTED_USED) // RG
    spare = ((g_idx // GROUPS_PER_BLOCK) % 2) * GROUPS_PER_BLOCK + g_idx % GROUPS_PER_BLOCK
    assert 2 * GROUPS_PER_BLOCK <= (nt - 1) * spare_per_step, "not enough spare groups for block padding"
    spare_row = (1 + spare // spare_per_step) * SORTED_ROWS + SORTED_USED + (spare % spare_per_step) * RG
    group_src = jnp.where(live, home, SORTED_ROWS - RG).astype(jnp.int32)
    group_dst = jnp.where(live, home, spare_row).astype(jnp.int32)

    ys_sorted = _expert_call(block_expert, n_used, group_src, group_dst,
                             xs_sorted.reshape(nt * SORTED_ROWS, HALF), w_gate_e, w_up_e, w_down_e, nb)
    out = _combine_call(ys_sorted.reshape(nt, SORTED_ROWS, HALF), pos, gate, h, ln2_g[None, :], ln2_b[None, :])
    return out.reshape(B, S, D)


def kernel(x, w_in, gate_bias, w_attn_up, w_conv_up, conv_w, w_out, ln1_g, ln1_b, router_group_w,
           router_group_b, router_expert_w, router_expert_b, w_gate_e, w_up_e, w_down_e, ln2_g, ln2_b):
    h = x
    for l in range(DEPTH):
        h = _block(h, w_in[l], gate_bias[l], w_attn_up[l], w_conv_up[l], conv_w[l], w_out[l],
                   ln1_g[l], ln1_b[l], router_group_w[l], router_group_b[l], router_expert_w[l],
                   router_expert_b[l], w_gate_e[l], w_up_e[l], w_down_e[l], ln2_g[l], ln2_b[l])
    return h
```

```python
import functools

import jax
import jax.numpy as jnp
import numpy as np
from jax import lax
from jax.experimental import pallas as pl
from jax.experimental.pallas import tpu as pltpu

D_MODEL = 1024
N_HEADS = 8
N_KV_HEADS = 2
HEAD_DIM = 64
Q_WIDTH = N_HEADS * HEAD_DIM
KV_WIDTH = N_KV_HEADS * HEAD_DIM
ROPE_THETA = 500000.0
ROT_DIMS = HEAD_DIM // 4
ROT_HALF = ROT_DIMS // 2
IDX_HEADS = 8
IDX_DIM = 64
IDXQ_WIDTH = IDX_HEADS * IDX_DIM
TOPK_MAX = 256
CONV_DIM = 512
CONV_WIDTH = 3
N_BRANCHES = 2
N_GROUPS = 4
EXPERTS_PER_GROUP = 8
N_EXPERTS = N_GROUPS * EXPERTS_PER_GROUP
D_EXPERT = 512
LN_EPS = 1e-5
DEPTH = 1
DEEPNORM_ALPHA = (2 * DEPTH) ** 0.25

V7X_SUBLANES = 8
V7X_VMEM_LIMIT_BYTES = 56 * 1024 * 1024

PROJ_ROWS = 512
ATTN_CHUNK = 256
ATTN_PAIR = 2
ATTN_KEY_TILE = 128
DEN_ROWS = 16
ACC_ROWS = HEAD_DIM + DEN_ROWS
LOG2_E = 1.4426950408889634
MIX_ROWS = 512
EXPERT_ROWS = 512
ROW_GROUP = V7X_SUBLANES
SORTED_USED = 2 * MIX_ROWS + N_EXPERTS * (ROW_GROUP - 1)
SORTED_ROWS = -(-(SORTED_USED + ROW_GROUP) // 256) * 256
HALF = D_MODEL // 2

T_Q0, T_QI0, T_V0, T_WI0 = 0, Q_WIDTH, Q_WIDTH + IDXQ_WIDTH, Q_WIDTH + IDXQ_WIDTH + KV_WIDTH
T_ROWS = T_WI0 + 16
KN_WIDTH = 256

INT_MIN = -2147483648
HALF_RANGE = 32768
ROUTER_LANES = 128
E0 = N_GROUPS


def _cparams(n_axes, vmem_bytes):
    return pltpu.CompilerParams(
        dimension_semantics=("arbitrary",) * n_axes,
        vmem_limit_bytes=int(min(vmem_bytes, V7X_VMEM_LIMIT_BYTES)),
    )


def _proj_kernel(x_ref, wt_ref, wn_ref, cost_ref, sint_ref, cosn_ref, sgnn_ref,
                 qt_ref, qit_ref, vt_ref, wit_ref, kn_ref):
    xb = x_ref[0].astype(jnp.bfloat16)
    pt = lax.dot_general(wt_ref[...], xb, (((1,), (1,)), ((), ())),
                         preferred_element_type=jnp.float32)
    cos = cost_ref[...]
    sin = sint_ref[...]

    def rope_t(dst_ref, base):
        for h in range(N_HEADS):
            r0 = base + h * HEAD_DIM
            x1 = pt[r0:r0 + ROT_HALF]
            x2 = pt[r0 + ROT_HALF:r0 + ROT_DIMS]
            head = jnp.concatenate([x1 * cos - x2 * sin, x2 * cos + x1 * sin,
                                    pt[r0 + ROT_DIMS:r0 + HEAD_DIM]], axis=0).astype(dst_ref.dtype)
            for jj, cols in enumerate(chunks):
                dst_ref[0, jj, h * HEAD_DIM:(h + 1) * HEAD_DIM, :] = head[:, cols]

    chunks = [slice(jj * ATTN_CHUNK, (jj + 1) * ATTN_CHUNK) for jj in range(PROJ_ROWS // ATTN_CHUNK)]
    rope_t(qt_ref, T_Q0)
    rope_t(qit_ref, T_QI0)
    for jj, cols in enumerate(chunks):
        vt_ref[0, jj] = pt[T_V0:T_V0 + KV_WIDTH, cols].astype(vt_ref.dtype)
        wit_ref[0, jj] = pt[T_WI0:T_WI0 + IDX_HEADS, cols]
    pn = jnp.dot(xb, wn_ref[...], preferred_element_type=jnp.float32)
    kn = pn[:, :KN_WIDTH] * cosn_ref[...] + pn[:, KN_WIDTH:] * sgnn_ref[...]
    kn_ref[0] = kn.astype(kn_ref.dtype)


def _proj_call(x, wt, wn, cos_t, sin_t, cos_n, sgn_n):
    B, S, D = x.shape
    R = PROJ_ROWS
    nt = S // R
    grid = (B, nt)
    out_shape = (
        jax.ShapeDtypeStruct((B, S // ATTN_CHUNK, Q_WIDTH, ATTN_CHUNK), jnp.bfloat16),
        jax.ShapeDtypeStruct((B, S // ATTN_CHUNK, IDXQ_WIDTH, ATTN_CHUNK), jnp.bfloat16),
        jax.ShapeDtypeStruct((B, S // ATTN_CHUNK, KV_WIDTH, ATTN_CHUNK), jnp.bfloat16),
        jax.ShapeDtypeStruct((B, S // ATTN_CHUNK, IDX_HEADS, ATTN_CHUNK), jnp.float32),
        jax.ShapeDtypeStruct((B, S, KN_WIDTH), jnp.bfloat16),
    )
    in_specs = [
        pl.BlockSpec((1, R, D), lambda b, j: (b, j, 0)),
        pl.BlockSpec((T_ROWS, D), lambda b, j: (0, 0)),
        pl.BlockSpec((D, 2 * KN_WIDTH), lambda b, j: (0, 0)),
        pl.BlockSpec((ROT_HALF, R), lambda b, j: (0, j)),
        pl.BlockSpec((ROT_HALF, R), lambda b, j: (0, j)),
        pl.BlockSpec((R, KN_WIDTH), lambda b, j: (j, 0)),
        pl.BlockSpec((R, KN_WIDTH), lambda b, j: (j, 0)),
    ]
    out_specs = (
        pl.BlockSpec((1, R // ATTN_CHUNK, Q_WIDTH, ATTN_CHUNK), lambda b, j: (b, j, 0, 0)),
        pl.BlockSpec((1, R // ATTN_CHUNK, IDXQ_WIDTH, ATTN_CHUNK), lambda b, j: (b, j, 0, 0)),
        pl.BlockSpec((1, R // ATTN_CHUNK, KV_WIDTH, ATTN_CHUNK), lambda b, j: (b, j, 0, 0)),
        pl.BlockSpec((1, R // ATTN_CHUNK, IDX_HEADS, ATTN_CHUNK), lambda b, j: (b, j, 0, 0)),
        pl.BlockSpec((1, R, KN_WIDTH), lambda b, j: (b, j, 0)),
    )
    vmem = 2 * (R * D * 4 + T_ROWS * D * 2 + D * 2 * KN_WIDTH * 2) + 6 * T_ROWS * R * 4 + (8 << 20)
    return pl.pallas_call(
        _proj_kernel, grid=grid, in_specs=in_specs, out_specs=out_specs, out_shape=out_shape,
        compiler_params=_cparams(2, vmem), name="dsa_proj",
    )(x, wt, wn, cos_t, sin_t, cos_n, sgn_n)


def _float_to_key(s):
    b = lax.bitcast_convert_type(s, jnp.int32)
    k = b ^ (lax.shift_right_arithmetic(b, 31) & jnp.int32(0x7FFFFFFF))
    return jnp.where(b == jnp.int32(INT_MIN), jnp.int32(0), k)


def _attn_kernel(qt_ref, qit_ref, wit_ref, kn_ref, vt_ref, o_ref,
                 key_ref, hi_ref, lo_ref, lga_ref, lgb_ref, acc_ref, m_ref, *, seq, top_k):
    C = ATTN_CHUNK
    E = range(ATTN_PAIR)
    c = pl.program_id(1)
    nkb = c + 1
    zeros_half = jnp.zeros((HEAD_DIM, C), jnp.bfloat16)
    row_i = lax.broadcasted_iota(jnp.int32, (C, C), 0)
    lane_i = lax.broadcasted_iota(jnp.int32, (C, C), 1)
    causal_in_block = row_i <= lane_i

    def score_block(e, kb):
        r0 = pl.multiple_of(kb * C, C)
        kix = kn_ref[e, pl.ds(r0, C), KV_WIDTH:KN_WIDTH]
        acc = None
        for h in range(IDX_HEADS):
            rhs = jnp.concatenate([qit_ref[e, 0, h * IDX_DIM:(h + 1) * IDX_DIM, :], zeros_half], axis=0)
            s = jnp.dot(kix, rhs, preferred_element_type=jnp.float32)
            t = jnp.maximum(s, 0.0) * wit_ref[e, 0, h:h + 1, :]
            acc = t if acc is None else acc + t
        score = acc * (IDX_DIM ** -0.5 * IDX_HEADS ** -0.5)
        keys = jnp.where(causal_in_block | (kb != c), _float_to_key(score), jnp.int32(INT_MIN))
        key_ref[e, kb] = keys
        hi_ref[e, kb] = lax.shift_right_arithmetic(keys, 16).astype(jnp.int16)
        lo_ref[e, kb] = ((keys & jnp.int32(0xFFFF)) - jnp.int32(HALF_RANGE)).astype(jnp.int16)

    def score_body(kb, carry):
        for e in E:
            score_block(e, kb)
        return carry

    lax.fori_loop(0, nkb, score_body, 0)

    def count(ref, pred, blocks=None):
        packed = ref.dtype == jnp.int16

        def body(kb, part):
            hit = pred(ref[kb], kb)
            if packed:
                words = pltpu.bitcast(jnp.where(hit, jnp.int16(1), jnp.int16(0)), jnp.int32)
            else:
                words = hit.astype(jnp.int32)
            return part + jnp.sum(words.reshape(-1, V7X_SUBLANES, C), axis=0)

        part = jnp.zeros((V7X_SUBLANES, C), jnp.int32)
        if blocks is None:
            part = lax.fori_loop(0, nkb, body, part)
        else:
            for kb in range(blocks):
                part = body(kb, part)
        if packed:
            part = (part & jnp.int32(0xFFFF)) + lax.shift_right_logical(part, 16)
        return jnp.sum(part, axis=0, keepdims=True)

    def kth_largest_16(refs, blocks):
        def body(i, answers):
            out = []
            for ref, ans_u in zip(refs, answers):
                cand_u = ans_u | lax.shift_left(jnp.int32(1), jnp.int32(15) - i)
                cand = (cand_u - jnp.int32(HALF_RANGE)).astype(jnp.int16)
                out.append(jnp.where(count(ref, lambda k, kb, cand=cand: k >= cand, blocks) >= top_k, cand_u, ans_u))
            return tuple(out)

        start = tuple(jnp.zeros((1, C), jnp.int32) for _ in refs)
        return [a - jnp.int32(HALF_RANGE) for a in lax.fori_loop(0, 16, body, start)]

    def threshold(blocks):
        hi_es = [hi_ref.at[e] for e in E]
        lo_es = [lo_ref.at[e] for e in E]
        t_his = kth_largest_16(hi_es, blocks)
        for hi_e, lo_e, t_hi in zip(hi_es, lo_es, t_his):
            t_hi16 = t_hi.astype(jnp.int16)
            for kb in range(blocks):
                hi = hi_e[kb]
                lo_e[kb] = jnp.where(hi > t_hi16, jnp.int16(HALF_RANGE - 1),
                                     jnp.where(hi == t_hi16, lo_e[kb], jnp.int16(-HALF_RANGE)))
        t_los = kth_largest_16(lo_es, blocks)
        n_ges = [count(lo_e, lambda k, kb, t=t_lo.astype(jnp.int16): k >= t, blocks)
                 for lo_e, t_lo in zip(lo_es, t_los)]
        return tuple(t_his) + tuple(t_los) + tuple(n_ges)

    neg_inf = jnp.float32(-jnp.inf)
    found = lax.switch(c, [functools.partial(threshold, n + 1) for n in range(seq // C)])
    thrs = []
    for e in E:
        t_hi, t_lo, n_ge = found[e], found[ATTN_PAIR + e], found[2 * ATTN_PAIR + e]
        thr_raw = t_hi * jnp.int32(2 * HALF_RANGE) + (t_lo + jnp.int32(HALF_RANGE))
        thr = jnp.maximum(thr_raw, jnp.int32(INT_MIN + 1))
        thrs.append(thr)
        key_e = key_ref.at[e]

        has_ties = jnp.max(jnp.where(thr_raw != jnp.int32(INT_MIN), n_ge, 0)) > top_k

        @pl.when(has_ties)
        def _(thr=thr, key_e=key_e):
            need = (top_k - count(key_e, lambda k, kb: k > thr)).astype(jnp.float32)
            earlier_rows = (row_i > lane_i).astype(jnp.bfloat16)

            def drop_body(kb, seen):
                kk = key_e[kb]
                tied = kk == thr
                tied_b = jnp.where(tied, 1.0, 0.0).astype(jnp.bfloat16)
                rank = jnp.dot(earlier_rows, tied_b, preferred_element_type=jnp.float32) + seen
                key_e[kb] = jnp.where(tied & (rank >= need), kk - 1, kk)
                return seen + jnp.sum(tied_b.astype(jnp.float32), axis=0, keepdims=True)

            lax.fori_loop(0, nkb, drop_body, jnp.zeros((1, C), jnp.float32))

    m_ref[...] = jnp.full(m_ref.shape, neg_inf, jnp.float32)
    acc_ref[...] = jnp.zeros(acc_ref.shape, jnp.float32)
    T = ATTN_KEY_TILE
    ones_rows = jnp.ones((DEN_ROWS, C), jnp.bfloat16)

    kv_group = N_HEADS // N_KV_HEADS

    def bias_body(kb, carry):
        for e in E:
            key_ref[e, kb] = lax.bitcast_convert_type(jnp.where(key_ref[e, kb] >= thrs[e], 0.0, neg_inf), jnp.int32)
        return carry

    lax.fori_loop(0, nkb, bias_body, 0)

    def store_logits(e, kb, dst_ref, h):
        kb = jnp.minimum(kb, nkb - 1)
        r0 = pl.multiple_of(kb * C, C)
        bias = lax.bitcast_convert_type(key_ref[e, kb], jnp.float32)
        k2 = kn_ref[e, pl.ds(r0, C), 0:KV_WIDTH]
        qh = qt_ref[e, 0, h * HEAD_DIM:(h + 1) * HEAD_DIM, :]
        rhs = jnp.concatenate([qh, zeros_half] if h < kv_group else [zeros_half, qh], axis=0)
        dst_ref[e, h] = jnp.dot(k2, rhs, preferred_element_type=jnp.float32) + bias

    def softmax_block(kb, src_ref, dst_ref):
        for h in range(N_HEADS):
            for e in E:
                if dst_ref is not None:
                    store_logits(e, kb + 1, dst_ref, h)
                g = h // kv_group
                m_old = m_ref[e, h:h + 1, :]
                m_new = jnp.maximum(m_old, jnp.max(src_ref[e, h], axis=0, keepdims=True))
                m_safe = jnp.where(m_new == neg_inf, 0.0, m_new)
                p = jnp.concatenate(
                    [jnp.exp2(src_ref[e, h, s * T:(s + 1) * T, :] - m_safe).astype(jnp.bfloat16)
                     for s in range(C // T)], axis=0)
                alpha = jnp.exp2(m_old - m_safe)
                vt = vt_ref[e, kb, g * HEAD_DIM:(g + 1) * HEAD_DIM, :]
                pv = jnp.dot(jnp.concatenate([vt, ones_rows], axis=0), p,
                             preferred_element_type=jnp.float32)
                hs = slice(h * ACC_ROWS, (h + 1) * ACC_ROWS)
                acc_ref[e, hs, :] = alpha * acc_ref[e, hs, :] + pv
                m_ref[e, h:h + 1, :] = m_new

    for h in range(N_HEADS):
        for e in E:
            store_logits(e, 0, lga_ref, h)

    def attn_body(pair, carry):
        kb = 2 * pair
        softmax_block(kb, lga_ref, lgb_ref)
        softmax_block(kb + 1, lgb_ref, lga_ref)
        return carry

    lax.fori_loop(0, nkb // 2, attn_body, 0)

    @pl.when(nkb % 2 == 1)
    def _():
        softmax_block(nkb - 1, lga_ref, None)

    for e in E:
        outs = []
        for h in range(N_HEADS):
            num = acc_ref[e, h * ACC_ROWS:h * ACC_ROWS + HEAD_DIM, :]
            den = acc_ref[e, h * ACC_ROWS + HEAD_DIM:h * ACC_ROWS + HEAD_DIM + 1, :]
            outs.append(num / den)
        o_ref[e] = jnp.transpose(jnp.concatenate(outs, axis=0)).astype(o_ref.dtype)


def _attn_call(qt, qit, wit, kn, vt, top_k):
    B, S = kn.shape[:2]
    C = ATTN_CHUNK
    P = ATTN_PAIR
    nc = S // C
    kern = functools.partial(_attn_kernel, seq=S, top_k=top_k)
    in_specs = [
        pl.BlockSpec((P, 1, Q_WIDTH, C), lambda b, c: (b, c, 0, 0)),
        pl.BlockSpec((P, 1, IDXQ_WIDTH, C), lambda b, c: (b, c, 0, 0)),
        pl.BlockSpec((P, 1, IDX_HEADS, C), lambda b, c: (b, c, 0, 0)),
        pl.BlockSpec((P, S, KN_WIDTH), lambda b, c: (b, 0, 0)),
        pl.BlockSpec((P, nc, KV_WIDTH, C), lambda b, c: (b, 0, 0, 0)),
    ]
    out_specs = pl.BlockSpec((P, C, Q_WIDTH), lambda b, c: (b, c, 0))
    scratch = [
        pltpu.VMEM((P, nc, C, C), jnp.int32),
        pltpu.VMEM((P, nc, C, C), jnp.int16),
        pltpu.VMEM((P, nc, C, C), jnp.int16),
        pltpu.VMEM((P, N_HEADS, C, C), jnp.float32),
        pltpu.VMEM((P, N_HEADS, C, C), jnp.float32),
        pltpu.VMEM((P, N_HEADS * ACC_ROWS, C), jnp.float32),
        pltpu.VMEM((P, N_HEADS, C), jnp.float32),
    ]
    vmem = P * (2 * S * C * 4 + Q_WIDTH * C * 4 + 2 * 2 * (2 * Q_WIDTH * C + S * KN_WIDTH + S * KV_WIDTH + C * Q_WIDTH)
                + 2 * N_HEADS * C * C * 4) + 24 * C * C * 4 + (8 << 20)
    return pl.pallas_call(
        kern, grid=(B // P, nc), in_specs=in_specs, out_specs=out_specs,
        out_shape=jax.ShapeDtypeStruct((B, S, Q_WIDTH), jnp.bfloat16),
        scratch_shapes=scratch, compiler_params=_cparams(2, vmem), name="dsa_attn",
    )(qt, qit, wit, kn, vt)


def _layer_norm(v, g, b):
    mu = jnp.mean(v, axis=-1, keepdims=True)
    d = v - mu
    var = jnp.mean(d * d, axis=-1, keepdims=True)
    return d * lax.rsqrt(var + LN_EPS) * g + b


def _pack_rows(h):
    hi = lax.bitcast_convert_type(h[:, :HALF].astype(jnp.bfloat16).astype(jnp.float32), jnp.int32)
    lo = lax.bitcast_convert_type(h[:, HALF:].astype(jnp.bfloat16).astype(jnp.float32), jnp.int32)
    return (hi & jnp.int32(-65536)) | lax.shift_right_logical(lo, 16)


def _unpack_rows(w):
    hi = lax.bitcast_convert_type(w & jnp.int32(-65536), jnp.float32)
    lo = lax.bitcast_convert_type(lax.shift_left(w, 16), jnp.float32)
    return jnp.concatenate([hi, lo], axis=1)


def _mix_kernel(x_ref, attn_ref, wc_ref, wg_ref, gb_ref, cw_ref, wau_ref, wcu_ref, wo_ref,
                g1_ref, b1_ref, wrh_ref, wrl_ref, rb_ref,
                h_ref, xs_ref, pos_ref, gate_ref, cnt_ref,
                ubuf_ref, *, steps_per_seq):
    R = MIX_ROWS
    i = pl.program_id(0)

    @pl.when(i % steps_per_seq == 0)
    def _():
        ubuf_ref[0:V7X_SUBLANES, :] = jnp.zeros((V7X_SUBLANES, CONV_DIM), jnp.float32)

    x = x_ref[...]
    xb = x.astype(jnp.bfloat16)
    cv = jnp.dot(xb, wc_ref[...], preferred_element_type=jnp.float32)
    u = cv[:, 2 * CONV_DIM:] * cv[:, :CONV_DIM]
    ubuf_ref[V7X_SUBLANES:, :] = u
    u1 = ubuf_ref[V7X_SUBLANES - 1:V7X_SUBLANES - 1 + R, :]
    u2 = ubuf_ref[V7X_SUBLANES - 2:V7X_SUBLANES - 2 + R, :]
    y = cw_ref[0:1, :] * u2 + cw_ref[1:2, :] * u1 + cw_ref[2:3, :] * u
    conv = (cv[:, CONV_DIM:2 * CONV_DIM] * y).astype(jnp.bfloat16)
    ubuf_ref[0:V7X_SUBLANES, :] = u[R - V7X_SUBLANES:, :]
    z = jnp.dot(xb, wg_ref[...], preferred_element_type=jnp.float32) + gb_ref[...]
    gates = 1.0 / (1.0 + jnp.exp(-z))
    au = jnp.dot(attn_ref[...], wau_ref[...], preferred_element_type=jnp.float32)
    cu = jnp.dot(conv, wcu_ref[...], preferred_element_type=jnp.float32)
    merged = gates[:, :D_MODEL] * au + gates[:, D_MODEL:] * cu
    mix = jnp.dot(merged.astype(jnp.bfloat16), wo_ref[...], preferred_element_type=jnp.float32)
    h = _layer_norm(DEEPNORM_ALPHA * x + mix, g1_ref[...], b1_ref[...])
    h_ref[...] = h

    h_hi = h.astype(jnp.bfloat16)
    h_lo = (h - h_hi.astype(jnp.float32)).astype(jnp.bfloat16)
    lg = (jnp.dot(h_hi, wrh_ref[...], preferred_element_type=jnp.float32)
          + jnp.dot(h_lo, wrh_ref[...], preferred_element_type=jnp.float32)
          + jnp.dot(h_hi, wrl_ref[...], preferred_element_type=jnp.float32)) + rb_ref[...]
    lane = lax.broadcasted_iota(jnp.int32, (R, ROUTER_LANES), 1).astype(jnp.float32)
    neg = jnp.float32(-jnp.inf)
    no_lane = jnp.float32(ROUTER_LANES)
    gmask = lane < N_GROUPS
    gl = jnp.where(gmask, lg, neg)
    gmax = jnp.max(gl, axis=1, keepdims=True)
    grp = jnp.min(jnp.where(gl == gmax, lane, no_lane), axis=1, keepdims=True)
    gsum = jnp.sum(jnp.where(gmask, jnp.exp(gl - gmax), 0.0), axis=1, keepdims=True)
    p_grp = 1.0 / gsum
    lo_lane = E0 + grp * EXPERTS_PER_GROUP
    emask = (lane >= lo_lane) & (lane < lo_lane + EXPERTS_PER_GROUP)
    el = jnp.where(emask, lg, neg)
    v1 = jnp.max(el, axis=1, keepdims=True)
    i1 = jnp.min(jnp.where(el == v1, lane, no_lane), axis=1, keepdims=True)
    el2 = jnp.where(lane == i1, neg, el)
    v2 = jnp.max(el2, axis=1, keepdims=True)
    i2 = jnp.min(jnp.where(el2 == v2, lane, no_lane), axis=1, keepdims=True)
    a = jnp.exp(v2 - v1)
    inv = 1.0 / (1.0 + a)
    gate_ref[:, 0:1] = p_grp * inv
    gate_ref[:, 1:2] = p_grp * (a * inv)
    oh1 = lane == i1
    oh2 = lane == i2
    oh = (oh1 | oh2).astype(jnp.bfloat16)
    r_i = lax.broadcasted_iota(jnp.int32, (R, R), 0)
    c_i = lax.broadcasted_iota(jnp.int32, (R, R), 1)
    tri = (r_i > c_i).astype(jnp.bfloat16)
    before = jnp.dot(tri, oh, preferred_element_type=jnp.float32)
    cnt = jnp.sum(oh.astype(jnp.float32), axis=0, keepdims=True)
    cnt_ref[0] = cnt

    seg = jnp.floor((cnt + (ROW_GROUP - 1)) * (1.0 / ROW_GROUP)) * ROW_GROUP
    l_r = lax.broadcasted_iota(jnp.int32, (ROUTER_LANES, ROUTER_LANES), 0)
    l_c = lax.broadcasted_iota(jnp.int32, (ROUTER_LANES, ROUTER_LANES), 1)
    lanes_before = (l_r < l_c).astype(jnp.bfloat16)
    seg8 = jnp.broadcast_to(seg, (V7X_SUBLANES, ROUTER_LANES)).astype(jnp.bfloat16)
    seg_off = jnp.dot(seg8, lanes_before, preferred_element_type=jnp.float32)[0:1, :]
    where_to = before + seg_off
    pos1 = jnp.sum(jnp.where(oh1, where_to, 0.0), axis=1, keepdims=True)
    pos2 = jnp.sum(jnp.where(oh2, where_to, 0.0), axis=1, keepdims=True)
    pos_ref[:, 0:1] = pos1.astype(jnp.int32)
    pos_ref[:, 1:2] = pos2.astype(jnp.int32)
    diag = r_i == c_i
    pos1_row = jnp.sum(jnp.where(diag, pos1, 0.0), axis=0, keepdims=True)
    pos2_row = jnp.sum(jnp.where(diag, pos2, 0.0), axis=0, keepdims=True)
    sorted_row = lax.broadcasted_iota(jnp.int32, (SORTED_ROWS, R), 0).astype(jnp.float32)
    place = ((sorted_row == pos1_row) | (sorted_row == pos2_row)).astype(jnp.bfloat16)
    xs = jnp.dot(place, h_hi, preferred_element_type=jnp.float32)
    xs_ref[0] = _pack_rows(xs)


def _mix_call(x2, attn2, wc, wg, gb, cw, wau, wcu, wo, g1, b1, wrh, wrl, rb, seq):
    N, D = x2.shape
    R = MIX_ROWS
    nt = N // R
    kern = functools.partial(_mix_kernel, steps_per_seq=seq // R)

    def full(a):
        return pl.BlockSpec(a.shape, lambda i: (0,) * a.ndim)

    in_specs = [
        pl.BlockSpec((R, D), lambda i: (i, 0)),
        pl.BlockSpec((R, Q_WIDTH), lambda i: (i, 0)),
        full(wc), full(wg), full(gb), full(cw), full(wau), full(wcu), full(wo),
        full(g1), full(b1), full(wrh), full(wrl), full(rb),
    ]
    out_shape = (
        jax.ShapeDtypeStruct((N, D), jnp.float32),
        jax.ShapeDtypeStruct((nt, SORTED_ROWS, HALF), jnp.int32),
        jax.ShapeDtypeStruct((N, 2), jnp.int32),
        jax.ShapeDtypeStruct((N, 2), jnp.float32),
        jax.ShapeDtypeStruct((nt, 1, ROUTER_LANES), jnp.float32),
    )
    out_specs = (
        pl.BlockSpec((R, D), lambda i: (i, 0)),
        pl.BlockSpec((1, SORTED_ROWS, HALF), lambda i: (i, 0, 0)),
        pl.BlockSpec((R, 2), lambda i: (i, 0)),
        pl.BlockSpec((R, 2), lambda i: (i, 0)),
        pl.BlockSpec((1, 1, ROUTER_LANES), lambda i: (i, 0, 0)),
    )
    scratch = [pltpu.VMEM((R + V7X_SUBLANES, CONV_DIM), jnp.float32)]
    w_bytes = 2 * (wc.size + wg.size + wau.size + wcu.size + wo.size + wrh.size + wrl.size)
    vmem = (2 * w_bytes + 2 * (R * D * 4 * 2 + R * Q_WIDTH * 2 + SORTED_ROWS * HALF * 4) + 10 * R * 2048 * 4
            + SORTED_ROWS * D * 6 + (6 << 20))
    return pl.pallas_call(
        kern, grid=(nt,), in_specs=in_specs, out_specs=out_specs, out_shape=out_shape,
        scratch_shapes=scratch, compiler_params=_cparams(1, vmem), name="mix_ln_router",
    )(x2, attn2, wc, wg, gb, cw, wau, wcu, wo, g1, b1, wrh, wrl, rb)


GROUPS_PER_BLOCK = EXPERT_ROWS // ROW_GROUP


def _expert_kernel(be_ref, nb_ref, gsrc_ref, gdst_ref, xs_ref, wg_ref, wu_ref, wd_ref, ys_ref,
                   wgb_ref, wub_ref, wdb_ref, xbuf_ref, ybuf_ref, gsems, ssems):
    del xs_ref
    j = pl.program_id(0)
    n_used = nb_ref[0]
    slot = j % 2

    def gather(block, to_slot):
        for g in range(GROUPS_PER_BLOCK):
            row = pl.multiple_of(gsrc_ref[block * GROUPS_PER_BLOCK + g], ROW_GROUP)
            pltpu.make_async_copy(ys_ref.at[pl.ds(row, ROW_GROUP), :],
                                  xbuf_ref.at[to_slot, pl.ds(g * ROW_GROUP, ROW_GROUP), :],
                                  gsems.at[to_slot]).start(priority=g % 2)

    def scatter(block, from_slot):
        for g in range(GROUPS_PER_BLOCK):
            row = pl.multiple_of(gdst_ref[block * GROUPS_PER_BLOCK + g], ROW_GROUP)
            pltpu.make_async_copy(ybuf_ref.at[from_slot, pl.ds(g * ROW_GROUP, ROW_GROUP), :],
                                  ys_ref.at[pl.ds(row, ROW_GROUP), :],
                                  ssems.at[from_slot]).start(priority=g % 2)

    def wait_gather(s):
        pltpu.make_async_copy(ys_ref.at[pl.ds(0, EXPERT_ROWS), :], xbuf_ref.at[s], gsems.at[s]).wait()

    def wait_scatter(s):
        pltpu.make_async_copy(ybuf_ref.at[s], ys_ref.at[pl.ds(0, EXPERT_ROWS), :], ssems.at[s]).wait()

    @pl.when((j == 0) & (n_used > 0))
    def _():
        gather(0, 0)

    @pl.when((j == 0) | (be_ref[j] != be_ref[jnp.maximum(j - 1, 0)]))
    def _():
        wgb_ref[...] = wg_ref[0].astype(jnp.bfloat16)
        wub_ref[...] = wu_ref[0].astype(jnp.bfloat16)
        wdb_ref[...] = wd_ref[0].astype(jnp.bfloat16)

    @pl.when(j + 1 < n_used)
    def _():
        gather(j + 1, 1 - slot)

    @pl.when((j >= 2) & (j < n_used))
    def _():
        wait_scatter(slot)

    @pl.when(j < n_used)
    def _():
        wait_gather(slot)
        xb = _unpack_rows(xbuf_ref[slot]).astype(jnp.bfloat16)
        hg = jnp.dot(xb, wgb_ref[...], preferred_element_type=jnp.float32)
        hu = jnp.dot(xb, wub_ref[...], preferred_element_type=jnp.float32)
        hid = (hg / (1.0 + jnp.exp(-hg))) * hu
        y = jnp.dot(hid.astype(jnp.bfloat16), wdb_ref[...], preferred_element_type=jnp.float32)
        ybuf_ref[slot] = _pack_rows(y)
        scatter(j, slot)

    @pl.when(j == n_used - 1)
    def _():
        wait_scatter(slot)

    @pl.when((j == n_used - 1) & (j >= 1))
    def _():
        wait_scatter(1 - slot)


def _expert_call(block_expert, n_used, group_src, group_dst, xs_sorted, wg, wu, wd, n_blocks):
    W = xs_sorted.shape[-1]
    RB = EXPERT_ROWS
    grid_spec = pltpu.PrefetchScalarGridSpec(
        num_scalar_prefetch=4, grid=(n_blocks,),
        in_specs=[
            pl.BlockSpec(memory_space=pl.ANY),
            pl.BlockSpec((1, D_MODEL, D_EXPERT), lambda j, be, nu, gs, gd: (be[j], 0, 0)),
            pl.BlockSpec((1, D_MODEL, D_EXPERT), lambda j, be, nu, gs, gd: (be[j], 0, 0)),
            pl.BlockSpec((1, D_EXPERT, D_MODEL), lambda j, be, nu, gs, gd: (be[j], 0, 0)),
        ],
        out_specs=pl.BlockSpec(memory_space=pl.ANY),
        scratch_shapes=[pltpu.VMEM((D_MODEL, D_EXPERT), jnp.bfloat16),
                        pltpu.VMEM((D_MODEL, D_EXPERT), jnp.bfloat16),
                        pltpu.VMEM((D_EXPERT, D_MODEL), jnp.bfloat16),
                        pltpu.VMEM((2, RB, W), jnp.int32),
                        pltpu.VMEM((2, RB, W), jnp.int32),
                        pltpu.SemaphoreType.DMA((2,)),
                        pltpu.SemaphoreType.DMA((2,))],
    )
    vmem = (2 * 4 + 2) * 3 * D_MODEL * D_EXPERT + 4 * RB * W * 4 + 8 * RB * D_MODEL * 4 + (6 << 20)
    return pl.pallas_call(
        _expert_kernel, grid_spec=grid_spec,
        out_shape=jax.ShapeDtypeStruct(xs_sorted.shape, xs_sorted.dtype),
        input_output_aliases={4: 0},
        compiler_params=_cparams(1, vmem), name="moe_experts",
    )(block_expert, n_used, group_src, group_dst, xs_sorted, wg, wu, wd)


def _combine_kernel(ys_ref, pos_ref, gate_ref, h_ref, g2_ref, b2_ref, o_ref):
    R = MIX_ROWS
    y = _unpack_rows(ys_ref[0]).astype(jnp.bfloat16)
    col = lax.broadcasted_iota(jnp.int32, (R, SORTED_ROWS), 1)
    pick = (jnp.where(col == pos_ref[:, 0:1], gate_ref[:, 0:1], 0.0)
            + jnp.where(col == pos_ref[:, 1:2], gate_ref[:, 1:2], 0.0)).astype(jnp.bfloat16)
    ffn = jnp.dot(pick, y, preferred_element_type=jnp.float32)
    o_ref[...] = _layer_norm(DEEPNORM_ALPHA * h_ref[...] + ffn, g2_ref[...], b2_ref[...])


def _combine_call(ys_sorted, pos, gate, h, g2, b2):
    N, D = h.shape
    R = MIX_ROWS
    nt = N // R
    W = ys_sorted.shape[-1]
    return pl.pallas_call(
        _combine_kernel, grid=(nt,),
        in_specs=[
            pl.BlockSpec((1, SORTED_ROWS, W), lambda i: (i, 0, 0)),
            pl.BlockSpec((R, 2), lambda i: (i, 0)),
            pl.BlockSpec((R, 2), lambda i: (i, 0)),
            pl.BlockSpec((R, D), lambda i: (i, 0)),
            pl.BlockSpec((1, D), lambda i: (0, 0)),
            pl.BlockSpec((1, D), lambda i: (0, 0)),
        ],
        out_specs=pl.BlockSpec((R, D), lambda i: (i, 0)),
        out_shape=jax.ShapeDtypeStruct((N, D), jnp.float32),
        compiler_params=_cparams(1, 2 * (SORTED_ROWS * W * 4 + 2 * R * D * 4) + SORTED_ROWS * D * 8
                                 + 3 * R * SORTED_ROWS * 4 + (6 << 20)),
        name="moe_combine",
    )(ys_sorted, pos, gate, h, g2, b2)


def _rope_tables(seq):
    inv_freq = ROPE_THETA ** (-jnp.arange(ROT_HALF, dtype=jnp.float32) / ROT_HALF)
    ang = jnp.arange(seq, dtype=jnp.int32).astype(jnp.float32)[:, None] * inv_freq[None, :]
    cos, sin = jnp.cos(ang), jnp.sin(ang)
    ones = jnp.ones((seq, HEAD_DIM - ROT_DIMS), jnp.float32)
    zeros = jnp.zeros((seq, HEAD_DIM - ROT_DIMS), jnp.float32)
    c_head = jnp.concatenate([cos, cos, ones], axis=1)
    s_head = jnp.concatenate([-sin, sin, zeros], axis=1)
    reps = KN_WIDTH // HEAD_DIM
    return cos.T, sin.T, jnp.tile(c_head, (1, reps)), jnp.tile(s_head, (1, reps))


def _swap_rot_cols(w):
    d, n = w.shape
    wh = w.reshape(d, n // HEAD_DIM, HEAD_DIM)
    sw = jnp.concatenate([wh[:, :, ROT_HALF:ROT_DIMS], wh[:, :, :ROT_HALF],
                          jnp.zeros((d, n // HEAD_DIM, HEAD_DIM - ROT_DIMS), w.dtype)], axis=2)
    return sw.reshape(d, n)


def _block(x, w_in, gate_bias, w_attn_up, w_conv_up, conv_w, w_out, ln_g, ln_b,
           rg_w, rg_b, re_w, re_b, w_gate_e, w_up_e, w_down_e, ln2_g, ln2_b):
    B, S, D = x.shape
    N = B * S
    top_k = min(TOPK_MAX, S // 4)
    bf = jnp.bfloat16
    o = np.cumsum([0, Q_WIDTH, KV_WIDTH, KV_WIDTH, IDXQ_WIDTH, IDX_DIM, IDX_HEADS,
                   CONV_DIM, CONV_DIM, CONV_DIM, N_BRANCHES * D_MODEL])
    w_q, w_k, w_v, w_qi, w_ki, w_wi = (w_in[:, o[i]:o[i + 1]] for i in range(6))
    w_conv = w_in[:, o[6]:o[9]]
    w_gates = w_in[:, o[9]:o[10]]

    wt = jnp.concatenate([w_q.T * (HEAD_DIM ** -0.5 * LOG2_E), w_qi.T, w_v.T, w_wi.T,
                          jnp.zeros((T_ROWS - T_WI0 - IDX_HEADS, D), w_in.dtype)], axis=0).astype(bf)
    pad = jnp.zeros((D, KN_WIDTH - KV_WIDTH - IDX_DIM), w_in.dtype)
    wn = jnp.concatenate([w_k, w_ki, pad, _swap_rot_cols(w_k), _swap_rot_cols(w_ki), pad], axis=1).astype(bf)
    cos_t, sin_t, cos_n, sgn_n = _rope_tables(S)

    qt, qit, vt, wit, kn = _proj_call(x, wt, wn, cos_t, sin_t, cos_n, sgn_n)
    attn = _attn_call(qt, qit, wit, kn, vt, top_k)

    w_r = jnp.concatenate([rg_w, jnp.transpose(re_w, (1, 0, 2)).reshape(D, N_EXPERTS),
                           jnp.zeros((D, ROUTER_LANES - E0 - N_EXPERTS), rg_w.dtype)], axis=1)
    b_r = jnp.concatenate([rg_b, re_b.reshape(-1),
                           jnp.zeros((ROUTER_LANES - E0 - N_EXPERTS,), rg_b.dtype)])[None, :]
    w_rh = w_r.astype(bf)
    w_rl = (w_r - w_rh.astype(jnp.float32)).astype(bf)

    h, xs_sorted, pos, gate, cnt = _mix_call(
        x.reshape(N, D), attn.reshape(N, Q_WIDTH), w_conv.astype(bf), w_gates.astype(bf),
        gate_bias[None, :], conv_w, w_attn_up.astype(bf), w_conv_up.astype(bf), w_out.astype(bf),
        ln_g[None, :], ln_b[None, :], w_rh, w_rl, b_r, S)

    RB, RG = EXPERT_ROWS, ROW_GROUP
    nt = N // MIX_ROWS
    counts = cnt[:, 0, E0:E0 + N_EXPERTS].astype(jnp.int32)
    seg_len = ((counts + RG - 1) // RG) * RG
    seg_local = jnp.cumsum(seg_len, axis=1) - seg_len
    region = jnp.sum(seg_len, axis=0)
    padded = ((region + RB - 1) // RB) * RB
    pad_end = jnp.cumsum(padded)
    pad_start = pad_end - padded
    seg_start = pad_start[None, :] + jnp.cumsum(seg_len, axis=0) - seg_len
    cap = -(-(N * 2 + nt * N_EXPERTS * (RG - 1) + N_EXPERTS * (RB - 1)) // RB) * RB
    nb = cap // RB
    block_start = jnp.arange(nb, dtype=jnp.int32) * RB
    block_expert = jnp.minimum(jnp.sum((block_start[:, None] >= pad_end[None, :]).astype(jnp.int32), axis=1),
                               N_EXPERTS - 1)
    n_used = (pad_end[-1:] // RB).astype(jnp.int32)
    step_base = (jnp.arange(nt, dtype=jnp.int32) * SORTED_ROWS)[:, None]
    g_idx = jnp.arange(cap // RG, dtype=jnp.int32)
    g_row = (g_idx * RG).reshape(nb, GROUPS_PER_BLOCK)
    seg_end_b = (seg_start + seg_len).T[block_expert]
    seg_start_b = seg_start.T[block_expert]
    seg_shift_b = (step_base + seg_local - seg_start).T[block_expert]
    ended = seg_end_b[:, None, :-1] <= g_row[:, :, None]

    def lookup(v):
        return v[:, None, 0] + jnp.sum(jnp.where(ended, (v[:, 1:] - v[:, :-1])[:, None, :], 0), axis=2)

    live = ((g_row >= lookup(seg_start_b)) & (g_row < lookup(seg_end_b))).reshape(-1)
    home = (g_row + lookup(seg_shift_b)).reshape(-1)
    spare_per_step = (SORTED_ROWS - SORTED_USED) // RG
    spare = ((g_idx // GROUPS_PER_BLOCK) % 2) * GROUPS_PER_BLOCK + g_idx % GROUPS_PER_BLOCK
    assert 2 * GROUPS_PER_BLOCK <= (nt - 1) * spare_per_step, "not enough spare groups for block padding"
    spare_row = (1 + spare // spare_per_step) * SORTED_ROWS + SORTED_USED + (spare % spare_per_step) * RG
    group_src = jnp.where(live, home, SORTED_ROWS - RG).astype(jnp.int32)
    group_dst = jnp.where(live, home, spare_row).astype(jnp.int32)

    ys_sorted = _expert_call(block_expert, n_used, group_src, group_dst,
                             xs_sorted.reshape(nt * SORTED_ROWS, HALF), w_gate_e, w_up_e, w_down_e, nb)
    out = _combine_call(ys_sorted.reshape(nt, SORTED_ROWS, HALF), pos, gate, h, ln2_g[None, :], ln2_b[None, :])
    return out.reshape(B, S, D)


def kernel(x, w_in, gate_bias, w_attn_up, w_conv_up, conv_w, w_out, ln1_g, ln1_b, router_group_w,
           router_group_b, router_expert_w, router_expert_b, w_gate_e, w_up_e, w_down_e, ln2_g, ln2_b):
    h = x
    for l in range(DEPTH):
        h = _block(h, w_in[l], gate_bias[l], w_attn_up[l], w_conv_up[l], conv_w[l], w_out[l],
                   ln1_g[l], ln1_b[l], router_group_w[l], router_group_b[l], router_expert_w[l],
                   router_expert_b[l], w_gate_e[l], w_up_e[l], w_down_e[l], ln2_g[l], ln2_b[l])
    return h
```

```python
import functools

import jax
import jax.numpy as jnp
import numpy as np
from jax import lax
from jax.experimental import pallas as pl
from jax.experimental.pallas import tpu as pltpu

D_MODEL = 1024
N_HEADS = 8
N_KV_HEADS = 2
HEAD_DIM = 64
Q_WIDTH = N_HEADS * HEAD_DIM
KV_WIDTH = N_KV_HEADS * HEAD_DIM
ROPE_THETA = 500000.0
ROT_DIMS = HEAD_DIM // 4
ROT_HALF = ROT_DIMS // 2
IDX_HEADS = 8
IDX_DIM = 64
IDXQ_WIDTH = IDX_HEADS * IDX_DIM
TOPK_MAX = 256
CONV_DIM = 512
CONV_WIDTH = 3
N_BRANCHES = 2
N_GROUPS = 4
EXPERTS_PER_GROUP = 8
N_EXPERTS = N_GROUPS * EXPERTS_PER_GROUP
D_EXPERT = 512
LN_EPS = 1e-5
DEPTH = 1
DEEPNORM_ALPHA = (2 * DEPTH) ** 0.25

V7X_SUBLANES = 8
V7X_VMEM_LIMIT_BYTES = 56 * 1024 * 1024

PROJ_ROWS = 512
ATTN_CHUNK = 256
ATTN_PAIR = 2
ATTN_KEY_TILE = 128
DEN_ROWS = 16
ACC_ROWS = HEAD_DIM + DEN_ROWS
LOG2_E = 1.4426950408889634
MIX_ROWS = 512
EXPERT_ROWS = 512
ROW_GROUP = V7X_SUBLANES
SORTED_USED = 2 * MIX_ROWS + N_EXPERTS * (ROW_GROUP - 1)
SORTED_ROWS = -(-(SORTED_USED + ROW_GROUP) // 256) * 256
HALF = D_MODEL // 2

T_Q0, T_QI0, T_V0, T_WI0 = 0, Q_WIDTH, Q_WIDTH + IDXQ_WIDTH, Q_WIDTH + IDXQ_WIDTH + KV_WIDTH
T_ROWS = T_WI0 + 16
KN_WIDTH = 256

INT_MIN = -2147483648
HALF_RANGE = 32768
ROUTER_LANES = 128
E0 = N_GROUPS


def _cparams(n_axes, vmem_bytes):
    return pltpu.CompilerParams(
        dimension_semantics=("arbitrary",) * n_axes,
        vmem_limit_bytes=int(min(vmem_bytes, V7X_VMEM_LIMIT_BYTES)),
    )


def _proj_kernel(x_ref, wt_ref, wn_ref, cost_ref, sint_ref, cosn_ref, sgnn_ref,
                 qt_ref, qit_ref, vt_ref, wit_ref, kn_ref):
    xb = x_ref[0].astype(jnp.bfloat16)
    pt = lax.dot_general(wt_ref[...], xb, (((1,), (1,)), ((), ())),
                         preferred_element_type=jnp.float32)
    cos = cost_ref[...]
    sin = sint_ref[...]

    def rope_t(dst_ref, base):
        for h in range(N_HEADS):
            r0 = base + h * HEAD_DIM
            x1 = pt[r0:r0 + ROT_HALF]
            x2 = pt[r0 + ROT_HALF:r0 + ROT_DIMS]
            head = jnp.concatenate([x1 * cos - x2 * sin, x2 * cos + x1 * sin,
                                    pt[r0 + ROT_DIMS:r0 + HEAD_DIM]], axis=0).astype(dst_ref.dtype)
            for jj, cols in enumerate(chunks):
                dst_ref[0, jj, h * HEAD_DIM:(h + 1) * HEAD_DIM, :] = head[:, cols]

    chunks = [slice(jj * ATTN_CHUNK, (jj + 1) * ATTN_CHUNK) for jj in range(PROJ_ROWS // ATTN_CHUNK)]
    rope_t(qt_ref, T_Q0)
    rope_t(qit_ref, T_QI0)
    for jj, cols in enumerate(chunks):
        vt_ref[0, jj] = pt[T_V0:T_V0 + KV_WIDTH, cols].astype(vt_ref.dtype)
        wit_ref[0, jj] = pt[T_WI0:T_WI0 + IDX_HEADS, cols]
    pn = jnp.dot(xb, wn_ref[...], preferred_element_type=jnp.float32)
    kn = pn[:, :KN_WIDTH] * cosn_ref[...] + pn[:, KN_WIDTH:] * sgnn_ref[...]
    kn_ref[0] = kn.astype(kn_ref.dtype)


def _proj_call(x, wt, wn, cos_t, sin_t, cos_n, sgn_n):
    B, S, D = x.shape
    R = PROJ_ROWS
    nt = S // R
    grid = (B, nt)
    out_shape = (
        jax.ShapeDtypeStruct((B, S // ATTN_CHUNK, Q_WIDTH, ATTN_CHUNK), jnp.bfloat16),
        jax.ShapeDtypeStruct((B, S // ATTN_CHUNK, IDXQ_WIDTH, ATTN_CHUNK), jnp.bfloat16),
        jax.ShapeDtypeStruct((B, S // ATTN_CHUNK, KV_WIDTH, ATTN_CHUNK), jnp.bfloat16),
        jax.ShapeDtypeStruct((B, S // ATTN_CHUNK, IDX_HEADS, ATTN_CHUNK), jnp.float32),
        jax.ShapeDtypeStruct((B, S, KN_WIDTH), jnp.bfloat16),
    )
    in_specs = [
        pl.BlockSpec((1, R, D), lambda b, j: (b, j, 0)),
        pl.BlockSpec((T_ROWS, D), lambda b, j: (0, 0)),
        pl.BlockSpec((D, 2 * KN_WIDTH), lambda b, j: (0, 0)),
        pl.BlockSpec((ROT_HALF, R), lambda b, j: (0, j)),
        pl.BlockSpec((ROT_HALF, R), lambda b, j: (0, j)),
        pl.BlockSpec((R, KN_WIDTH), lambda b, j: (j, 0)),
        pl.BlockSpec((R, KN_WIDTH), lambda b, j: (j, 0)),
    ]
    out_specs = (
        pl.BlockSpec((1, R // ATTN_CHUNK, Q_WIDTH, ATTN_CHUNK), lambda b, j: (b, j, 0, 0)),
        pl.BlockSpec((1, R // ATTN_CHUNK, IDXQ_WIDTH, ATTN_CHUNK), lambda b, j: (b, j, 0, 0)),
        pl.BlockSpec((1, R // ATTN_CHUNK, KV_WIDTH, ATTN_CHUNK), lambda b, j: (b, j, 0, 0)),
        pl.BlockSpec((1, R // ATTN_CHUNK, IDX_HEADS, ATTN_CHUNK), lambda b, j: (b, j, 0, 0)),
        pl.BlockSpec((1, R, KN_WIDTH), lambda b, j: (b, j, 0)),
    )
    vmem = 2 * (R * D * 4 + T_ROWS * D * 2 + D * 2 * KN_WIDTH * 2) + 6 * T_ROWS * R * 4 + (8 << 20)
    return pl.pallas_call(
        _proj_kernel, grid=grid, in_specs=in_specs, out_specs=out_specs, out_shape=out_shape,
        compiler_params=_cparams(2, vmem), name="dsa_proj",
    )(x, wt, wn, cos_t, sin_t, cos_n, sgn_n)


def _float_to_key(s):
    b = lax.bitcast_convert_type(s, jnp.int32)
    k = b ^ (lax.shift_right_arithmetic(b, 31) & jnp.int32(0x7FFFFFFF))
    return jnp.where(b == jnp.int32(INT_MIN), jnp.int32(0), k)


def _attn_kernel(qt_ref, qit_ref, wit_ref, kn_ref, vt_ref, o_ref,
                 key_ref, hi_ref, lo_ref, lga_ref, lgb_ref, acc_ref, m_ref, *, seq, top_k):
    C = ATTN_CHUNK
    E = range(ATTN_PAIR)
    c = pl.program_id(1)
    nkb = c + 1
    zeros_half = jnp.zeros((HEAD_DIM, C), jnp.bfloat16)
    row_i = lax.broadcasted_iota(jnp.int32, (C, C), 0)
    lane_i = lax.broadcasted_iota(jnp.int32, (C, C), 1)
    causal_in_block = row_i <= lane_i

    def score_block(e, kb):
        r0 = pl.multiple_of(kb * C, C)
        kix = kn_ref[e, pl.ds(r0, C), KV_WIDTH:KN_WIDTH]
        acc = None
        for h in range(IDX_HEADS):
            rhs = jnp.concatenate([qit_ref[e, 0, h * IDX_DIM:(h + 1) * IDX_DIM, :], zeros_half], axis=0)
            s = jnp.dot(kix, rhs, preferred_element_type=jnp.float32)
            t = jnp.maximum(s, 0.0) * wit_ref[e, 0, h:h + 1, :]
            acc = t if acc is None else acc + t
        score = acc * (IDX_DIM ** -0.5 * IDX_HEADS ** -0.5)
        keys = jnp.where(causal_in_block | (kb != c), _float_to_key(score), jnp.int32(INT_MIN))
        key_ref[e, kb] = keys
        hi_ref[e, kb] = lax.shift_right_arithmetic(keys, 16).astype(jnp.int16)
        lo_ref[e, kb] = ((keys & jnp.int32(0xFFFF)) - jnp.int32(HALF_RANGE)).astype(jnp.int16)

    def score_body(kb, carry):
        for e in E:
            score_block(e, kb)
        return carry

    lax.fori_loop(0, nkb, score_body, 0)

    def count(ref, pred, blocks=None):
        packed = ref.dtype == jnp.int16

        def body(kb, part):
            hit = pred(ref[kb], kb)
            if packed:
                words = pltpu.bitcast(jnp.where(hit, jnp.int16(1), jnp.int16(0)), jnp.int32)
            else:
                words = hit.astype(jnp.int32)
            return part + jnp.sum(words.reshape(-1, V7X_SUBLANES, C), axis=0)

        part = jnp.zeros((V7X_SUBLANES, C), jnp.int32)
        if blocks is None:
            part = lax.fori_loop(0, nkb, body, part)
        else:
            for kb in range(blocks):
                part = body(kb, part)
        if packed:
            part = (part & jnp.int32(0xFFFF)) + lax.shift_right_logical(part, 16)
        return jnp.sum(part, axis=0, keepdims=True)

    def kth_largest_16(refs, blocks):
        def body(i, answers):
            out = []
            for ref, ans_u in zip(refs, answers):
                cand_u = ans_u | lax.shift_left(jnp.int32(1), jnp.int32(15) - i)
                cand = (cand_u - jnp.int32(HALF_RANGE)).astype(jnp.int16)
                out.append(jnp.where(count(ref, lambda k, kb, cand=cand: k >= cand, blocks) >= top_k, cand_u, ans_u))
            return tuple(out)

        start = tuple(jnp.zeros((1, C), jnp.int32) for _ in refs)
        return [a - jnp.int32(HALF_RANGE) for a in lax.fori_loop(0, 16, body, start)]

    def threshold(blocks):
        if blocks * C <= top_k:
            lowest = jnp.full((1, C), -HALF_RANGE, jnp.int32)
            return (lowest,) * (2 * ATTN_PAIR) + (jnp.zeros((1, C), jnp.int32),) * ATTN_PAIR
        hi_es = [hi_ref.at[e] for e in E]
        lo_es = [lo_ref.at[e] for e in E]
        t_his = kth_largest_16(hi_es, blocks)
        for hi_e, lo_e, t_hi in zip(hi_es, lo_es, t_his):
            t_hi16 = t_hi.astype(jnp.int16)
            for kb in range(blocks):
                hi = hi_e[kb]
                lo_e[kb] = jnp.where(hi > t_hi16, jnp.int16(HALF_RANGE - 1),
                                     jnp.where(hi == t_hi16, lo_e[kb], jnp.int16(-HALF_RANGE)))
        t_los = kth_largest_16(lo_es, blocks)
        n_ges = [count(lo_e, lambda k, kb, t=t_lo.astype(jnp.int16): k >= t, blocks)
                 for lo_e, t_lo in zip(lo_es, t_los)]
        return tuple(t_his) + tuple(t_los) + tuple(n_ges)

    neg_inf = jnp.float32(-jnp.inf)
    found = lax.switch(c, [functools.partial(threshold, n + 1) for n in range(seq // C)])
    thrs = []
    for e in E:
        t_hi, t_lo, n_ge = found[e], found[ATTN_PAIR + e], found[2 * ATTN_PAIR + e]
        thr_raw = t_hi * jnp.int32(2 * HALF_RANGE) + (t_lo + jnp.int32(HALF_RANGE))
        thr = jnp.maximum(thr_raw, jnp.int32(INT_MIN + 1))
        thrs.append(thr)
        key_e = key_ref.at[e]

        has_ties = jnp.max(jnp.where(thr_raw != jnp.int32(INT_MIN), n_ge, 0)) > top_k

        @pl.when(has_ties)
        def _(thr=thr, key_e=key_e):
            need = (top_k - count(key_e, lambda k, kb: k > thr)).astype(jnp.float32)
            earlier_rows = (row_i > lane_i).astype(jnp.bfloat16)

            def drop_body(kb, seen):
                kk = key_e[kb]
                tied = kk == thr
                tied_b = jnp.where(tied, 1.0, 0.0).astype(jnp.bfloat16)
                rank = jnp.dot(earlier_rows, tied_b, preferred_element_type=jnp.float32) + seen
                key_e[kb] = jnp.where(tied & (rank >= need), kk - 1, kk)
                return seen + jnp.sum(tied_b.astype(jnp.float32), axis=0, keepdims=True)

            lax.fori_loop(0, nkb, drop_body, jnp.zeros((1, C), jnp.float32))

    m_ref[...] = jnp.full(m_ref.shape, neg_inf, jnp.float32)
    acc_ref[...] = jnp.zeros(acc_ref.shape, jnp.float32)
    T = ATTN_KEY_TILE
    ones_rows = jnp.ones((DEN_ROWS, C), jnp.bfloat16)

    kv_group = N_HEADS // N_KV_HEADS

    def bias_body(kb, carry):
        for e in E:
            key_ref[e, kb] = lax.bitcast_convert_type(jnp.where(key_ref[e, kb] >= thrs[e], 0.0, neg_inf), jnp.int32)
        return carry

    lax.fori_loop(0, nkb, bias_body, 0)

    def store_logits(e, kb, dst_ref, h):
        kb = jnp.minimum(kb, nkb - 1)
        r0 = pl.multiple_of(kb * C, C)
        bias = lax.bitcast_convert_type(key_ref[e, kb], jnp.float32)
        k2 = kn_ref[e, pl.ds(r0, C), 0:KV_WIDTH]
        qh = qt_ref[e, 0, h * HEAD_DIM:(h + 1) * HEAD_DIM, :]
        rhs = jnp.concatenate([qh, zeros_half] if h < kv_group else [zeros_half, qh], axis=0)
        dst_ref[e, h] = jnp.dot(k2, rhs, preferred_element_type=jnp.float32) + bias

    def softmax_block(kb, src_ref, dst_ref):
        for h in range(N_HEADS):
            for e in E:
                if dst_ref is not None:
                    store_logits(e, kb + 1, dst_ref, h)
                g = h // kv_group
                m_old = m_ref[e, h:h + 1, :]
                m_new = jnp.maximum(m_old, jnp.max(src_ref[e, h], axis=0, keepdims=True))
                m_safe = jnp.where(m_new == neg_inf, 0.0, m_new)
                p = jnp.concatenate(
                    [jnp.exp2(src_ref[e, h, s * T:(s + 1) * T, :] - m_safe).astype(jnp.bfloat16)
                     for s in range(C // T)], axis=0)
                alpha = jnp.exp2(m_old - m_safe)
                vt = vt_ref[e, kb, g * HEAD_DIM:(g + 1) * HEAD_DIM, :]
                pv = jnp.dot(jnp.concatenate([vt, ones_rows], axis=0), p,
                             preferred_element_type=jnp.float32)
                hs = slice(h * ACC_ROWS, (h + 1) * ACC_ROWS)
                acc_ref[e, hs, :] = alpha * acc_ref[e, hs, :] + pv
                m_ref[e, h:h + 1, :] = m_new

    for h in range(N_HEADS):
        for e in E:
            store_logits(e, 0, lga_ref, h)

    def attn_body(pair, carry):
        kb = 2 * pair
        softmax_block(kb, lga_ref, lgb_ref)
        softmax_block(kb + 1, lgb_ref, lga_ref)
        return carry

    lax.fori_loop(0, nkb // 2, attn_body, 0)

    @pl.when(nkb % 2 == 1)
    def _():
        softmax_block(nkb - 1, lga_ref, None)

    for e in E:
        outs = []
        for h in range(N_HEADS):
            num = acc_ref[e, h * ACC_ROWS:h * ACC_ROWS + HEAD_DIM, :]
            den = acc_ref[e, h * ACC_ROWS + HEAD_DIM:h * ACC_ROWS + HEAD_DIM + 1, :]
            outs.append(num / den)
        o_ref[e] = jnp.transpose(jnp.concatenate(outs, axis=0)).astype(o_ref.dtype)


def _attn_call(qt, qit, wit, kn, vt, top_k):
    B, S = kn.shape[:2]
    C = ATTN_CHUNK
    P = ATTN_PAIR
    nc = S // C
    kern = functools.partial(_attn_kernel, seq=S, top_k=top_k)
    in_specs = [
        pl.BlockSpec((P, 1, Q_WIDTH, C), lambda b, c: (b, c, 0, 0)),
        pl.BlockSpec((P, 1, IDXQ_WIDTH, C), lambda b, c: (b, c, 0, 0)),
        pl.BlockSpec((P, 1, IDX_HEADS, C), lambda b, c: (b, c, 0, 0)),
        pl.BlockSpec((P, S, KN_WIDTH), lambda b, c: (b, 0, 0)),
        pl.BlockSpec((P, nc, KV_WIDTH, C), lambda b, c: (b, 0, 0, 0)),
    ]
    out_specs = pl.BlockSpec((P, C, Q_WIDTH), lambda b, c: (b, c, 0))
    scratch = [
        pltpu.VMEM((P, nc, C, C), jnp.int32),
        pltpu.VMEM((P, nc, C, C), jnp.int16),
        pltpu.VMEM((P, nc, C, C), jnp.int16),
        pltpu.VMEM((P, N_HEADS, C, C), jnp.float32),
        pltpu.VMEM((P, N_HEADS, C, C), jnp.float32),
        pltpu.VMEM((P, N_HEADS * ACC_ROWS, C), jnp.float32),
        pltpu.VMEM((P, N_HEADS, C), jnp.float32),
    ]
    vmem = P * (2 * S * C * 4 + Q_WIDTH * C * 4 + 2 * 2 * (2 * Q_WIDTH * C + S * KN_WIDTH + S * KV_WIDTH + C * Q_WIDTH)
                + 2 * N_HEADS * C * C * 4) + 24 * C * C * 4 + (8 << 20)
    return pl.pallas_call(
        kern, grid=(B // P, nc), in_specs=in_specs, out_specs=out_specs,
        out_shape=jax.ShapeDtypeStruct((B, S, Q_WIDTH), jnp.bfloat16),
        scratch_shapes=scratch, compiler_params=_cparams(2, vmem), name="dsa_attn",
    )(qt, qit, wit, kn, vt)


def _layer_norm(v, g, b):
    mu = jnp.mean(v, axis=-1, keepdims=True)
    d = v - mu
    var = jnp.mean(d * d, axis=-1, keepdims=True)
    return d * lax.rsqrt(var + LN_EPS) * g + b


def _pack_rows(h):
    hi = lax.bitcast_convert_type(h[:, :HALF].astype(jnp.bfloat16).astype(jnp.float32), jnp.int32)
    lo = lax.bitcast_convert_type(h[:, HALF:].astype(jnp.bfloat16).astype(jnp.float32), jnp.int32)
    return (hi & jnp.int32(-65536)) | lax.shift_right_logical(lo, 16)


def _unpack_rows(w):
    hi = lax.bitcast_convert_type(w & jnp.int32(-65536), jnp.float32)
    lo = lax.bitcast_convert_type(lax.shift_left(w, 16), jnp.float32)
    return jnp.concatenate([hi, lo], axis=1)


def _mix_kernel(x_ref, attn_ref, wc_ref, wg_ref, gb_ref, cw_ref, wau_ref, wcu_ref, wo_ref,
                g1_ref, b1_ref, wrh_ref, wrl_ref, rb_ref,
                h_ref, xs_ref, pos_ref, gate_ref, cnt_ref,
                ubuf_ref, *, steps_per_seq):
    R = MIX_ROWS
    i = pl.program_id(0)

    @pl.when(i % steps_per_seq == 0)
    def _():
        ubuf_ref[0:V7X_SUBLANES, :] = jnp.zeros((V7X_SUBLANES, CONV_DIM), jnp.float32)

    x = x_ref[...]
    xb = x.astype(jnp.bfloat16)
    cv = jnp.dot(xb, wc_ref[...], preferred_element_type=jnp.float32)
    u = cv[:, 2 * CONV_DIM:] * cv[:, :CONV_DIM]
    ubuf_ref[V7X_SUBLANES:, :] = u
    u1 = ubuf_ref[V7X_SUBLANES - 1:V7X_SUBLANES - 1 + R, :]
    u2 = ubuf_ref[V7X_SUBLANES - 2:V7X_SUBLANES - 2 + R, :]
    y = cw_ref[0:1, :] * u2 + cw_ref[1:2, :] * u1 + cw_ref[2:3, :] * u
    conv = (cv[:, CONV_DIM:2 * CONV_DIM] * y).astype(jnp.bfloat16)
    ubuf_ref[0:V7X_SUBLANES, :] = u[R - V7X_SUBLANES:, :]
    z = jnp.dot(xb, wg_ref[...], preferred_element_type=jnp.float32) + gb_ref[...]
    gates = 1.0 / (1.0 + jnp.exp(-z))
    au = jnp.dot(attn_ref[...], wau_ref[...], preferred_element_type=jnp.float32)
    cu = jnp.dot(conv, wcu_ref[...], preferred_element_type=jnp.float32)
    merged = gates[:, :D_MODEL] * au + gates[:, D_MODEL:] * cu
    mix = jnp.dot(merged.astype(jnp.bfloat16), wo_ref[...], preferred_element_type=jnp.float32)
    h = _layer_norm(DEEPNORM_ALPHA * x + mix, g1_ref[...], b1_ref[...])
    h_ref[...] = h

    h_hi = h.astype(jnp.bfloat16)
    h_lo = (h - h_hi.astype(jnp.float32)).astype(jnp.bfloat16)
    lg = (jnp.dot(h_hi, wrh_ref[...], preferred_element_type=jnp.float32)
          + jnp.dot(h_lo, wrh_ref[...], preferred_element_type=jnp.float32)
          + jnp.dot(h_hi, wrl_ref[...], preferred_element_type=jnp.float32)) + rb_ref[...]
    lane = lax.broadcasted_iota(jnp.int32, (R, ROUTER_LANES), 1).astype(jnp.float32)
    neg = jnp.float32(-jnp.inf)
    no_lane = jnp.float32(ROUTER_LANES)
    gmask = lane < N_GROUPS
    gl = jnp.where(gmask, lg, neg)
    gmax = jnp.max(gl, axis=1, keepdims=True)
    grp = jnp.min(jnp.where(gl == gmax, lane, no_lane), axis=1, keepdims=True)
    gsum = jnp.sum(jnp.where(gmask, jnp.exp(gl - gmax), 0.0), axis=1, keepdims=True)
    p_grp = 1.0 / gsum
    lo_lane = E0 + grp * EXPERTS_PER_GROUP
    emask = (lane >= lo_lane) & (lane < lo_lane + EXPERTS_PER_GROUP)
    el = jnp.where(emask, lg, neg)
    v1 = jnp.max(el, axis=1, keepdims=True)
    i1 = jnp.min(jnp.where(el == v1, lane, no_lane), axis=1, keepdims=True)
    el2 = jnp.where(lane == i1, neg, el)
    v2 = jnp.max(el2, axis=1, keepdims=True)
    i2 = jnp.min(jnp.where(el2 == v2, lane, no_lane), axis=1, keepdims=True)
    a = jnp.exp(v2 - v1)
    inv = 1.0 / (1.0 + a)
    gate_ref[:, 0:1] = p_grp * inv
    gate_ref[:, 1:2] = p_grp * (a * inv)
    oh1 = lane == i1
    oh2 = lane == i2
    oh = (oh1 | oh2).astype(jnp.bfloat16)
    r_i = lax.broadcasted_iota(jnp.int32, (R, R), 0)
    c_i = lax.broadcasted_iota(jnp.int32, (R, R), 1)
    tri = (r_i > c_i).astype(jnp.bfloat16)
    before = jnp.dot(tri, oh, preferred_element_type=jnp.float32)
    cnt = jnp.sum(oh.astype(jnp.float32), axis=0, keepdims=True)
    cnt_ref[0] = cnt

    seg = jnp.floor((cnt + (ROW_GROUP - 1)) * (1.0 / ROW_GROUP)) * ROW_GROUP
    l_r = lax.broadcasted_iota(jnp.int32, (ROUTER_LANES, ROUTER_LANES), 0)
    l_c = lax.broadcasted_iota(jnp.int32, (ROUTER_LANES, ROUTER_LANES), 1)
    lanes_before = (l_r < l_c).astype(jnp.bfloat16)
    seg8 = jnp.broadcast_to(seg, (V7X_SUBLANES, ROUTER_LANES)).astype(jnp.bfloat16)
    seg_off = jnp.dot(seg8, lanes_before, preferred_element_type=jnp.float32)[0:1, :]
    where_to = before + seg_off
    pos1 = jnp.sum(jnp.where(oh1, where_to, 0.0), axis=1, keepdims=True)
    pos2 = jnp.sum(jnp.where(oh2, where_to, 0.0), axis=1, keepdims=True)
    pos_ref[:, 0:1] = pos1.astype(jnp.int32)
    pos_ref[:, 1:2] = pos2.astype(jnp.int32)
    diag = r_i == c_i
    pos1_row = jnp.sum(jnp.where(diag, pos1, 0.0), axis=0, keepdims=True)
    pos2_row = jnp.sum(jnp.where(diag, pos2, 0.0), axis=0, keepdims=True)
    sorted_row = lax.broadcasted_iota(jnp.int32, (SORTED_ROWS, R), 0).astype(jnp.float32)
    place = ((sorted_row == pos1_row) | (sorted_row == pos2_row)).astype(jnp.bfloat16)
    xs = jnp.dot(place, h_hi, preferred_element_type=jnp.float32)
    xs_ref[0] = _pack_rows(xs)


def _mix_call(x2, attn2, wc, wg, gb, cw, wau, wcu, wo, g1, b1, wrh, wrl, rb, seq):
    N, D = x2.shape
    R = MIX_ROWS
    nt = N // R
    kern = functools.partial(_mix_kernel, steps_per_seq=seq // R)

    def full(a):
        return pl.BlockSpec(a.shape, lambda i: (0,) * a.ndim)

    in_specs = [
        pl.BlockSpec((R, D), lambda i: (i, 0)),
        pl.BlockSpec((R, Q_WIDTH), lambda i: (i, 0)),
        full(wc), full(wg), full(gb), full(cw), full(wau), full(wcu), full(wo),
        full(g1), full(b1), full(wrh), full(wrl), full(rb),
    ]
    out_shape = (
        jax.ShapeDtypeStruct((N, D), jnp.float32),
        jax.ShapeDtypeStruct((nt, SORTED_ROWS, HALF), jnp.int32),
        jax.ShapeDtypeStruct((N, 2), jnp.int32),
        jax.ShapeDtypeStruct((N, 2), jnp.float32),
        jax.ShapeDtypeStruct((nt, 1, ROUTER_LANES), jnp.float32),
    )
    out_specs = (
        pl.BlockSpec((R, D), lambda i: (i, 0)),
        pl.BlockSpec((1, SORTED_ROWS, HALF), lambda i: (i, 0, 0)),
        pl.BlockSpec((R, 2), lambda i: (i, 0)),
        pl.BlockSpec((R, 2), lambda i: (i, 0)),
        pl.BlockSpec((1, 1, ROUTER_LANES), lambda i: (i, 0, 0)),
    )
    scratch = [pltpu.VMEM((R + V7X_SUBLANES, CONV_DIM), jnp.float32)]
    w_bytes = 2 * (wc.size + wg.size + wau.size + wcu.size + wo.size + wrh.size + wrl.size)
    vmem = (2 * w_bytes + 2 * (R * D * 4 * 2 + R * Q_WIDTH * 2 + SORTED_ROWS * HALF * 4) + 10 * R * 2048 * 4
            + SORTED_ROWS * D * 6 + (6 << 20))
    return pl.pallas_call(
        kern, grid=(nt,), in_specs=in_specs, out_specs=out_specs, out_shape=out_shape,
        scratch_shapes=scratch, compiler_params=_cparams(1, vmem), name="mix_ln_router",
    )(x2, attn2, wc, wg, gb, cw, wau, wcu, wo, g1, b1, wrh, wrl, rb)


GROUPS_PER_BLOCK = EXPERT_ROWS // ROW_GROUP


def _expert_kernel(be_ref, nb_ref, gsrc_ref, gdst_ref, xs_ref, wg_ref, wu_ref, wd_ref, ys_ref,
                   wgb_ref, wub_ref, wdb_ref, xbuf_ref, ybuf_ref, gsems, ssems):
    del xs_ref
    j = pl.program_id(0)
    n_used = nb_ref[0]
    slot = j % 2

    def gather(block, to_slot):
        for g in range(GROUPS_PER_BLOCK):
            row = pl.multiple_of(gsrc_ref[block * GROUPS_PER_BLOCK + g], ROW_GROUP)
            pltpu.make_async_copy(ys_ref.at[pl.ds(row, ROW_GROUP), :],
                                  xbuf_ref.at[to_slot, pl.ds(g * ROW_GROUP, ROW_GROUP), :],
                                  gsems.at[to_slot]).start(priority=g % 2)

    def scatter(block, from_slot):
        for g in range(GROUPS_PER_BLOCK):
            row = pl.multiple_of(gdst_ref[block * GROUPS_PER_BLOCK + g], ROW_GROUP)
            pltpu.make_async_copy(ybuf_ref.at[from_slot, pl.ds(g * ROW_GROUP, ROW_GROUP), :],
                                  ys_ref.at[pl.ds(row, ROW_GROUP), :],
                                  ssems.at[from_slot]).start(priority=g % 2)

    def wait_gather(s):
        pltpu.make_async_copy(ys_ref.at[pl.ds(0, EXPERT_ROWS), :], xbuf_ref.at[s], gsems.at[s]).wait()

    def wait_scatter(s):
        pltpu.make_async_copy(ybuf_ref.at[s], ys_ref.at[pl.ds(0, EXPERT_ROWS), :], ssems.at[s]).wait()

    @pl.when((j == 0) & (n_used > 0))
    def _():
        gather(0, 0)

    @pl.when((j == 0) | (be_ref[j] != be_ref[jnp.maximum(j - 1, 0)]))
    def _():
        wgb_ref[...] = wg_ref[0].astype(jnp.bfloat16)
        wub_ref[...] = wu_ref[0].astype(jnp.bfloat16)
        wdb_ref[...] = wd_ref[0].astype(jnp.bfloat16)

    @pl.when(j + 1 < n_used)
    def _():
        gather(j + 1, 1 - slot)

    @pl.when((j >= 2) & (j < n_used))
    def _():
        wait_scatter(slot)

    @pl.when(j < n_used)
    def _():
        wait_gather(slot)
        xb = _unpack_rows(xbuf_ref[slot]).astype(jnp.bfloat16)
        hg = jnp.dot(xb, wgb_ref[...], preferred_element_type=jnp.float32)
        hu = jnp.dot(xb, wub_ref[...], preferred_element_type=jnp.float32)
        hid = (hg / (1.0 + jnp.exp(-hg))) * hu
        y = jnp.dot(hid.astype(jnp.bfloat16), wdb_ref[...], preferred_element_type=jnp.float32)
        ybuf_ref[slot] = _pack_rows(y)
        scatter(j, slot)

    @pl.when(j == n_used - 1)
    def _():
        wait_scatter(slot)

    @pl.when((j == n_used - 1) & (j >= 1))
    def _():
        wait_scatter(1 - slot)


def _expert_call(block_expert, n_used, group_src, group_dst, xs_sorted, wg, wu, wd, n_blocks):
    W = xs_sorted.shape[-1]
    RB = EXPERT_ROWS
    grid_spec = pltpu.PrefetchScalarGridSpec(
        num_scalar_prefetch=4, grid=(n_blocks,),
        in_specs=[
            pl.BlockSpec(memory_space=pl.ANY),
            pl.BlockSpec((1, D_MODEL, D_EXPERT), lambda j, be, nu, gs, gd: (be[j], 0, 0)),
            pl.BlockSpec((1, D_MODEL, D_EXPERT), lambda j, be, nu, gs, gd: (be[j], 0, 0)),
            pl.BlockSpec((1, D_EXPERT, D_MODEL), lambda j, be, nu, gs, gd: (be[j], 0, 0)),
        ],
        out_specs=pl.BlockSpec(memory_space=pl.ANY),
        scratch_shapes=[pltpu.VMEM((D_MODEL, D_EXPERT), jnp.bfloat16),
                        pltpu.VMEM((D_MODEL, D_EXPERT), jnp.bfloat16),
                        pltpu.VMEM((D_EXPERT, D_MODEL), jnp.bfloat16),
                        pltpu.VMEM((2, RB, W), jnp.int32),
                        pltpu.VMEM((2, RB, W), jnp.int32),
                        pltpu.SemaphoreType.DMA((2,)),
                        pltpu.SemaphoreType.DMA((2,))],
    )
    vmem = (2 * 4 + 2) * 3 * D_MODEL * D_EXPERT + 4 * RB * W * 4 + 8 * RB * D_MODEL * 4 + (6 << 20)
    return pl.pallas_call(
        _expert_kernel, grid_spec=grid_spec,
        out_shape=jax.ShapeDtypeStruct(xs_sorted.shape, xs_sorted.dtype),
        input_output_aliases={4: 0},
        compiler_params=_cparams(1, vmem), name="moe_experts",
    )(block_expert, n_used, group_src, group_dst, xs_sorted, wg, wu, wd)


def _combine_kernel(ys_ref, pos_ref, gate_ref, h_ref, g2_ref, b2_ref, o_ref):
    R = MIX_ROWS
    y = _unpack_rows(ys_ref[0]).astype(jnp.bfloat16)
    col = lax.broadcasted_iota(jnp.int32, (R, SORTED_ROWS), 1)
    pick = (jnp.where(col == pos_ref[:, 0:1], gate_ref[:, 0:1], 0.0)
            + jnp.where(col == pos_ref[:, 1:2], gate_ref[:, 1:2], 0.0)).astype(jnp.bfloat16)
    ffn = jnp.dot(pick, y, preferred_element_type=jnp.float32)
    o_ref[...] = _layer_norm(DEEPNORM_ALPHA * h_ref[...] + ffn, g2_ref[...], b2_ref[...])


def _combine_call(ys_sorted, pos, gate, h, g2, b2):
    N, D = h.shape
    R = MIX_ROWS
    nt = N // R
    W = ys_sorted.shape[-1]
    return pl.pallas_call(
        _combine_kernel, grid=(nt,),
        in_specs=[
            pl.BlockSpec((1, SORTED_ROWS, W), lambda i: (i, 0, 0)),
            pl.BlockSpec((R, 2), lambda i: (i, 0)),
            pl.BlockSpec((R, 2), lambda i: (i, 0)),
            pl.BlockSpec((R, D), lambda i: (i, 0)),
            pl.BlockSpec((1, D), lambda i: (0, 0)),
            pl.BlockSpec((1, D), lambda i: (0, 0)),
        ],
        out_specs=pl.BlockSpec((R, D), lambda i: (i, 0)),
        out_shape=jax.ShapeDtypeStruct((N, D), jnp.float32),
        compiler_params=_cparams(1, 2 * (SORTED_ROWS * W * 4 + 2 * R * D * 4) + SORTED_ROWS * D * 8
                                 + 3 * R * SORTED_ROWS * 4 + (6 << 20)),
        name="moe_combine",
    )(ys_sorted, pos, gate, h, g2, b2)


def _rope_tables(seq):
    inv_freq = ROPE_THETA ** (-jnp.arange(ROT_HALF, dtype=jnp.float32) / ROT_HALF)
    ang = jnp.arange(seq, dtype=jnp.int32).astype(jnp.float32)[:, None] * inv_freq[None, :]
    cos, sin = jnp.cos(ang), jnp.sin(ang)
    ones = jnp.ones((seq, HEAD_DIM - ROT_DIMS), jnp.float32)
    zeros = jnp.zeros((seq, HEAD_DIM - ROT_DIMS), jnp.float32)
    c_head = jnp.concatenate([cos, cos, ones], axis=1)
    s_head = jnp.concatenate([-sin, sin, zeros], axis=1)
    reps = KN_WIDTH // HEAD_DIM
    return cos.T, sin.T, jnp.tile(c_head, (1, reps)), jnp.tile(s_head, (1, reps))


def _swap_rot_cols(w):
    d, n = w.shape
    wh = w.reshape(d, n // HEAD_DIM, HEAD_DIM)
    sw = jnp.concatenate([wh[:, :, ROT_HALF:ROT_DIMS], wh[:, :, :ROT_HALF],
                          jnp.zeros((d, n // HEAD_DIM, HEAD_DIM - ROT_DIMS), w.dtype)], axis=2)
    return sw.reshape(d, n)


def _block(x, w_in, gate_bias, w_attn_up, w_conv_up, conv_w, w_out, ln_g, ln_b,
           rg_w, rg_b, re_w, re_b, w_gate_e, w_up_e, w_down_e, ln2_g, ln2_b):
    B, S, D = x.shape
    N = B * S
    top_k = min(TOPK_MAX, S // 4)
    bf = jnp.bfloat16
    o = np.cumsum([0, Q_WIDTH, KV_WIDTH, KV_WIDTH, IDXQ_WIDTH, IDX_DIM, IDX_HEADS,
                   CONV_DIM, CONV_DIM, CONV_DIM, N_BRANCHES * D_MODEL])
    w_q, w_k, w_v, w_qi, w_ki, w_wi = (w_in[:, o[i]:o[i + 1]] for i in range(6))
    w_conv = w_in[:, o[6]:o[9]]
    w_gates = w_in[:, o[9]:o[10]]

    wt = jnp.concatenate([w_q.T * (HEAD_DIM ** -0.5 * LOG2_E), w_qi.T, w_v.T, w_wi.T,
                          jnp.zeros((T_ROWS - T_WI0 - IDX_HEADS, D), w_in.dtype)], axis=0).astype(bf)
    pad = jnp.zeros((D, KN_WIDTH - KV_WIDTH - IDX_DIM), w_in.dtype)
    wn = jnp.concatenate([w_k, w_ki, pad, _swap_rot_cols(w_k), _swap_rot_cols(w_ki), pad], axis=1).astype(bf)
    cos_t, sin_t, cos_n, sgn_n = _rope_tables(S)

    qt, qit, vt, wit, kn = _proj_call(x, wt, wn, cos_t, sin_t, cos_n, sgn_n)
    attn = _attn_call(qt, qit, wit, kn, vt, top_k)

    w_r = jnp.concatenate([rg_w, jnp.transpose(re_w, (1, 0, 2)).reshape(D, N_EXPERTS),
                           jnp.zeros((D, ROUTER_LANES - E0 - N_EXPERTS), rg_w.dtype)], axis=1)
    b_r = jnp.concatenate([rg_b, re_b.reshape(-1),
                           jnp.zeros((ROUTER_LANES - E0 - N_EXPERTS,), rg_b.dtype)])[None, :]
    w_rh = w_r.astype(bf)
    w_rl = (w_r - w_rh.astype(jnp.float32)).astype(bf)

    h, xs_sorted, pos, gate, cnt = _mix_call(
        x.reshape(N, D), attn.reshape(N, Q_WIDTH), w_conv.astype(bf), w_gates.astype(bf),
        gate_bias[None, :], conv_w, w_attn_up.astype(bf), w_conv_up.astype(bf), w_out.astype(bf),
        ln_g[None, :], ln_b[None, :], w_rh, w_rl, b_r, S)

    RB, RG = EXPERT_ROWS, ROW_GROUP
    nt = N // MIX_ROWS
    counts = cnt[:, 0, E0:E0 + N_EXPERTS].astype(jnp.int32)
    seg_len = ((counts + RG - 1) // RG) * RG
    seg_local = jnp.cumsum(seg_len, axis=1) - seg_len
    region = jnp.sum(seg_len, axis=0)
    padded = ((region + RB - 1) // RB) * RB
    pad_end = jnp.cumsum(padded)
    pad_start = pad_end - padded
    seg_start = pad_start[None, :] + jnp.cumsum(seg_len, axis=0) - seg_len
    cap = -(-(N * 2 + nt * N_EXPERTS * (RG - 1) + N_EXPERTS * (RB - 1)) // RB) * RB
    nb = cap // RB
    block_start = jnp.arange(nb, dtype=jnp.int32) * RB
    block_expert = jnp.minimum(jnp.sum((block_start[:, None] >= pad_end[None, :]).astype(jnp.int32), axis=1),
                               N_EXPERTS - 1)
    n_used = (pad_end[-1:] // RB).astype(jnp.int32)
    step_base = (jnp.arange(nt, dtype=jnp.int32) * SORTED_ROWS)[:, None]
    g_idx = jnp.arange(cap // RG, dtype=jnp.int32)
    g_row = (g_idx * RG).reshape(nb, GROUPS_PER_BLOCK)
    seg_end_b = (seg_start + seg_len).T[block_expert]
    seg_start_b = seg_start.T[block_expert]
    seg_shift_b = (step_base + seg_local - seg_start).T[block_expert]
    ended = seg_end_b[:, None, :-1] <= g_row[:, :, None]

    def lookup(v):
        return v[:, None, 0] + jnp.sum(jnp.where(ended, (v[:, 1:] - v[:, :-1])[:, None, :], 0), axis=2)

    live = ((g_row >= lookup(seg_start_b)) & (g_row < lookup(seg_end_b))).reshape(-1)
    home = (g_row + lookup(seg_shift_b)).reshape(-1)
    spare_per_step = (SORTED_ROWS - SORTED_USED) // RG
    spare = ((g_idx // GROUPS_PER_BLOCK) % 2) * GROUPS_PER_BLOCK + g_idx % GROUPS_PER_BLOCK
    assert 2 * GROUPS_PER_BLOCK <= (nt - 1) * spare_per_step, "not enough spare groups for block padding"
    spare_row = (1 + spare // spare_per_step) * SORTED_ROWS + SORTED_USED + (spare % spare_per_step) * RG
    group_src = jnp.where(live, home, SORTED_ROWS - RG).astype(jnp.int32)
    group_dst = jnp.where(live, home, spare_row).astype(jnp.int32)

    ys_sorted = _expert_call(block_expert, n_used, group_src, group_dst,
                             xs_sorted.reshape(nt * SORTED_ROWS, HALF), w_gate_e, w_up_e, w_down_e, nb)
    out = _combine_call(ys_sorted.reshape(nt, SORTED_ROWS, HALF), pos, gate, h, ln2_g[None, :], ln2_b[None, :])
    return out.reshape(B, S, D)


def kernel(x, w_in, gate_bias, w_attn_up, w_conv_up, conv_w, w_out, ln1_g, ln1_b, router_group_w,
           router_group_b, router_expert_w, router_expert_b, w_gate_e, w_up_e, w_down_e, ln2_g, ln2_b):
    h = x
    for l in range(DEPTH):
        h = _block(h, w_in[l], gate_bias[l], w_attn_up[l], w_conv_up[l], conv_w[l], w_out[l],
                   ln1_g[l], ln1_b[l], router_group_w[l], router_group_b[l], router_expert_w[l],
                   router_expert_b[l], w_gate_e[l], w_up_e[l], w_down_e[l], ln2_g[l], ln2_b[l])
    return h
```

```python
import functools

import jax
import jax.numpy as jnp
import numpy as np
from jax import lax
from jax.experimental import pallas as pl
from jax.experimental.pallas import tpu as pltpu

D_MODEL = 1024
N_HEADS = 8
N_KV_HEADS = 2
HEAD_DIM = 64
Q_WIDTH = N_HEADS * HEAD_DIM
KV_WIDTH = N_KV_HEADS * HEAD_DIM
ROPE_THETA = 500000.0
ROT_DIMS = HEAD_DIM // 4
ROT_HALF = ROT_DIMS // 2
IDX_HEADS = 8
IDX_DIM = 64
IDXQ_WIDTH = IDX_HEADS * IDX_DIM
TOPK_MAX = 256
CONV_DIM = 512
CONV_WIDTH = 3
N_BRANCHES = 2
N_GROUPS = 4
EXPERTS_PER_GROUP = 8
N_EXPERTS = N_GROUPS * EXPERTS_PER_GROUP
D_EXPERT = 512
LN_EPS = 1e-5
DEPTH = 1
DEEPNORM_ALPHA = (2 * DEPTH) ** 0.25

V7X_SUBLANES = 8
V7X_VMEM_LIMIT_BYTES = 56 * 1024 * 1024

PROJ_ROWS = 512
ATTN_CHUNK = 256
ATTN_PAIR = 2
ATTN_KEY_TILE = 128
DEN_ROWS = 16
ACC_ROWS = HEAD_DIM + DEN_ROWS
LOG2_E = 1.4426950408889634
MIX_ROWS = 512
EXPERT_ROWS = 512
ROW_GROUP = V7X_SUBLANES
SORTED_USED = 2 * MIX_ROWS + N_EXPERTS * (ROW_GROUP - 1)
SORTED_ROWS = -(-(SORTED_USED + ROW_GROUP) // 256) * 256
HALF = D_MODEL // 2

T_Q0, T_QI0, T_V0, T_WI0 = 0, Q_WIDTH, Q_WIDTH + IDXQ_WIDTH, Q_WIDTH + IDXQ_WIDTH + KV_WIDTH
T_ROWS = T_WI0 + 16
KN_WIDTH = 256

INT_MIN = -2147483648
HALF_RANGE = 32768
ROUTER_LANES = 128
E0 = N_GROUPS


def _cparams(n_axes, vmem_bytes):
    return pltpu.CompilerParams(
        dimension_semantics=("arbitrary",) * n_axes,
        vmem_limit_bytes=int(min(vmem_bytes, V7X_VMEM_LIMIT_BYTES)),
    )


def _proj_kernel(x_ref, wt_ref, wn_ref, cost_ref, sint_ref, cosn_ref, sgnn_ref,
                 qt_ref, qit_ref, vt_ref, wit_ref, kn_ref):
    xb = x_ref[0].astype(jnp.bfloat16)
    pt = lax.dot_general(wt_ref[...], xb, (((1,), (1,)), ((), ())),
                         preferred_element_type=jnp.float32)
    cos = cost_ref[...]
    sin = sint_ref[...]

    def rope_t(dst_ref, base):
        for h in range(N_HEADS):
            r0 = base + h * HEAD_DIM
            x1 = pt[r0:r0 + ROT_HALF]
            x2 = pt[r0 + ROT_HALF:r0 + ROT_DIMS]
            head = jnp.concatenate([x1 * cos - x2 * sin, x2 * cos + x1 * sin,
                                    pt[r0 + ROT_DIMS:r0 + HEAD_DIM]], axis=0).astype(dst_ref.dtype)
            for jj, cols in enumerate(chunks):
                dst_ref[0, jj, h * HEAD_DIM:(h + 1) * HEAD_DIM, :] = head[:, cols]

    chunks = [slice(jj * ATTN_CHUNK, (jj + 1) * ATTN_CHUNK) for jj in range(PROJ_ROWS // ATTN_CHUNK)]
    rope_t(qt_ref, T_Q0)
    rope_t(qit_ref, T_QI0)
    for jj, cols in enumerate(chunks):
        vt_ref[0, jj] = pt[T_V0:T_V0 + KV_WIDTH, cols].astype(vt_ref.dtype)
        wit_ref[0, jj] = pt[T_WI0:T_WI0 + IDX_HEADS, cols]
    pn = jnp.dot(xb, wn_ref[...], preferred_element_type=jnp.float32)
    kn = pn[:, :KN_WIDTH] * cosn_ref[...] + pn[:, KN_WIDTH:] * sgnn_ref[...]
    kn_ref[0] = kn.astype(kn_ref.dtype)


def _proj_call(x, wt, wn, cos_t, sin_t, cos_n, sgn_n):
    B, S, D = x.shape
    R = PROJ_ROWS
    nt = S // R
    grid = (B, nt)
    out_shape = (
        jax.ShapeDtypeStruct((B, S // ATTN_CHUNK, Q_WIDTH, ATTN_CHUNK), jnp.bfloat16),
        jax.ShapeDtypeStruct((B, S // ATTN_CHUNK, IDXQ_WIDTH, ATTN_CHUNK), jnp.bfloat16),
        jax.ShapeDtypeStruct((B, S // ATTN_CHUNK, KV_WIDTH, ATTN_CHUNK), jnp.bfloat16),
        jax.ShapeDtypeStruct((B, S // ATTN_CHUNK, IDX_HEADS, ATTN_CHUNK), jnp.float32),
        jax.ShapeDtypeStruct((B, S, KN_WIDTH), jnp.bfloat16),
    )
    in_specs = [
        pl.BlockSpec((1, R, D), lambda b, j: (b, j, 0)),
        pl.BlockSpec((T_ROWS, D), lambda b, j: (0, 0)),
        pl.BlockSpec((D, 2 * KN_WIDTH), lambda b, j: (0, 0)),
        pl.BlockSpec((ROT_HALF, R), lambda b, j: (0, j)),
        pl.BlockSpec((ROT_HALF, R), lambda b, j: (0, j)),
        pl.BlockSpec((R, KN_WIDTH), lambda b, j: (j, 0)),
        pl.BlockSpec((R, KN_WIDTH), lambda b, j: (j, 0)),
    ]
    out_specs = (
        pl.BlockSpec((1, R // ATTN_CHUNK, Q_WIDTH, ATTN_CHUNK), lambda b, j: (b, j, 0, 0)),
        pl.BlockSpec((1, R // ATTN_CHUNK, IDXQ_WIDTH, ATTN_CHUNK), lambda b, j: (b, j, 0, 0)),
        pl.BlockSpec((1, R // ATTN_CHUNK, KV_WIDTH, ATTN_CHUNK), lambda b, j: (b, j, 0, 0)),
        pl.BlockSpec((1, R // ATTN_CHUNK, IDX_HEADS, ATTN_CHUNK), lambda b, j: (b, j, 0, 0)),
        pl.BlockSpec((1, R, KN_WIDTH), lambda b, j: (b, j, 0)),
    )
    vmem = 2 * (R * D * 4 + T_ROWS * D * 2 + D * 2 * KN_WIDTH * 2) + 6 * T_ROWS * R * 4 + (8 << 20)
    return pl.pallas_call(
        _proj_kernel, grid=grid, in_specs=in_specs, out_specs=out_specs, out_shape=out_shape,
        compiler_params=_cparams(2, vmem), name="dsa_proj",
    )(x, wt, wn, cos_t, sin_t, cos_n, sgn_n)


def _float_to_key(s):
    b = lax.bitcast_convert_type(s, jnp.int32)
    k = b ^ (lax.shift_right_arithmetic(b, 31) & jnp.int32(0x7FFFFFFF))
    return jnp.where(b == jnp.int32(INT_MIN), jnp.int32(0), k)


def _attn_kernel(qt_ref, qit_ref, wit_ref, kn_ref, vt_ref, o_ref,
                 key_ref, hi_ref, lo_ref, lga_ref, lgb_ref, acc_ref, m_ref, *, seq, top_k):
    C = ATTN_CHUNK
    E = range(ATTN_PAIR)
    c = pl.program_id(1)
    nkb = c + 1
    zeros_half = jnp.zeros((HEAD_DIM, C), jnp.bfloat16)
    row_i = lax.broadcasted_iota(jnp.int32, (C, C), 0)
    lane_i = lax.broadcasted_iota(jnp.int32, (C, C), 1)
    causal_in_block = row_i <= lane_i

    def score_block(e, kb):
        r0 = pl.multiple_of(kb * C, C)
        kix = kn_ref[e, pl.ds(r0, C), KV_WIDTH:KN_WIDTH]
        acc = None
        for h in range(IDX_HEADS):
            rhs = jnp.concatenate([qit_ref[e, 0, h * IDX_DIM:(h + 1) * IDX_DIM, :], zeros_half], axis=0)
            s = jnp.dot(kix, rhs, preferred_element_type=jnp.float32)
            t = jnp.maximum(s, 0.0) * wit_ref[e, 0, h:h + 1, :]
            acc = t if acc is None else acc + t
        score = acc * (IDX_DIM ** -0.5 * IDX_HEADS ** -0.5)
        keys = jnp.where(causal_in_block | (kb != c), _float_to_key(score), jnp.int32(INT_MIN))
        key_ref[e, kb] = keys
        hi_ref[e, kb] = lax.shift_right_arithmetic(keys, 16).astype(jnp.int16)
        lo_ref[e, kb] = ((keys & jnp.int32(0xFFFF)) - jnp.int32(HALF_RANGE)).astype(jnp.int16)

    def score_body(kb, carry):
        for e in E:
            score_block(e, kb)
        return carry

    lax.fori_loop(0, nkb, score_body, 0)

    def count(ref, pred, blocks=None):
        packed = ref.dtype == jnp.int16

        def body(kb, part):
            hit = pred(ref[kb], kb)
            if packed:
                words = pltpu.bitcast(jnp.where(hit, jnp.int16(1), jnp.int16(0)), jnp.int32)
            else:
                words = hit.astype(jnp.int32)
            return part + jnp.sum(words.reshape(-1, V7X_SUBLANES, C), axis=0)

        part = jnp.zeros((V7X_SUBLANES, C), jnp.int32)
        if blocks is None:
            part = lax.fori_loop(0, nkb, body, part)
        else:
            for kb in range(blocks):
                part = body(kb, part)
        if packed:
            part = (part & jnp.int32(0xFFFF)) + lax.shift_right_logical(part, 16)
        return jnp.sum(part, axis=0, keepdims=True)

    def kth_largest_16(refs, blocks):
        def body(i, answers):
            out = []
            for ref, ans_u in zip(refs, answers):
                cand_u = ans_u | lax.shift_left(jnp.int32(1), jnp.int32(15) - i)
                cand = (cand_u - jnp.int32(HALF_RANGE)).astype(jnp.int16)
                out.append(jnp.where(count(ref, lambda k, kb, cand=cand: k >= cand, blocks) >= top_k, cand_u, ans_u))
            return tuple(out)

        start = tuple(jnp.zeros((1, C), jnp.int32) for _ in refs)
        return [a - jnp.int32(HALF_RANGE) for a in lax.fori_loop(0, 16, body, start)]

    def threshold(blocks):
        if blocks * C <= top_k:
            lowest = jnp.full((1, C), -HALF_RANGE, jnp.int32)
            return (lowest,) * (2 * ATTN_PAIR) + (jnp.zeros((1, C), jnp.int32),) * ATTN_PAIR
        hi_es = [hi_ref.at[e] for e in E]
        lo_es = [lo_ref.at[e] for e in E]
        t_his = kth_largest_16(hi_es, blocks)
        for hi_e, lo_e, t_hi in zip(hi_es, lo_es, t_his):
            t_hi16 = t_hi.astype(jnp.int16)
            for kb in range(blocks):
                hi = hi_e[kb]
                lo_e[kb] = jnp.where(hi > t_hi16, jnp.int16(HALF_RANGE - 1),
                                     jnp.where(hi == t_hi16, lo_e[kb], jnp.int16(-HALF_RANGE)))
        t_los = kth_largest_16(lo_es, blocks)
        n_ges = [count(lo_e, lambda k, kb, t=t_lo.astype(jnp.int16): k >= t, blocks)
                 for lo_e, t_lo in zip(lo_es, t_los)]
        return tuple(t_his) + tuple(t_los) + tuple(n_ges)

    neg_inf = jnp.float32(-jnp.inf)
    found = lax.switch(c, [functools.partial(threshold, n + 1) for n in range(seq // C)])
    thrs = []
    for e in E:
        t_hi, t_lo, n_ge = found[e], found[ATTN_PAIR + e], found[2 * ATTN_PAIR + e]
        thr_raw = t_hi * jnp.int32(2 * HALF_RANGE) + (t_lo + jnp.int32(HALF_RANGE))
        thr = jnp.maximum(thr_raw, jnp.int32(INT_MIN + 1))
        thrs.append(thr)
        key_e = key_ref.at[e]

        has_ties = jnp.max(jnp.where(thr_raw != jnp.int32(INT_MIN), n_ge, 0)) > top_k

        @pl.when(has_ties)
        def _(thr=thr, key_e=key_e):
            need = (top_k - count(key_e, lambda k, kb: k > thr)).astype(jnp.float32)
            earlier_rows = (row_i > lane_i).astype(jnp.bfloat16)

            def drop_body(kb, seen):
                kk = key_e[kb]
                tied = kk == thr
                tied_b = jnp.where(tied, 1.0, 0.0).astype(jnp.bfloat16)
                rank = jnp.dot(earlier_rows, tied_b, preferred_element_type=jnp.float32) + seen
                key_e[kb] = jnp.where(tied & (rank >= need), kk - 1, kk)
                return seen + jnp.sum(tied_b.astype(jnp.float32), axis=0, keepdims=True)

            lax.fori_loop(0, nkb, drop_body, jnp.zeros((1, C), jnp.float32))

    m_ref[...] = jnp.full(m_ref.shape, neg_inf, jnp.float32)
    acc_ref[...] = jnp.zeros(acc_ref.shape, jnp.float32)
    T = ATTN_KEY_TILE
    ones_rows = jnp.ones((DEN_ROWS, C), jnp.bfloat16)

    kv_group = N_HEADS // N_KV_HEADS

    def bias_body(kb, carry):
        for e in E:
            key_ref[e, kb] = lax.bitcast_convert_type(jnp.where(key_ref[e, kb] >= thrs[e], 0.0, neg_inf), jnp.int32)
        return carry

    lax.fori_loop(0, nkb, bias_body, 0)

    def store_logits(e, kb, dst_ref, h):
        kb = jnp.minimum(kb, nkb - 1)
        r0 = pl.multiple_of(kb * C, C)
        bias = lax.bitcast_convert_type(key_ref[e, kb], jnp.float32)
        k2 = kn_ref[e, pl.ds(r0, C), 0:KV_WIDTH]
        qh = qt_ref[e, 0, h * HEAD_DIM:(h + 1) * HEAD_DIM, :]
        rhs = jnp.concatenate([qh, zeros_half] if h < kv_group else [zeros_half, qh], axis=0)
        dst_ref[e, h] = jnp.dot(k2, rhs, preferred_element_type=jnp.float32) + bias

    def softmax_block(kb, src_ref, dst_ref):
        for h in range(N_HEADS):
            for e in E:
                if dst_ref is not None:
                    store_logits(e, kb + 1, dst_ref, h)
                g = h // kv_group
                m_old = m_ref[e, h:h + 1, :]
                m_new = jnp.maximum(m_old, jnp.max(src_ref[e, h], axis=0, keepdims=True))
                m_safe = jnp.where(m_new == neg_inf, 0.0, m_new)
                p = jnp.concatenate(
                    [jnp.exp2(src_ref[e, h, s * T:(s + 1) * T, :] - m_safe).astype(jnp.bfloat16)
                     for s in range(C // T)], axis=0)
                alpha = jnp.exp2(m_old - m_safe)
                vt = vt_ref[e, kb, g * HEAD_DIM:(g + 1) * HEAD_DIM, :]
                pv = jnp.dot(jnp.concatenate([vt, ones_rows], axis=0), p,
                             preferred_element_type=jnp.float32)
                hs = slice(h * ACC_ROWS, (h + 1) * ACC_ROWS)
                acc_ref[e, hs, :] = alpha * acc_ref[e, hs, :] + pv
                m_ref[e, h:h + 1, :] = m_new

    for h in range(N_HEADS):
        for e in E:
            store_logits(e, 0, lga_ref, h)

    def attn_body(pair, carry):
        kb = 2 * pair
        softmax_block(kb, lga_ref, lgb_ref)
        softmax_block(kb + 1, lgb_ref, lga_ref)
        return carry

    lax.fori_loop(0, nkb // 2, attn_body, 0)

    @pl.when(nkb % 2 == 1)
    def _():
        softmax_block(nkb - 1, lga_ref, None)

    for e in E:
        outs = []
        for h in range(N_HEADS):
            num = acc_ref[e, h * ACC_ROWS:h * ACC_ROWS + HEAD_DIM, :]
            den = acc_ref[e, h * ACC_ROWS + HEAD_DIM:h * ACC_ROWS + HEAD_DIM + 1, :]
            outs.append(num / den)
        o_ref[e] = jnp.transpose(jnp.concatenate(outs, axis=0)).astype(o_ref.dtype)


def _attn_call(qt, qit, wit, kn, vt, top_k):
    B, S = kn.shape[:2]
    C = ATTN_CHUNK
    P = ATTN_PAIR
    nc = S // C
    assert B % P == 0 and S % C == 0, "attention steps take whole batch pairs and whole query chunks"
    kern = functools.partial(_attn_kernel, seq=S, top_k=top_k)
    in_specs = [
        pl.BlockSpec((P, 1, Q_WIDTH, C), lambda b, c: (b, c, 0, 0)),
        pl.BlockSpec((P, 1, IDXQ_WIDTH, C), lambda b, c: (b, c, 0, 0)),
        pl.BlockSpec((P, 1, IDX_HEADS, C), lambda b, c: (b, c, 0, 0)),
        pl.BlockSpec((P, S, KN_WIDTH), lambda b, c: (b, 0, 0)),
        pl.BlockSpec((P, nc, KV_WIDTH, C), lambda b, c: (b, 0, 0, 0)),
    ]
    out_specs = pl.BlockSpec((P, C, Q_WIDTH), lambda b, c: (b, c, 0))
    scratch = [
        pltpu.VMEM((P, nc, C, C), jnp.int32),
        pltpu.VMEM((P, nc, C, C), jnp.int16),
        pltpu.VMEM((P, nc, C, C), jnp.int16),
        pltpu.VMEM((P, N_HEADS, C, C), jnp.float32),
        pltpu.VMEM((P, N_HEADS, C, C), jnp.float32),
        pltpu.VMEM((P, N_HEADS * ACC_ROWS, C), jnp.float32),
        pltpu.VMEM((P, N_HEADS, C), jnp.float32),
    ]
    vmem = P * (2 * S * C * 4 + Q_WIDTH * C * 4 + 2 * 2 * (2 * Q_WIDTH * C + S * KN_WIDTH + S * KV_WIDTH + C * Q_WIDTH)
                + 2 * N_HEADS * C * C * 4) + 24 * C * C * 4 + (8 << 20)
    return pl.pallas_call(
        kern, grid=(B // P, nc), in_specs=in_specs, out_specs=out_specs,
        out_shape=jax.ShapeDtypeStruct((B, S, Q_WIDTH), jnp.bfloat16),
        scratch_shapes=scratch, compiler_params=_cparams(2, vmem), name="dsa_attn",
    )(qt, qit, wit, kn, vt)


def _layer_norm(v, g, b):
    mu = jnp.mean(v, axis=-1, keepdims=True)
    d = v - mu
    var = jnp.mean(d * d, axis=-1, keepdims=True)
    return d * lax.rsqrt(var + LN_EPS) * g + b


def _pack_rows(h):
    hi = lax.bitcast_convert_type(h[:, :HALF].astype(jnp.bfloat16).astype(jnp.float32), jnp.int32)
    lo = lax.bitcast_convert_type(h[:, HALF:].astype(jnp.bfloat16).astype(jnp.float32), jnp.int32)
    return (hi & jnp.int32(-65536)) | lax.shift_right_logical(lo, 16)


def _unpack_rows(w):
    hi = lax.bitcast_convert_type(w & jnp.int32(-65536), jnp.float32)
    lo = lax.bitcast_convert_type(lax.shift_left(w, 16), jnp.float32)
    return jnp.concatenate([hi, lo], axis=1)


def _mix_kernel(x_ref, attn_ref, wc_ref, wg_ref, gb_ref, cw_ref, wau_ref, wcu_ref, wo_ref,
                g1_ref, b1_ref, wrh_ref, wrl_ref, rb_ref,
                h_ref, xs_ref, pos_ref, gate_ref, cnt_ref,
                ubuf_ref, *, steps_per_seq):
    R = MIX_ROWS
    i = pl.program_id(0)

    @pl.when(i % steps_per_seq == 0)
    def _():
        ubuf_ref[0:V7X_SUBLANES, :] = jnp.zeros((V7X_SUBLANES, CONV_DIM), jnp.float32)

    x = x_ref[...]
    xb = x.astype(jnp.bfloat16)
    cv = jnp.dot(xb, wc_ref[...], preferred_element_type=jnp.float32)
    u = cv[:, 2 * CONV_DIM:] * cv[:, :CONV_DIM]
    ubuf_ref[V7X_SUBLANES:, :] = u
    u1 = ubuf_ref[V7X_SUBLANES - 1:V7X_SUBLANES - 1 + R, :]
    u2 = ubuf_ref[V7X_SUBLANES - 2:V7X_SUBLANES - 2 + R, :]
    y = cw_ref[0:1, :] * u2 + cw_ref[1:2, :] * u1 + cw_ref[2:3, :] * u
    conv = (cv[:, CONV_DIM:2 * CONV_DIM] * y).astype(jnp.bfloat16)
    ubuf_ref[0:V7X_SUBLANES, :] = u[R - V7X_SUBLANES:, :]
    z = jnp.dot(xb, wg_ref[...], preferred_element_type=jnp.float32) + gb_ref[...]
    gates = 1.0 / (1.0 + jnp.exp(-z))
    au = jnp.dot(attn_ref[...], wau_ref[...], preferred_element_type=jnp.float32)
    cu = jnp.dot(conv, wcu_ref[...], preferred_element_type=jnp.float32)
    merged = gates[:, :D_MODEL] * au + gates[:, D_MODEL:] * cu
    mix = jnp.dot(merged.astype(jnp.bfloat16), wo_ref[...], preferred_element_type=jnp.float32)
    h = _layer_norm(DEEPNORM_ALPHA * x + mix, g1_ref[...], b1_ref[...])
    h_ref[...] = h

    h_hi = h.astype(jnp.bfloat16)
    h_lo = (h - h_hi.astype(jnp.float32)).astype(jnp.bfloat16)
    lg = (jnp.dot(h_hi, wrh_ref[...], preferred_element_type=jnp.float32)
          + jnp.dot(h_lo, wrh_ref[...], preferred_element_type=jnp.float32)
          + jnp.dot(h_hi, wrl_ref[...], preferred_element_type=jnp.float32)) + rb_ref[...]
    lane = lax.broadcasted_iota(jnp.int32, (R, ROUTER_LANES), 1).astype(jnp.float32)
    neg = jnp.float32(-jnp.inf)
    no_lane = jnp.float32(ROUTER_LANES)
    gmask = lane < N_GROUPS
    gl = jnp.where(gmask, lg, neg)
    gmax = jnp.max(gl, axis=1, keepdims=True)
    grp = jnp.min(jnp.where(gl == gmax, lane, no_lane), axis=1, keepdims=True)
    gsum = jnp.sum(jnp.where(gmask, jnp.exp(gl - gmax), 0.0), axis=1, keepdims=True)
    p_grp = 1.0 / gsum
    lo_lane = E0 + grp * EXPERTS_PER_GROUP
    emask = (lane >= lo_lane) & (lane < lo_lane + EXPERTS_PER_GROUP)
    el = jnp.where(emask, lg, neg)
    v1 = jnp.max(el, axis=1, keepdims=True)
    i1 = jnp.min(jnp.where(el == v1, lane, no_lane), axis=1, keepdims=True)
    el2 = jnp.where(lane == i1, neg, el)
    v2 = jnp.max(el2, axis=1, keepdims=True)
    i2 = jnp.min(jnp.where(el2 == v2, lane, no_lane), axis=1, keepdims=True)
    a = jnp.exp(v2 - v1)
    inv = 1.0 / (1.0 + a)
    gate_ref[:, 0:1] = p_grp * inv
    gate_ref[:, 1:2] = p_grp * (a * inv)
    oh1 = lane == i1
    oh2 = lane == i2
    oh = (oh1 | oh2).astype(jnp.bfloat16)
    r_i = lax.broadcasted_iota(jnp.int32, (R, R), 0)
    c_i = lax.broadcasted_iota(jnp.int32, (R, R), 1)
    tri = (r_i > c_i).astype(jnp.bfloat16)
    before = jnp.dot(tri, oh, preferred_element_type=jnp.float32)
    cnt = jnp.sum(oh.astype(jnp.float32), axis=0, keepdims=True)
    cnt_ref[0] = cnt

    seg = jnp.floor((cnt + (ROW_GROUP - 1)) * (1.0 / ROW_GROUP)) * ROW_GROUP
    l_r = lax.broadcasted_iota(jnp.int32, (ROUTER_LANES, ROUTER_LANES), 0)
    l_c = lax.broadcasted_iota(jnp.int32, (ROUTER_LANES, ROUTER_LANES), 1)
    lanes_before = (l_r < l_c).astype(jnp.bfloat16)
    seg8 = jnp.broadcast_to(seg, (V7X_SUBLANES, ROUTER_LANES)).astype(jnp.bfloat16)
    seg_off = jnp.dot(seg8, lanes_before, preferred_element_type=jnp.float32)[0:1, :]
    where_to = before + seg_off
    pos1 = jnp.sum(jnp.where(oh1, where_to, 0.0), axis=1, keepdims=True)
    pos2 = jnp.sum(jnp.where(oh2, where_to, 0.0), axis=1, keepdims=True)
    pos_ref[:, 0:1] = pos1.astype(jnp.int32)
    pos_ref[:, 1:2] = pos2.astype(jnp.int32)
    diag = r_i == c_i
    pos1_row = jnp.sum(jnp.where(diag, pos1, 0.0), axis=0, keepdims=True)
    pos2_row = jnp.sum(jnp.where(diag, pos2, 0.0), axis=0, keepdims=True)
    sorted_row = lax.broadcasted_iota(jnp.int32, (SORTED_ROWS, R), 0).astype(jnp.float32)
    place = ((sorted_row == pos1_row) | (sorted_row == pos2_row)).astype(jnp.bfloat16)
    xs = jnp.dot(place, h_hi, preferred_element_type=jnp.float32)
    xs_ref[0] = _pack_rows(xs)


def _mix_call(x2, attn2, wc, wg, gb, cw, wau, wcu, wo, g1, b1, wrh, wrl, rb, seq):
    N, D = x2.shape
    R = MIX_ROWS
    nt = N // R
    kern = functools.partial(_mix_kernel, steps_per_seq=seq // R)

    def full(a):
        return pl.BlockSpec(a.shape, lambda i: (0,) * a.ndim)

    in_specs = [
        pl.BlockSpec((R, D), lambda i: (i, 0)),
        pl.BlockSpec((R, Q_WIDTH), lambda i: (i, 0)),
        full(wc), full(wg), full(gb), full(cw), full(wau), full(wcu), full(wo),
        full(g1), full(b1), full(wrh), full(wrl), full(rb),
    ]
    out_shape = (
        jax.ShapeDtypeStruct((N, D), jnp.float32),
        jax.ShapeDtypeStruct((nt, SORTED_ROWS, HALF), jnp.int32),
        jax.ShapeDtypeStruct((N, 2), jnp.int32),
        jax.ShapeDtypeStruct((N, 2), jnp.float32),
        jax.ShapeDtypeStruct((nt, 1, ROUTER_LANES), jnp.float32),
    )
    out_specs = (
        pl.BlockSpec((R, D), lambda i: (i, 0)),
        pl.BlockSpec((1, SORTED_ROWS, HALF), lambda i: (i, 0, 0)),
        pl.BlockSpec((R, 2), lambda i: (i, 0)),
        pl.BlockSpec((R, 2), lambda i: (i, 0)),
        pl.BlockSpec((1, 1, ROUTER_LANES), lambda i: (i, 0, 0)),
    )
    scratch = [pltpu.VMEM((R + V7X_SUBLANES, CONV_DIM), jnp.float32)]
    w_bytes = 2 * (wc.size + wg.size + wau.size + wcu.size + wo.size + wrh.size + wrl.size)
    vmem = (2 * w_bytes + 2 * (R * D * 4 * 2 + R * Q_WIDTH * 2 + SORTED_ROWS * HALF * 4) + 10 * R * 2048 * 4
            + SORTED_ROWS * D * 6 + (6 << 20))
    return pl.pallas_call(
        kern, grid=(nt,), in_specs=in_specs, out_specs=out_specs, out_shape=out_shape,
        scratch_shapes=scratch, compiler_params=_cparams(1, vmem), name="mix_ln_router",
    )(x2, attn2, wc, wg, gb, cw, wau, wcu, wo, g1, b1, wrh, wrl, rb)


GROUPS_PER_BLOCK = EXPERT_ROWS // ROW_GROUP


def _expert_kernel(be_ref, nb_ref, gsrc_ref, gdst_ref, xs_ref, wg_ref, wu_ref, wd_ref, ys_ref,
                   wgb_ref, wub_ref, wdb_ref, xbuf_ref, ybuf_ref, gsems, ssems):
    del xs_ref
    j = pl.program_id(0)
    n_used = nb_ref[0]
    slot = j % 2

    def gather(block, to_slot):
        for g in range(GROUPS_PER_BLOCK):
            row = pl.multiple_of(gsrc_ref[block * GROUPS_PER_BLOCK + g], ROW_GROUP)
            pltpu.make_async_copy(ys_ref.at[pl.ds(row, ROW_GROUP), :],
                                  xbuf_ref.at[to_slot, pl.ds(g * ROW_GROUP, ROW_GROUP), :],
                                  gsems.at[to_slot]).start(priority=g % 2)

    def scatter(block, from_slot):
        for g in range(GROUPS_PER_BLOCK):
            row = pl.multiple_of(gdst_ref[block * GROUPS_PER_BLOCK + g], ROW_GROUP)
            pltpu.make_async_copy(ybuf_ref.at[from_slot, pl.ds(g * ROW_GROUP, ROW_GROUP), :],
                                  ys_ref.at[pl.ds(row, ROW_GROUP), :],
                                  ssems.at[from_slot]).start(priority=g % 2)

    def wait_gather(s):
        pltpu.make_async_copy(ys_ref.at[pl.ds(0, EXPERT_ROWS), :], xbuf_ref.at[s], gsems.at[s]).wait()

    def wait_scatter(s):
        pltpu.make_async_copy(ybuf_ref.at[s], ys_ref.at[pl.ds(0, EXPERT_ROWS), :], ssems.at[s]).wait()

    @pl.when((j == 0) & (n_used > 0))
    def _():
        gather(0, 0)

    @pl.when((j == 0) | (be_ref[j] != be_ref[jnp.maximum(j - 1, 0)]))
    def _():
        wgb_ref[...] = wg_ref[0].astype(jnp.bfloat16)
        wub_ref[...] = wu_ref[0].astype(jnp.bfloat16)
        wdb_ref[...] = wd_ref[0].astype(jnp.bfloat16)

    @pl.when(j + 1 < n_used)
    def _():
        gather(j + 1, 1 - slot)

    @pl.when((j >= 2) & (j < n_used))
    def _():
        wait_scatter(slot)

    @pl.when(j < n_used)
    def _():
        wait_gather(slot)
        xb = _unpack_rows(xbuf_ref[slot]).astype(jnp.bfloat16)
        hg = jnp.dot(xb, wgb_ref[...], preferred_element_type=jnp.float32)
        hu = jnp.dot(xb, wub_ref[...], preferred_element_type=jnp.float32)
        hid = (hg / (1.0 + jnp.exp(-hg))) * hu
        y = jnp.dot(hid.astype(jnp.bfloat16), wdb_ref[...], preferred_element_type=jnp.float32)
        ybuf_ref[slot] = _pack_rows(y)
        scatter(j, slot)

    @pl.when(j == n_used - 1)
    def _():
        wait_scatter(slot)

    @pl.when((j == n_used - 1) & (j >= 1))
    def _():
        wait_scatter(1 - slot)


def _expert_call(block_expert, n_used, group_src, group_dst, xs_sorted, wg, wu, wd, n_blocks):
    W = xs_sorted.shape[-1]
    RB = EXPERT_ROWS
    grid_spec = pltpu.PrefetchScalarGridSpec(
        num_scalar_prefetch=4, grid=(n_blocks,),
        in_specs=[
            pl.BlockSpec(memory_space=pl.ANY),
            pl.BlockSpec((1, D_MODEL, D_EXPERT), lambda j, be, nu, gs, gd: (be[j], 0, 0)),
            pl.BlockSpec((1, D_MODEL, D_EXPERT), lambda j, be, nu, gs, gd: (be[j], 0, 0)),
            pl.BlockSpec((1, D_EXPERT, D_MODEL), lambda j, be, nu, gs, gd: (be[j], 0, 0)),
        ],
        out_specs=pl.BlockSpec(memory_space=pl.ANY),
        scratch_shapes=[pltpu.VMEM((D_MODEL, D_EXPERT), jnp.bfloat16),
                        pltpu.VMEM((D_MODEL, D_EXPERT), jnp.bfloat16),
                        pltpu.VMEM((D_EXPERT, D_MODEL), jnp.bfloat16),
                        pltpu.VMEM((2, RB, W), jnp.int32),
                        pltpu.VMEM((2, RB, W), jnp.int32),
                        pltpu.SemaphoreType.DMA((2,)),
                        pltpu.SemaphoreType.DMA((2,))],
    )
    vmem = (2 * 4 + 2) * 3 * D_MODEL * D_EXPERT + 4 * RB * W * 4 + 8 * RB * D_MODEL * 4 + (6 << 20)
    return pl.pallas_call(
        _expert_kernel, grid_spec=grid_spec,
        out_shape=jax.ShapeDtypeStruct(xs_sorted.shape, xs_sorted.dtype),
        input_output_aliases={4: 0},
        compiler_params=_cparams(1, vmem), name="moe_experts",
    )(block_expert, n_used, group_src, group_dst, xs_sorted, wg, wu, wd)


def _combine_kernel(ys_ref, pos_ref, gate_ref, h_ref, g2_ref, b2_ref, o_ref):
    R = MIX_ROWS
    y = _unpack_rows(ys_ref[0]).astype(jnp.bfloat16)
    col = lax.broadcasted_iota(jnp.int32, (R, SORTED_ROWS), 1)
    pick = (jnp.where(col == pos_ref[:, 0:1], gate_ref[:, 0:1], 0.0)
            + jnp.where(col == pos_ref[:, 1:2], gate_ref[:, 1:2], 0.0)).astype(jnp.bfloat16)
    ffn = jnp.dot(pick, y, preferred_element_type=jnp.float32)
    o_ref[...] = _layer_norm(DEEPNORM_ALPHA * h_ref[...] + ffn, g2_ref[...], b2_ref[...])


def _combine_call(ys_sorted, pos, gate, h, g2, b2):
    N, D = h.shape
    R = MIX_ROWS
    nt = N // R
    W = ys_sorted.shape[-1]
    return pl.pallas_call(
        _combine_kernel, grid=(nt,),
        in_specs=[
            pl.BlockSpec((1, SORTED_ROWS, W), lambda i: (i, 0, 0)),
            pl.BlockSpec((R, 2), lambda i: (i, 0)),
            pl.BlockSpec((R, 2), lambda i: (i, 0)),
            pl.BlockSpec((R, D), lambda i: (i, 0)),
            pl.BlockSpec((1, D), lambda i: (0, 0)),
            pl.BlockSpec((1, D), lambda i: (0, 0)),
        ],
        out_specs=pl.BlockSpec((R, D), lambda i: (i, 0)),
        out_shape=jax.ShapeDtypeStruct((N, D), jnp.float32),
        compiler_params=_cparams(1, 2 * (SORTED_ROWS * W * 4 + 2 * R * D * 4) + SORTED_ROWS * D * 8
                                 + 3 * R * SORTED_ROWS * 4 + (6 << 20)),
        name="moe_combine",
    )(ys_sorted, pos, gate, h, g2, b2)


def _rope_tables(seq):
    inv_freq = ROPE_THETA ** (-jnp.arange(ROT_HALF, dtype=jnp.float32) / ROT_HALF)
    ang = jnp.arange(seq, dtype=jnp.int32).astype(jnp.float32)[:, None] * inv_freq[None, :]
    cos, sin = jnp.cos(ang), jnp.sin(ang)
    ones = jnp.ones((seq, HEAD_DIM - ROT_DIMS), jnp.float32)
    zeros = jnp.zeros((seq, HEAD_DIM - ROT_DIMS), jnp.float32)
    c_head = jnp.concatenate([cos, cos, ones], axis=1)
    s_head = jnp.concatenate([-sin, sin, zeros], axis=1)
    reps = KN_WIDTH // HEAD_DIM
    return cos.T, sin.T, jnp.tile(c_head, (1, reps)), jnp.tile(s_head, (1, reps))


def _swap_rot_cols(w):
    d, n = w.shape
    wh = w.reshape(d, n // HEAD_DIM, HEAD_DIM)
    sw = jnp.concatenate([wh[:, :, ROT_HALF:ROT_DIMS], wh[:, :, :ROT_HALF],
                          jnp.zeros((d, n // HEAD_DIM, HEAD_DIM - ROT_DIMS), w.dtype)], axis=2)
    return sw.reshape(d, n)


def _block(x, w_in, gate_bias, w_attn_up, w_conv_up, conv_w, w_out, ln_g, ln_b,
           rg_w, rg_b, re_w, re_b, w_gate_e, w_up_e, w_down_e, ln2_g, ln2_b):
    B, S, D = x.shape
    N = B * S
    top_k = min(TOPK_MAX, S // 4)
    bf = jnp.bfloat16
    o = np.cumsum([0, Q_WIDTH, KV_WIDTH, KV_WIDTH, IDXQ_WIDTH, IDX_DIM, IDX_HEADS,
                   CONV_DIM, CONV_DIM, CONV_DIM, N_BRANCHES * D_MODEL])
    w_q, w_k, w_v, w_qi, w_ki, w_wi = (w_in[:, o[i]:o[i + 1]] for i in range(6))
    w_conv = w_in[:, o[6]:o[9]]
    w_gates = w_in[:, o[9]:o[10]]

    wt = jnp.concatenate([w_q.T * (HEAD_DIM ** -0.5 * LOG2_E), w_qi.T, w_v.T, w_wi.T,
                          jnp.zeros((T_ROWS - T_WI0 - IDX_HEADS, D), w_in.dtype)], axis=0).astype(bf)
    pad = jnp.zeros((D, KN_WIDTH - KV_WIDTH - IDX_DIM), w_in.dtype)
    wn = jnp.concatenate([w_k, w_ki, pad, _swap_rot_cols(w_k), _swap_rot_cols(w_ki), pad], axis=1).astype(bf)
    cos_t, sin_t, cos_n, sgn_n = _rope_tables(S)

    qt, qit, vt, wit, kn = _proj_call(x, wt, wn, cos_t, sin_t, cos_n, sgn_n)
    attn = _attn_call(qt, qit, wit, kn, vt, top_k)

    w_r = jnp.concatenate([rg_w, jnp.transpose(re_w, (1, 0, 2)).reshape(D, N_EXPERTS),
                           jnp.zeros((D, ROUTER_LANES - E0 - N_EXPERTS), rg_w.dtype)], axis=1)
    b_r = jnp.concatenate([rg_b, re_b.reshape(-1),
                           jnp.zeros((ROUTER_LANES - E0 - N_EXPERTS,), rg_b.dtype)])[None, :]
    w_rh = w_r.astype(bf)
    w_rl = (w_r - w_rh.astype(jnp.float32)).astype(bf)

    h, xs_sorted, pos, gate, cnt = _mix_call(
        x.reshape(N, D), attn.reshape(N, Q_WIDTH), w_conv.astype(bf), w_gates.astype(bf),
        gate_bias[None, :], conv_w, w_attn_up.astype(bf), w_conv_up.astype(bf), w_out.astype(bf),
        ln_g[None, :], ln_b[None, :], w_rh, w_rl, b_r, S)

    RB, RG = EXPERT_ROWS, ROW_GROUP
    nt = N // MIX_ROWS
    counts = cnt[:, 0, E0:E0 + N_EXPERTS].astype(jnp.int32)
    seg_len = ((counts + RG - 1) // RG) * RG
    seg_local = jnp.cumsum(seg_len, axis=1) - seg_len
    region = jnp.sum(seg_len, axis=0)
    padded = ((region + RB - 1) // RB) * RB
    pad_end = jnp.cumsum(padded)
    pad_start = pad_end - padded
    seg_start = pad_start[None, :] + jnp.cumsum(seg_len, axis=0) - seg_len
    cap = -(-(N * 2 + nt * N_EXPERTS * (RG - 1) + N_EXPERTS * (RB - 1)) // RB) * RB
    nb = cap // RB
    block_start = jnp.arange(nb, dtype=jnp.int32) * RB
    block_expert = jnp.minimum(jnp.sum((block_start[:, None] >= pad_end[None, :]).astype(jnp.int32), axis=1),
                               N_EXPERTS - 1)
    n_used = (pad_end[-1:] // RB).astype(jnp.int32)
    step_base = (jnp.arange(nt, dtype=jnp.int32) * SORTED_ROWS)[:, None]
    g_idx = jnp.arange(cap // RG, dtype=jnp.int32)
    g_row = (g_idx * RG).reshape(nb, GROUPS_PER_BLOCK)
    seg_end_b = (seg_start + seg_len).T[block_expert]
    seg_start_b = seg_start.T[block_expert]
    seg_shift_b = (step_base + seg_local - seg_start).T[block_expert]
    ended = seg_end_b[:, None, :-1] <= g_row[:, :, None]

    def lookup(v):
        return v[:, None, 0] + jnp.sum(jnp.where(ended, (v[:, 1:] - v[:, :-1])[:, None, :], 0), axis=2)

    live = ((g_row >= lookup(seg_start_b)) & (g_row < lookup(seg_end_b))).reshape(-1)
    home = (g_row + lookup(seg_shift_b)).reshape(-1)
    spare_per_step = (SORTED_ROWS - SORTED_USED) // RG
    spare = ((g_idx // GROUPS_PER_BLOCK) % 2) * GROUPS_PER_BLOCK + g_idx % GROUPS_PER_BLOCK
    assert 2 * GROUPS_PER_BLOCK <= (nt - 1) * spare_per_step, "not enough spare groups for block padding"
    spare_row = (1 + spare // spare_per_step) * SORTED_ROWS + SORTED_USED + (spare % spare_per_step) * RG
    group_src = jnp.where(live, home, SORTED_ROWS - RG).astype(jnp.int32)
    group_dst = jnp.where(live, home, spare_row).astype(jnp.int32)

    ys_sorted = _expert_call(block_expert, n_used, group_src, group_dst,
                             xs_sorted.reshape(nt * SORTED_ROWS, HALF), w_gate_e, w_up_e, w_down_e, nb)
    out = _combine_call(ys_sorted.reshape(nt, SORTED_ROWS, HALF), pos, gate, h, ln2_g[None, :], ln2_b[None, :])
    return out.reshape(B, S, D)


def kernel(x, w_in, gate_bias, w_attn_up, w_conv_up, conv_w, w_out, ln1_g, ln1_b, router_group_w,
           router_group_b, router_expert_w, router_expert_b, w_gate_e, w_up_e, w_down_e, ln2_g, ln2_b):
    h = x
    for l in range(DEPTH):
        h = _block(h, w_in[l], gate_bias[l], w_attn_up[l], w_conv_up[l], conv_w[l], w_out[l],
                   ln1_g[l], ln1_b[l], router_group_w[l], router_group_b[l], router_expert_w[l],
                   router_expert_b[l], w_gate_e[l], w_up_e[l], w_down_e[l], ln2_g[l], ln2_b[l])
    return h
```
